```python
import math
import jax, jax.numpy as jnp
from jax import lax
import numpy as np

D_MODEL = 2048
BATCH = 8
SEQ = 2048
DEPTH = 1

PLE_DIM = 256
D_FF = 5632
DIFF_HEADS = 8
DIFF_HEAD_DIM = 64
DIFF_V_DIM = 2 * DIFF_HEAD_DIM
DIFF_WIDTH = DIFF_HEADS * 2 * DIFF_HEAD_DIM
HGRN_HEADS = 8
HGRN_K = 128
HGRN_V = 128
HGRN_WIDTH = HGRN_HEADS * HGRN_K
HGRN_CHUNK = 64
Q_BLOCK = 128
REL_BUCKETS = 32
REL_MAX_DIST = 128
N_IN = 3 * DIFF_WIDTH + 4 * HGRN_WIDTH + 2 * D_MODEL
EPS = 1e-6

kernel_name = "hybrid_diffattn_hgrn2_macaron_block"


def rmsnorm(x, g):
    xf = x.astype(jnp.float32)
    y = xf * lax.rsqrt(jnp.mean(xf * xf, axis=-1, keepdims=True) + EPS)
    return (y * g.astype(jnp.float32)).astype(x.dtype)


def swiglu(h, w_gate, w_up, w_down):
    return (jax.nn.silu(h @ w_gate) * (h @ w_up)) @ w_down


def t5_bucket(dist):
    n = jnp.maximum(dist, 0)
    max_exact = REL_BUCKETS // 2
    nf = jnp.maximum(n, 1).astype(jnp.float32)
    large = max_exact + (jnp.log(nf / max_exact) / math.log(REL_MAX_DIST / max_exact)
                         * (REL_BUCKETS - max_exact)).astype(jnp.int32)
    large = jnp.minimum(large, REL_BUCKETS - 1)
    return jnp.where(n < max_exact, n, large)


def diff_attention(q, k, v, q_gain, k_gain, lam, rel_bias, subln_gain, lambda_init):
    B, S, _ = q.shape
    q = rmsnorm(q.reshape(B, S, DIFF_HEADS, 2, DIFF_HEAD_DIM), q_gain).transpose(0, 2, 3, 1, 4)
    k = rmsnorm(k.reshape(B, S, DIFF_HEADS, 2, DIFF_HEAD_DIM), k_gain).transpose(0, 2, 3, 1, 4)
    v = v.reshape(B, S, DIFF_HEADS, DIFF_V_DIM).transpose(0, 2, 1, 3).astype(jnp.float32)
    q1, q2 = q[:, :, 0], q[:, :, 1]
    k1, k2 = k[:, :, 0], k[:, :, 1]
    scale = DIFF_HEAD_DIM ** -0.5
    k_pos = jnp.arange(S)

    def block(blk):
        start = blk * Q_BLOCK
        q1b = lax.dynamic_slice_in_dim(q1, start, Q_BLOCK, axis=2)
        q2b = lax.dynamic_slice_in_dim(q2, start, Q_BLOCK, axis=2)
        dist = (start + jnp.arange(Q_BLOCK))[:, None] - k_pos[None, :]
        bias = jnp.transpose(rel_bias[t5_bucket(dist)], (2, 0, 1)).astype(jnp.float32)
        visible = dist >= 0

        def probs(qb, kk):
            s = jnp.einsum('bhqd,bhkd->bhqk', qb, kk).astype(jnp.float32) * scale + bias
            return jax.nn.softmax(jnp.where(visible, s, -jnp.inf), axis=-1)

        a = probs(q1b, k1) - lam * probs(q2b, k2)
        return jnp.einsum('bhqk,bhkv->bhqv', a, v)

    o = lax.map(block, jnp.arange(S // Q_BLOCK))
    o = o.transpose(1, 0, 3, 2, 4).reshape(B, S, DIFF_HEADS, DIFF_V_DIM)
    o = rmsnorm(o, subln_gain) * (1.0 - lambda_init)
    return o.reshape(B, S, DIFF_WIDTH).astype(q.dtype)


def hgrn2(q, f_pre, i_in, og, lb, norm_gain):
    B, S, _ = q.shape
    nc = S // HGRN_CHUNK
    f32 = jnp.float32

    def chunks(t, dh):
        return t.reshape(B, nc, HGRN_CHUNK, HGRN_HEADS, dh).transpose(1, 0, 3, 2, 4)

    lb = lb.astype(f32)
    forget = lb + (1.0 - lb) * jax.nn.sigmoid(f_pre.astype(f32))
    qf = jax.nn.silu(q.astype(f32))
    kf = 1.0 - forget
    logf = jnp.log(forget)
    tri = jnp.arange(HGRN_CHUNK)[:, None] >= jnp.arange(HGRN_CHUNK)[None, :]

    def step(state, inp):
        qc, kc, vc, gc = inp
        b = jnp.cumsum(gc, axis=2)
        rel = jnp.where(tri[:, :, None], b[:, :, :, None, :] - b[:, :, None, :, :], -jnp.inf)
        scores = jnp.einsum('bhtk,bhsk,bhtsk->bhts', qc, kc, jnp.exp(rel))
        o = (jnp.einsum('bhts,bhsv->bhtv', scores, vc)
             + jnp.einsum('bhtk,bhkv->bhtv', qc * jnp.exp(b), state))
        b_last = b[:, :, -1:, :]
        state = (jnp.exp(b_last[:, :, 0, :])[..., None] * state
                 + jnp.einsum('bhsk,bhsv->bhkv', kc * jnp.exp(b_last - b), vc))
        return state, o

    state0 = jnp.zeros((B, HGRN_HEADS, HGRN_K, HGRN_V), f32)
    _, o = lax.scan(step, state0, (chunks(qf, HGRN_K), chunks(kf, HGRN_K),
                                   chunks(i_in.astype(f32), HGRN_V), chunks(logf, HGRN_K)))
    o = o.transpose(1, 0, 3, 2, 4).reshape(B, S, HGRN_HEADS, HGRN_V)
    o = rmsnorm(o, norm_gain) * jax.nn.silu(og.astype(f32)).reshape(B, S, HGRN_HEADS, HGRN_V)
    return o.reshape(B, S, HGRN_WIDTH).astype(q.dtype)


def setup_inputs(seed: int = 0) -> dict:
    key = jax.random.key(seed)
    ks = jax.random.split(key, 32)

    def nrm(k, shape, scale):
        return jax.random.normal(k, shape, jnp.float32) * scale

    def gain(k, shape):
        return 1.0 + 0.1 * jax.random.normal(k, shape, jnp.float32)

    L = DEPTH
    return {
        'x': nrm(ks[0], (BATCH, SEQ, D_MODEL), 1.0),
        'p': nrm(ks[1], (L, BATCH, SEQ, PLE_DIM), 1.0),
        'ffn1_norm': gain(ks[2], (L, D_MODEL)),
        'ffn1_w_gate': nrm(ks[3], (L, D_MODEL, D_FF), D_MODEL ** -0.5),
        'ffn1_w_up': nrm(ks[4], (L, D_MODEL, D_FF), D_MODEL ** -0.5),
        'ffn1_w_down': nrm(ks[5], (L, D_FF, D_MODEL), D_FF ** -0.5),
        'mix_norm': gain(ks[6], (L, D_MODEL)),
        'w_in': nrm(ks[7], (L, D_MODEL, N_IN), D_MODEL ** -0.5),
        'q_norm': gain(ks[8], (L, DIFF_HEAD_DIM)),
        'k_norm': gain(ks[9], (L, DIFF_HEAD_DIM)),
        'lambda_q1': nrm(ks[10], (L, DIFF_HEAD_DIM), 0.1),
        'lambda_k1': nrm(ks[11], (L, DIFF_HEAD_DIM), 0.1),
        'lambda_q2': nrm(ks[12], (L, DIFF_HEAD_DIM), 0.1),
        'lambda_k2': nrm(ks[13], (L, DIFF_HEAD_DIM), 0.1),
        'diff_subln': gain(ks[14], (L, DIFF_V_DIM)),
        'rel_bias': nrm(ks[15], (REL_BUCKETS, DIFF_HEADS), 0.5),
        'hgrn_lb_logits': nrm(ks[16], (L + 1, HGRN_WIDTH), 0.5),
        'hgrn_norm': gain(ks[17], (L, HGRN_V)),
        'w_branch_a': nrm(ks[18], (L, DIFF_WIDTH, D_MODEL), DIFF_WIDTH ** -0.5),
        'w_branch_b': nrm(ks[19], (L, HGRN_WIDTH, D_MODEL), HGRN_WIDTH ** -0.5),
        'w_out': nrm(ks[20], (L, D_MODEL, D_MODEL), D_MODEL ** -0.5),
        'ffn2_norm': gain(ks[21], (L, D_MODEL)),
        'ffn2_w_gate': nrm(ks[22], (L, D_MODEL, D_FF), D_MODEL ** -0.5),
        'ffn2_w_up': nrm(ks[23], (L, D_MODEL, D_FF), D_MODEL ** -0.5),
        'ffn2_w_down': nrm(ks[24], (L, D_FF, D_MODEL), D_FF ** -0.5),
        'ple_gate_norm': gain(ks[25], (L, D_MODEL)),
        'w_ple_gate': nrm(ks[26], (L, D_MODEL, D_MODEL), D_MODEL ** -0.5),
        'w_ple_proj': nrm(ks[27], (L, PLE_DIM, D_MODEL), PLE_DIM ** -0.5),
        'ple_post_norm': gain(ks[28], (L, D_MODEL)),
    }


def reference(x, p, ffn1_norm, ffn1_w_gate, ffn1_w_up, ffn1_w_down, mix_norm, w_in,
              q_norm, k_norm, lambda_q1, lambda_k1, lambda_q2, lambda_k2, diff_subln,
              rel_bias, hgrn_lb_logits, hgrn_norm, w_branch_a, w_branch_b, w_out,
              ffn2_norm, ffn2_w_gate, ffn2_w_up, ffn2_w_down,
              ple_gate_norm, w_ple_gate, w_ple_proj, ple_post_norm):
    lower_bounds = jnp.cumsum(jax.nn.softmax(hgrn_lb_logits.astype(jnp.float32), axis=0), axis=0)
    split_points = list(np.cumsum([DIFF_WIDTH, DIFF_WIDTH, DIFF_WIDTH,
                                   HGRN_WIDTH, HGRN_WIDTH, HGRN_WIDTH, HGRN_WIDTH, D_MODEL]))
    for i in range(DEPTH):
        x = x + 0.5 * swiglu(rmsnorm(x, ffn1_norm[i]), ffn1_w_gate[i], ffn1_w_up[i], ffn1_w_down[i])

        h = rmsnorm(x, mix_norm[i])
        proj = h @ w_in[i]
        dq, dk, dv, hq, hf, hi, hg, gate_a, gate_b = jnp.split(proj, split_points, axis=-1)

        lambda_init = 0.8 - 0.6 * math.exp(-0.3 * i)
        lam = (jnp.exp(jnp.sum(lambda_q1[i].astype(jnp.float32) * lambda_k1[i].astype(jnp.float32)))
               - jnp.exp(jnp.sum(lambda_q2[i].astype(jnp.float32) * lambda_k2[i].astype(jnp.float32)))
               + lambda_init)
        y_a = diff_attention(dq, dk, dv, q_norm[i], k_norm[i], lam, rel_bias, diff_subln[i], lambda_init)
        y_b = hgrn2(hq, hf, hi, hg, lower_bounds[i], hgrn_norm[i])

        merged = jax.nn.sigmoid(gate_a) * (y_a @ w_branch_a[i]) + jax.nn.sigmoid(gate_b) * (y_b @ w_branch_b[i])
        x = x + merged @ w_out[i]

        x = x + 0.5 * swiglu(rmsnorm(x, ffn2_norm[i]), ffn2_w_gate[i], ffn2_w_up[i], ffn2_w_down[i])

        ple = rmsnorm(p[i] @ w_ple_proj[i], ple_post_norm[i])
        x = x + jax.nn.sigmoid(rmsnorm(x, ple_gate_norm[i]) @ w_ple_gate[i]) * ple
    return x
```

```python
import functools
import math

import jax
import jax.numpy as jnp
import numpy as np
from jax import lax
from jax.experimental import pallas as pl
from jax.experimental.pallas import tpu as pltpu

F32 = jnp.float32
BF16 = jnp.bfloat16

EPS = 1e-6
HEADS = 8
HEAD_W = 128
QK_DIM = 64
MIX_W = HEADS * HEAD_W
REL_BUCKETS = 32
REL_MAX_DIST = 128
LAMBDA_INIT = 0.8 - 0.6 * math.exp(-0.3 * 0)

V7X_VMEM_BYTES = 64 * 1024 * 1024
VMEM_LIMIT = 56 * 1024 * 1024


def _cparams(sem):
    return pltpu.CompilerParams(dimension_semantics=sem, vmem_limit_bytes=VMEM_LIMIT)


def _sigmoid(x):
    return 1.0 / (1.0 + jnp.exp(-x))


def _rms(x, gain):
    ms = jnp.mean(x * x, axis=-1, keepdims=True)
    return x * lax.rsqrt(ms + EPS) * gain


def _dot(a, b):
    return jnp.dot(a, b, preferred_element_type=F32)


def _dot_nt(a, b):
    return lax.dot_general(a, b, (((1,), (1,)), ((), ())), preferred_element_type=F32)


def _dot_tn(a, b):
    return lax.dot_general(a, b, (((0,), (0,)), ((), ())), preferred_element_type=F32)


def _ffn_kernel(x_ref, g_ref, wg_ref, wu_ref, wd_ref, o_ref, h_ref):
    j = pl.program_id(1)

    @pl.when(j == 0)
    def _():
        x = x_ref[...]
        h_ref[...] = _rms(x, g_ref[...]).astype(BF16)
        o_ref[...] = x

    h = h_ref[...]
    gate = _dot(h, wg_ref[...])
    up = _dot(h, wu_ref[...])
    act = (gate * _sigmoid(gate) * (0.5 * up)).astype(BF16)
    o_ref[...] += _dot(act, wd_ref[...])


def _ffn(x, gain, w_gate, w_up, w_down, *, tm, tf):
    m, d = x.shape
    f = w_gate.shape[1]
    return pl.pallas_call(
        _ffn_kernel,
        out_shape=jax.ShapeDtypeStruct((m, d), F32),
        grid=(m // tm, f // tf),
        in_specs=[
            pl.BlockSpec((tm, d), lambda i, j: (i, 0)),
            pl.BlockSpec((1, d), lambda i, j: (0, 0)),
            pl.BlockSpec((d, tf), lambda i, j: (0, j)),
            pl.BlockSpec((d, tf), lambda i, j: (0, j)),
            pl.BlockSpec((tf, d), lambda i, j: (j, 0)),
        ],
        out_specs=pl.BlockSpec((tm, d), lambda i, j: (i, 0)),
        scratch_shapes=[pltpu.VMEM((tm, d), BF16)],
        compiler_params=_cparams(("parallel", "arbitrary")),
        name="ffn",
    )(x, gain.reshape(1, d), w_gate, w_up, w_down)


def _proj_kernel(x_ref, g_ref, w_ref, qg_ref, kg_ref, lb_ref, gm_ref,
                 oqkv_ref, oh_ref, olf_ref, og_ref, h_ref, *, tn):
    j = pl.program_id(1)
    tps = MIX_W // tn
    hpt = tn // HEAD_W

    @pl.when(j == 0)
    def _():
        h_ref[...] = _rms(x_ref[...], g_ref[...]).astype(BF16)

    acc = _dot(h_ref[...], w_ref[...])

    def heads(vals):
        return [vals[:, s * HEAD_W:(s + 1) * HEAD_W] for s in range(hpt)]

    @pl.when(j < 2 * tps)
    def _():
        gain = jnp.where(j < tps, qg_ref[...], kg_ref[...])
        for s, a in enumerate(heads(acc)):
            ms = _dot((a * a).astype(BF16), gm_ref[...])
            oqkv_ref[0, s] = (a * lax.rsqrt(ms + EPS) * gain).astype(BF16)

    @pl.when((j >= 2 * tps) & (j < 3 * tps))
    def _():
        for s, a in enumerate(heads(acc)):
            oqkv_ref[0, s] = a.astype(BF16)

    @pl.when(((j >= 3 * tps) & (j < 4 * tps)) | ((j >= 6 * tps) & (j < 7 * tps)))
    def _():
        for s, a in enumerate(heads(acc * _sigmoid(acc))):
            oh_ref[0, s] = a.astype(BF16)

    @pl.when((j >= 4 * tps) & (j < 5 * tps))
    def _():
        jj = j - 4 * tps
        for s, a in enumerate(heads(acc)):
            lb = lb_ref[pl.ds(jj * hpt + s, 1), :]
            forget = lb + (1.0 - lb) * _sigmoid(a)
            oh_ref[0, s] = (1.0 - forget).astype(BF16)
            olf_ref[0, s] = jnp.log(forget)

    @pl.when((j >= 5 * tps) & (j < 6 * tps))
    def _():
        for s, a in enumerate(heads(acc)):
            oh_ref[0, s] = a.astype(BF16)

    @pl.when(j >= 7 * tps)
    def _():
        og_ref[...] = _sigmoid(acc).astype(BF16)


def _proj(x, gain, w_in, qg, kg, lb, *, batch, seq, tm, tn):
    m, d = x.shape
    n_in = w_in.shape[1]
    tps = MIX_W // tn
    hpt = tn // HEAD_W
    spb = seq // tm
    ng = 2 * d // tn
    nj = 7 * tps + ng
    assert n_in == 7 * MIX_W + 2 * d and MIX_W % tn == 0 and (2 * d) % tn == 0

    lane = np.arange(HEAD_W)
    gm = (lane[:, None] // QK_DIM == lane[None, :] // QK_DIM).astype(np.float32) / QK_DIM

    def hm_block(lo, n):
        return lambda i, j: (i // spb, jnp.clip(j - lo, 0, n - 1), i % spb, 0)

    return pl.pallas_call(
        functools.partial(_proj_kernel, tn=tn),
        out_shape=(
            jax.ShapeDtypeStruct((batch, 3 * HEADS, seq, HEAD_W), BF16),
            jax.ShapeDtypeStruct((batch, 4 * HEADS, seq, HEAD_W), BF16),
            jax.ShapeDtypeStruct((batch, HEADS, seq, HEAD_W), F32),
            jax.ShapeDtypeStruct((m, 2 * d), BF16),
        ),
        grid=(m // tm, nj),
        in_specs=[
            pl.BlockSpec((tm, d), lambda i, j: (i, 0)),
            pl.BlockSpec((1, d), lambda i, j: (0, 0)),
            pl.BlockSpec((d, tn), lambda i, j: (0, j)),
            pl.BlockSpec((1, HEAD_W), lambda i, j: (0, 0)),
            pl.BlockSpec((1, HEAD_W), lambda i, j: (0, 0)),
            pl.BlockSpec((HEADS, HEAD_W), lambda i, j: (0, 0)),
            pl.BlockSpec((HEAD_W, HEAD_W), lambda i, j: (0, 0)),
        ],
        out_specs=(
            pl.BlockSpec((1, hpt, tm, HEAD_W), hm_block(0, 3 * tps)),
            pl.BlockSpec((1, hpt, tm, HEAD_W), hm_block(3 * tps, 4 * tps)),
            pl.BlockSpec((1, hpt, tm, HEAD_W), hm_block(4 * tps, tps)),
            pl.BlockSpec((tm, tn), lambda i, j: (i, jnp.clip(j - 7 * tps, 0, ng - 1))),
        ),
        scratch_shapes=[pltpu.VMEM((tm, d), BF16)],
        compiler_params=_cparams(("parallel", "arbitrary")),
        name="proj",
    )(x, gain.reshape(1, d), w_in, qg, kg, lb, jnp.asarray(gm, BF16))


def _attn_kernel(lam_ref, cfar_ref, q_ref, k_ref, v_ref, bias_ref, sg_ref, o_ref,
                 qs_ref, m_ref, l_ref, acc_ref, *, t):
    h = pl.program_id(0)
    i = pl.program_id(2)
    lam = lam_ref[0]
    cfar = cfar_ref[h]

    q = q_ref[0, 0]
    lane = lax.broadcasted_iota(jnp.int32, (t, HEAD_W), 1)
    zero = jnp.zeros_like(q)
    qs_ref[0:t, :] = jnp.where(lane < QK_DIM, q, zero)
    qs_ref[t:2 * t, :] = jnp.where(lane >= QK_DIM, q, zero)
    m_ref[...] = jnp.full(m_ref.shape, -jnp.inf, F32)
    l_ref[...] = jnp.zeros(l_ref.shape, F32)
    acc_ref[...] = jnp.zeros(acc_ref.shape, F32)

    def step(j, prep, shift):
        rows = pl.ds(pl.multiple_of(j * t, t), t)
        k = k_ref[0, 0, rows, :]
        v = v_ref[0, 0, rows, :]
        s = prep(_dot_nt(qs_ref[...], k))
        m_old = m_ref[...]
        m_new = jnp.maximum(m_old, jnp.max(s, axis=1, keepdims=True) + shift)
        p = jnp.exp(s - (m_new - shift))
        alpha = jnp.exp(m_old - m_new)
        l_ref[...] = alpha * l_ref[...] + jnp.sum(p, axis=1, keepdims=True)
        acc_ref[...] = alpha * acc_ref[...] + _dot(p.astype(BF16), v)
        m_ref[...] = m_new

    def far(j, carry):
        step(j, lambda s: s, cfar)
        return carry

    lax.fori_loop(0, i - 1, far, 0)

    def tile2(x):
        return jnp.concatenate([x, x], axis=0)

    @pl.when(i >= 1)
    def _():
        step(i - 1, lambda s: s + tile2(bias_ref[0, 1]), 0.0)

    r = lax.broadcasted_iota(jnp.int32, (t, t), 0)
    c = lax.broadcasted_iota(jnp.int32, (t, t), 1)
    visible = tile2(r >= c)
    step(i, lambda s: jnp.where(visible, s + tile2(bias_ref[0, 0]), -jnp.inf), 0.0)

    acc = acc_ref[...]
    l = l_ref[...]
    o = acc[0:t] / l[0:t] - lam * (acc[t:2 * t] / l[t:2 * t])
    o_ref[...] = (_rms(o, sg_ref[...]) * (1.0 - LAMBDA_INIT)).astype(BF16)


def _t5_bucket(dist):
    n = jnp.maximum(dist, 0)
    max_exact = REL_BUCKETS // 2
    nf = jnp.maximum(n, 1).astype(F32)
    large = max_exact + (jnp.log(nf / max_exact) / math.log(REL_MAX_DIST / max_exact)
                         * (REL_BUCKETS - max_exact)).astype(jnp.int32)
    large = jnp.minimum(large, REL_BUCKETS - 1)
    return jnp.where(n < max_exact, n, large)


def _attn(qkv, lam, rel_bias, subln, *, batch, seq, t):
    assert t >= REL_MAX_DIST
    nq = seq // t
    r = jnp.arange(t)[:, None]
    c = jnp.arange(t)[None, :]
    dist = jnp.stack([r - c, t + r - c])
    bias = jnp.transpose(rel_bias.astype(F32)[_t5_bucket(dist)], (3, 0, 1, 2))
    cfar = rel_bias.astype(F32)[REL_BUCKETS - 1]

    return pl.pallas_call(
        functools.partial(_attn_kernel, t=t),
        out_shape=jax.ShapeDtypeStruct((batch * seq, MIX_W), BF16),
        grid=(HEADS, batch, nq),
        in_specs=[
            pl.BlockSpec(memory_space=pltpu.SMEM),
            pl.BlockSpec(memory_space=pltpu.SMEM),
            pl.BlockSpec((1, 1, t, HEAD_W), lambda h, b, i: (b, h, i, 0)),
            pl.BlockSpec((1, 1, seq, HEAD_W), lambda h, b, i: (b, HEADS + h, 0, 0)),
            pl.BlockSpec((1, 1, seq, HEAD_W), lambda h, b, i: (b, 2 * HEADS + h, 0, 0)),
            pl.BlockSpec((1, 2, t, t), lambda h, b, i: (h, 0, 0, 0)),
            pl.BlockSpec((1, HEAD_W), lambda h, b, i: (0, 0)),
        ],
        out_specs=pl.BlockSpec((t, HEAD_W), lambda h, b, i: (b * nq + i, h)),
        scratch_shapes=[
            pltpu.VMEM((2 * t, HEAD_W), BF16),
            pltpu.VMEM((2 * t, 1), F32),
            pltpu.VMEM((2 * t, 1), F32),
            pltpu.VMEM((2 * t, HEAD_W), F32),
        ],
        compiler_params=_cparams(("parallel", "parallel", "arbitrary")),
        name="attn",
    )(lam.reshape(1), cfar, qkv, qkv, qkv, bias, subln.reshape(1, HEAD_W))


def _hgrn_tables(c):
    levels = int(math.log2(c))
    assert 2 ** levels == c
    t = np.arange(c)[:, None]
    s = np.arange(c)[None, :]
    blocks = [(s <= t), (s > t)]
    level_id = np.where(t == s, 0, -1)
    for l in range(1, levels + 1):
        g, half = 2 ** l, 2 ** (l - 1)
        mid = (t // g) * g + half - 1
        upper = (t % g) >= half
        blocks.append(upper & (s > mid) & (s <= t))
        blocks.append((~upper) & (s > t) & (s <= mid))
        pair = (t // g == s // g) & upper & ((s % g) < half)
        level_id = np.where(pair, l, level_id)
    w = np.concatenate(blocks, axis=0).astype(np.float32)
    return jnp.asarray(w, BF16), jnp.asarray(level_id, jnp.int32), levels


def _hgrn_kernel(w_ref, lvl_ref, q_ref, k_ref, v_ref, og_ref, lf_ref, gn_ref, o_ref, *, c, levels, seq):
    lvl = lvl_ref[...]
    gain = gn_ref[...]

    def chunk(ci, state_t):
        rows = pl.ds(pl.multiple_of(ci * c, c), c)
        g = lf_ref[0, 0, rows, :]
        g_hi = g.astype(BF16)
        g_lo = (g - g_hi.astype(F32)).astype(BF16)
        e2 = _dot(w_ref[...], jnp.concatenate([g_hi, g_lo], axis=1))
        e = e2[:, 0:HEAD_W] + e2[:, HEAD_W:2 * HEAD_W]

        def blk(n):
            return e[n * c:(n + 1) * c]

        qb = q_ref[0, 0, rows, :]
        kb = k_ref[0, 0, rows, :]
        v = v_ref[0, 0, rows, :]
        q = qb.astype(F32)
        k = kb.astype(F32)
        b = blk(0)

        a = jnp.where(lvl == 0, _dot_nt(qb, kb), 0.0)
        for l in range(1, levels + 1):
            ql = (q * jnp.exp(blk(2 * l))).astype(BF16)
            kl = (k * jnp.exp(blk(2 * l + 1))).astype(BF16)
            a = jnp.where(lvl == l, _dot_nt(ql, kl), a)

        q_in = (q * jnp.exp(b)).astype(BF16)
        o = _dot(a.astype(BF16), v) + _dot_nt(q_in, state_t.astype(BF16))
        k_out = (k * jnp.exp(blk(1))).astype(BF16)
        decay = jnp.exp(b[c - 1:c, :])
        state_t = state_t * decay + _dot_tn(v, k_out)

        y = _rms(o, gain) * og_ref[0, 0, rows, :].astype(F32)
        o_ref[rows, :] = y.astype(BF16)
        return state_t

    lax.fori_loop(0, seq // c, chunk, jnp.zeros((HEAD_W, HEAD_W), F32))


def _hgrn(hin, logf, gain, *, batch, seq, c):
    w, lvl, levels = _hgrn_tables(c)

    def head(slab):
        return pl.BlockSpec((1, 1, seq, HEAD_W), lambda b, h: (b, slab * HEADS + h, 0, 0))

    return pl.pallas_call(
        functools.partial(_hgrn_kernel, c=c, levels=levels, seq=seq),
        out_shape=jax.ShapeDtypeStruct((batch * seq, MIX_W), BF16),
        grid=(batch, HEADS),
        in_specs=[
            pl.BlockSpec(w.shape, lambda b, h: (0, 0)),
            pl.BlockSpec((c, c), lambda b, h: (0, 0)),
            head(0), head(1), head(2), head(3), head(0),
            pl.BlockSpec((1, HEAD_W), lambda b, h: (0, 0)),
        ],
        out_specs=pl.BlockSpec((seq, HEAD_W), lambda b, h: (b, h)),
        compiler_params=_cparams(("parallel", "parallel")),
        name="hgrn",
    )(w, lvl, hin, hin, hin, hin, logf, gain.reshape(1, HEAD_W))


def _merge_kernel(x_ref, ya_ref, yb_ref, gt_ref, wa_ref, wb_ref, wo_ref, o_ref):
    d = x_ref.shape[1]
    gt = gt_ref[...].astype(F32)
    merged = gt[:, 0:d] * _dot(ya_ref[...], wa_ref[...]) + gt[:, d:2 * d] * _dot(yb_ref[...], wb_ref[...])
    o_ref[...] = x_ref[...] + _dot(merged.astype(BF16), wo_ref[...])


def _resident(shape):
    return pl.BlockSpec(shape, lambda i: (0,) * len(shape), pipeline_mode=pl.Buffered(1))


def _merge(x, ya, yb, gates, wa, wb, wo, *, tm):
    m, d = x.shape
    return pl.pallas_call(
        _merge_kernel,
        out_shape=jax.ShapeDtypeStruct((m, d), F32),
        grid=(m // tm,),
        in_specs=[
            pl.BlockSpec((tm, d), lambda i: (i, 0)),
            pl.BlockSpec((tm, MIX_W), lambda i: (i, 0)),
            pl.BlockSpec((tm, MIX_W), lambda i: (i, 0)),
            pl.BlockSpec((tm, 2 * d), lambda i: (i, 0)),
            _resident(wa.shape), _resident(wb.shape), _resident(wo.shape),
        ],
        out_specs=pl.BlockSpec((tm, d), lambda i: (i, 0)),
        compiler_params=_cparams(("parallel",)),
        name="merge",
    )(x, ya, yb, gates, wa, wb, wo)


def _ple_kernel(x_ref, p_ref, gg_ref, pg_ref, wg_ref, wp_ref, o_ref):
    x = x_ref[...]
    gate = _sigmoid(_dot(_rms(x, gg_ref[...]).astype(BF16), wg_ref[...]))
    ple = _rms(_dot(p_ref[...].astype(BF16), wp_ref[...]), pg_ref[...])
    o_ref[...] = x + gate * ple


def _ple(x, p, gate_gain, post_gain, w_gate, w_proj, *, tm):
    m, d = x.shape
    pd = p.shape[1]
    return pl.pallas_call(
        _ple_kernel,
        out_shape=jax.ShapeDtypeStruct((m, d), F32),
        grid=(m // tm,),
        in_specs=[
            pl.BlockSpec((tm, d), lambda i: (i, 0)),
            pl.BlockSpec((tm, pd), lambda i: (i, 0)),
            pl.BlockSpec((1, d), lambda i: (0, 0)),
            pl.BlockSpec((1, d), lambda i: (0, 0)),
            _resident(w_gate.shape), _resident(w_proj.shape),
        ],
        out_specs=pl.BlockSpec((tm, d), lambda i: (i, 0)),
        compiler_params=_cparams(("parallel",)),
        name="ple",
    )(x, p, gate_gain.reshape(1, d), post_gain.reshape(1, d), w_gate, w_proj)


def _tiles(seq, d, f):
    def pick(n, choices):
        return next(c for c in choices if n % c == 0)
    return dict(
        tm=pick(seq, (512, 256, 128)),
        tf=pick(f, (512, 256, 128)),
        tn=pick(math.gcd(MIX_W, 2 * d), (1024, 512, 256, 128)),
        ta=pick(seq, (512, 256, 128)),
        c=128,
    )


def kernel(x, p, ffn1_norm, ffn1_w_gate, ffn1_w_up, ffn1_w_down, mix_norm, w_in, q_norm, k_norm, lambda_q1, lambda_k1, lambda_q2, lambda_k2, diff_subln, rel_bias, hgrn_lb_logits, hgrn_norm, w_branch_a, w_branch_b, w_out, ffn2_norm, ffn2_w_gate, ffn2_w_up, ffn2_w_down, ple_gate_norm, w_ple_gate, w_ple_proj, ple_post_norm):
    batch, seq, d = x.shape
    depth = ffn1_norm.shape[0]
    assert depth == 1
    m = batch * seq
    tl = _tiles(seq, d, ffn1_w_gate.shape[2])
    bf = lambda w: w.astype(BF16)

    lower_bounds = jnp.cumsum(jax.nn.softmax(hgrn_lb_logits.astype(F32), axis=0), axis=0)
    lb = lower_bounds[0].reshape(HEADS, HEAD_W)
    lam = (jnp.exp(jnp.sum(lambda_q1[0].astype(F32) * lambda_k1[0].astype(F32)))
           - jnp.exp(jnp.sum(lambda_q2[0].astype(F32) * lambda_k2[0].astype(F32)))
           + LAMBDA_INIT)
    qg = jnp.tile(q_norm[0].astype(F32), 2).reshape(1, HEAD_W) * (QK_DIM ** -0.5)
    kg = jnp.tile(k_norm[0].astype(F32), 2).reshape(1, HEAD_W)

    x2 = x.reshape(m, d)
    x2 = _ffn(x2, ffn1_norm[0], bf(ffn1_w_gate[0]), bf(ffn1_w_up[0]), bf(ffn1_w_down[0]),
              tm=tl["tm"], tf=tl["tf"])

    qkv, hin, logf, gates = _proj(x2, mix_norm[0], bf(w_in[0]), qg, kg, lb,
                                  batch=batch, seq=seq, tm=tl["tm"], tn=tl["tn"])
    ya = _attn(qkv, lam, rel_bias, diff_subln[0], batch=batch, seq=seq, t=tl["ta"])
    yb = _hgrn(hin, logf, hgrn_norm[0], batch=batch, seq=seq, c=tl["c"])
    x2 = _merge(x2, ya, yb, gates, bf(w_branch_a[0]), bf(w_branch_b[0]), bf(w_out[0]), tm=tl["tm"])

    x2 = _ffn(x2, ffn2_norm[0], bf(ffn2_w_gate[0]), bf(ffn2_w_up[0]), bf(ffn2_w_down[0]),
              tm=tl["tm"], tf=tl["tf"])
    x2 = _ple(x2, p[0].reshape(m, -1), ple_gate_norm[0], ple_post_norm[0],
              bf(w_ple_gate[0]), bf(w_ple_proj[0]), tm=tl["tm"])
    return x2.reshape(batch, seq, d)
```

```python
import functools
import math

import jax
import jax.numpy as jnp
import numpy as np
from jax import lax
from jax.experimental import pallas as pl
from jax.experimental.pallas import tpu as pltpu

F32 = jnp.float32
BF16 = jnp.bfloat16

EPS = 1e-6
HEADS = 8
HEAD_W = 128
QK_DIM = 64
MIX_W = HEADS * HEAD_W
REL_BUCKETS = 32
REL_MAX_DIST = 128
LAMBDA_INIT = 0.8 - 0.6 * math.exp(-0.3 * 0)

V7X_VMEM_BYTES = 64 * 1024 * 1024
VMEM_LIMIT = 56 * 1024 * 1024


def _cparams(sem):
    return pltpu.CompilerParams(dimension_semantics=sem, vmem_limit_bytes=VMEM_LIMIT)


def _sigmoid(x):
    return 1.0 / (1.0 + jnp.exp(-x))


def _rms(x, gain):
    ms = jnp.mean(x * x, axis=-1, keepdims=True)
    return x * lax.rsqrt(ms + EPS) * gain


def _dot(a, b):
    return jnp.dot(a, b, preferred_element_type=F32)


def _dot_nt(a, b):
    return lax.dot_general(a, b, (((1,), (1,)), ((), ())), preferred_element_type=F32)


def _dot_tn(a, b):
    return lax.dot_general(a, b, (((0,), (0,)), ((), ())), preferred_element_type=F32)


def _ffn_kernel(x_ref, g_ref, wg_ref, wu_ref, wd_ref, o_ref, h_ref):
    j = pl.program_id(1)

    @pl.when(j == 0)
    def _():
        x = x_ref[...]
        h_ref[...] = _rms(x, g_ref[...]).astype(BF16)
        o_ref[...] = x

    h = h_ref[...]
    gate = _dot(h, wg_ref[...])
    up = _dot(h, wu_ref[...])
    act = (gate * _sigmoid(gate) * (0.5 * up)).astype(BF16)
    o_ref[...] += _dot(act, wd_ref[...])


def _ffn(x, gain, w_gate, w_up, w_down, *, tm, tf):
    m, d = x.shape
    f = w_gate.shape[1]
    return pl.pallas_call(
        _ffn_kernel,
        out_shape=jax.ShapeDtypeStruct((m, d), F32),
        grid=(m // tm, f // tf),
        in_specs=[
            pl.BlockSpec((tm, d), lambda i, j: (i, 0)),
            pl.BlockSpec((1, d), lambda i, j: (0, 0)),
            pl.BlockSpec((d, tf), lambda i, j: (0, j)),
            pl.BlockSpec((d, tf), lambda i, j: (0, j)),
            pl.BlockSpec((tf, d), lambda i, j: (j, 0)),
        ],
        out_specs=pl.BlockSpec((tm, d), lambda i, j: (i, 0)),
        scratch_shapes=[pltpu.VMEM((tm, d), BF16)],
        compiler_params=_cparams(("parallel", "arbitrary")),
        name="ffn",
    )(x, gain.reshape(1, d), w_gate, w_up, w_down)


def _proj_kernel(x_ref, g_ref, w_ref, qg_ref, kg_ref, lb_ref, gm_ref,
                 oqkv_ref, oh_ref, olf_ref, og_ref, h_ref, acc_ref, *, tn, nj):
    j = pl.program_id(1)
    jp = j - 1
    tps = MIX_W // tn
    hpt = tn // HEAD_W

    def heads(vals):
        return [vals[:, s * HEAD_W:(s + 1) * HEAD_W] for s in range(hpt)]

    def epi_qk(acc):
        gain = jnp.where(jp < tps, qg_ref[...], kg_ref[...])
        for s, a in enumerate(heads(acc)):
            ms = _dot((a * a).astype(BF16), gm_ref[...])
            oqkv_ref[0, s] = (a * lax.rsqrt(ms + EPS) * gain).astype(BF16)

    def epi_v(acc):
        for s, a in enumerate(heads(acc)):
            oqkv_ref[0, s] = a.astype(BF16)

    def epi_silu(acc):
        for s, a in enumerate(heads(acc * _sigmoid(acc))):
            oh_ref[0, s] = a.astype(BF16)

    def epi_forget(acc):
        jj = jp - 4 * tps
        for s, a in enumerate(heads(acc)):
            lb = lb_ref[pl.ds(jj * hpt + s, 1), :]
            forget = lb + (1.0 - lb) * _sigmoid(a)
            oh_ref[0, s] = (1.0 - forget).astype(BF16)
            olf_ref[0, s] = jnp.log(forget)

    def epi_hv(acc):
        for s, a in enumerate(heads(acc)):
            oh_ref[0, s] = a.astype(BF16)

    def epi_gates(acc):
        og_ref[...] = _sigmoid(acc).astype(BF16)

    @pl.when(j == 0)
    def _():
        h_ref[...] = _rms(x_ref[...], g_ref[...]).astype(BF16)
        acc_ref[...] = _dot(h_ref[...], w_ref[...])

    def fused(cond, epi):
        @pl.when(cond & (j >= 1) & (j < nj))
        def _():
            new = _dot(h_ref[...], w_ref[...])
            epi(acc_ref[...])
            acc_ref[...] = new

    fused(jp < 2 * tps, epi_qk)
    fused((jp >= 2 * tps) & (jp < 3 * tps), epi_v)
    fused(((jp >= 3 * tps) & (jp < 4 * tps)) | ((jp >= 6 * tps) & (jp < 7 * tps)), epi_silu)
    fused((jp >= 4 * tps) & (jp < 5 * tps), epi_forget)
    fused((jp >= 5 * tps) & (jp < 6 * tps), epi_hv)
    fused(jp >= 7 * tps, epi_gates)

    @pl.when(j == nj)
    def _():
        epi_gates(acc_ref[...])


def _proj(x, gain, w_in, qg, kg, lb, *, batch, seq, tm, tn):
    m, d = x.shape
    n_in = w_in.shape[1]
    tps = MIX_W // tn
    hpt = tn // HEAD_W
    spb = seq // tm
    ng = 2 * d // tn
    nj = 7 * tps + ng
    assert n_in == 7 * MIX_W + 2 * d and MIX_W % tn == 0 and (2 * d) % tn == 0

    lane = np.arange(HEAD_W)
    gm = (lane[:, None] // QK_DIM == lane[None, :] // QK_DIM).astype(np.float32) / QK_DIM

    def hm_block(lo, n):
        return lambda i, j: (i // spb, jnp.clip(j - 1 - lo, 0, n - 1), i % spb, 0)

    return pl.pallas_call(
        functools.partial(_proj_kernel, tn=tn, nj=nj),
        out_shape=(
            jax.ShapeDtypeStruct((batch, 3 * HEADS, seq, HEAD_W), BF16),
            jax.ShapeDtypeStruct((batch, 4 * HEADS, seq, HEAD_W), BF16),
            jax.ShapeDtypeStruct((batch, HEADS, seq, HEAD_W), F32),
            jax.ShapeDtypeStruct((m, 2 * d), BF16),
        ),
        grid=(m // tm, nj + 1),
        in_specs=[
            pl.BlockSpec((tm, d), lambda i, j: (i, 0)),
            pl.BlockSpec((1, d), lambda i, j: (0, 0)),
            pl.BlockSpec((d, tn), lambda i, j: (0, jnp.minimum(j, nj - 1))),
            pl.BlockSpec((1, HEAD_W), lambda i, j: (0, 0)),
            pl.BlockSpec((1, HEAD_W), lambda i, j: (0, 0)),
            pl.BlockSpec((HEADS, HEAD_W), lambda i, j: (0, 0)),
            pl.BlockSpec((HEAD_W, HEAD_W), lambda i, j: (0, 0)),
        ],
        out_specs=(
            pl.BlockSpec((1, hpt, tm, HEAD_W), hm_block(0, 3 * tps)),
            pl.BlockSpec((1, hpt, tm, HEAD_W), hm_block(3 * tps, 4 * tps)),
            pl.BlockSpec((1, hpt, tm, HEAD_W), hm_block(4 * tps, tps)),
            pl.BlockSpec((tm, tn), lambda i, j: (i, jnp.clip(j - 1 - 7 * tps, 0, ng - 1))),
        ),
        scratch_shapes=[pltpu.VMEM((tm, d), BF16), pltpu.VMEM((tm, tn), F32)],
        compiler_params=_cparams(("parallel", "arbitrary")),
        name="proj",
    )(x, gain.reshape(1, d), w_in, qg, kg, lb, jnp.asarray(gm, BF16))


def _attn_kernel(lam_ref, cfar_ref, q_ref, k_ref, v_ref, bias_ref, sg_ref, o_ref,
                 qs_ref, vt_ref, m_ref, l_ref, acc_ref, *, t, nkb):
    h = pl.program_id(0)
    i = pl.program_id(2)
    lam = lam_ref[0]
    cfar = cfar_ref[h]

    @pl.when(i == 0)
    def _():
        for jb in range(nkb):
            vt_ref[jb] = v_ref[0, 0, jb * t:(jb + 1) * t, :].astype(F32).T.astype(BF16)

    q = q_ref[0, 0]
    lane = lax.broadcasted_iota(jnp.int32, (t, HEAD_W), 1)
    zero = jnp.zeros_like(q)
    qs_ref[0:t, :] = jnp.where(lane < QK_DIM, q, zero)
    qs_ref[t:2 * t, :] = jnp.where(lane >= QK_DIM, q, zero)
    m_ref[...] = jnp.full(m_ref.shape, -jnp.inf, F32)
    l_ref[...] = jnp.zeros(l_ref.shape, F32)
    acc_ref[...] = jnp.zeros(acc_ref.shape, F32)

    def step(j, prep, shift):
        k = k_ref[0, 0, pl.ds(pl.multiple_of(j * t, t), t), :]
        s = prep(_dot_nt(k, qs_ref[...]))
        m_old = m_ref[...]
        m_new = jnp.maximum(m_old, jnp.max(s, axis=0, keepdims=True) + shift)
        p = jnp.exp(s - (m_new - shift))
        alpha = jnp.exp(m_old - m_new)
        l_ref[...] = alpha * l_ref[...] + jnp.sum(p, axis=0, keepdims=True)
        acc_ref[...] = alpha * acc_ref[...] + _dot(vt_ref[j], p.astype(BF16))
        m_ref[...] = m_new

    def far(j, carry):
        step(j, lambda s: s, cfar)
        return carry

    lax.fori_loop(0, i - 1, far, 0)

    def both(x):
        return jnp.concatenate([x, x], axis=1)

    @pl.when(i >= 1)
    def _():
        step(i - 1, lambda s: s + both(bias_ref[0, 1]), 0.0)

    kpos = lax.broadcasted_iota(jnp.int32, (t, t), 0)
    qpos = lax.broadcasted_iota(jnp.int32, (t, t), 1)
    visible = both(qpos >= kpos)
    step(i, lambda s: jnp.where(visible, s + both(bias_ref[0, 0]), -jnp.inf), 0.0)

    acc = acc_ref[...]
    l = l_ref[...]
    o_t = acc[:, 0:t] / l[:, 0:t] - lam * (acc[:, t:2 * t] / l[:, t:2 * t])
    o_ref[...] = (_rms(o_t.T, sg_ref[...]) * (1.0 - LAMBDA_INIT)).astype(BF16)


def _t5_bucket(dist):
    n = jnp.maximum(dist, 0)
    max_exact = REL_BUCKETS // 2
    nf = jnp.maximum(n, 1).astype(F32)
    large = max_exact + (jnp.log(nf / max_exact) / math.log(REL_MAX_DIST / max_exact)
                         * (REL_BUCKETS - max_exact)).astype(jnp.int32)
    large = jnp.minimum(large, REL_BUCKETS - 1)
    return jnp.where(n < max_exact, n, large)


def _bias_tiles(rel_bias, t):
    f = rel_bias.astype(F32)[_t5_bucket(jnp.arange(2 * t))].T
    zero = jnp.zeros((HEADS, 1), F32)
    w0 = jnp.concatenate([f[:, 0:t], jnp.zeros((HEADS, t), F32)], axis=1)
    w1 = jnp.concatenate([f[:, t:2 * t], zero, f[:, 1:t]], axis=1)
    w = jnp.stack([w0, w1], axis=1)
    x = jnp.tile(w, (1, 1, t))[:, :, :t * (2 * t - 1)].reshape(HEADS, 2, t, 2 * t - 1)
    return x[:, :, :, 0:t]


def _attn(qkv, lam, rel_bias, subln, *, batch, seq, t):
    assert t >= REL_MAX_DIST
    nq = seq // t
    bias = _bias_tiles(rel_bias, t)
    cfar = rel_bias.astype(F32)[REL_BUCKETS - 1]

    return pl.pallas_call(
        functools.partial(_attn_kernel, t=t, nkb=nq),
        out_shape=jax.ShapeDtypeStruct((batch * seq, MIX_W), BF16),
        grid=(HEADS, batch, nq),
        in_specs=[
            pl.BlockSpec(memory_space=pltpu.SMEM),
            pl.BlockSpec(memory_space=pltpu.SMEM),
            pl.BlockSpec((1, 1, t, HEAD_W), lambda h, b, i: (b, h, i, 0)),
            pl.BlockSpec((1, 1, seq, HEAD_W), lambda h, b, i: (b, HEADS + h, 0, 0)),
            pl.BlockSpec((1, 1, seq, HEAD_W), lambda h, b, i: (b, 2 * HEADS + h, 0, 0)),
            pl.BlockSpec((1, 2, t, t), lambda h, b, i: (h, 0, 0, 0)),
            pl.BlockSpec((1, HEAD_W), lambda h, b, i: (0, 0)),
        ],
        out_specs=pl.BlockSpec((t, HEAD_W), lambda h, b, i: (b * nq + i, h)),
        scratch_shapes=[
            pltpu.VMEM((2 * t, HEAD_W), BF16),
            pltpu.VMEM((nq, HEAD_W, t), BF16),
            pltpu.VMEM((1, 2 * t), F32),
            pltpu.VMEM((1, 2 * t), F32),
            pltpu.VMEM((HEAD_W, 2 * t), F32),
        ],
        compiler_params=_cparams(("parallel", "parallel", "arbitrary")),
        name="attn",
    )(lam.reshape(1), cfar, qkv, qkv, qkv, bias, subln.reshape(1, HEAD_W))


def _hgrn_tables(c):
    levels = int(math.log2(c))
    assert 2 ** levels == c
    t = np.arange(c)[:, None]
    s = np.arange(c)[None, :]
    blocks = [(s <= t).astype(np.float32), (s > t).astype(np.float32)]
    level_id = np.where(t == s, 0, -1)
    for l in range(1, levels + 1):
        g, half = 2 ** l, 2 ** (l - 1)
        mid = (t // g) * g + half - 1
        upper = (t % g) >= half
        plus = upper & (s > mid) & (s <= t)
        minus = (~upper) & (s > t) & (s <= mid)
        blocks.append(plus.astype(np.float32) - minus.astype(np.float32))
        pair = (t // g == s // g) & upper & ((s % g) < half)
        level_id = np.where(pair, l, level_id)
    w = np.concatenate(blocks, axis=0)
    return jnp.asarray(w, BF16), jnp.asarray(level_id, jnp.int32), levels


def _hgrn_kernel(w_ref, lvl_ref, q_ref, k_ref, v_ref, og_ref, lf_ref, gn_ref, o_ref, *,
                 c, levels, seq, unroll):
    gain = gn_ref[...]

    def intra(ci):
        rows = pl.ds(pl.multiple_of(ci * c, c), c)
        g = lf_ref[0, 0, rows, :]
        g_hi = g.astype(BF16)
        g_lo = (g - g_hi.astype(F32)).astype(BF16)
        e2 = _dot(w_ref[...], jnp.concatenate([g_hi, g_lo], axis=1))
        e = e2[:, 0:HEAD_W] + e2[:, HEAD_W:2 * HEAD_W]

        def blk(n):
            return e[n * c:(n + 1) * c]

        qb = q_ref[0, 0, rows, :]
        kb = k_ref[0, 0, rows, :]
        v = v_ref[0, 0, rows, :]
        q = qb.astype(F32)
        k = kb.astype(F32)
        b = blk(0)
        lvl = lvl_ref[...]

        a = jnp.where(lvl == 0, _dot_nt(qb, kb), 0.0)
        for l in range(1, levels + 1):
            x = jnp.exp(-jnp.abs(blk(1 + l)))
            a = jnp.where(lvl == l, _dot_nt((q * x).astype(BF16), (k * x).astype(BF16)), a)

        o_intra = _dot(a.astype(BF16), v)
        q_in = (q * jnp.exp(b)).astype(BF16)
        k_out = (k * jnp.exp(blk(1))).astype(BF16)
        ds_t = _dot_tn(v, k_out)
        decay = jnp.exp(b[c - 1:c, :])
        return rows, o_intra, q_in, ds_t, decay

    def body(it, state_t):
        parts = [intra(it * unroll + u) for u in range(unroll)]
        for rows, o_intra, q_in, ds_t, decay in parts:
            o = o_intra + _dot_nt(q_in, state_t.astype(BF16))
            state_t = state_t * decay + ds_t
            y = _rms(o, gain) * og_ref[0, 0, rows, :].astype(F32)
            o_ref[rows, :] = y.astype(BF16)
        return state_t

    lax.fori_loop(0, seq // (c * unroll), body, jnp.zeros((HEAD_W, HEAD_W), F32))


def _hgrn(hin, logf, gain, *, batch, seq, c, unroll):
    w, lvl, levels = _hgrn_tables(c)
    assert seq % (c * unroll) == 0

    def head(slab):
        return pl.BlockSpec((1, 1, seq, HEAD_W), lambda b, h: (b, slab * HEADS + h, 0, 0))

    return pl.pallas_call(
        functools.partial(_hgrn_kernel, c=c, levels=levels, seq=seq, unroll=unroll),
        out_shape=jax.ShapeDtypeStruct((batch * seq, MIX_W), BF16),
        grid=(batch, HEADS),
        in_specs=[
            pl.BlockSpec(w.shape, lambda b, h: (0, 0)),
            pl.BlockSpec((c, c), lambda b, h: (0, 0)),
            head(0), head(1), head(2), head(3), head(0),
            pl.BlockSpec((1, HEAD_W), lambda b, h: (0, 0)),
        ],
        out_specs=pl.BlockSpec((seq, HEAD_W), lambda b, h: (b, h)),
        compiler_params=_cparams(("parallel", "parallel")),
        name="hgrn",
    )(w, lvl, hin, hin, hin, hin, logf, gain.reshape(1, HEAD_W))


def _merge_kernel(x_ref, ya_ref, yb_ref, gt_ref, wa_ref, wb_ref, wo_ref, o_ref):
    d = x_ref.shape[1]
    gt = gt_ref[...].astype(F32)
    merged = gt[:, 0:d] * _dot(ya_ref[...], wa_ref[...]) + gt[:, d:2 * d] * _dot(yb_ref[...], wb_ref[...])
    o_ref[...] = x_ref[...] + _dot(merged.astype(BF16), wo_ref[...])


def _resident(shape):
    return pl.BlockSpec(shape, lambda i: (0,) * len(shape), pipeline_mode=pl.Buffered(1))


def _merge(x, ya, yb, gates, wa, wb, wo, *, tm):
    m, d = x.shape
    return pl.pallas_call(
        _merge_kernel,
        out_shape=jax.ShapeDtypeStruct((m, d), F32),
        grid=(m // tm,),
        in_specs=[
            pl.BlockSpec((tm, d), lambda i: (i, 0)),
            pl.BlockSpec((tm, MIX_W), lambda i: (i, 0)),
            pl.BlockSpec((tm, MIX_W), lambda i: (i, 0)),
            pl.BlockSpec((tm, 2 * d), lambda i: (i, 0)),
            _resident(wa.shape), _resident(wb.shape), _resident(wo.shape),
        ],
        out_specs=pl.BlockSpec((tm, d), lambda i: (i, 0)),
        compiler_params=_cparams(("parallel",)),
        name="merge",
    )(x, ya, yb, gates, wa, wb, wo)


def _ple_kernel(x_ref, p_ref, gg_ref, pg_ref, wg_ref, wp_ref, o_ref):
    x = x_ref[...]
    gate = _sigmoid(_dot(_rms(x, gg_ref[...]).astype(BF16), wg_ref[...]))
    ple = _rms(_dot(p_ref[...].astype(BF16), wp_ref[...]), pg_ref[...])
    o_ref[...] = x + gate * ple


def _ple(x, p, gate_gain, post_gain, w_gate, w_proj, *, tm):
    m, d = x.shape
    pd = p.shape[1]
    return pl.pallas_call(
        _ple_kernel,
        out_shape=jax.ShapeDtypeStruct((m, d), F32),
        grid=(m // tm,),
        in_specs=[
            pl.BlockSpec((tm, d), lambda i: (i, 0)),
            pl.BlockSpec((tm, pd), lambda i: (i, 0)),
            pl.BlockSpec((1, d), lambda i: (0, 0)),
            pl.BlockSpec((1, d), lambda i: (0, 0)),
            _resident(w_gate.shape), _resident(w_proj.shape),
        ],
        out_specs=pl.BlockSpec((tm, d), lambda i: (i, 0)),
        compiler_params=_cparams(("parallel",)),
        name="ple",
    )(x, p, gate_gain.reshape(1, d), post_gain.reshape(1, d), w_gate, w_proj)


def _tiles(seq, d, f):
    def pick(n, choices):
        return next(c for c in choices if n % c == 0)
    return dict(
        tm=pick(seq, (512, 256, 128)),
        tf=pick(f, (512, 256, 128)),
        tn=pick(math.gcd(MIX_W, 2 * d), (1024, 512, 256, 128)),
        ta=pick(seq, (512, 256, 128)),
        c=128,
        cu=4,
    )


def kernel(x, p, ffn1_norm, ffn1_w_gate, ffn1_w_up, ffn1_w_down, mix_norm, w_in, q_norm, k_norm, lambda_q1, lambda_k1, lambda_q2, lambda_k2, diff_subln, rel_bias, hgrn_lb_logits, hgrn_norm, w_branch_a, w_branch_b, w_out, ffn2_norm, ffn2_w_gate, ffn2_w_up, ffn2_w_down, ple_gate_norm, w_ple_gate, w_ple_proj, ple_post_norm):
    batch, seq, d = x.shape
    depth = ffn1_norm.shape[0]
    assert depth == 1
    m = batch * seq
    tl = _tiles(seq, d, ffn1_w_gate.shape[2])
    bf = lambda w: w.astype(BF16)

    lower_bounds = jnp.cumsum(jax.nn.softmax(hgrn_lb_logits.astype(F32), axis=0), axis=0)
    lb = lower_bounds[0].reshape(HEADS, HEAD_W)
    lam = (jnp.exp(jnp.sum(lambda_q1[0].astype(F32) * lambda_k1[0].astype(F32)))
           - jnp.exp(jnp.sum(lambda_q2[0].astype(F32) * lambda_k2[0].astype(F32)))
           + LAMBDA_INIT)
    qg = jnp.tile(q_norm[0].astype(F32), 2).reshape(1, HEAD_W) * (QK_DIM ** -0.5)
    kg = jnp.tile(k_norm[0].astype(F32), 2).reshape(1, HEAD_W)

    x2 = x.reshape(m, d)
    x2 = _ffn(x2, ffn1_norm[0], bf(ffn1_w_gate[0]), bf(ffn1_w_up[0]), bf(ffn1_w_down[0]),
              tm=tl["tm"], tf=tl["tf"])

    qkv, hin, logf, gates = _proj(x2, mix_norm[0], bf(w_in[0]), qg, kg, lb,
                                  batch=batch, seq=seq, tm=tl["tm"], tn=tl["tn"])
    ya = _attn(qkv, lam, rel_bias, diff_subln[0], batch=batch, seq=seq, t=tl["ta"])
    yb = _hgrn(hin, logf, hgrn_norm[0], batch=batch, seq=seq, c=tl["c"], unroll=tl["cu"])
    x2 = _merge(x2, ya, yb, gates, bf(w_branch_a[0]), bf(w_branch_b[0]), bf(w_out[0]), tm=tl["tm"])

    x2 = _ffn(x2, ffn2_norm[0], bf(ffn2_w_gate[0]), bf(ffn2_w_up[0]), bf(ffn2_w_down[0]),
              tm=tl["tm"], tf=tl["tf"])
    x2 = _ple(x2, p[0].reshape(m, -1), ple_gate_norm[0], ple_post_norm[0],
              bf(w_ple_gate[0]), bf(w_ple_proj[0]), tm=tl["tm"])
    return x2.reshape(batch, seq, d)
```

```python
import functools
import math

import jax
import jax.numpy as jnp
import numpy as np
from jax import lax
from jax.experimental import pallas as pl
from jax.experimental.pallas import tpu as pltpu

F32 = jnp.float32
BF16 = jnp.bfloat16

EPS = 1e-6
HEADS = 8
HEAD_W = 128
QK_DIM = 64
MIX_W = HEADS * HEAD_W
REL_BUCKETS = 32
REL_MAX_DIST = 128
LAMBDA_INIT = 0.8 - 0.6 * math.exp(-0.3 * 0)
LOG2E = math.log2(math.e)
MAX_STATIC_SOFTMAX_SPREAD = 60.0

V7X_VMEM_BYTES = 64 * 1024 * 1024
VMEM_LIMIT = 56 * 1024 * 1024


def _cparams(sem):
    return pltpu.CompilerParams(dimension_semantics=sem, vmem_limit_bytes=VMEM_LIMIT)


def _sigmoid(x):
    return 1.0 / (1.0 + jnp.exp(-x))


def _rms(x, gain):
    ms = jnp.mean(x * x, axis=-1, keepdims=True)
    return x * lax.rsqrt(ms + EPS) * gain


def _dot(a, b):
    return jnp.dot(a, b, preferred_element_type=F32)


def _dot_nt(a, b):
    return lax.dot_general(a, b, (((1,), (1,)), ((), ())), preferred_element_type=F32)


def _dot_tn(a, b):
    return lax.dot_general(a, b, (((0,), (0,)), ((), ())), preferred_element_type=F32)


def _ffn_kernel(x_ref, g_ref, wg_ref, wu_ref, wd_ref, o_ref, h_ref):
    j = pl.program_id(1)

    @pl.when(j == 0)
    def _():
        x = x_ref[...]
        h_ref[...] = _rms(x, g_ref[...]).astype(BF16)
        o_ref[...] = x

    h = h_ref[...]
    gate = _dot(h, wg_ref[...])
    up = _dot(h, wu_ref[...])
    act = (gate * _sigmoid(gate) * (0.5 * up)).astype(BF16)
    o_ref[...] += _dot(act, wd_ref[...])


def _ffn(x, gain, w_gate, w_up, w_down, *, tm, tf):
    m, d = x.shape
    f = w_gate.shape[1]
    return pl.pallas_call(
        _ffn_kernel,
        out_shape=jax.ShapeDtypeStruct((m, d), F32),
        grid=(m // tm, f // tf),
        in_specs=[
            pl.BlockSpec((tm, d), lambda i, j: (i, 0)),
            pl.BlockSpec((1, d), lambda i, j: (0, 0)),
            pl.BlockSpec((d, tf), lambda i, j: (0, j)),
            pl.BlockSpec((d, tf), lambda i, j: (0, j)),
            pl.BlockSpec((tf, d), lambda i, j: (j, 0)),
        ],
        out_specs=pl.BlockSpec((tm, d), lambda i, j: (i, 0)),
        scratch_shapes=[pltpu.VMEM((tm, d), BF16)],
        compiler_params=_cparams(("parallel", "arbitrary")),
        name="ffn",
    )(x, gain.reshape(1, d), w_gate, w_up, w_down)


def _proj_kernel(x_ref, g_ref, w_ref, qg_ref, kg_ref, lb_ref, gm_ref,
                 oqkv_ref, oh_ref, olf_ref, og_ref, h_ref, acc_ref, *, tn, nj):
    j = pl.program_id(1)
    jp = j - 1
    tps = MIX_W // tn
    hpt = tn // HEAD_W

    def heads(vals):
        return [vals[:, s * HEAD_W:(s + 1) * HEAD_W] for s in range(hpt)]

    def epi_qk(acc):
        gain = jnp.where(jp < tps, qg_ref[...], kg_ref[...])
        for s, a in enumerate(heads(acc)):
            ms = _dot((a * a).astype(BF16), gm_ref[...])
            oqkv_ref[0, s] = (a * lax.rsqrt(ms + EPS) * gain).astype(BF16)

    def epi_v(acc):
        for s, a in enumerate(heads(acc)):
            oqkv_ref[0, s] = a.astype(BF16)

    def epi_silu(acc):
        for s, a in enumerate(heads(acc * _sigmoid(acc))):
            oh_ref[0, s] = a.astype(BF16)

    def epi_forget(acc):
        jj = jp - 4 * tps
        for s, a in enumerate(heads(acc)):
            lb = lb_ref[pl.ds(jj * hpt + s, 1), :]
            forget = lb + (1.0 - lb) * _sigmoid(a)
            oh_ref[0, s] = (1.0 - forget).astype(BF16)
            olf_ref[0, s] = jnp.log(forget)

    def epi_hv(acc):
        for s, a in enumerate(heads(acc)):
            oh_ref[0, s] = a.astype(BF16)

    def epi_gates(acc):
        og_ref[...] = _sigmoid(acc).astype(BF16)

    @pl.when(j == 0)
    def _():
        h_ref[...] = _rms(x_ref[...], g_ref[...]).astype(BF16)
        acc_ref[...] = _dot(h_ref[...], w_ref[...])

    def fused(cond, epi):
        @pl.when(cond & (j >= 1) & (j < nj))
        def _():
            new = _dot(h_ref[...], w_ref[...])
            epi(acc_ref[...])
            acc_ref[...] = new

    fused(jp < 2 * tps, epi_qk)
    fused((jp >= 2 * tps) & (jp < 3 * tps), epi_v)
    fused(((jp >= 3 * tps) & (jp < 4 * tps)) | ((jp >= 6 * tps) & (jp < 7 * tps)), epi_silu)
    fused((jp >= 4 * tps) & (jp < 5 * tps), epi_forget)
    fused((jp >= 5 * tps) & (jp < 6 * tps), epi_hv)
    fused(jp >= 7 * tps, epi_gates)

    @pl.when(j == nj)
    def _():
        epi_gates(acc_ref[...])


def _proj(x, gain, w_in, qg, kg, lb, *, batch, seq, tm, tn):
    m, d = x.shape
    n_in = w_in.shape[1]
    tps = MIX_W // tn
    hpt = tn // HEAD_W
    spb = seq // tm
    ng = 2 * d // tn
    nj = 7 * tps + ng
    assert n_in == 7 * MIX_W + 2 * d and MIX_W % tn == 0 and (2 * d) % tn == 0

    lane = np.arange(HEAD_W)
    gm = (lane[:, None] // QK_DIM == lane[None, :] // QK_DIM).astype(np.float32) / QK_DIM

    def hm_block(lo, n):
        return lambda i, j: (i // spb, jnp.clip(j - 1 - lo, 0, n - 1), i % spb, 0)

    return pl.pallas_call(
        functools.partial(_proj_kernel, tn=tn, nj=nj),
        out_shape=(
            jax.ShapeDtypeStruct((batch, 3 * HEADS, seq, HEAD_W), BF16),
            jax.ShapeDtypeStruct((batch, 4 * HEADS, seq, HEAD_W), BF16),
            jax.ShapeDtypeStruct((batch, HEADS, seq, HEAD_W), F32),
            jax.ShapeDtypeStruct((m, 2 * d), BF16),
        ),
        grid=(m // tm, nj + 1),
        in_specs=[
            pl.BlockSpec((tm, d), lambda i, j: (i, 0)),
            pl.BlockSpec((1, d), lambda i, j: (0, 0)),
            pl.BlockSpec((d, tn), lambda i, j: (0, jnp.minimum(j, nj - 1))),
            pl.BlockSpec((1, HEAD_W), lambda i, j: (0, 0)),
            pl.BlockSpec((1, HEAD_W), lambda i, j: (0, 0)),
            pl.BlockSpec((HEADS, HEAD_W), lambda i, j: (0, 0)),
            pl.BlockSpec((HEAD_W, HEAD_W), lambda i, j: (0, 0)),
        ],
        out_specs=(
            pl.BlockSpec((1, hpt, tm, HEAD_W), hm_block(0, 3 * tps)),
            pl.BlockSpec((1, hpt, tm, HEAD_W), hm_block(3 * tps, 4 * tps)),
            pl.BlockSpec((1, hpt, tm, HEAD_W), hm_block(4 * tps, tps)),
            pl.BlockSpec((tm, tn), lambda i, j: (i, jnp.clip(j - 1 - 7 * tps, 0, ng - 1))),
        ),
        scratch_shapes=[pltpu.VMEM((tm, d), BF16), pltpu.VMEM((tm, tn), F32)],
        compiler_params=_cparams(("parallel", "arbitrary")),
        name="proj",
    )(x, gain.reshape(1, d), w_in, qg, kg, lb, jnp.asarray(gm, BF16))


def _attn_kernel(lam_ref, off_ref, q_ref, k_ref, v_ref, bias_ref, sg_ref, o_ref,
                 qs_ref, vt_ref, m_ref, l_ref, acc_ref, *, t, nkb, online):
    h = pl.program_id(0)
    i = pl.program_id(2)
    lam = lam_ref[0]
    far_off = off_ref[h]

    @pl.when(i == 0)
    def _():
        for jb in range(nkb):
            vt_ref[jb] = v_ref[0, 0, jb * t:(jb + 1) * t, :].astype(F32).T.astype(BF16)

    q = q_ref[0, 0]
    lane = lax.broadcasted_iota(jnp.int32, (t, HEAD_W), 1)
    zero = jnp.zeros_like(q)
    qs_ref[0:t, :] = jnp.where(lane < QK_DIM, q, zero)
    qs_ref[t:2 * t, :] = jnp.where(lane >= QK_DIM, q, zero)
    if online:
        m_ref[...] = jnp.full(m_ref.shape, -jnp.inf, F32)
    l_ref[...] = jnp.zeros(l_ref.shape, F32)
    acc_ref[...] = jnp.zeros(acc_ref.shape, F32)

    def colsum8(p):
        return jnp.sum(p.reshape(t // 8, 8, 2 * t), axis=0)

    def step(j, badd):
        k = k_ref[0, 0, pl.ds(pl.multiple_of(j * t, t), t), :]
        s = _dot_nt(k, qs_ref[...]) + badd
        if online:
            m_old = m_ref[...]
            m_new = jnp.maximum(m_old, jnp.max(s, axis=0, keepdims=True))
            alpha = jnp.exp2(m_old - m_new)
            p = jnp.exp2(s - m_new)
            l_ref[...] = alpha * l_ref[...] + colsum8(p)
            acc_ref[...] = alpha * acc_ref[...] + _dot(vt_ref[j], p.astype(BF16))
            m_ref[...] = m_new
        else:
            p = jnp.exp2(s)
            l_ref[...] += colsum8(p)
            acc_ref[...] += _dot(vt_ref[j], p.astype(BF16))

    def far(j, carry):
        step(j, far_off)
        return carry

    lax.fori_loop(0, i - 1, far, 0)

    def both(x):
        return jnp.concatenate([x, x], axis=1)

    @pl.when(i >= 1)
    def _():
        step(i - 1, both(bias_ref[0, 1]))

    step(i, both(bias_ref[0, 0]))

    acc = acc_ref[...]
    l = jnp.sum(l_ref[...], axis=0, keepdims=True)
    o_t = acc[:, 0:t] / l[:, 0:t] - lam * (acc[:, t:2 * t] / l[:, t:2 * t])
    o_ref[...] = (_rms(o_t.T, sg_ref[...]) * (1.0 - LAMBDA_INIT)).astype(BF16)


def _t5_bucket(dist):
    n = jnp.maximum(dist, 0)
    max_exact = REL_BUCKETS // 2
    nf = jnp.maximum(n, 1).astype(F32)
    large = max_exact + (jnp.log(nf / max_exact) / math.log(REL_MAX_DIST / max_exact)
                         * (REL_BUCKETS - max_exact)).astype(jnp.int32)
    large = jnp.minimum(large, REL_BUCKETS - 1)
    return jnp.where(n < max_exact, n, large)


def _bias_tiles(f, t):
    hole = jnp.full((HEADS, 1), -jnp.inf, F32)
    w0 = jnp.concatenate([f[:, 0:t], jnp.full((HEADS, t), -jnp.inf, F32)], axis=1)
    w1 = jnp.concatenate([f[:, t:2 * t], hole, f[:, 1:t]], axis=1)
    w = jnp.stack([w0, w1], axis=1)
    x = jnp.tile(w, (1, 1, t))[:, :, :t * (2 * t - 1)].reshape(HEADS, 2, t, 2 * t - 1)
    return x[:, :, :, 0:t]


def _attn(qkv, lam, rel_bias, qk_bound, subln, *, batch, seq, t):
    assert t >= REL_MAX_DIST
    nq = seq // t
    rb = rel_bias.astype(F32)
    spread = 2.0 * qk_bound + jnp.max(jnp.max(rb, axis=0) - jnp.min(rb, axis=0))
    static_ok = spread <= MAX_STATIC_SOFTMAX_SPREAD
    upper = jnp.where(static_ok, qk_bound + jnp.max(rb, axis=0), 0.0)
    f = rb[_t5_bucket(jnp.arange(2 * t))].T - upper[:, None]
    bias = _bias_tiles(f * LOG2E, t)
    far_off = (rb[REL_BUCKETS - 1] - upper) * LOG2E

    def call(online):
        return pl.pallas_call(
            functools.partial(_attn_kernel, t=t, nkb=nq, online=online),
            out_shape=jax.ShapeDtypeStruct((batch * seq, MIX_W), BF16),
            grid=(HEADS, batch, nq),
            in_specs=[
                pl.BlockSpec(memory_space=pltpu.SMEM),
                pl.BlockSpec(memory_space=pltpu.SMEM),
                pl.BlockSpec((1, 1, t, HEAD_W), lambda h, b, i: (b, h, i, 0)),
                pl.BlockSpec((1, 1, seq, HEAD_W), lambda h, b, i: (b, HEADS + h, 0, 0)),
                pl.BlockSpec((1, 1, seq, HEAD_W), lambda h, b, i: (b, 2 * HEADS + h, 0, 0)),
                pl.BlockSpec((1, 2, t, t), lambda h, b, i: (h, 0, 0, 0)),
                pl.BlockSpec((1, HEAD_W), lambda h, b, i: (0, 0)),
            ],
            out_specs=pl.BlockSpec((t, HEAD_W), lambda h, b, i: (b * nq + i, h)),
            scratch_shapes=[
                pltpu.VMEM((2 * t, HEAD_W), BF16),
                pltpu.VMEM((nq, HEAD_W, t), BF16),
                pltpu.VMEM((1, 2 * t), F32),
                pltpu.VMEM((8, 2 * t), F32),
                pltpu.VMEM((HEAD_W, 2 * t), F32),
            ],
            compiler_params=_cparams(("parallel", "parallel", "arbitrary")),
            name="attn_online" if online else "attn",
        )(lam.reshape(1), far_off, qkv, qkv, qkv, bias, subln.reshape(1, HEAD_W))

    return lax.cond(static_ok, lambda: call(False), lambda: call(True))


def _hgrn_tables(c):
    levels = int(math.log2(c))
    assert 2 ** levels == c
    t = np.arange(c)[:, None]
    s = np.arange(c)[None, :]
    blocks = [(s <= t).astype(np.float32), (s > t).astype(np.float32)]
    level_id = np.where(t == s, 0, -1)
    for l in range(1, levels + 1):
        g, half = 2 ** l, 2 ** (l - 1)
        mid = (t // g) * g + half - 1
        upper = (t % g) >= half
        plus = upper & (s > mid) & (s <= t)
        minus = (~upper) & (s > t) & (s <= mid)
        blocks.append(plus.astype(np.float32) - minus.astype(np.float32))
        pair = (t // g == s // g) & upper & ((s % g) < half)
        level_id = np.where(pair, l, level_id)
    w = np.concatenate(blocks, axis=0)
    return jnp.asarray(w, BF16), jnp.asarray(level_id, jnp.int32), levels


def _hgrn_kernel(w_ref, lvl_ref, q_ref, k_ref, v_ref, og_ref, lf_ref, gn_ref, o_ref, *,
                 c, levels, seq, unroll):
    gain = gn_ref[...]

    def intra(ci):
        rows = pl.ds(pl.multiple_of(ci * c, c), c)
        g = lf_ref[0, 0, rows, :]
        g_hi = g.astype(BF16)
        g_lo = (g - g_hi.astype(F32)).astype(BF16)
        e2 = _dot(w_ref[...], jnp.concatenate([g_hi, g_lo], axis=1))
        e = e2[:, 0:HEAD_W] + e2[:, HEAD_W:2 * HEAD_W]

        def blk(n):
            return e[n * c:(n + 1) * c]

        qb = q_ref[0, 0, rows, :]
        kb = k_ref[0, 0, rows, :]
        v = v_ref[0, 0, rows, :]
        q = qb.astype(F32)
        k = kb.astype(F32)
        b = blk(0)
        lvl = lvl_ref[...]

        a = jnp.where(lvl == 0, _dot_nt(qb, kb), 0.0)
        for l in range(1, levels + 1):
            x = jnp.exp(-jnp.abs(blk(1 + l)))
            a = jnp.where(lvl == l, _dot_nt((q * x).astype(BF16), (k * x).astype(BF16)), a)

        o_intra = _dot(a.astype(BF16), v)
        q_in = (q * jnp.exp(b)).astype(BF16)
        k_out = (k * jnp.exp(blk(1))).astype(BF16)
        ds_t = _dot_tn(v, k_out)
        decay = jnp.exp(b[c - 1:c, :])
        return rows, o_intra, q_in, ds_t, decay

    def body(it, state_t):
        parts = [intra(it * unroll + u) for u in range(unroll)]
        for rows, o_intra, q_in, ds_t, decay in parts:
            o = o_intra + _dot_nt(q_in, state_t.astype(BF16))
            state_t = state_t * decay + ds_t
            y = _rms(o, gain) * og_ref[0, 0, rows, :].astype(F32)
            o_ref[rows, :] = y.astype(BF16)
        return state_t

    lax.fori_loop(0, seq // (c * unroll), body, jnp.zeros((HEAD_W, HEAD_W), F32))


def _hgrn(hin, logf, gain, *, batch, seq, c, unroll):
    w, lvl, levels = _hgrn_tables(c)
    assert seq % (c * unroll) == 0

    def head(slab):
        return pl.BlockSpec((1, 1, seq, HEAD_W), lambda b, h: (b, slab * HEADS + h, 0, 0))

    return pl.pallas_call(
        functools.partial(_hgrn_kernel, c=c, levels=levels, seq=seq, unroll=unroll),
        out_shape=jax.ShapeDtypeStruct((batch * seq, MIX_W), BF16),
        grid=(batch, HEADS),
        in_specs=[
            pl.BlockSpec(w.shape, lambda b, h: (0, 0)),
            pl.BlockSpec((c, c), lambda b, h: (0, 0)),
            head(0), head(1), head(2), head(3), head(0),
            pl.BlockSpec((1, HEAD_W), lambda b, h: (0, 0)),
        ],
        out_specs=pl.BlockSpec((seq, HEAD_W), lambda b, h: (b, h)),
        compiler_params=_cparams(("parallel", "parallel")),
        name="hgrn",
    )(w, lvl, hin, hin, hin, hin, logf, gain.reshape(1, HEAD_W))


def _merge_kernel(x_ref, ya_ref, yb_ref, gt_ref, wa_ref, wb_ref, wo_ref, o_ref):
    d = x_ref.shape[1]
    gt = gt_ref[...].astype(F32)
    merged = gt[:, 0:d] * _dot(ya_ref[...], wa_ref[...]) + gt[:, d:2 * d] * _dot(yb_ref[...], wb_ref[...])
    o_ref[...] = x_ref[...] + _dot(merged.astype(BF16), wo_ref[...])


def _resident(shape):
    return pl.BlockSpec(shape, lambda i: (0,) * len(shape), pipeline_mode=pl.Buffered(1))


def _merge(x, ya, yb, gates, wa, wb, wo, *, tm):
    m, d = x.shape
    return pl.pallas_call(
        _merge_kernel,
        out_shape=jax.ShapeDtypeStruct((m, d), F32),
        grid=(m // tm,),
        in_specs=[
            pl.BlockSpec((tm, d), lambda i: (i, 0)),
            pl.BlockSpec((tm, MIX_W), lambda i: (i, 0)),
            pl.BlockSpec((tm, MIX_W), lambda i: (i, 0)),
            pl.BlockSpec((tm, 2 * d), lambda i: (i, 0)),
            _resident(wa.shape), _resident(wb.shape), _resident(wo.shape),
        ],
        out_specs=pl.BlockSpec((tm, d), lambda i: (i, 0)),
        compiler_params=_cparams(("parallel",)),
        name="merge",
    )(x, ya, yb, gates, wa, wb, wo)


def _ple_kernel(x_ref, p_ref, gg_ref, pg_ref, wg_ref, wp_ref, o_ref):
    x = x_ref[...]
    gate = _sigmoid(_dot(_rms(x, gg_ref[...]).astype(BF16), wg_ref[...]))
    ple = _rms(_dot(p_ref[...].astype(BF16), wp_ref[...]), pg_ref[...])
    o_ref[...] = x + gate * ple


def _ple(x, p, gate_gain, post_gain, w_gate, w_proj, *, tm):
    m, d = x.shape
    pd = p.shape[1]
    return pl.pallas_call(
        _ple_kernel,
        out_shape=jax.ShapeDtypeStruct((m, d), F32),
        grid=(m // tm,),
        in_specs=[
            pl.BlockSpec((tm, d), lambda i: (i, 0)),
            pl.BlockSpec((tm, pd), lambda i: (i, 0)),
            pl.BlockSpec((1, d), lambda i: (0, 0)),
            pl.BlockSpec((1, d), lambda i: (0, 0)),
            _resident(w_gate.shape), _resident(w_proj.shape),
        ],
        out_specs=pl.BlockSpec((tm, d), lambda i: (i, 0)),
        compiler_params=_cparams(("parallel",)),
        name="ple",
    )(x, p, gate_gain.reshape(1, d), post_gain.reshape(1, d), w_gate, w_proj)


def _tiles(seq, d, f):
    def pick(n, choices):
        return next(c for c in choices if n % c == 0)
    return dict(
        tm=pick(seq, (512, 256, 128)),
        tf=pick(f, (512, 256, 128)),
        tp=pick(seq, (1024, 512, 256, 128)),
        tn=pick(math.gcd(MIX_W, 2 * d), (512, 256, 128)),
        ta=pick(seq, (512, 256, 128)),
        c=128,
        cu=4,
    )


def kernel(x, p, ffn1_norm, ffn1_w_gate, ffn1_w_up, ffn1_w_down, mix_norm, w_in, q_norm, k_norm, lambda_q1, lambda_k1, lambda_q2, lambda_k2, diff_subln, rel_bias, hgrn_lb_logits, hgrn_norm, w_branch_a, w_branch_b, w_out, ffn2_norm, ffn2_w_gate, ffn2_w_up, ffn2_w_down, ple_gate_norm, w_ple_gate, w_ple_proj, ple_post_norm):
    batch, seq, d = x.shape
    depth = ffn1_norm.shape[0]
    assert depth == 1
    m = batch * seq
    tl = _tiles(seq, d, ffn1_w_gate.shape[2])
    bf = lambda w: w.astype(BF16)

    lower_bounds = jnp.cumsum(jax.nn.softmax(hgrn_lb_logits.astype(F32), axis=0), axis=0)
    lb = lower_bounds[0].reshape(HEADS, HEAD_W)
    lam = (jnp.exp(jnp.sum(lambda_q1[0].astype(F32) * lambda_k1[0].astype(F32)))
           - jnp.exp(jnp.sum(lambda_q2[0].astype(F32) * lambda_k2[0].astype(F32)))
           + LAMBDA_INIT)
    scale = QK_DIM ** -0.5
    qg = jnp.tile(q_norm[0].astype(F32), 2).reshape(1, HEAD_W) * (scale * LOG2E)
    kg = jnp.tile(k_norm[0].astype(F32), 2).reshape(1, HEAD_W)
    qk_bound = (QK_DIM * scale * jnp.max(jnp.abs(q_norm[0].astype(F32)))
                * jnp.max(jnp.abs(k_norm[0].astype(F32))))

    x2 = x.reshape(m, d)
    x2 = _ffn(x2, ffn1_norm[0], bf(ffn1_w_gate[0]), bf(ffn1_w_up[0]), bf(ffn1_w_down[0]),
              tm=tl["tm"], tf=tl["tf"])

    qkv, hin, logf, gates = _proj(x2, mix_norm[0], bf(w_in[0]), qg, kg, lb,
                                  batch=batch, seq=seq, tm=tl["tp"], tn=tl["tn"])
    ya = _attn(qkv, lam, rel_bias, qk_bound, diff_subln[0], batch=batch, seq=seq, t=tl["ta"])
    yb = _hgrn(hin, logf, hgrn_norm[0], batch=batch, seq=seq, c=tl["c"], unroll=tl["cu"])
    x2 = _merge(x2, ya, yb, gates, bf(w_branch_a[0]), bf(w_branch_b[0]), bf(w_out[0]), tm=tl["tm"])

    x2 = _ffn(x2, ffn2_norm[0], bf(ffn2_w_gate[0]), bf(ffn2_w_up[0]), bf(ffn2_w_down[0]),
              tm=tl["tm"], tf=tl["tf"])
    x2 = _ple(x2, p[0].reshape(m, -1), ple_gate_norm[0], ple_post_norm[0],
              bf(w_ple_gate[0]), bf(w_ple_proj[0]), tm=tl["tm"])
    return x2.reshape(batch, seq, d)
```

```python
import functools
import math

import jax
import jax.numpy as jnp
import numpy as np
from jax import lax
from jax.experimental import pallas as pl
from jax.experimental.pallas import tpu as pltpu

F32 = jnp.float32
BF16 = jnp.bfloat16

EPS = 1e-6
HEADS = 8
HEAD_W = 128
QK_DIM = 64
MIX_W = HEADS * HEAD_W
REL_BUCKETS = 32
REL_MAX_DIST = 128
LAMBDA_INIT = 0.8 - 0.6 * math.exp(-0.3 * 0)
LOG2E = math.log2(math.e)
MAX_STATIC_SOFTMAX_SPREAD = 60.0

V7X_VMEM_BYTES = 64 * 1024 * 1024
VMEM_LIMIT = 56 * 1024 * 1024


def _cparams(sem, flags=None):
    return pltpu.CompilerParams(dimension_semantics=sem, vmem_limit_bytes=VMEM_LIMIT, flags=flags)


def _sigmoid(x):
    return 1.0 / (1.0 + jnp.exp(-x))


def _rms(x, gain):
    ms = jnp.mean(x * x, axis=-1, keepdims=True)
    return x * lax.rsqrt(ms + EPS) * gain


def _dot(a, b):
    return jnp.dot(a, b, preferred_element_type=F32)


def _dot_nt(a, b):
    return lax.dot_general(a, b, (((1,), (1,)), ((), ())), preferred_element_type=F32)


def _dot_tn(a, b):
    return lax.dot_general(a, b, (((0,), (0,)), ((), ())), preferred_element_type=F32)


def _ffn_kernel(x_ref, g_ref, wg_ref, wu_ref, wd_ref, o_ref, h_ref):
    j = pl.program_id(1)

    @pl.when(j == 0)
    def _():
        x = x_ref[...]
        h_ref[...] = _rms(x, g_ref[...]).astype(BF16)
        o_ref[...] = x

    h = h_ref[...]
    gate = _dot(h, wg_ref[...])
    up = _dot(h, wu_ref[...])
    act = (gate * _sigmoid(gate) * (0.5 * up)).astype(BF16)
    o_ref[...] += _dot(act, wd_ref[...])


def _ffn(x, gain, w_gate, w_up, w_down, *, tm, tf):
    m, d = x.shape
    f = w_gate.shape[1]
    return pl.pallas_call(
        _ffn_kernel,
        out_shape=jax.ShapeDtypeStruct((m, d), F32),
        grid=(m // tm, f // tf),
        in_specs=[
            pl.BlockSpec((tm, d), lambda i, j: (i, 0)),
            pl.BlockSpec((1, d), lambda i, j: (0, 0)),
            pl.BlockSpec((d, tf), lambda i, j: (0, j)),
            pl.BlockSpec((d, tf), lambda i, j: (0, j)),
            pl.BlockSpec((tf, d), lambda i, j: (j, 0)),
        ],
        out_specs=pl.BlockSpec((tm, d), lambda i, j: (i, 0)),
        scratch_shapes=[pltpu.VMEM((tm, d), BF16)],
        compiler_params=_cparams(("parallel", "arbitrary")),
        name="ffn",
    )(x, gain.reshape(1, d), w_gate, w_up, w_down)


def _proj_kernel(x_ref, g_ref, w_ref, qg_ref, kg_ref, lb_ref, gm_ref,
                 oqkv_ref, oh_ref, olf_ref, og_ref, h_ref, acc_ref, *, tn, nj):
    j = pl.program_id(1)
    jp = j - 1
    tps = MIX_W // tn
    hpt = tn // HEAD_W

    def heads(vals):
        return [vals[:, s * HEAD_W:(s + 1) * HEAD_W] for s in range(hpt)]

    def epi_qk(acc):
        gain = jnp.where(jp < tps, qg_ref[...], kg_ref[...])
        for s, a in enumerate(heads(acc)):
            ms = _dot((a * a).astype(BF16), gm_ref[...])
            oqkv_ref[0, s] = (a * lax.rsqrt(ms + EPS) * gain).astype(BF16)

    def epi_v(acc):
        for s, a in enumerate(heads(acc)):
            oqkv_ref[0, s] = a.astype(BF16)

    def epi_silu(acc):
        for s, a in enumerate(heads(acc * _sigmoid(acc))):
            oh_ref[0, s] = a.astype(BF16)

    def epi_forget(acc):
        jj = jp - 4 * tps
        for s, a in enumerate(heads(acc)):
            lb = lb_ref[pl.ds(jj * hpt + s, 1), :]
            forget = lb + (1.0 - lb) * _sigmoid(a)
            oh_ref[0, s] = (1.0 - forget).astype(BF16)
            olf_ref[0, s] = jnp.log(forget)

    def epi_hv(acc):
        for s, a in enumerate(heads(acc)):
            oh_ref[0, s] = a.astype(BF16)

    def epi_gates(acc):
        og_ref[...] = _sigmoid(acc).astype(BF16)

    @pl.when(j == 0)
    def _():
        h_ref[...] = _rms(x_ref[...], g_ref[...]).astype(BF16)
        acc_ref[...] = _dot(h_ref[...], w_ref[...])

    def fused(cond, epi):
        @pl.when(cond & (j >= 1) & (j < nj))
        def _():
            new = _dot(h_ref[...], w_ref[...])
            epi(acc_ref[...])
            acc_ref[...] = new

    fused(jp < 2 * tps, epi_qk)
    fused((jp >= 2 * tps) & (jp < 3 * tps), epi_v)
    fused(((jp >= 3 * tps) & (jp < 4 * tps)) | ((jp >= 6 * tps) & (jp < 7 * tps)), epi_silu)
    fused((jp >= 4 * tps) & (jp < 5 * tps), epi_forget)
    fused((jp >= 5 * tps) & (jp < 6 * tps), epi_hv)
    fused(jp >= 7 * tps, epi_gates)

    @pl.when(j == nj)
    def _():
        epi_gates(acc_ref[...])


def _proj(x, gain, w_in, qg, kg, lb, *, batch, seq, tm, tn):
    m, d = x.shape
    n_in = w_in.shape[1]
    tps = MIX_W // tn
    hpt = tn // HEAD_W
    spb = seq // tm
    ng = 2 * d // tn
    nj = 7 * tps + ng
    assert n_in == 7 * MIX_W + 2 * d and MIX_W % tn == 0 and (2 * d) % tn == 0

    lane = np.arange(HEAD_W)
    gm = (lane[:, None] // QK_DIM == lane[None, :] // QK_DIM).astype(np.float32) / QK_DIM

    def hm_block(lo, n):
        return lambda i, j: (i // spb, jnp.clip(j - 1 - lo, 0, n - 1), i % spb, 0)

    return pl.pallas_call(
        functools.partial(_proj_kernel, tn=tn, nj=nj),
        out_shape=(
            jax.ShapeDtypeStruct((batch, 3 * HEADS, seq, HEAD_W), BF16),
            jax.ShapeDtypeStruct((batch, 4 * HEADS, seq, HEAD_W), BF16),
            jax.ShapeDtypeStruct((batch, HEADS, seq, HEAD_W), F32),
            jax.ShapeDtypeStruct((m, 2 * d), BF16),
        ),
        grid=(m // tm, nj + 1),
        in_specs=[
            pl.BlockSpec((tm, d), lambda i, j: (i, 0)),
            pl.BlockSpec((1, d), lambda i, j: (0, 0)),
            pl.BlockSpec((d, tn), lambda i, j: (0, jnp.minimum(j, nj - 1))),
            pl.BlockSpec((1, HEAD_W), lambda i, j: (0, 0)),
            pl.BlockSpec((1, HEAD_W), lambda i, j: (0, 0)),
            pl.BlockSpec((HEADS, HEAD_W), lambda i, j: (0, 0)),
            pl.BlockSpec((HEAD_W, HEAD_W), lambda i, j: (0, 0)),
        ],
        out_specs=(
            pl.BlockSpec((1, hpt, tm, HEAD_W), hm_block(0, 3 * tps)),
            pl.BlockSpec((1, hpt, tm, HEAD_W), hm_block(3 * tps, 4 * tps)),
            pl.BlockSpec((1, hpt, tm, HEAD_W), hm_block(4 * tps, tps)),
            pl.BlockSpec((tm, tn), lambda i, j: (i, jnp.clip(j - 1 - 7 * tps, 0, ng - 1))),
        ),
        scratch_shapes=[pltpu.VMEM((tm, d), BF16), pltpu.VMEM((tm, tn), F32)],
        compiler_params=_cparams(("parallel", "arbitrary")),
        name="proj",
    )(x, gain.reshape(1, d), w_in, qg, kg, lb, jnp.asarray(gm, BF16))


def _attn_kernel(lam_ref, off_ref, q_ref, k_ref, v_ref, bias_ref, sg_ref, o_ref,
                 qs_ref, vt_ref, m_ref, l_ref, acc_ref, *, t, nkb, online):
    h = pl.program_id(0)
    i = pl.program_id(2)
    lam = lam_ref[0]
    far_off = off_ref[h]

    @pl.when(i == 0)
    def _():
        for jb in range(nkb):
            vt_ref[jb] = v_ref[0, 0, jb * t:(jb + 1) * t, :].astype(F32).T.astype(BF16)

    q = q_ref[0, 0]
    lane = lax.broadcasted_iota(jnp.int32, (t, HEAD_W), 1)
    zero = jnp.zeros_like(q)
    qs_ref[0:t, :] = jnp.where(lane < QK_DIM, q, zero)
    qs_ref[t:2 * t, :] = jnp.where(lane >= QK_DIM, q, zero)
    if online:
        m_ref[...] = jnp.full(m_ref.shape, -jnp.inf, F32)
    l_ref[...] = jnp.zeros(l_ref.shape, F32)
    acc_ref[...] = jnp.zeros(acc_ref.shape, F32)

    def colsum8(p):
        return jnp.sum(p.reshape(t // 8, 8, 2 * t), axis=0)

    def step(j, badd):
        k = k_ref[0, 0, pl.ds(pl.multiple_of(j * t, t), t), :]
        s = _dot_nt(k, qs_ref[...]) + badd
        if online:
            m_old = m_ref[...]
            m_new = jnp.maximum(m_old, jnp.max(s, axis=0, keepdims=True))
            alpha = jnp.exp2(m_old - m_new)
            p = jnp.exp2(s - m_new)
            l_ref[...] = alpha * l_ref[...] + colsum8(p)
            acc_ref[...] = alpha * acc_ref[...] + _dot(vt_ref[j], p.astype(BF16))
            m_ref[...] = m_new
        else:
            p = jnp.exp2(s)
            l_ref[...] += colsum8(p)
            acc_ref[...] += _dot(vt_ref[j], p.astype(BF16))

    def far(j, carry):
        step(j, far_off)
        return carry

    lax.fori_loop(0, i - 1, far, 0)

    def both(x):
        return jnp.concatenate([x, x], axis=1)

    @pl.when(i >= 1)
    def _():
        step(i - 1, both(bias_ref[0, 1]))

    step(i, both(bias_ref[0, 0]))

    acc = acc_ref[...]
    l = jnp.sum(l_ref[...], axis=0, keepdims=True)
    o_t = acc[:, 0:t] / l[:, 0:t] - lam * (acc[:, t:2 * t] / l[:, t:2 * t])
    o_ref[...] = (_rms(o_t.T, sg_ref[...]) * (1.0 - LAMBDA_INIT)).astype(BF16)


def _t5_bucket(dist):
    n = jnp.maximum(dist, 0)
    max_exact = REL_BUCKETS // 2
    nf = jnp.maximum(n, 1).astype(F32)
    large = max_exact + (jnp.log(nf / max_exact) / math.log(REL_MAX_DIST / max_exact)
                         * (REL_BUCKETS - max_exact)).astype(jnp.int32)
    large = jnp.minimum(large, REL_BUCKETS - 1)
    return jnp.where(n < max_exact, n, large)


def _bias_tiles(f, t):
    hole = jnp.full((HEADS, 1), -jnp.inf, F32)
    w0 = jnp.concatenate([f[:, 0:t], jnp.full((HEADS, t), -jnp.inf, F32)], axis=1)
    w1 = jnp.concatenate([f[:, t:2 * t], hole, f[:, 1:t]], axis=1)
    w = jnp.stack([w0, w1], axis=1)
    x = jnp.tile(w, (1, 1, t))[:, :, :t * (2 * t - 1)].reshape(HEADS, 2, t, 2 * t - 1)
    return x[:, :, :, 0:t]


def _attn(qkv, lam, rel_bias, qk_bound, subln, *, batch, seq, t):
    assert t >= REL_MAX_DIST
    nq = seq // t
    rb = rel_bias.astype(F32)
    spread = 2.0 * qk_bound + jnp.max(jnp.max(rb, axis=0) - jnp.min(rb, axis=0))
    static_ok = spread <= MAX_STATIC_SOFTMAX_SPREAD
    upper = jnp.where(static_ok, qk_bound + jnp.max(rb, axis=0), 0.0)
    f = rb[_t5_bucket(jnp.arange(2 * t))].T - upper[:, None]
    bias = _bias_tiles(f * LOG2E, t)
    far_off = (rb[REL_BUCKETS - 1] - upper) * LOG2E

    def call(online):
        return pl.pallas_call(
            functools.partial(_attn_kernel, t=t, nkb=nq, online=online),
            out_shape=jax.ShapeDtypeStruct((batch * seq, MIX_W), BF16),
            grid=(HEADS, batch, nq),
            in_specs=[
                pl.BlockSpec(memory_space=pltpu.SMEM),
                pl.BlockSpec(memory_space=pltpu.SMEM),
                pl.BlockSpec((1, 1, t, HEAD_W), lambda h, b, i: (b, h, i, 0)),
                pl.BlockSpec((1, 1, seq, HEAD_W), lambda h, b, i: (b, HEADS + h, 0, 0)),
                pl.BlockSpec((1, 1, seq, HEAD_W), lambda h, b, i: (b, 2 * HEADS + h, 0, 0)),
                pl.BlockSpec((1, 2, t, t), lambda h, b, i: (h, 0, 0, 0)),
                pl.BlockSpec((1, HEAD_W), lambda h, b, i: (0, 0)),
            ],
            out_specs=pl.BlockSpec((t, HEAD_W), lambda h, b, i: (b * nq + i, h)),
            scratch_shapes=[
                pltpu.VMEM((2 * t, HEAD_W), BF16),
                pltpu.VMEM((nq, HEAD_W, t), BF16),
                pltpu.VMEM((1, 2 * t), F32),
                pltpu.VMEM((8, 2 * t), F32),
                pltpu.VMEM((HEAD_W, 2 * t), F32),
            ],
            compiler_params=_cparams(("parallel", "parallel", "arbitrary")),
            name="attn_online" if online else "attn",
        )(lam.reshape(1), far_off, qkv, qkv, qkv, bias, subln.reshape(1, HEAD_W))

    return lax.cond(static_ok, lambda: call(False), lambda: call(True))


SUBLANES = 8


def _hgrn_tables(c):
    levels = int(math.log2(c))
    assert 2 ** levels == c
    t = np.arange(c)[:, None]
    s = np.arange(c)[None, :]
    blocks = [(s <= t).astype(np.float32)]
    level_id = np.where(t == s, 0, -1)
    n_fine = 0
    for l in range(1, levels + 1):
        g, half = 2 ** l, 2 ** (l - 1)
        mid = (t // g) * g + half - 1
        upper = (t % g) >= half
        if half < SUBLANES:
            plus = upper & (s > mid) & (s <= t)
            minus = (~upper) & (s > t) & (s <= mid)
            blocks.append(plus.astype(np.float32) - minus.astype(np.float32))
            n_fine += 1
        pair = (t // g == s // g) & upper & ((s % g) < half)
        level_id = np.where(pair, l, level_id)
    w = np.concatenate(blocks, axis=0)
    return jnp.asarray(w, BF16), jnp.asarray(level_id, jnp.int32), levels, n_fine


def _hgrn_kernel(w_ref, lvl_ref, q_ref, k_ref, v_ref, og_ref, lf_ref, gn_ref, o_ref, *,
                 c, levels, n_fine, seq, unroll):
    gain = gn_ref[...]

    def cumsums(ci):
        rows = pl.ds(pl.multiple_of(ci * c, c), c)
        g = lf_ref[0, 0, rows, :]
        g_hi = g.astype(BF16)
        g_lo = (g - g_hi.astype(F32)).astype(BF16)
        e2 = _dot(w_ref[...], jnp.concatenate([g_hi, g_lo], axis=1))
        return rows, e2[:, 0:HEAD_W] + e2[:, HEAD_W:2 * HEAD_W]

    def scores(rows, e):
        qb = q_ref[0, 0, rows, :]
        kb = k_ref[0, 0, rows, :]
        q = qb.astype(F32)
        k = kb.astype(F32)
        b = e[0:c]
        lvl = lvl_ref[...]
        a = jnp.where(lvl == 0, _dot_nt(qb, kb), 0.0)
        for l in range(1, levels + 1):
            if l <= n_fine:
                d = e[l * c:(l + 1) * c]
            else:
                g, half = 2 ** l, 2 ** (l - 1)
                b3 = b.reshape(c // g, g, HEAD_W)
                d = (b3 - b3[:, half - 1:half, :]).reshape(c, HEAD_W)
            x = jnp.exp(-jnp.abs(d))
            a = jnp.where(lvl == l, _dot_nt((q * x).astype(BF16), (k * x).astype(BF16)), a)
        b_last = b[c - 1:c, :]
        q_in = (q * jnp.exp(b)).astype(BF16)
        k_out = (k * jnp.exp(b_last - b)).astype(BF16)
        return a.astype(BF16), q_in, k_out, jnp.exp(b_last)

    def body(it, state_t):
        st1 = [cumsums(it * unroll + u) for u in range(unroll)]
        st2 = [scores(rows, e) for rows, e in st1]
        st3 = []
        for (rows, _), (a, q_in, k_out, decay) in zip(st1, st2):
            v = v_ref[0, 0, rows, :]
            st3.append((rows, _dot(a, v), q_in, _dot_tn(v, k_out), decay))
        for rows, o_intra, q_in, ds_t, decay in st3:
            o = o_intra + _dot_nt(q_in, state_t.astype(BF16))
            state_t = state_t * decay + ds_t
            y = _rms(o, gain) * og_ref[0, 0, rows, :].astype(F32)
            o_ref[rows, :] = y.astype(BF16)
        return state_t

    lax.fori_loop(0, seq // (c * unroll), body, jnp.zeros((HEAD_W, HEAD_W), F32))


def _hgrn(hin, logf, gain, *, batch, seq, c, unroll):
    w, lvl, levels, n_fine = _hgrn_tables(c)
    assert seq % (c * unroll) == 0

    def head(slab):
        return pl.BlockSpec((1, 1, seq, HEAD_W), lambda b, h: (b, slab * HEADS + h, 0, 0))

    return pl.pallas_call(
        functools.partial(_hgrn_kernel, c=c, levels=levels, n_fine=n_fine, seq=seq, unroll=unroll),
        out_shape=jax.ShapeDtypeStruct((batch * seq, MIX_W), BF16),
        grid=(batch, HEADS),
        in_specs=[
            pl.BlockSpec(w.shape, lambda b, h: (0, 0)),
            pl.BlockSpec((c, c), lambda b, h: (0, 0)),
            head(0), head(1), head(2), head(3), head(0),
            pl.BlockSpec((1, HEAD_W), lambda b, h: (0, 0)),
        ],
        out_specs=pl.BlockSpec((seq, HEAD_W), lambda b, h: (b, h)),
        compiler_params=_cparams(("parallel", "parallel")),
        name="hgrn",
    )(w, lvl, hin, hin, hin, hin, logf, gain.reshape(1, HEAD_W))


def _merge_kernel(x_ref, ya_ref, yb_ref, gt_ref, wa_ref, wb_ref, wo_ref, o_ref):
    d = x_ref.shape[1]
    gt = gt_ref[...].astype(F32)
    merged = gt[:, 0:d] * _dot(ya_ref[...], wa_ref[...]) + gt[:, d:2 * d] * _dot(yb_ref[...], wb_ref[...])
    o_ref[...] = x_ref[...] + _dot(merged.astype(BF16), wo_ref[...])


def _resident(shape):
    return pl.BlockSpec(shape, lambda i: (0,) * len(shape), pipeline_mode=pl.Buffered(1))


def _merge(x, ya, yb, gates, wa, wb, wo, *, tm):
    m, d = x.shape
    return pl.pallas_call(
        _merge_kernel,
        out_shape=jax.ShapeDtypeStruct((m, d), F32),
        grid=(m // tm,),
        in_specs=[
            pl.BlockSpec((tm, d), lambda i: (i, 0)),
            pl.BlockSpec((tm, MIX_W), lambda i: (i, 0)),
            pl.BlockSpec((tm, MIX_W), lambda i: (i, 0)),
            pl.BlockSpec((tm, 2 * d), lambda i: (i, 0)),
            _resident(wa.shape), _resident(wb.shape), _resident(wo.shape),
        ],
        out_specs=pl.BlockSpec((tm, d), lambda i: (i, 0)),
        compiler_params=_cparams(("parallel",)),
        name="merge",
    )(x, ya, yb, gates, wa, wb, wo)


def _ple_kernel(x_ref, p_ref, gg_ref, pg_ref, wg_ref, wp_ref, o_ref):
    x = x_ref[...]
    gate = _sigmoid(_dot(_rms(x, gg_ref[...]).astype(BF16), wg_ref[...]))
    ple = _rms(_dot(p_ref[...].astype(BF16), wp_ref[...]), pg_ref[...])
    o_ref[...] = x + gate * ple


def _ple(x, p, gate_gain, post_gain, w_gate, w_proj, *, tm):
    m, d = x.shape
    pd = p.shape[1]
    return pl.pallas_call(
        _ple_kernel,
        out_shape=jax.ShapeDtypeStruct((m, d), F32),
        grid=(m // tm,),
        in_specs=[
            pl.BlockSpec((tm, d), lambda i: (i, 0)),
            pl.BlockSpec((tm, pd), lambda i: (i, 0)),
            pl.BlockSpec((1, d), lambda i: (0, 0)),
            pl.BlockSpec((1, d), lambda i: (0, 0)),
            _resident(w_gate.shape), _resident(w_proj.shape),
        ],
        out_specs=pl.BlockSpec((tm, d), lambda i: (i, 0)),
        compiler_params=_cparams(("parallel",)),
        name="ple",
    )(x, p, gate_gain.reshape(1, d), post_gain.reshape(1, d), w_gate, w_proj)


def _tiles(seq, d, f):
    def pick(n, choices):
        return next(c for c in choices if n % c == 0)
    return dict(
        tm=pick(seq, (512, 256, 128)),
        tf=pick(f, (512, 256, 128)),
        tp=pick(seq, (1024, 512, 256, 128)),
        tn=pick(math.gcd(MIX_W, 2 * d), (512, 256, 128)),
        ta=pick(seq, (512, 256, 128)),
        c=128,
        cu=4,
    )


def kernel(x, p, ffn1_norm, ffn1_w_gate, ffn1_w_up, ffn1_w_down, mix_norm, w_in, q_norm, k_norm, lambda_q1, lambda_k1, lambda_q2, lambda_k2, diff_subln, rel_bias, hgrn_lb_logits, hgrn_norm, w_branch_a, w_branch_b, w_out, ffn2_norm, ffn2_w_gate, ffn2_w_up, ffn2_w_down, ple_gate_norm, w_ple_gate, w_ple_proj, ple_post_norm):
    batch, seq, d = x.shape
    depth = ffn1_norm.shape[0]
    assert depth == 1
    m = batch * seq
    tl = _tiles(seq, d, ffn1_w_gate.shape[2])
    bf = lambda w: w.astype(BF16)

    lower_bounds = jnp.cumsum(jax.nn.softmax(hgrn_lb_logits.astype(F32), axis=0), axis=0)
    lb = lower_bounds[0].reshape(HEADS, HEAD_W)
    lam = (jnp.exp(jnp.sum(lambda_q1[0].astype(F32) * lambda_k1[0].astype(F32)))
           - jnp.exp(jnp.sum(lambda_q2[0].astype(F32) * lambda_k2[0].astype(F32)))
           + LAMBDA_INIT)
    scale = QK_DIM ** -0.5
    qg = jnp.tile(q_norm[0].astype(F32), 2).reshape(1, HEAD_W) * (scale * LOG2E)
    kg = jnp.tile(k_norm[0].astype(F32), 2).reshape(1, HEAD_W)
    qk_bound = (QK_DIM * scale * jnp.max(jnp.abs(q_norm[0].astype(F32)))
                * jnp.max(jnp.abs(k_norm[0].astype(F32))))

    x2 = x.reshape(m, d)
    x2 = _ffn(x2, ffn1_norm[0], bf(ffn1_w_gate[0]), bf(ffn1_w_up[0]), bf(ffn1_w_down[0]),
              tm=tl["tm"], tf=tl["tf"])

    qkv, hin, logf, gates = _proj(x2, mix_norm[0], bf(w_in[0]), qg, kg, lb,
                                  batch=batch, seq=seq, tm=tl["tp"], tn=tl["tn"])
    ya = _attn(qkv, lam, rel_bias, qk_bound, diff_subln[0], batch=batch, seq=seq, t=tl["ta"])
    yb = _hgrn(hin, logf, hgrn_norm[0], batch=batch, seq=seq, c=tl["c"], unroll=tl["cu"])
    x2 = _merge(x2, ya, yb, gates, bf(w_branch_a[0]), bf(w_branch_b[0]), bf(w_out[0]), tm=tl["tm"])

    x2 = _ffn(x2, ffn2_norm[0], bf(ffn2_w_gate[0]), bf(ffn2_w_up[0]), bf(ffn2_w_down[0]),
              tm=tl["tm"], tf=tl["tf"])
    x2 = _ple(x2, p[0].reshape(m, -1), ple_gate_norm[0], ple_post_norm[0],
              bf(w_ple_gate[0]), bf(w_ple_proj[0]), tm=tl["tm"])
    return x2.reshape(batch, seq, d)
```

```python
import functools
import math

import jax
import jax.numpy as jnp
import numpy as np
from jax import lax
from jax.experimental import pallas as pl
from jax.experimental.pallas import tpu as pltpu

F32 = jnp.float32
BF16 = jnp.bfloat16

EPS = 1e-6
HEADS = 8
HEAD_W = 128
QK_DIM = 64
MIX_W = HEADS * HEAD_W
REL_BUCKETS = 32
REL_MAX_DIST = 128
LAMBDA_INIT = 0.8 - 0.6 * math.exp(-0.3 * 0)
LOG2E = math.log2(math.e)
MAX_STATIC_SOFTMAX_SPREAD = 60.0

V7X_VMEM_BYTES = 64 * 1024 * 1024
VMEM_LIMIT = 56 * 1024 * 1024


def _cparams(sem, flags=None):
    return pltpu.CompilerParams(dimension_semantics=sem, vmem_limit_bytes=VMEM_LIMIT, flags=flags)


def _sigmoid(x):
    return 1.0 / (1.0 + jnp.exp(-x))


def _rms(x, gain):
    ms = jnp.mean(x * x, axis=-1, keepdims=True)
    return x * lax.rsqrt(ms + EPS) * gain


def _dot(a, b):
    return jnp.dot(a, b, preferred_element_type=F32)


def _dot_nt(a, b):
    return lax.dot_general(a, b, (((1,), (1,)), ((), ())), preferred_element_type=F32)


def _dot_tn(a, b):
    return lax.dot_general(a, b, (((0,), (0,)), ((), ())), preferred_element_type=F32)


def _ffn_kernel(x_ref, g_ref, wg_ref, wu_ref, wd_ref, o_ref, h_ref):
    j = pl.program_id(1)

    @pl.when(j == 0)
    def _():
        x = x_ref[...]
        h_ref[...] = _rms(x, g_ref[...]).astype(BF16)
        o_ref[...] = x

    h = h_ref[...]
    gate = _dot(h, wg_ref[...])
    up = _dot(h, wu_ref[...])
    act = (gate * _sigmoid(gate) * (0.5 * up)).astype(BF16)
    o_ref[...] += _dot(act, wd_ref[...])


def _ffn(x, gain, w_gate, w_up, w_down, *, tm, tf):
    m, d = x.shape
    f = w_gate.shape[1]
    return pl.pallas_call(
        _ffn_kernel,
        out_shape=jax.ShapeDtypeStruct((m, d), F32),
        grid=(m // tm, f // tf),
        in_specs=[
            pl.BlockSpec((tm, d), lambda i, j: (i, 0), pipeline_mode=pl.Buffered(1)),
            pl.BlockSpec((1, d), lambda i, j: (0, 0)),
            pl.BlockSpec((d, tf), lambda i, j: (0, j)),
            pl.BlockSpec((d, tf), lambda i, j: (0, j)),
            pl.BlockSpec((tf, d), lambda i, j: (j, 0)),
        ],
        out_specs=pl.BlockSpec((tm, d), lambda i, j: (i, 0)),
        scratch_shapes=[pltpu.VMEM((tm, d), BF16)],
        compiler_params=_cparams(("parallel", "arbitrary")),
        name="ffn",
    )(x, gain.reshape(1, d), w_gate, w_up, w_down)


def _proj_kernel(x_ref, g_ref, w_ref, qg_ref, kg_ref, lb_ref, gm_ref,
                 oqkv_ref, oh_ref, olf_ref, og_ref, h_ref, acc_ref, *, tn, nj):
    j = pl.program_id(1)
    jp = j - 1
    tps = MIX_W // tn
    hpt = tn // HEAD_W

    def heads(vals):
        return [vals[:, s * HEAD_W:(s + 1) * HEAD_W] for s in range(hpt)]

    def epi_qk(acc):
        gain = jnp.where(jp < tps, qg_ref[...], kg_ref[...])
        for s, a in enumerate(heads(acc)):
            ms = _dot((a * a).astype(BF16), gm_ref[...])
            oqkv_ref[0, s] = (a * lax.rsqrt(ms + EPS) * gain).astype(BF16)

    def epi_v(acc):
        for s, a in enumerate(heads(acc)):
            oqkv_ref[0, s] = a.astype(BF16)

    def epi_silu(acc):
        for s, a in enumerate(heads(acc * _sigmoid(acc))):
            oh_ref[0, s] = a.astype(BF16)

    def epi_forget(acc):
        jj = jp - 4 * tps
        for s, a in enumerate(heads(acc)):
            lb = lb_ref[pl.ds(jj * hpt + s, 1), :]
            forget = lb + (1.0 - lb) * _sigmoid(a)
            oh_ref[0, s] = (1.0 - forget).astype(BF16)
            olf_ref[0, s] = jnp.log(forget)

    def epi_hv(acc):
        for s, a in enumerate(heads(acc)):
            oh_ref[0, s] = a.astype(BF16)

    def epi_gates(acc):
        og_ref[...] = _sigmoid(acc).astype(BF16)

    @pl.when(j == 0)
    def _():
        h_ref[...] = _rms(x_ref[...], g_ref[...]).astype(BF16)
        acc_ref[...] = _dot(h_ref[...], w_ref[...])

    def fused(cond, epi):
        @pl.when(cond & (j >= 1) & (j < nj))
        def _():
            new = _dot(h_ref[...], w_ref[...])
            epi(acc_ref[...])
            acc_ref[...] = new

    fused(jp < 2 * tps, epi_qk)
    fused((jp >= 2 * tps) & (jp < 3 * tps), epi_v)
    fused(((jp >= 3 * tps) & (jp < 4 * tps)) | ((jp >= 6 * tps) & (jp < 7 * tps)), epi_silu)
    fused((jp >= 4 * tps) & (jp < 5 * tps), epi_forget)
    fused((jp >= 5 * tps) & (jp < 6 * tps), epi_hv)
    fused(jp >= 7 * tps, epi_gates)

    @pl.when(j == nj)
    def _():
        epi_gates(acc_ref[...])


def _proj(x, gain, w_in, qg, kg, lb, *, batch, seq, tm, tn):
    m, d = x.shape
    n_in = w_in.shape[1]
    tps = MIX_W // tn
    hpt = tn // HEAD_W
    spb = seq // tm
    ng = 2 * d // tn
    nj = 7 * tps + ng
    assert n_in == 7 * MIX_W + 2 * d and MIX_W % tn == 0 and (2 * d) % tn == 0

    lane = np.arange(HEAD_W)
    gm = (lane[:, None] // QK_DIM == lane[None, :] // QK_DIM).astype(np.float32) / QK_DIM

    def hm_block(lo, n):
        return lambda i, j: (i // spb, jnp.clip(j - 1 - lo, 0, n - 1), i % spb, 0)

    return pl.pallas_call(
        functools.partial(_proj_kernel, tn=tn, nj=nj),
        out_shape=(
            jax.ShapeDtypeStruct((batch, 3 * HEADS, seq, HEAD_W), BF16),
            jax.ShapeDtypeStruct((batch, 4 * HEADS, seq, HEAD_W), BF16),
            jax.ShapeDtypeStruct((batch, HEADS, seq, HEAD_W), F32),
            jax.ShapeDtypeStruct((m, 2 * d), BF16),
        ),
        grid=(m // tm, nj + 1),
        in_specs=[
            pl.BlockSpec((tm, d), lambda i, j: (i, 0), pipeline_mode=pl.Buffered(1)),
            pl.BlockSpec((1, d), lambda i, j: (0, 0)),
            pl.BlockSpec((d, tn), lambda i, j: (0, jnp.minimum(j, nj - 1))),
            pl.BlockSpec((1, HEAD_W), lambda i, j: (0, 0)),
            pl.BlockSpec((1, HEAD_W), lambda i, j: (0, 0)),
            pl.BlockSpec((HEADS, HEAD_W), lambda i, j: (0, 0)),
            pl.BlockSpec((HEAD_W, HEAD_W), lambda i, j: (0, 0)),
        ],
        out_specs=(
            pl.BlockSpec((1, hpt, tm, HEAD_W), hm_block(0, 3 * tps)),
            pl.BlockSpec((1, hpt, tm, HEAD_W), hm_block(3 * tps, 4 * tps)),
            pl.BlockSpec((1, hpt, tm, HEAD_W), hm_block(4 * tps, tps)),
            pl.BlockSpec((tm, tn), lambda i, j: (i, jnp.clip(j - 1 - 7 * tps, 0, ng - 1))),
        ),
        scratch_shapes=[pltpu.VMEM((tm, d), BF16), pltpu.VMEM((tm, tn), F32)],
        compiler_params=_cparams(("parallel", "arbitrary")),
        name="proj",
    )(x, gain.reshape(1, d), w_in, qg, kg, lb, jnp.asarray(gm, BF16))


def _attn_kernel(lam_ref, off_ref, q_ref, k_ref, v_ref, bias_ref, sg_ref, o_ref,
                 qs_ref, vt_ref, m_ref, l_ref, acc_ref, *, t, nkb, online):
    h = pl.program_id(0)
    i = pl.program_id(2)
    lam = lam_ref[0]
    far_off = off_ref[h]

    @pl.when(i == 0)
    def _():
        for jb in range(nkb):
            vt_ref[jb] = v_ref[0, 0, jb * t:(jb + 1) * t, :].astype(F32).T.astype(BF16)

    q = q_ref[0, 0]
    lane = lax.broadcasted_iota(jnp.int32, (t, HEAD_W), 1)
    zero = jnp.zeros_like(q)
    qs_ref[0:t, :] = jnp.where(lane < QK_DIM, q, zero)
    qs_ref[t:2 * t, :] = jnp.where(lane >= QK_DIM, q, zero)
    if online:
        m_ref[...] = jnp.full(m_ref.shape, -jnp.inf, F32)
    l_ref[...] = jnp.zeros(l_ref.shape, F32)
    acc_ref[...] = jnp.zeros(acc_ref.shape, F32)

    def colsum8(p):
        return jnp.sum(p.reshape(t // 8, 8, 2 * t), axis=0)

    def step(j, badd):
        k = k_ref[0, 0, pl.ds(pl.multiple_of(j * t, t), t), :]
        s = _dot_nt(k, qs_ref[...]) + badd
        if online:
            m_old = m_ref[...]
            m_new = jnp.maximum(m_old, jnp.max(s, axis=0, keepdims=True))
            alpha = jnp.exp2(m_old - m_new)
            p = jnp.exp2(s - m_new)
            l_ref[...] = alpha * l_ref[...] + colsum8(p)
            acc_ref[...] = alpha * acc_ref[...] + _dot(vt_ref[j], p.astype(BF16))
            m_ref[...] = m_new
        else:
            p = jnp.exp2(s)
            l_ref[...] += colsum8(p)
            acc_ref[...] += _dot(vt_ref[j], p.astype(BF16))

    def far(j, carry):
        step(j, far_off)
        return carry

    lax.fori_loop(0, i - 1, far, 0)

    def both(x):
        return jnp.concatenate([x, x], axis=1)

    @pl.when(i >= 1)
    def _():
        step(i - 1, both(bias_ref[0, 1]))

    step(i, both(bias_ref[0, 0]))

    acc = acc_ref[...]
    l = jnp.sum(l_ref[...], axis=0, keepdims=True)
    o_t = acc[:, 0:t] / l[:, 0:t] - lam * (acc[:, t:2 * t] / l[:, t:2 * t])
    o_ref[...] = (_rms(o_t.T, sg_ref[...]) * (1.0 - LAMBDA_INIT)).astype(BF16)


def _t5_bucket(dist):
    n = jnp.maximum(dist, 0)
    max_exact = REL_BUCKETS // 2
    nf = jnp.maximum(n, 1).astype(F32)
    large = max_exact + (jnp.log(nf / max_exact) / math.log(REL_MAX_DIST / max_exact)
                         * (REL_BUCKETS - max_exact)).astype(jnp.int32)
    large = jnp.minimum(large, REL_BUCKETS - 1)
    return jnp.where(n < max_exact, n, large)


def _bias_tiles(f, t):
    hole = jnp.full((HEADS, 1), -jnp.inf, F32)
    w0 = jnp.concatenate([f[:, 0:t], jnp.full((HEADS, t), -jnp.inf, F32)], axis=1)
    w1 = jnp.concatenate([f[:, t:2 * t], hole, f[:, 1:t]], axis=1)
    w = jnp.stack([w0, w1], axis=1)
    x = jnp.tile(w, (1, 1, t))[:, :, :t * (2 * t - 1)].reshape(HEADS, 2, t, 2 * t - 1)
    return x[:, :, :, 0:t]


def _attn(qkv, lam, rel_bias, qk_bound, subln, *, batch, seq, t):
    assert t >= REL_MAX_DIST
    nq = seq // t
    rb = rel_bias.astype(F32)
    spread = 2.0 * qk_bound + jnp.max(jnp.max(rb, axis=0) - jnp.min(rb, axis=0))
    static_ok = spread <= MAX_STATIC_SOFTMAX_SPREAD
    upper = jnp.where(static_ok, qk_bound + jnp.max(rb, axis=0), 0.0)
    f = rb[_t5_bucket(jnp.arange(2 * t))].T - upper[:, None]
    bias = _bias_tiles(f * LOG2E, t)
    far_off = (rb[REL_BUCKETS - 1] - upper) * LOG2E

    def call(online):
        return pl.pallas_call(
            functools.partial(_attn_kernel, t=t, nkb=nq, online=online),
            out_shape=jax.ShapeDtypeStruct((batch * seq, MIX_W), BF16),
            grid=(HEADS, batch, nq),
            in_specs=[
                pl.BlockSpec(memory_space=pltpu.SMEM),
                pl.BlockSpec(memory_space=pltpu.SMEM),
                pl.BlockSpec((1, 1, t, HEAD_W), lambda h, b, i: (b, h, i, 0)),
                pl.BlockSpec((1, 1, seq, HEAD_W), lambda h, b, i: (b, HEADS + h, 0, 0)),
                pl.BlockSpec((1, 1, seq, HEAD_W), lambda h, b, i: (b, 2 * HEADS + h, 0, 0)),
                pl.BlockSpec((1, 2, t, t), lambda h, b, i: (h, 0, 0, 0)),
                pl.BlockSpec((1, HEAD_W), lambda h, b, i: (0, 0)),
            ],
            out_specs=pl.BlockSpec((t, HEAD_W), lambda h, b, i: (b * nq + i, h)),
            scratch_shapes=[
                pltpu.VMEM((2 * t, HEAD_W), BF16),
                pltpu.VMEM((nq, HEAD_W, t), BF16),
                pltpu.VMEM((1, 2 * t), F32),
                pltpu.VMEM((8, 2 * t), F32),
                pltpu.VMEM((HEAD_W, 2 * t), F32),
            ],
            compiler_params=_cparams(("parallel", "parallel", "arbitrary")),
            name="attn_online" if online else "attn",
        )(lam.reshape(1), far_off, qkv, qkv, qkv, bias, subln.reshape(1, HEAD_W))

    return lax.cond(static_ok, lambda: call(False), lambda: call(True))


SUBLANES = 8


def _hgrn_tables(c):
    levels = int(math.log2(c))
    assert 2 ** levels == c
    t = np.arange(c)[:, None]
    s = np.arange(c)[None, :]
    blocks = [(s <= t).astype(np.float32)]
    level_id = np.where(t == s, 0, -1)
    n_fine = 0
    for l in range(1, levels + 1):
        g, half = 2 ** l, 2 ** (l - 1)
        mid = (t // g) * g + half - 1
        upper = (t % g) >= half
        if half < SUBLANES:
            plus = upper & (s > mid) & (s <= t)
            minus = (~upper) & (s > t) & (s <= mid)
            blocks.append(plus.astype(np.float32) - minus.astype(np.float32))
            n_fine += 1
        pair = (t // g == s // g) & upper & ((s % g) < half)
        level_id = np.where(pair, l, level_id)
    w = np.concatenate(blocks, axis=0)
    return jnp.asarray(w, BF16), jnp.asarray(level_id, jnp.int32), levels, n_fine


def _hgrn_kernel(w_ref, lvl_ref, q_ref, k_ref, v_ref, og_ref, lf_ref, gn_ref, o_ref, *,
                 c, levels, n_fine, seq, unroll):
    gain = gn_ref[...]

    def cumsums(ci):
        rows = pl.ds(pl.multiple_of(ci * c, c), c)
        g = lf_ref[0, 0, rows, :]
        g_hi = g.astype(BF16)
        g_lo = (g - g_hi.astype(F32)).astype(BF16)
        e2 = _dot(w_ref[...], jnp.concatenate([g_hi, g_lo], axis=1))
        return rows, e2[:, 0:HEAD_W] + e2[:, HEAD_W:2 * HEAD_W]

    def scores(rows, e):
        qb = q_ref[0, 0, rows, :]
        kb = k_ref[0, 0, rows, :]
        q = qb.astype(F32)
        k = kb.astype(F32)
        b = e[0:c]
        lvl = lvl_ref[...]
        a = jnp.where(lvl == 0, _dot_nt(qb, kb), 0.0)
        for l in range(1, levels + 1):
            if l <= n_fine:
                d = e[l * c:(l + 1) * c]
            else:
                g, half = 2 ** l, 2 ** (l - 1)
                b3 = b.reshape(c // g, g, HEAD_W)
                d = (b3 - b3[:, half - 1:half, :]).reshape(c, HEAD_W)
            x = jnp.exp(-jnp.abs(d))
            a = jnp.where(lvl == l, _dot_nt((q * x).astype(BF16), (k * x).astype(BF16)), a)
        b_last = b[c - 1:c, :]
        q_in = (q * jnp.exp(b)).astype(BF16)
        k_out = (k * jnp.exp(b_last - b)).astype(BF16)
        return a.astype(BF16), q_in, k_out, jnp.exp(b_last)

    def body(it, state_t):
        st1 = [cumsums(it * unroll + u) for u in range(unroll)]
        st2 = [scores(rows, e) for rows, e in st1]
        st3 = []
        for (rows, _), (a, q_in, k_out, decay) in zip(st1, st2):
            v = v_ref[0, 0, rows, :]
            st3.append((rows, _dot(a, v), q_in, _dot_tn(v, k_out), decay))
        for rows, o_intra, q_in, ds_t, decay in st3:
            o = o_intra + _dot_nt(q_in, state_t.astype(BF16))
            state_t = state_t * decay + ds_t
            y = _rms(o, gain) * og_ref[0, 0, rows, :].astype(F32)
            o_ref[rows, :] = y.astype(BF16)
        return state_t

    lax.fori_loop(0, seq // (c * unroll), body, jnp.zeros((HEAD_W, HEAD_W), F32))


def _hgrn(hin, logf, gain, *, batch, seq, c, unroll):
    w, lvl, levels, n_fine = _hgrn_tables(c)
    assert seq % (c * unroll) == 0

    def head(slab):
        return pl.BlockSpec((1, 1, seq, HEAD_W), lambda b, h: (b, slab * HEADS + h, 0, 0))

    return pl.pallas_call(
        functools.partial(_hgrn_kernel, c=c, levels=levels, n_fine=n_fine, seq=seq, unroll=unroll),
        out_shape=jax.ShapeDtypeStruct((batch * seq, MIX_W), BF16),
        grid=(batch, HEADS),
        in_specs=[
            pl.BlockSpec(w.shape, lambda b, h: (0, 0)),
            pl.BlockSpec((c, c), lambda b, h: (0, 0)),
            head(0), head(1), head(2), head(3), head(0),
            pl.BlockSpec((1, HEAD_W), lambda b, h: (0, 0)),
        ],
        out_specs=pl.BlockSpec((seq, HEAD_W), lambda b, h: (b, h)),
        compiler_params=_cparams(("parallel", "parallel")),
        name="hgrn",
    )(w, lvl, hin, hin, hin, hin, logf, gain.reshape(1, HEAD_W))


def _merge_kernel(x_ref, ya_ref, yb_ref, gt_ref, wa_ref, wb_ref, wo_ref, o_ref):
    d = x_ref.shape[1]
    gt = gt_ref[...].astype(F32)
    merged = gt[:, 0:d] * _dot(ya_ref[...], wa_ref[...]) + gt[:, d:2 * d] * _dot(yb_ref[...], wb_ref[...])
    o_ref[...] = x_ref[...] + _dot(merged.astype(BF16), wo_ref[...])


def _resident(shape):
    return pl.BlockSpec(shape, lambda i: (0,) * len(shape), pipeline_mode=pl.Buffered(1))


def _merge(x, ya, yb, gates, wa, wb, wo, *, tm):
    m, d = x.shape
    return pl.pallas_call(
        _merge_kernel,
        out_shape=jax.ShapeDtypeStruct((m, d), F32),
        grid=(m // tm,),
        in_specs=[
            pl.BlockSpec((tm, d), lambda i: (i, 0)),
            pl.BlockSpec((tm, MIX_W), lambda i: (i, 0)),
            pl.BlockSpec((tm, MIX_W), lambda i: (i, 0)),
            pl.BlockSpec((tm, 2 * d), lambda i: (i, 0)),
            _resident(wa.shape), _resident(wb.shape), _resident(wo.shape),
        ],
        out_specs=pl.BlockSpec((tm, d), lambda i: (i, 0)),
        compiler_params=_cparams(("parallel",)),
        name="merge",
    )(x, ya, yb, gates, wa, wb, wo)


def _ple_kernel(x_ref, p_ref, gg_ref, pg_ref, wg_ref, wp_ref, o_ref):
    x = x_ref[...]
    gate = _sigmoid(_dot(_rms(x, gg_ref[...]).astype(BF16), wg_ref[...]))
    ple = _rms(_dot(p_ref[...].astype(BF16), wp_ref[...]), pg_ref[...])
    o_ref[...] = x + gate * ple


def _ple(x, p, gate_gain, post_gain, w_gate, w_proj, *, tm):
    m, d = x.shape
    pd = p.shape[1]
    return pl.pallas_call(
        _ple_kernel,
        out_shape=jax.ShapeDtypeStruct((m, d), F32),
        grid=(m // tm,),
        in_specs=[
            pl.BlockSpec((tm, d), lambda i: (i, 0)),
            pl.BlockSpec((tm, pd), lambda i: (i, 0)),
            pl.BlockSpec((1, d), lambda i: (0, 0)),
            pl.BlockSpec((1, d), lambda i: (0, 0)),
            _resident(w_gate.shape), _resident(w_proj.shape),
        ],
        out_specs=pl.BlockSpec((tm, d), lambda i: (i, 0)),
        compiler_params=_cparams(("parallel",)),
        name="ple",
    )(x, p, gate_gain.reshape(1, d), post_gain.reshape(1, d), w_gate, w_proj)


def _tiles(seq, d, f):
    def pick(n, choices):
        return next(c for c in choices if n % c == 0)
    return dict(
        tm=pick(seq, (512, 256, 128)),
        tfm=pick(seq, (1024, 512, 256, 128)),
        tf=pick(f, (512, 256, 128)),
        tp=pick(seq, (1024, 512, 256, 128)),
        tn=pick(math.gcd(MIX_W, 2 * d), (1024, 512, 256, 128)),
        ta=pick(seq, (512, 256, 128)),
        c=128,
        cu=4,
    )


def kernel(x, p, ffn1_norm, ffn1_w_gate, ffn1_w_up, ffn1_w_down, mix_norm, w_in, q_norm, k_norm, lambda_q1, lambda_k1, lambda_q2, lambda_k2, diff_subln, rel_bias, hgrn_lb_logits, hgrn_norm, w_branch_a, w_branch_b, w_out, ffn2_norm, ffn2_w_gate, ffn2_w_up, ffn2_w_down, ple_gate_norm, w_ple_gate, w_ple_proj, ple_post_norm):
    batch, seq, d = x.shape
    depth = ffn1_norm.shape[0]
    assert depth == 1
    m = batch * seq
    tl = _tiles(seq, d, ffn1_w_gate.shape[2])
    bf = lambda w: w.astype(BF16)

    lower_bounds = jnp.cumsum(jax.nn.softmax(hgrn_lb_logits.astype(F32), axis=0), axis=0)
    lb = lower_bounds[0].reshape(HEADS, HEAD_W)
    lam = (jnp.exp(jnp.sum(lambda_q1[0].astype(F32) * lambda_k1[0].astype(F32)))
           - jnp.exp(jnp.sum(lambda_q2[0].astype(F32) * lambda_k2[0].astype(F32)))
           + LAMBDA_INIT)
    scale = QK_DIM ** -0.5
    qg = jnp.tile(q_norm[0].astype(F32), 2).reshape(1, HEAD_W) * (scale * LOG2E)
    kg = jnp.tile(k_norm[0].astype(F32), 2).reshape(1, HEAD_W)
    qk_bound = (QK_DIM * scale * jnp.max(jnp.abs(q_norm[0].astype(F32)))
                * jnp.max(jnp.abs(k_norm[0].astype(F32))))

    x2 = x.reshape(m, d)
    x2 = _ffn(x2, ffn1_norm[0], bf(ffn1_w_gate[0]), bf(ffn1_w_up[0]), bf(ffn1_w_down[0]),
              tm=tl["tfm"], tf=tl["tf"])

    qkv, hin, logf, gates = _proj(x2, mix_norm[0], bf(w_in[0]), qg, kg, lb,
                                  batch=batch, seq=seq, tm=tl["tp"], tn=tl["tn"])
    ya = _attn(qkv, lam, rel_bias, qk_bound, diff_subln[0], batch=batch, seq=seq, t=tl["ta"])
    yb = _hgrn(hin, logf, hgrn_norm[0], batch=batch, seq=seq, c=tl["c"], unroll=tl["cu"])
    x2 = _merge(x2, ya, yb, gates, bf(w_branch_a[0]), bf(w_branch_b[0]), bf(w_out[0]), tm=tl["tm"])

    x2 = _ffn(x2, ffn2_norm[0], bf(ffn2_w_gate[0]), bf(ffn2_w_up[0]), bf(ffn2_w_down[0]),
              tm=tl["tfm"], tf=tl["tf"])
    x2 = _ple(x2, p[0].reshape(m, -1), ple_gate_norm[0], ple_post_norm[0],
              bf(w_ple_gate[0]), bf(w_ple_proj[0]), tm=tl["tm"])
    return x2.reshape(batch, seq, d)
```

```python
import functools
import math

import jax
import jax.numpy as jnp
import numpy as np
from jax import lax
from jax.experimental import pallas as pl
from jax.experimental.pallas import tpu as pltpu

F32 = jnp.float32
BF16 = jnp.bfloat16

EPS = 1e-6
HEADS = 8
HEAD_W = 128
QK_DIM = 64
MIX_W = HEADS * HEAD_W
REL_BUCKETS = 32
REL_MAX_DIST = 128
LAMBDA_INIT = 0.8 - 0.6 * math.exp(-0.3 * 0)
LOG2E = math.log2(math.e)
MAX_STATIC_SOFTMAX_SPREAD = 60.0

V7X_VMEM_BYTES = 64 * 1024 * 1024
VMEM_LIMIT = 56 * 1024 * 1024


def _cparams(sem, flags=None):
    return pltpu.CompilerParams(dimension_semantics=sem, vmem_limit_bytes=VMEM_LIMIT, flags=flags)


def _sigmoid(x):
    return 1.0 / (1.0 + jnp.exp(-x))


def _rms(x, gain):
    ms = jnp.mean(x * x, axis=-1, keepdims=True)
    return x * lax.rsqrt(ms + EPS) * gain


def _dot(a, b):
    return jnp.dot(a, b, preferred_element_type=F32)


def _dot_nt(a, b):
    return lax.dot_general(a, b, (((1,), (1,)), ((), ())), preferred_element_type=F32)


def _dot_tn(a, b):
    return lax.dot_general(a, b, (((0,), (0,)), ((), ())), preferred_element_type=F32)


def _ffn_kernel(x_ref, g_ref, wg_ref, wu_ref, wd_ref, o_ref, h_ref):
    j = pl.program_id(1)

    @pl.when(j == 0)
    def _():
        x = x_ref[...]
        h_ref[...] = _rms(x, g_ref[...]).astype(BF16)
        o_ref[...] = x

    h = h_ref[...]
    gate = _dot(h, wg_ref[...])
    up = _dot(h, wu_ref[...])
    act = (gate * _sigmoid(gate) * (0.5 * up)).astype(BF16)
    o_ref[...] += _dot(act, wd_ref[...])


def _ffn(x, gain, w_gate, w_up, w_down, *, tm, tf):
    m, d = x.shape
    f = w_gate.shape[1]
    return pl.pallas_call(
        _ffn_kernel,
        out_shape=jax.ShapeDtypeStruct((m, d), F32),
        grid=(m // tm, f // tf),
        in_specs=[
            pl.BlockSpec((tm, d), lambda i, j: (i, 0), pipeline_mode=pl.Buffered(1)),
            pl.BlockSpec((1, d), lambda i, j: (0, 0)),
            pl.BlockSpec((d, tf), lambda i, j: (0, j)),
            pl.BlockSpec((d, tf), lambda i, j: (0, j)),
            pl.BlockSpec((tf, d), lambda i, j: (j, 0)),
        ],
        out_specs=pl.BlockSpec((tm, d), lambda i, j: (i, 0)),
        scratch_shapes=[pltpu.VMEM((tm, d), BF16)],
        compiler_params=_cparams(("parallel", "arbitrary")),
        name="ffn",
    )(x, gain.reshape(1, d), w_gate, w_up, w_down)


def _proj_kernel(x_ref, g_ref, w_ref, qg_ref, kg_ref, lb_ref, gm_ref,
                 oqkv_ref, oh_ref, olf_ref, og_ref, h_ref, acc_ref, *, tn, nj):
    j = pl.program_id(1)
    jp = j - 1
    tps = MIX_W // tn
    hpt = tn // HEAD_W

    def heads(vals):
        return [vals[:, s * HEAD_W:(s + 1) * HEAD_W] for s in range(hpt)]

    def epi_qk(acc):
        gain = jnp.where(jp < tps, qg_ref[...], kg_ref[...])
        for s, a in enumerate(heads(acc)):
            ms = _dot((a * a).astype(BF16), gm_ref[...])
            oqkv_ref[0, s] = (a * lax.rsqrt(ms + EPS) * gain).astype(BF16)

    def epi_v(acc):
        for s, a in enumerate(heads(acc)):
            oqkv_ref[0, s] = a.astype(BF16)

    def epi_silu(acc):
        for s, a in enumerate(heads(acc * _sigmoid(acc))):
            oh_ref[0, s] = a.astype(BF16)

    def epi_forget(acc):
        jj = jp - 4 * tps
        for s, a in enumerate(heads(acc)):
            lb = lb_ref[pl.ds(jj * hpt + s, 1), :]
            forget = lb + (1.0 - lb) * _sigmoid(a)
            oh_ref[0, s] = (1.0 - forget).astype(BF16)
            olf_ref[0, s] = jnp.log2(forget)

    def epi_hv(acc):
        for s, a in enumerate(heads(acc)):
            oh_ref[0, s] = a.astype(BF16)

    def epi_gates(acc):
        og_ref[...] = _sigmoid(acc).astype(BF16)

    @pl.when(j == 0)
    def _():
        h_ref[...] = _rms(x_ref[...], g_ref[...]).astype(BF16)
        acc_ref[...] = _dot(h_ref[...], w_ref[...])

    def fused(cond, epi):
        @pl.when(cond & (j >= 1) & (j < nj))
        def _():
            new = _dot(h_ref[...], w_ref[...])
            epi(acc_ref[...])
            acc_ref[...] = new

    fused(jp < 2 * tps, epi_qk)
    fused((jp >= 2 * tps) & (jp < 3 * tps), epi_v)
    fused(((jp >= 3 * tps) & (jp < 4 * tps)) | ((jp >= 6 * tps) & (jp < 7 * tps)), epi_silu)
    fused((jp >= 4 * tps) & (jp < 5 * tps), epi_forget)
    fused((jp >= 5 * tps) & (jp < 6 * tps), epi_hv)
    fused(jp >= 7 * tps, epi_gates)

    @pl.when(j == nj)
    def _():
        epi_gates(acc_ref[...])


def _proj(x, gain, w_in, qg, kg, lb, *, batch, seq, tm, tn):
    m, d = x.shape
    n_in = w_in.shape[1]
    tps = MIX_W // tn
    hpt = tn // HEAD_W
    spb = seq // tm
    ng = 2 * d // tn
    nj = 7 * tps + ng
    assert n_in == 7 * MIX_W + 2 * d and MIX_W % tn == 0 and (2 * d) % tn == 0

    lane = np.arange(HEAD_W)
    gm = (lane[:, None] // QK_DIM == lane[None, :] // QK_DIM).astype(np.float32) / QK_DIM

    def hm_block(lo, n):
        return lambda i, j: (i // spb, jnp.clip(j - 1 - lo, 0, n - 1), i % spb, 0)

    return pl.pallas_call(
        functools.partial(_proj_kernel, tn=tn, nj=nj),
        out_shape=(
            jax.ShapeDtypeStruct((batch, 3 * HEADS, seq, HEAD_W), BF16),
            jax.ShapeDtypeStruct((batch, 4 * HEADS, seq, HEAD_W), BF16),
            jax.ShapeDtypeStruct((batch, HEADS, seq, HEAD_W), F32),
            jax.ShapeDtypeStruct((m, 2 * d), BF16),
        ),
        grid=(m // tm, nj + 1),
        in_specs=[
            pl.BlockSpec((tm, d), lambda i, j: (i, 0), pipeline_mode=pl.Buffered(1)),
            pl.BlockSpec((1, d), lambda i, j: (0, 0)),
            pl.BlockSpec((d, tn), lambda i, j: (0, jnp.minimum(j, nj - 1))),
            pl.BlockSpec((1, HEAD_W), lambda i, j: (0, 0)),
            pl.BlockSpec((1, HEAD_W), lambda i, j: (0, 0)),
            pl.BlockSpec((HEADS, HEAD_W), lambda i, j: (0, 0)),
            pl.BlockSpec((HEAD_W, HEAD_W), lambda i, j: (0, 0)),
        ],
        out_specs=(
            pl.BlockSpec((1, hpt, tm, HEAD_W), hm_block(0, 3 * tps)),
            pl.BlockSpec((1, hpt, tm, HEAD_W), hm_block(3 * tps, 4 * tps)),
            pl.BlockSpec((1, hpt, tm, HEAD_W), hm_block(4 * tps, tps)),
            pl.BlockSpec((tm, tn), lambda i, j: (i, jnp.clip(j - 1 - 7 * tps, 0, ng - 1))),
        ),
        scratch_shapes=[pltpu.VMEM((tm, d), BF16), pltpu.VMEM((tm, tn), F32)],
        compiler_params=_cparams(("parallel", "arbitrary")),
        name="proj",
    )(x, gain.reshape(1, d), w_in, qg, kg, lb, jnp.asarray(gm, BF16))


def _attn_kernel(lam_ref, off_ref, q_ref, k_ref, v_ref, bias_ref, sg_ref, o_ref,
                 vt_ref, *, t, nkb, online):
    h = pl.program_id(0)
    i = pl.program_id(2)
    lam = lam_ref[0]
    far_off = off_ref[h]

    @pl.when(i == 0)
    def _():
        for jb in range(nkb):
            vt_ref[jb] = v_ref[0, 0, jb * t:(jb + 1) * t, :].astype(F32).T.astype(BF16)

    def colsum8(p):
        return jnp.sum(p.reshape(t // 8, 8, 2 * t), axis=0)

    def both(x):
        return jnp.concatenate([x, x], axis=1)

    def query_block(iq):
        q = q_ref[0, 0]
        lane = lax.broadcasted_iota(jnp.int32, (t, HEAD_W), 1)
        zero = jnp.zeros_like(q)
        qs = jnp.concatenate([jnp.where(lane < QK_DIM, q, zero),
                              jnp.where(lane >= QK_DIM, q, zero)], axis=0)
        offsets = [far_off] * max(iq - 1, 0)
        if iq >= 1:
            offsets.append(both(bias_ref[0, 1]))
        offsets.append(both(bias_ref[0, 0]))
        m = l = acc = None
        for j, badd in enumerate(offsets):
            s = _dot_nt(k_ref[0, 0, j * t:(j + 1) * t, :], qs) + badd
            if online:
                bmax = jnp.max(s, axis=0, keepdims=True)
                m_new = bmax if m is None else jnp.maximum(m, bmax)
                p = jnp.exp2(s - m_new)
            else:
                p = jnp.exp2(s)
            psum = colsum8(p)
            pv = _dot(vt_ref[j], p.astype(BF16))
            if l is None:
                l, acc = psum, pv
            elif online:
                alpha = jnp.exp2(m - m_new)
                l, acc = alpha * l + psum, alpha * acc + pv
            else:
                l, acc = l + psum, acc + pv
            if online:
                m = m_new
        l = jnp.sum(l, axis=0, keepdims=True)
        o_t = acc[:, 0:t] / l[:, 0:t] - lam * (acc[:, t:2 * t] / l[:, t:2 * t])
        o_ref[...] = (_rms(o_t.T, sg_ref[...]) * (1.0 - LAMBDA_INIT)).astype(BF16)

    for iq in range(nkb):
        pl.when(i == iq)(functools.partial(query_block, iq))


def _t5_bucket(dist):
    n = jnp.maximum(dist, 0)
    max_exact = REL_BUCKETS // 2
    nf = jnp.maximum(n, 1).astype(F32)
    large = max_exact + (jnp.log(nf / max_exact) / math.log(REL_MAX_DIST / max_exact)
                         * (REL_BUCKETS - max_exact)).astype(jnp.int32)
    large = jnp.minimum(large, REL_BUCKETS - 1)
    return jnp.where(n < max_exact, n, large)


def _bias_tiles(f, t):
    hole = jnp.full((HEADS, 1), -jnp.inf, F32)
    w0 = jnp.concatenate([f[:, 0:t], jnp.full((HEADS, t), -jnp.inf, F32)], axis=1)
    w1 = jnp.concatenate([f[:, t:2 * t], hole, f[:, 1:t]], axis=1)
    w = jnp.stack([w0, w1], axis=1)
    x = jnp.tile(w, (1, 1, t))[:, :, :t * (2 * t - 1)].reshape(HEADS, 2, t, 2 * t - 1)
    return x[:, :, :, 0:t]


def _attn(qkv, lam, rel_bias, qk_bound, subln, *, batch, seq, t):
    assert t >= REL_MAX_DIST
    nq = seq // t
    rb = rel_bias.astype(F32)
    spread = 2.0 * qk_bound + jnp.max(jnp.max(rb, axis=0) - jnp.min(rb, axis=0))
    static_ok = spread <= MAX_STATIC_SOFTMAX_SPREAD
    upper = jnp.where(static_ok, qk_bound + jnp.max(rb, axis=0), 0.0)
    f = rb[_t5_bucket(jnp.arange(2 * t))].T - upper[:, None]
    bias = _bias_tiles(f * LOG2E, t)
    far_off = (rb[REL_BUCKETS - 1] - upper) * LOG2E

    def call(online):
        return pl.pallas_call(
            functools.partial(_attn_kernel, t=t, nkb=nq, online=online),
            out_shape=jax.ShapeDtypeStruct((batch * seq, MIX_W), BF16),
            grid=(HEADS, batch, nq),
            in_specs=[
                pl.BlockSpec(memory_space=pltpu.SMEM),
                pl.BlockSpec(memory_space=pltpu.SMEM),
                pl.BlockSpec((1, 1, t, HEAD_W), lambda h, b, i: (b, h, i, 0)),
                pl.BlockSpec((1, 1, seq, HEAD_W), lambda h, b, i: (b, HEADS + h, 0, 0)),
                pl.BlockSpec((1, 1, seq, HEAD_W), lambda h, b, i: (b, 2 * HEADS + h, 0, 0)),
                pl.BlockSpec((1, 2, t, t), lambda h, b, i: (h, 0, 0, 0)),
                pl.BlockSpec((1, HEAD_W), lambda h, b, i: (0, 0)),
            ],
            out_specs=pl.BlockSpec((t, HEAD_W), lambda h, b, i: (b * nq + i, h)),
            scratch_shapes=[pltpu.VMEM((nq, HEAD_W, t), BF16)],
            compiler_params=_cparams(("parallel", "parallel", "arbitrary")),
            name="attn_online" if online else "attn",
        )(lam.reshape(1), far_off, qkv, qkv, qkv, bias, subln.reshape(1, HEAD_W))

    return lax.cond(static_ok, lambda: call(False), lambda: call(True))


SUBLANES = 8


def _hgrn_tables(c):
    levels = int(math.log2(c))
    assert 2 ** levels == c
    t = np.arange(c)[:, None]
    s = np.arange(c)[None, :]
    blocks = [(s <= t).astype(np.float32)]
    level_id = np.where(t == s, 0, -1)
    n_fine = 0
    for l in range(1, levels + 1):
        g, half = 2 ** l, 2 ** (l - 1)
        mid = (t // g) * g + half - 1
        upper = (t % g) >= half
        if half < SUBLANES:
            above = upper & (s > mid) & (s <= t)
            below = (~upper) & (s > t) & (s <= mid)
            blocks.append((above | below).astype(np.float32))
            n_fine += 1
        pair = (t // g == s // g) & upper & ((s % g) < half)
        level_id = np.where(pair, l, level_id)
    w = np.concatenate(blocks, axis=0)
    return jnp.asarray(w, BF16), jnp.asarray(level_id, jnp.int32), levels, n_fine


def _hgrn_kernel(w_ref, lvl_ref, q_ref, k_ref, v_ref, og_ref, lf_ref, gn_ref, o_ref, *,
                 c, levels, n_fine, seq, unroll):
    gain = gn_ref[...]

    def cumsums(ci):
        rows = pl.ds(pl.multiple_of(ci * c, c), c)
        g = lf_ref[0, 0, rows, :]
        g_hi = g.astype(BF16)
        g_lo = (g - g_hi.astype(F32)).astype(BF16)
        e2 = _dot(w_ref[...], jnp.concatenate([g_hi, g_lo], axis=1))
        return rows, e2[:, 0:HEAD_W] + e2[:, HEAD_W:2 * HEAD_W]

    def scores(rows, e):
        qb = q_ref[0, 0, rows, :]
        kb = k_ref[0, 0, rows, :]
        q = qb.astype(F32)
        k = kb.astype(F32)
        b = e[0:c]
        lvl = lvl_ref[...]
        a = jnp.where(lvl == 0, _dot_nt(qb, kb), 0.0)
        for l in range(1, levels + 1):
            if l <= n_fine:
                d = e[l * c:(l + 1) * c]
            else:
                g, half = 2 ** l, 2 ** (l - 1)
                b3 = b.reshape(c // g, g, HEAD_W)
                mid = b3[:, half - 1:half, :]
                d = jnp.concatenate([mid - b3[:, 0:half, :], b3[:, half:g, :] - mid],
                                    axis=1).reshape(c, HEAD_W)
            x = jnp.exp2(d)
            a = jnp.where(lvl == l, _dot_nt((q * x).astype(BF16), (k * x).astype(BF16)), a)
        b_last = b[c - 1:c, :]
        q_in = (q * jnp.exp2(b)).astype(BF16)
        k_out = (k * jnp.exp2(b_last - b)).astype(BF16)
        return a.astype(BF16), q_in, k_out, jnp.exp2(b_last)

    def body(it, state_t):
        st1 = [cumsums(it * unroll + u) for u in range(unroll)]
        st2 = [scores(rows, e) for rows, e in st1]
        st3 = []
        for (rows, _), (a, q_in, k_out, decay) in zip(st1, st2):
            v = v_ref[0, 0, rows, :]
            st3.append((rows, _dot(a, v), q_in, _dot_tn(v, k_out), decay))
        for rows, o_intra, q_in, ds_t, decay in st3:
            o = o_intra + _dot_nt(q_in, state_t.astype(BF16))
            state_t = state_t * decay + ds_t
            y = _rms(o, gain) * og_ref[0, 0, rows, :].astype(F32)
            o_ref[rows, :] = y.astype(BF16)
        return state_t

    lax.fori_loop(0, seq // (c * unroll), body, jnp.zeros((HEAD_W, HEAD_W), F32))


def _hgrn(hin, logf, gain, *, batch, seq, c, unroll):
    w, lvl, levels, n_fine = _hgrn_tables(c)
    assert seq % (c * unroll) == 0

    def head(slab):
        return pl.BlockSpec((1, 1, seq, HEAD_W), lambda b, h: (b, slab * HEADS + h, 0, 0))

    return pl.pallas_call(
        functools.partial(_hgrn_kernel, c=c, levels=levels, n_fine=n_fine, seq=seq, unroll=unroll),
        out_shape=jax.ShapeDtypeStruct((batch * seq, MIX_W), BF16),
        grid=(batch, HEADS),
        in_specs=[
            pl.BlockSpec(w.shape, lambda b, h: (0, 0)),
            pl.BlockSpec((c, c), lambda b, h: (0, 0)),
            head(0), head(1), head(2), head(3), head(0),
            pl.BlockSpec((1, HEAD_W), lambda b, h: (0, 0)),
        ],
        out_specs=pl.BlockSpec((seq, HEAD_W), lambda b, h: (b, h)),
        compiler_params=_cparams(("parallel", "parallel")),
        name="hgrn",
    )(w, lvl, hin, hin, hin, hin, logf, gain.reshape(1, HEAD_W))


def _merge_kernel(x_ref, ya_ref, yb_ref, gt_ref, wa_ref, wb_ref, wo_ref, o_ref):
    d = x_ref.shape[1]
    gt = gt_ref[...].astype(F32)
    merged = gt[:, 0:d] * _dot(ya_ref[...], wa_ref[...]) + gt[:, d:2 * d] * _dot(yb_ref[...], wb_ref[...])
    o_ref[...] = x_ref[...] + _dot(merged.astype(BF16), wo_ref[...])


def _resident(shape):
    return pl.BlockSpec(shape, lambda i: (0,) * len(shape), pipeline_mode=pl.Buffered(1))


def _merge(x, ya, yb, gates, wa, wb, wo, *, tm):
    m, d = x.shape
    return pl.pallas_call(
        _merge_kernel,
        out_shape=jax.ShapeDtypeStruct((m, d), F32),
        grid=(m // tm,),
        in_specs=[
            pl.BlockSpec((tm, d), lambda i: (i, 0)),
            pl.BlockSpec((tm, MIX_W), lambda i: (i, 0)),
            pl.BlockSpec((tm, MIX_W), lambda i: (i, 0)),
            pl.BlockSpec((tm, 2 * d), lambda i: (i, 0)),
            _resident(wa.shape), _resident(wb.shape), _resident(wo.shape),
        ],
        out_specs=pl.BlockSpec((tm, d), lambda i: (i, 0)),
        compiler_params=_cparams(("parallel",)),
        name="merge",
    )(x, ya, yb, gates, wa, wb, wo)


def _ple_kernel(x_ref, p_ref, gg_ref, pg_ref, wg_ref, wp_ref, o_ref):
    x = x_ref[...]
    gate = _sigmoid(_dot(_rms(x, gg_ref[...]).astype(BF16), wg_ref[...]))
    ple = _rms(_dot(p_ref[...].astype(BF16), wp_ref[...]), pg_ref[...])
    o_ref[...] = x + gate * ple


def _ple(x, p, gate_gain, post_gain, w_gate, w_proj, *, tm):
    m, d = x.shape
    pd = p.shape[1]
    return pl.pallas_call(
        _ple_kernel,
        out_shape=jax.ShapeDtypeStruct((m, d), F32),
        grid=(m // tm,),
        in_specs=[
            pl.BlockSpec((tm, d), lambda i: (i, 0)),
            pl.BlockSpec((tm, pd), lambda i: (i, 0)),
            pl.BlockSpec((1, d), lambda i: (0, 0)),
            pl.BlockSpec((1, d), lambda i: (0, 0)),
            _resident(w_gate.shape), _resident(w_proj.shape),
        ],
        out_specs=pl.BlockSpec((tm, d), lambda i: (i, 0)),
        compiler_params=_cparams(("parallel",)),
        name="ple",
    )(x, p, gate_gain.reshape(1, d), post_gain.reshape(1, d), w_gate, w_proj)


def _tiles(seq, d, f):
    def pick(n, choices):
        return next(c for c in choices if n % c == 0)
    return dict(
        tm=pick(seq, (512, 256, 128)),
        tfm=pick(seq, (1024, 512, 256, 128)),
        tf=pick(f, (512, 256, 128)),
        tp=pick(seq, (1024, 512, 256, 128)),
        tn=pick(math.gcd(MIX_W, 2 * d), (1024, 512, 256, 128)),
        ta=pick(seq, (512, 256, 128)),
        c=128,
        cu=16,
    )


def kernel(x, p, ffn1_norm, ffn1_w_gate, ffn1_w_up, ffn1_w_down, mix_norm, w_in, q_norm, k_norm, lambda_q1, lambda_k1, lambda_q2, lambda_k2, diff_subln, rel_bias, hgrn_lb_logits, hgrn_norm, w_branch_a, w_branch_b, w_out, ffn2_norm, ffn2_w_gate, ffn2_w_up, ffn2_w_down, ple_gate_norm, w_ple_gate, w_ple_proj, ple_post_norm):
    batch, seq, d = x.shape
    depth = ffn1_norm.shape[0]
    assert depth == 1
    m = batch * seq
    tl = _tiles(seq, d, ffn1_w_gate.shape[2])
    bf = lambda w: w.astype(BF16)

    lower_bounds = jnp.cumsum(jax.nn.softmax(hgrn_lb_logits.astype(F32), axis=0), axis=0)
    lb = lower_bounds[0].reshape(HEADS, HEAD_W)
    lam = (jnp.exp(jnp.sum(lambda_q1[0].astype(F32) * lambda_k1[0].astype(F32)))
           - jnp.exp(jnp.sum(lambda_q2[0].astype(F32) * lambda_k2[0].astype(F32)))
           + LAMBDA_INIT)
    scale = QK_DIM ** -0.5
    qg = jnp.tile(q_norm[0].astype(F32), 2).reshape(1, HEAD_W) * (scale * LOG2E)
    kg = jnp.tile(k_norm[0].astype(F32), 2).reshape(1, HEAD_W)
    qk_bound = (QK_DIM * scale * jnp.max(jnp.abs(q_norm[0].astype(F32)))
                * jnp.max(jnp.abs(k_norm[0].astype(F32))))

    x2 = x.reshape(m, d)
    x2 = _ffn(x2, ffn1_norm[0], bf(ffn1_w_gate[0]), bf(ffn1_w_up[0]), bf(ffn1_w_down[0]),
              tm=tl["tfm"], tf=tl["tf"])

    qkv, hin, logf, gates = _proj(x2, mix_norm[0], bf(w_in[0]), qg, kg, lb,
                                  batch=batch, seq=seq, tm=tl["tp"], tn=tl["tn"])
    ya = _attn(qkv, lam, rel_bias, qk_bound, diff_subln[0], batch=batch, seq=seq, t=tl["ta"])
    yb = _hgrn(hin, logf, hgrn_norm[0], batch=batch, seq=seq, c=tl["c"], unroll=tl["cu"])
    x2 = _merge(x2, ya, yb, gates, bf(w_branch_a[0]), bf(w_branch_b[0]), bf(w_out[0]), tm=tl["tm"])

    x2 = _ffn(x2, ffn2_norm[0], bf(ffn2_w_gate[0]), bf(ffn2_w_up[0]), bf(ffn2_w_down[0]),
              tm=tl["tfm"], tf=tl["tf"])
    x2 = _ple(x2, p[0].reshape(m, -1), ple_gate_norm[0], ple_post_norm[0],
              bf(w_ple_gate[0]), bf(w_ple_proj[0]), tm=tl["tm"])
    return x2.reshape(batch, seq, d)
```

```python
import functools
import math

import jax
import jax.numpy as jnp
import numpy as np
from jax import lax
from jax.experimental import pallas as pl
from jax.experimental.pallas import tpu as pltpu

F32 = jnp.float32
BF16 = jnp.bfloat16

EPS = 1e-6
HEADS = 8
HEAD_W = 128
QK_DIM = 64
MIX_W = HEADS * HEAD_W
REL_BUCKETS = 32
REL_MAX_DIST = 128
LAMBDA_INIT = 0.8 - 0.6 * math.exp(-0.3 * 0)
LOG2E = math.log2(math.e)
MAX_STATIC_SOFTMAX_SPREAD = 60.0

V7X_VMEM_BYTES = 64 * 1024 * 1024
VMEM_LIMIT = 56 * 1024 * 1024


def _cparams(sem, flags=None):
    return pltpu.CompilerParams(dimension_semantics=sem, vmem_limit_bytes=VMEM_LIMIT, flags=flags)


def _sigmoid(x):
    return 1.0 / (1.0 + jnp.exp(-x))


def _rms(x, gain):
    ms = jnp.mean(x * x, axis=-1, keepdims=True)
    return x * lax.rsqrt(ms + EPS) * gain


def _dot(a, b):
    return jnp.dot(a, b, preferred_element_type=F32)


def _dot_nt(a, b):
    return lax.dot_general(a, b, (((1,), (1,)), ((), ())), preferred_element_type=F32)


def _dot_tn(a, b):
    return lax.dot_general(a, b, (((0,), (0,)), ((), ())), preferred_element_type=F32)


def _row_tile_fetch(x_hbm, xs_ref, sem, tm):
    i = pl.program_id(0)
    j = pl.program_id(1)

    def copy(row_tile):
        return pltpu.make_async_copy(x_hbm.at[pl.ds(row_tile * tm, tm), :], xs_ref, sem)

    @pl.when((i == 0) & (j == 0))
    def _():
        copy(0).start()

    @pl.when((j == 1) & (i + 1 < pl.num_programs(0)))
    def _():
        copy(i + 1).start()

    return copy(i)


def _ffn_kernel(x_hbm, g_ref, wg_ref, wu_ref, wd_ref, o_ref, xs_ref, h_ref, sem, *, tm):
    j = pl.program_id(1)
    x_copy = _row_tile_fetch(x_hbm, xs_ref, sem, tm)

    @pl.when(j == 0)
    def _():
        x_copy.wait()
        x = xs_ref[...]
        h_ref[...] = _rms(x, g_ref[...]).astype(BF16)
        o_ref[...] = x

    h = h_ref[...]
    gate = _dot(h, wg_ref[...])
    up = _dot(h, wu_ref[...])
    act = (gate * _sigmoid(gate) * (0.5 * up)).astype(BF16)
    o_ref[...] += _dot(act, wd_ref[...])


def _ffn(x, gain, w_gate, w_up, w_down, *, tm, tf):
    m, d = x.shape
    f = w_gate.shape[1]
    assert f // tf >= 2
    return pl.pallas_call(
        functools.partial(_ffn_kernel, tm=tm),
        out_shape=jax.ShapeDtypeStruct((m, d), F32),
        grid=(m // tm, f // tf),
        in_specs=[
            pl.BlockSpec(memory_space=pl.ANY),
            pl.BlockSpec((1, d), lambda i, j: (0, 0)),
            pl.BlockSpec((d, tf), lambda i, j: (0, j)),
            pl.BlockSpec((d, tf), lambda i, j: (0, j)),
            pl.BlockSpec((tf, d), lambda i, j: (j, 0)),
        ],
        out_specs=pl.BlockSpec((tm, d), lambda i, j: (i, 0)),
        scratch_shapes=[pltpu.VMEM((tm, d), F32), pltpu.VMEM((tm, d), BF16),
                        pltpu.SemaphoreType.DMA(())],
        compiler_params=_cparams(("arbitrary", "arbitrary")),
        name="ffn",
    )(x, gain.reshape(1, d), w_gate, w_up, w_down)


def _proj_kernel(x_hbm, g_ref, w_ref, qg_ref, kg_ref, lb_ref, gm_ref,
                 oqkv_ref, oh_ref, olf_ref, og_ref, xs_ref, h_ref, acc_ref, sem, *, tm, tn, nj):
    j = pl.program_id(1)
    jp = j - 1
    tps = MIX_W // tn
    hpt = tn // HEAD_W

    def heads(vals):
        return [vals[:, s * HEAD_W:(s + 1) * HEAD_W] for s in range(hpt)]

    def epi_qk(acc):
        gain = jnp.where(jp < tps, qg_ref[...], kg_ref[...])
        for s, a in enumerate(heads(acc)):
            ms = _dot((a * a).astype(BF16), gm_ref[...])
            oqkv_ref[0, s] = (a * lax.rsqrt(ms + EPS) * gain).astype(BF16)

    def epi_v(acc):
        for s, a in enumerate(heads(acc)):
            oqkv_ref[0, s] = a.astype(BF16)

    def epi_silu(acc):
        for s, a in enumerate(heads(acc * _sigmoid(acc))):
            oh_ref[0, s] = a.astype(BF16)

    def epi_forget(acc):
        jj = jp - 4 * tps
        for s, a in enumerate(heads(acc)):
            lb = lb_ref[pl.ds(jj * hpt + s, 1), :]
            forget = lb + (1.0 - lb) * _sigmoid(a)
            oh_ref[0, s] = (1.0 - forget).astype(BF16)
            olf_ref[0, s] = jnp.log2(forget)

    def epi_hv(acc):
        for s, a in enumerate(heads(acc)):
            oh_ref[0, s] = a.astype(BF16)

    def epi_gates(acc):
        og_ref[...] = _sigmoid(acc).astype(BF16)

    x_copy = _row_tile_fetch(x_hbm, xs_ref, sem, tm)

    @pl.when(j == 0)
    def _():
        x_copy.wait()
        h_ref[...] = _rms(xs_ref[...], g_ref[...]).astype(BF16)
        acc_ref[...] = _dot(h_ref[...], w_ref[...])

    def fused(cond, epi):
        @pl.when(cond & (j >= 1) & (j < nj))
        def _():
            new = _dot(h_ref[...], w_ref[...])
            epi(acc_ref[...])
            acc_ref[...] = new

    fused(jp < 2 * tps, epi_qk)
    fused((jp >= 2 * tps) & (jp < 3 * tps), epi_v)
    fused(((jp >= 3 * tps) & (jp < 4 * tps)) | ((jp >= 6 * tps) & (jp < 7 * tps)), epi_silu)
    fused((jp >= 4 * tps) & (jp < 5 * tps), epi_forget)
    fused((jp >= 5 * tps) & (jp < 6 * tps), epi_hv)
    fused(jp >= 7 * tps, epi_gates)

    @pl.when(j == nj)
    def _():
        epi_gates(acc_ref[...])


def _proj(x, gain, w_in, qg, kg, lb, *, batch, seq, tm, tn):
    m, d = x.shape
    n_in = w_in.shape[1]
    tps = MIX_W // tn
    hpt = tn // HEAD_W
    spb = seq // tm
    ng = 2 * d // tn
    nj = 7 * tps + ng
    assert n_in == 7 * MIX_W + 2 * d and MIX_W % tn == 0 and (2 * d) % tn == 0

    lane = np.arange(HEAD_W)
    gm = (lane[:, None] // QK_DIM == lane[None, :] // QK_DIM).astype(np.float32) / QK_DIM

    def hm_block(lo, n):
        return lambda i, j: (i // spb, jnp.clip(j - 1 - lo, 0, n - 1), i % spb, 0)

    return pl.pallas_call(
        functools.partial(_proj_kernel, tm=tm, tn=tn, nj=nj),
        out_shape=(
            jax.ShapeDtypeStruct((batch, 3 * HEADS, seq, HEAD_W), BF16),
            jax.ShapeDtypeStruct((batch, 4 * HEADS, seq, HEAD_W), BF16),
            jax.ShapeDtypeStruct((batch, HEADS, seq, HEAD_W), F32),
            jax.ShapeDtypeStruct((m, 2 * d), BF16),
        ),
        grid=(m // tm, nj + 1),
        in_specs=[
            pl.BlockSpec(memory_space=pl.ANY),
            pl.BlockSpec((1, d), lambda i, j: (0, 0)),
            pl.BlockSpec((d, tn), lambda i, j: (0, jnp.minimum(j, nj - 1))),
            pl.BlockSpec((1, HEAD_W), lambda i, j: (0, 0)),
            pl.BlockSpec((1, HEAD_W), lambda i, j: (0, 0)),
            pl.BlockSpec((HEADS, HEAD_W), lambda i, j: (0, 0)),
            pl.BlockSpec((HEAD_W, HEAD_W), lambda i, j: (0, 0)),
        ],
        out_specs=(
            pl.BlockSpec((1, hpt, tm, HEAD_W), hm_block(0, 3 * tps)),
            pl.BlockSpec((1, hpt, tm, HEAD_W), hm_block(3 * tps, 4 * tps)),
            pl.BlockSpec((1, hpt, tm, HEAD_W), hm_block(4 * tps, tps)),
            pl.BlockSpec((tm, tn), lambda i, j: (i, jnp.clip(j - 1 - 7 * tps, 0, ng - 1))),
        ),
        scratch_shapes=[pltpu.VMEM((tm, d), F32), pltpu.VMEM((tm, d), BF16),
                        pltpu.VMEM((tm, tn), F32), pltpu.SemaphoreType.DMA(())],
        compiler_params=_cparams(("arbitrary", "arbitrary")),
        name="proj",
    )(x, gain.reshape(1, d), w_in, qg, kg, lb, jnp.asarray(gm, BF16))


def _attn_kernel(lam_ref, off_ref, q_ref, k_ref, v_ref, w_ref, sg_ref, o_ref,
                 vt_ref, bias_ref, *, t, nkb, online):
    h = pl.program_id(0)
    b = pl.program_id(1)
    i = pl.program_id(2)
    lam = lam_ref[0]
    far_off = off_ref[h]

    @pl.when((b == 0) & (i == 0))
    def _():
        for dd in range(2):
            rows = jnp.broadcast_to(w_ref[0, dd:dd + 1, :], (t, 2 * t))
            bias_ref[dd] = pltpu.roll(rows, 0, 1, stride=1, stride_axis=0)[:, 0:t]

    @pl.when(i == 0)
    def _():
        for jb in range(nkb):
            vt_ref[jb] = v_ref[0, 0, jb * t:(jb + 1) * t, :].astype(F32).T.astype(BF16)

    def colsum8(p):
        return jnp.sum(p.reshape(t // 8, 8, 2 * t), axis=0)

    def both(x):
        return jnp.concatenate([x, x], axis=1)

    def query_block(iq):
        q = q_ref[0, 0]
        lane = lax.broadcasted_iota(jnp.int32, (t, HEAD_W), 1)
        zero = jnp.zeros_like(q)
        qs = jnp.concatenate([jnp.where(lane < QK_DIM, q, zero),
                              jnp.where(lane >= QK_DIM, q, zero)], axis=0)
        offsets = [far_off] * max(iq - 1, 0)
        if iq >= 1:
            offsets.append(both(bias_ref[1]))
        offsets.append(both(bias_ref[0]))
        m = l = acc = None
        for j, badd in enumerate(offsets):
            s = _dot_nt(k_ref[0, 0, j * t:(j + 1) * t, :], qs) + badd
            if online:
                bmax = jnp.max(s, axis=0, keepdims=True)
                m_new = bmax if m is None else jnp.maximum(m, bmax)
                p = jnp.exp2(s - m_new)
            else:
                p = jnp.exp2(s)
            psum = colsum8(p)
            pv = _dot(vt_ref[j], p.astype(BF16))
            if l is None:
                l, acc = psum, pv
            elif online:
                alpha = jnp.exp2(m - m_new)
                l, acc = alpha * l + psum, alpha * acc + pv
            else:
                l, acc = l + psum, acc + pv
            if online:
                m = m_new
        l = jnp.sum(l, axis=0, keepdims=True)
        o_t = acc[:, 0:t] / l[:, 0:t] - lam * (acc[:, t:2 * t] / l[:, t:2 * t])
        o_ref[...] = (_rms(o_t.T, sg_ref[...]) * (1.0 - LAMBDA_INIT)).astype(BF16)

    for iq in range(nkb):
        pl.when(i == iq)(functools.partial(query_block, iq))


def _t5_bucket(dist):
    n = jnp.maximum(dist, 0)
    max_exact = REL_BUCKETS // 2
    nf = jnp.maximum(n, 1).astype(F32)
    large = max_exact + (jnp.log(nf / max_exact) / math.log(REL_MAX_DIST / max_exact)
                         * (REL_BUCKETS - max_exact)).astype(jnp.int32)
    large = jnp.minimum(large, REL_BUCKETS - 1)
    return jnp.where(n < max_exact, n, large)


def _bias_generators(f, t):
    hole = jnp.full((HEADS, 1), -jnp.inf, F32)
    w0 = jnp.concatenate([f[:, 0:t], jnp.full((HEADS, t), -jnp.inf, F32)], axis=1)
    w1 = jnp.concatenate([f[:, t:2 * t], hole, f[:, 1:t]], axis=1)
    return jnp.stack([w0, w1], axis=1)


def _attn(qkv, lam, rel_bias, qk_bound, subln, *, batch, seq, t):
    assert t >= REL_MAX_DIST
    nq = seq // t
    rb = rel_bias.astype(F32)
    spread = 2.0 * qk_bound + jnp.max(jnp.max(rb, axis=0) - jnp.min(rb, axis=0))
    static_ok = spread <= MAX_STATIC_SOFTMAX_SPREAD
    upper = jnp.where(static_ok, qk_bound + jnp.max(rb, axis=0), 0.0)
    f = rb[_t5_bucket(jnp.arange(2 * t))].T - upper[:, None]
    bias = _bias_generators(f * LOG2E, t)
    far_off = (rb[REL_BUCKETS - 1] - upper) * LOG2E

    def call(online):
        return pl.pallas_call(
            functools.partial(_attn_kernel, t=t, nkb=nq, online=online),
            out_shape=jax.ShapeDtypeStruct((batch * seq, MIX_W), BF16),
            grid=(HEADS, batch, nq),
            in_specs=[
                pl.BlockSpec(memory_space=pltpu.SMEM),
                pl.BlockSpec(memory_space=pltpu.SMEM),
                pl.BlockSpec((1, 1, t, HEAD_W), lambda h, b, i: (b, h, i, 0)),
                pl.BlockSpec((1, 1, seq, HEAD_W), lambda h, b, i: (b, HEADS + h, 0, 0)),
                pl.BlockSpec((1, 1, seq, HEAD_W), lambda h, b, i: (b, 2 * HEADS + h, 0, 0)),
                pl.BlockSpec((1, 2, 2 * t), lambda h, b, i: (h, 0, 0)),
                pl.BlockSpec((1, HEAD_W), lambda h, b, i: (0, 0)),
            ],
            out_specs=pl.BlockSpec((t, HEAD_W), lambda h, b, i: (b * nq + i, h)),
            scratch_shapes=[pltpu.VMEM((nq, HEAD_W, t), BF16),
                            pltpu.VMEM((2, t, t), F32)],
            compiler_params=_cparams(("parallel", "arbitrary", "arbitrary")),
            name="attn_online" if online else "attn",
        )(lam.reshape(1), far_off, qkv, qkv, qkv, bias, subln.reshape(1, HEAD_W))

    return lax.cond(static_ok, lambda: call(False), lambda: call(True))


SUBLANES = 8


def _hgrn_tables(c):
    levels = int(math.log2(c))
    assert 2 ** levels == c
    t = np.arange(c)[:, None]
    s = np.arange(c)[None, :]
    blocks = [(s <= t).astype(np.float32)]
    level_id = np.where(t == s, 0, -1)
    n_fine = 0
    for l in range(1, levels + 1):
        g, half = 2 ** l, 2 ** (l - 1)
        mid = (t // g) * g + half - 1
        upper = (t % g) >= half
        if half < SUBLANES:
            above = upper & (s > mid) & (s <= t)
            below = (~upper) & (s > t) & (s <= mid)
            blocks.append((above | below).astype(np.float32))
            n_fine += 1
        pair = (t // g == s // g) & upper & ((s % g) < half)
        level_id = np.where(pair, l, level_id)
    w = np.concatenate(blocks, axis=0)
    return jnp.asarray(w, BF16), jnp.asarray(level_id, jnp.int32), levels, n_fine


def _hgrn_kernel(w_ref, lvl_ref, q_ref, k_ref, v_ref, og_ref, lf_ref, gn_ref, o_ref, *,
                 c, levels, n_fine, seq, unroll):
    gain = gn_ref[...]

    def cumsums(ci):
        rows = pl.ds(pl.multiple_of(ci * c, c), c)
        g = lf_ref[0, 0, rows, :]
        g_hi = g.astype(BF16)
        g_lo = (g - g_hi.astype(F32)).astype(BF16)
        e2 = _dot(w_ref[...], jnp.concatenate([g_hi, g_lo], axis=1))
        return rows, e2[:, 0:HEAD_W] + e2[:, HEAD_W:2 * HEAD_W]

    def scores(rows, e):
        qb = q_ref[0, 0, rows, :]
        kb = k_ref[0, 0, rows, :]
        q = qb.astype(F32)
        k = kb.astype(F32)
        b = e[0:c]
        lvl = lvl_ref[...]
        a = jnp.where(lvl == 0, _dot_nt(qb, kb), 0.0)
        for l in range(1, levels + 1):
            if l <= n_fine:
                d = e[l * c:(l + 1) * c]
            else:
                g, half = 2 ** l, 2 ** (l - 1)
                b3 = b.reshape(c // g, g, HEAD_W)
                mid = b3[:, half - 1:half, :]
                d = jnp.concatenate([mid - b3[:, 0:half, :], b3[:, half:g, :] - mid],
                                    axis=1).reshape(c, HEAD_W)
            x = jnp.exp2(d)
            a = jnp.where(lvl == l, _dot_nt((q * x).astype(BF16), (k * x).astype(BF16)), a)
        b_last = b[c - 1:c, :]
        q_in = (q * jnp.exp2(b)).astype(BF16)
        k_out = (k * jnp.exp2(b_last - b)).astype(BF16)
        return a.astype(BF16), q_in, k_out, jnp.exp2(b_last)

    def body(it, state_t):
        st1 = [cumsums(it * unroll + u) for u in range(unroll)]
        st2 = [scores(rows, e) for rows, e in st1]
        st3 = []
        for (rows, _), (a, q_in, k_out, decay) in zip(st1, st2):
            v = v_ref[0, 0, rows, :]
            st3.append((rows, _dot(a, v), q_in, _dot_tn(v, k_out), decay))
        for rows, o_intra, q_in, ds_t, decay in st3:
            o = o_intra + _dot_nt(q_in, state_t.astype(BF16))
            state_t = state_t * decay + ds_t
            y = _rms(o, gain) * og_ref[0, 0, rows, :].astype(F32)
            o_ref[rows, :] = y.astype(BF16)
        return state_t

    lax.fori_loop(0, seq // (c * unroll), body, jnp.zeros((HEAD_W, HEAD_W), F32))


def _hgrn(hin, logf, gain, *, batch, seq, c, unroll):
    w, lvl, levels, n_fine = _hgrn_tables(c)
    assert seq % (c * unroll) == 0

    def head(slab):
        return pl.BlockSpec((1, 1, seq, HEAD_W), lambda b, h: (b, slab * HEADS + h, 0, 0))

    return pl.pallas_call(
        functools.partial(_hgrn_kernel, c=c, levels=levels, n_fine=n_fine, seq=seq, unroll=unroll),
        out_shape=jax.ShapeDtypeStruct((batch * seq, MIX_W), BF16),
        grid=(batch, HEADS),
        in_specs=[
            pl.BlockSpec(w.shape, lambda b, h: (0, 0)),
            pl.BlockSpec((c, c), lambda b, h: (0, 0)),
            head(0), head(1), head(2), head(3), head(0),
            pl.BlockSpec((1, HEAD_W), lambda b, h: (0, 0)),
        ],
        out_specs=pl.BlockSpec((seq, HEAD_W), lambda b, h: (b, h)),
        compiler_params=_cparams(("parallel", "parallel")),
        name="hgrn",
    )(w, lvl, hin, hin, hin, hin, logf, gain.reshape(1, HEAD_W))


def _merge_kernel(x_ref, ya_ref, yb_ref, gt_ref, wa_ref, wb_ref, wo_ref, o_ref):
    d = x_ref.shape[1]
    gt = gt_ref[...].astype(F32)
    merged = gt[:, 0:d] * _dot(ya_ref[...], wa_ref[...]) + gt[:, d:2 * d] * _dot(yb_ref[...], wb_ref[...])
    o_ref[...] = x_ref[...] + _dot(merged.astype(BF16), wo_ref[...])


def _resident(shape):
    return pl.BlockSpec(shape, lambda i: (0,) * len(shape), pipeline_mode=pl.Buffered(1))


def _merge(x, ya, yb, gates, wa, wb, wo, *, tm):
    m, d = x.shape
    return pl.pallas_call(
        _merge_kernel,
        out_shape=jax.ShapeDtypeStruct((m, d), F32),
        grid=(m // tm,),
        in_specs=[
            pl.BlockSpec((tm, d), lambda i: (i, 0)),
            pl.BlockSpec((tm, MIX_W), lambda i: (i, 0)),
            pl.BlockSpec((tm, MIX_W), lambda i: (i, 0)),
            pl.BlockSpec((tm, 2 * d), lambda i: (i, 0)),
            _resident(wa.shape), _resident(wb.shape), _resident(wo.shape),
        ],
        out_specs=pl.BlockSpec((tm, d), lambda i: (i, 0)),
        compiler_params=_cparams(("parallel",)),
        name="merge",
    )(x, ya, yb, gates, wa, wb, wo)


def _ple_kernel(x_ref, p_ref, gg_ref, pg_ref, wg_ref, wp_ref, o_ref):
    x = x_ref[...]
    gate = _sigmoid(_dot(_rms(x, gg_ref[...]).astype(BF16), wg_ref[...]))
    ple = _rms(_dot(p_ref[...].astype(BF16), wp_ref[...]), pg_ref[...])
    o_ref[...] = x + gate * ple


def _ple(x, p, gate_gain, post_gain, w_gate, w_proj, *, tm):
    m, d = x.shape
    pd = p.shape[1]
    return pl.pallas_call(
        _ple_kernel,
        out_shape=jax.ShapeDtypeStruct((m, d), F32),
        grid=(m // tm,),
        in_specs=[
            pl.BlockSpec((tm, d), lambda i: (i, 0)),
            pl.BlockSpec((tm, pd), lambda i: (i, 0)),
            pl.BlockSpec((1, d), lambda i: (0, 0)),
            pl.BlockSpec((1, d), lambda i: (0, 0)),
            _resident(w_gate.shape), _resident(w_proj.shape),
        ],
        out_specs=pl.BlockSpec((tm, d), lambda i: (i, 0)),
        compiler_params=_cparams(("parallel",)),
        name="ple",
    )(x, p, gate_gain.reshape(1, d), post_gain.reshape(1, d), w_gate, w_proj)


def _tiles(seq, d, f):
    def pick(n, choices):
        return next(c for c in choices if n % c == 0)
    return dict(
        tm=pick(seq, (512, 256, 128)),
        tfm=pick(seq, (1024, 512, 256, 128)),
        tf=pick(f, (512, 256, 128)),
        tp=pick(seq, (1024, 512, 256, 128)),
        tn=pick(math.gcd(MIX_W, 2 * d), (1024, 512, 256, 128)),
        ta=pick(seq, (512, 256, 128)),
        c=128,
        cu=16,
    )


def kernel(x, p, ffn1_norm, ffn1_w_gate, ffn1_w_up, ffn1_w_down, mix_norm, w_in, q_norm, k_norm, lambda_q1, lambda_k1, lambda_q2, lambda_k2, diff_subln, rel_bias, hgrn_lb_logits, hgrn_norm, w_branch_a, w_branch_b, w_out, ffn2_norm, ffn2_w_gate, ffn2_w_up, ffn2_w_down, ple_gate_norm, w_ple_gate, w_ple_proj, ple_post_norm):
    batch, seq, d = x.shape
    depth = ffn1_norm.shape[0]
    assert depth == 1
    m = batch * seq
    tl = _tiles(seq, d, ffn1_w_gate.shape[2])
    bf = lambda w: w.astype(BF16)

    lower_bounds = jnp.cumsum(jax.nn.softmax(hgrn_lb_logits.astype(F32), axis=0), axis=0)
    lb = lower_bounds[0].reshape(HEADS, HEAD_W)
    lam = (jnp.exp(jnp.sum(lambda_q1[0].astype(F32) * lambda_k1[0].astype(F32)))
           - jnp.exp(jnp.sum(lambda_q2[0].astype(F32) * lambda_k2[0].astype(F32)))
           + LAMBDA_INIT)
    scale = QK_DIM ** -0.5
    qg = jnp.tile(q_norm[0].astype(F32), 2).reshape(1, HEAD_W) * (scale * LOG2E)
    kg = jnp.tile(k_norm[0].astype(F32), 2).reshape(1, HEAD_W)
    qk_bound = (QK_DIM * scale * jnp.max(jnp.abs(q_norm[0].astype(F32)))
                * jnp.max(jnp.abs(k_norm[0].astype(F32))))

    x2 = x.reshape(m, d)
    x2 = _ffn(x2, ffn1_norm[0], bf(ffn1_w_gate[0]), bf(ffn1_w_up[0]), bf(ffn1_w_down[0]),
              tm=tl["tfm"], tf=tl["tf"])

    qkv, hin, logf, gates = _proj(x2, mix_norm[0], bf(w_in[0]), qg, kg, lb,
                                  batch=batch, seq=seq, tm=tl["tp"], tn=tl["tn"])
    ya = _attn(qkv, lam, rel_bias, qk_bound, diff_subln[0], batch=batch, seq=seq, t=tl["ta"])
    yb = _hgrn(hin, logf, hgrn_norm[0], batch=batch, seq=seq, c=tl["c"], unroll=tl["cu"])
    x2 = _merge(x2, ya, yb, gates, bf(w_branch_a[0]), bf(w_branch_b[0]), bf(w_out[0]), tm=tl["tm"])

    x2 = _ffn(x2, ffn2_norm[0], bf(ffn2_w_gate[0]), bf(ffn2_w_up[0]), bf(ffn2_w_down[0]),
              tm=tl["tfm"], tf=tl["tf"])
    x2 = _ple(x2, p[0].reshape(m, -1), ple_gate_norm[0], ple_post_norm[0],
              bf(w_ple_gate[0]), bf(w_ple_proj[0]), tm=tl["tm"])
    return x2.reshape(batch, seq, d)
```

```python
import functools
import math

import jax
import jax.numpy as jnp
import numpy as np
from jax import lax
from jax.experimental import pallas as pl
from jax.experimental.pallas import tpu as pltpu

F32 = jnp.float32
BF16 = jnp.bfloat16

EPS = 1e-6
HEADS = 8
HEAD_W = 128
QK_DIM = 64
MIX_W = HEADS * HEAD_W
REL_BUCKETS = 32
REL_MAX_DIST = 128
LAMBDA_INIT = 0.8 - 0.6 * math.exp(-0.3 * 0)
LOG2E = math.log2(math.e)
MAX_STATIC_SOFTMAX_SPREAD = 60.0

V7X_MXU_WIDTH = 256
V7X_VMEM_BYTES = 64 * 1024 * 1024
VMEM_LIMIT = 56 * 1024 * 1024


def _cparams(sem, flags=None):
    return pltpu.CompilerParams(dimension_semantics=sem, vmem_limit_bytes=VMEM_LIMIT, flags=flags)


def _sigmoid(x):
    return 1.0 / (1.0 + jnp.exp(-x))


def _rms(x, gain):
    ms = jnp.mean(x * x, axis=-1, keepdims=True)
    return x * lax.rsqrt(ms + EPS) * gain


def _dot(a, b):
    return jnp.dot(a, b, preferred_element_type=F32)


def _dot_nt(a, b):
    return lax.dot_general(a, b, (((1,), (1,)), ((), ())), preferred_element_type=F32)


def _dot_tn(a, b):
    return lax.dot_general(a, b, (((0,), (0,)), ((), ())), preferred_element_type=F32)


def _column_tiles(w, tn):
    k, n = w.shape
    return w.reshape(k, n // tn, tn).transpose(1, 0, 2)


def _row_tile_fetch(x_hbm, xs_ref, sem, tm):
    i = pl.program_id(0)
    j = pl.program_id(1)

    def copy(row_tile):
        return pltpu.make_async_copy(x_hbm.at[pl.ds(row_tile * tm, tm), :], xs_ref, sem)

    @pl.when((i == 0) & (j == 0))
    def _():
        copy(0).start()

    @pl.when((j == 1) & (i + 1 < pl.num_programs(0)))
    def _():
        copy(i + 1).start()

    return copy(i)


def _ffn_kernel(x_hbm, g_ref, wg_ref, wu_ref, wd_ref, o_ref, xs_ref, h_ref, sem, *, tm):
    j = pl.program_id(1)
    x_copy = _row_tile_fetch(x_hbm, xs_ref, sem, tm)

    @pl.when(j == 0)
    def _():
        x_copy.wait()
        x = xs_ref[...]
        h_ref[...] = _rms(x, g_ref[...]).astype(BF16)
        o_ref[...] = x

    h = h_ref[...]
    gate = _dot(h, wg_ref[0])
    up = _dot(h, wu_ref[0])
    act = (gate * _sigmoid(gate) * (0.5 * up)).astype(BF16)
    o_ref[...] += _dot(act, wd_ref[...])


def _ffn(x, gain, w_gate, w_up, w_down, *, tm, tf):
    m, d = x.shape
    f = w_gate.shape[1]
    assert f // tf >= 2
    return pl.pallas_call(
        functools.partial(_ffn_kernel, tm=tm),
        out_shape=jax.ShapeDtypeStruct((m, d), F32),
        grid=(m // tm, f // tf),
        in_specs=[
            pl.BlockSpec(memory_space=pl.ANY),
            pl.BlockSpec((1, d), lambda i, j: (0, 0)),
            pl.BlockSpec((1, d, tf), lambda i, j: (j, 0, 0)),
            pl.BlockSpec((1, d, tf), lambda i, j: (j, 0, 0)),
            pl.BlockSpec((tf, d), lambda i, j: (j, 0)),
        ],
        out_specs=pl.BlockSpec((tm, d), lambda i, j: (i, 0)),
        scratch_shapes=[pltpu.VMEM((tm, d), F32), pltpu.VMEM((tm, d), BF16),
                        pltpu.SemaphoreType.DMA(())],
        compiler_params=_cparams(("arbitrary", "arbitrary")),
        name="ffn",
    )(x, gain.reshape(1, d), _column_tiles(w_gate, tf), _column_tiles(w_up, tf), w_down)


def _proj_kernel(x_hbm, g_ref, w_ref, qg_ref, kg_ref, lb_ref, gm_ref,
                 oqkv_ref, oh_ref, olf_ref, og_ref, xs_ref, h_ref, acc_ref, sem, *, tm, tn, nj):
    j = pl.program_id(1)
    jp = j - 1
    tps = MIX_W // tn
    hpt = tn // HEAD_W

    def heads(vals):
        return [vals[:, s * HEAD_W:(s + 1) * HEAD_W] for s in range(hpt)]

    def epi_qk(acc):
        gain = jnp.where(jp < tps, qg_ref[...], kg_ref[...])
        pw = gm_ref.shape[0]
        for s in range(tn // pw):
            a = acc[:, s * pw:(s + 1) * pw]
            ms = _dot((a * a).astype(BF16), gm_ref[...])
            y = a * lax.rsqrt(ms + EPS)
            for u in range(pw // HEAD_W):
                oqkv_ref[0, s * (pw // HEAD_W) + u] = (
                    y[:, u * HEAD_W:(u + 1) * HEAD_W] * gain).astype(BF16)

    def epi_v(acc):
        for s, a in enumerate(heads(acc)):
            oqkv_ref[0, s] = a.astype(BF16)

    def epi_silu(acc):
        for s, a in enumerate(heads(acc * _sigmoid(acc))):
            oh_ref[0, s] = a.astype(BF16)

    def epi_forget(acc):
        jj = jp - 4 * tps
        for s, a in enumerate(heads(acc)):
            lb = lb_ref[pl.ds(jj * hpt + s, 1), :]
            forget = lb + (1.0 - lb) * _sigmoid(a)
            oh_ref[0, s] = (1.0 - forget).astype(BF16)
            olf_ref[0, s] = jnp.log2(forget)

    def epi_hv(acc):
        for s, a in enumerate(heads(acc)):
            oh_ref[0, s] = a.astype(BF16)

    def epi_gates(acc):
        og_ref[...] = _sigmoid(acc).astype(BF16)

    x_copy = _row_tile_fetch(x_hbm, xs_ref, sem, tm)

    @pl.when(j == 0)
    def _():
        x_copy.wait()
        h_ref[...] = _rms(xs_ref[...], g_ref[...]).astype(BF16)
        acc_ref[...] = _dot(h_ref[...], w_ref[0])

    def fused(cond, epi):
        @pl.when(cond & (j >= 1) & (j < nj))
        def _():
            acc = acc_ref[...]
            epi(acc)
            acc_ref[...] = _dot(h_ref[...], w_ref[0])

    fused(jp < 2 * tps, epi_qk)
    fused((jp >= 2 * tps) & (jp < 3 * tps), epi_v)
    fused(((jp >= 3 * tps) & (jp < 4 * tps)) | ((jp >= 6 * tps) & (jp < 7 * tps)), epi_silu)
    fused((jp >= 4 * tps) & (jp < 5 * tps), epi_forget)
    fused((jp >= 5 * tps) & (jp < 6 * tps), epi_hv)
    fused(jp >= 7 * tps, epi_gates)

    @pl.when(j == nj)
    def _():
        epi_gates(acc_ref[...])


def _proj(x, gain, w_in, qg, kg, lb, *, batch, seq, tm, tn):
    m, d = x.shape
    n_in = w_in.shape[1]
    tps = MIX_W // tn
    hpt = tn // HEAD_W
    spb = seq // tm
    ng = 2 * d // tn
    nj = 7 * tps + ng
    assert n_in == 7 * MIX_W + 2 * d and MIX_W % tn == 0 and (2 * d) % tn == 0

    pw = math.gcd(tn, V7X_MXU_WIDTH)
    lane = np.arange(pw)
    gm = (lane[:, None] // QK_DIM == lane[None, :] // QK_DIM).astype(np.float32) / QK_DIM

    def hm_block(lo, n):
        return lambda i, j: (i // spb, jnp.clip(j - 1 - lo, 0, n - 1), i % spb, 0)

    return pl.pallas_call(
        functools.partial(_proj_kernel, tm=tm, tn=tn, nj=nj),
        out_shape=(
            jax.ShapeDtypeStruct((batch, 3 * HEADS, seq, HEAD_W), BF16),
            jax.ShapeDtypeStruct((batch, 4 * HEADS, seq, HEAD_W), BF16),
            jax.ShapeDtypeStruct((batch, HEADS, seq, HEAD_W), F32),
            jax.ShapeDtypeStruct((m, 2 * d), BF16),
        ),
        grid=(m // tm, nj + 1),
        in_specs=[
            pl.BlockSpec(memory_space=pl.ANY),
            pl.BlockSpec((1, d), lambda i, j: (0, 0)),
            pl.BlockSpec((1, d, tn), lambda i, j: (jnp.minimum(j, nj - 1), 0, 0)),
            pl.BlockSpec((1, HEAD_W), lambda i, j: (0, 0)),
            pl.BlockSpec((1, HEAD_W), lambda i, j: (0, 0)),
            pl.BlockSpec((HEADS, HEAD_W), lambda i, j: (0, 0)),
            pl.BlockSpec((pw, pw), lambda i, j: (0, 0)),
        ],
        out_specs=(
            pl.BlockSpec((1, hpt, tm, HEAD_W), hm_block(0, 3 * tps)),
            pl.BlockSpec((1, hpt, tm, HEAD_W), hm_block(3 * tps, 4 * tps)),
            pl.BlockSpec((1, hpt, tm, HEAD_W), hm_block(4 * tps, tps)),
            pl.BlockSpec((tm, tn), lambda i, j: (i, jnp.clip(j - 1 - 7 * tps, 0, ng - 1))),
        ),
        scratch_shapes=[pltpu.VMEM((tm, d), F32), pltpu.VMEM((tm, d), BF16),
                        pltpu.VMEM((tm, tn), F32), pltpu.SemaphoreType.DMA(())],
        compiler_params=_cparams(("arbitrary", "arbitrary")),
        name="proj",
    )(x, gain.reshape(1, d), _column_tiles(w_in, tn), qg, kg, lb, jnp.asarray(gm, BF16))


def _attn_kernel(lam_ref, off_ref, q_ref, k_ref, v_ref, w_ref, sg_ref, o_ref,
                 vt_ref, bias_ref, *, t, nkb, online):
    h = pl.program_id(0)
    b = pl.program_id(1)
    i = pl.program_id(2)
    lam = lam_ref[0]
    far_off = off_ref[h]

    @pl.when((b == 0) & (i == 0))
    def _():
        for dd in range(2):
            rows = jnp.broadcast_to(w_ref[0, dd:dd + 1, :], (t, 2 * t))
            bias_ref[dd] = pltpu.roll(rows, 0, 1, stride=1, stride_axis=0)[:, 0:t]

    @pl.when(i == 0)
    def _():
        for jb in range(nkb):
            vt_ref[jb] = v_ref[0, 0, jb * t:(jb + 1) * t, :].astype(F32).T.astype(BF16)

    def colsum8(p):
        return jnp.sum(p.reshape(t // 8, 8, 2 * t), axis=0)

    def both(x):
        return jnp.concatenate([x, x], axis=1)

    def query_block(iq):
        q = q_ref[0, 0]
        lane = lax.broadcasted_iota(jnp.int32, (t, HEAD_W), 1)
        zero = jnp.zeros_like(q)
        qs = jnp.concatenate([jnp.where(lane < QK_DIM, q, zero),
                              jnp.where(lane >= QK_DIM, q, zero)], axis=0)
        offsets = [far_off] * max(iq - 1, 0)
        if iq >= 1:
            offsets.append(both(bias_ref[1]))
        offsets.append(both(bias_ref[0]))
        m = l = acc = None
        for j, badd in enumerate(offsets):
            s = _dot_nt(k_ref[0, 0, j * t:(j + 1) * t, :], qs) + badd
            if online:
                bmax = jnp.max(s, axis=0, keepdims=True)
                m_new = bmax if m is None else jnp.maximum(m, bmax)
                p = jnp.exp2(s - m_new)
            else:
                p = jnp.exp2(s)
            psum = colsum8(p)
            pv = _dot(vt_ref[j], p.astype(BF16))
            if l is None:
                l, acc = psum, pv
            elif online:
                alpha = jnp.exp2(m - m_new)
                l, acc = alpha * l + psum, alpha * acc + pv
            else:
                l, acc = l + psum, acc + pv
            if online:
                m = m_new
        l = jnp.sum(l, axis=0, keepdims=True)
        o_t = acc[:, 0:t] / l[:, 0:t] - lam * (acc[:, t:2 * t] / l[:, t:2 * t])
        o_ref[...] = (_rms(o_t.T, sg_ref[...]) * (1.0 - LAMBDA_INIT)).astype(BF16)

    for iq in range(nkb):
        pl.when(i == iq)(functools.partial(query_block, iq))


def _t5_bucket(dist):
    n = jnp.maximum(dist, 0)
    max_exact = REL_BUCKETS // 2
    nf = jnp.maximum(n, 1).astype(F32)
    large = max_exact + (jnp.log(nf / max_exact) / math.log(REL_MAX_DIST / max_exact)
                         * (REL_BUCKETS - max_exact)).astype(jnp.int32)
    large = jnp.minimum(large, REL_BUCKETS - 1)
    return jnp.where(n < max_exact, n, large)


def _bias_generators(f, t):
    hole = jnp.full((HEADS, 1), -jnp.inf, F32)
    w0 = jnp.concatenate([f[:, 0:t], jnp.full((HEADS, t), -jnp.inf, F32)], axis=1)
    w1 = jnp.concatenate([f[:, t:2 * t], hole, f[:, 1:t]], axis=1)
    return jnp.stack([w0, w1], axis=1)


def _attn(qkv, lam, rel_bias, qk_bound, subln, *, batch, seq, t):
    assert t >= REL_MAX_DIST
    nq = seq // t
    rb = rel_bias.astype(F32)
    spread = 2.0 * qk_bound + jnp.max(jnp.max(rb, axis=0) - jnp.min(rb, axis=0))
    static_ok = spread <= MAX_STATIC_SOFTMAX_SPREAD
    upper = jnp.where(static_ok, qk_bound + jnp.max(rb, axis=0), 0.0)
    f = rb[_t5_bucket(jnp.arange(2 * t))].T - upper[:, None]
    bias = _bias_generators(f * LOG2E, t)
    far_off = (rb[REL_BUCKETS - 1] - upper) * LOG2E

    def call(online):
        return pl.pallas_call(
            functools.partial(_attn_kernel, t=t, nkb=nq, online=online),
            out_shape=jax.ShapeDtypeStruct((batch * seq, MIX_W), BF16),
            grid=(HEADS, batch, nq),
            in_specs=[
                pl.BlockSpec(memory_space=pltpu.SMEM),
                pl.BlockSpec(memory_space=pltpu.SMEM),
                pl.BlockSpec((1, 1, t, HEAD_W), lambda h, b, i: (b, h, i, 0)),
                pl.BlockSpec((1, 1, seq, HEAD_W), lambda h, b, i: (b, HEADS + h, 0, 0)),
                pl.BlockSpec((1, 1, seq, HEAD_W), lambda h, b, i: (b, 2 * HEADS + h, 0, 0)),
                pl.BlockSpec((1, 2, 2 * t), lambda h, b, i: (h, 0, 0)),
                pl.BlockSpec((1, HEAD_W), lambda h, b, i: (0, 0)),
            ],
            out_specs=pl.BlockSpec((t, HEAD_W), lambda h, b, i: (b * nq + i, h)),
            scratch_shapes=[pltpu.VMEM((nq, HEAD_W, t), BF16),
                            pltpu.VMEM((2, t, t), F32)],
            compiler_params=_cparams(("parallel", "arbitrary", "arbitrary")),
            name="attn_online" if online else "attn",
        )(lam.reshape(1), far_off, qkv, qkv, qkv, bias, subln.reshape(1, HEAD_W))

    return lax.cond(static_ok, lambda: call(False), lambda: call(True))


SUBLANES = 8


def _hgrn_tables(c):
    levels = int(math.log2(c))
    assert 2 ** levels == c
    t = np.arange(c)[:, None]
    s = np.arange(c)[None, :]
    blocks = [(s <= t).astype(np.float32)]
    level_id = np.where(t == s, 0, -1)
    n_fine = 0
    for l in range(1, levels + 1):
        g, half = 2 ** l, 2 ** (l - 1)
        mid = (t // g) * g + half - 1
        upper = (t % g) >= half
        if half < SUBLANES:
            above = upper & (s > mid) & (s <= t)
            below = (~upper) & (s > t) & (s <= mid)
            blocks.append((above | below).astype(np.float32))
            n_fine += 1
        pair = (t // g == s // g) & upper & ((s % g) < half)
        level_id = np.where(pair, l, level_id)
    w = np.concatenate(blocks, axis=0)
    return jnp.asarray(w, BF16), jnp.asarray(level_id, jnp.int32), levels, n_fine


def _hgrn_kernel(w_ref, lvl_ref, q_ref, k_ref, v_ref, og_ref, lf_ref, gn_ref, o_ref, *,
                 c, levels, n_fine, seq, unroll):
    gain = gn_ref[...]

    def cumsums(ci):
        rows = pl.ds(pl.multiple_of(ci * c, c), c)
        g = lf_ref[0, 0, rows, :]
        g_hi = g.astype(BF16)
        g_lo = (g - g_hi.astype(F32)).astype(BF16)
        e2 = _dot(w_ref[...], jnp.concatenate([g_hi, g_lo], axis=1))
        return rows, e2[:, 0:HEAD_W] + e2[:, HEAD_W:2 * HEAD_W]

    def scores(rows, e):
        qb = q_ref[0, 0, rows, :]
        kb = k_ref[0, 0, rows, :]
        q = qb.astype(F32)
        k = kb.astype(F32)
        b = e[0:c]
        lvl = lvl_ref[...]
        a = jnp.where(lvl == 0, _dot_nt(qb, kb), 0.0)
        for l in range(1, levels + 1):
            if l <= n_fine:
                d = e[l * c:(l + 1) * c]
            else:
                g, half = 2 ** l, 2 ** (l - 1)
                b3 = b.reshape(c // g, g, HEAD_W)
                mid = b3[:, half - 1:half, :]
                d = jnp.concatenate([mid - b3[:, 0:half, :], b3[:, half:g, :] - mid],
                                    axis=1).reshape(c, HEAD_W)
            x = jnp.exp2(d)
            a = jnp.where(lvl == l, _dot_nt((q * x).astype(BF16), (k * x).astype(BF16)), a)
        b_last = b[c - 1:c, :]
        q_in = (q * jnp.exp2(b)).astype(BF16)
        k_out = (k * jnp.exp2(b_last - b)).astype(BF16)
        return a.astype(BF16), q_in, k_out, jnp.exp2(b_last)

    def body(it, state_t):
        st1 = [cumsums(it * unroll + u) for u in range(unroll)]
        st2 = [scores(rows, e) for rows, e in st1]
        st3 = []
        for (rows, _), (a, q_in, k_out, decay) in zip(st1, st2):
            v = v_ref[0, 0, rows, :]
            st3.append((rows, _dot(a, v), q_in, _dot_tn(v, k_out), decay))
        for rows, o_intra, q_in, ds_t, decay in st3:
            o = o_intra + _dot_nt(q_in, state_t.astype(BF16))
            state_t = state_t * decay + ds_t
            y = _rms(o, gain) * og_ref[0, 0, rows, :].astype(F32)
            o_ref[rows, :] = y.astype(BF16)
        return state_t

    lax.fori_loop(0, seq // (c * unroll), body, jnp.zeros((HEAD_W, HEAD_W), F32))


def _hgrn(hin, logf, gain, *, batch, seq, c, unroll):
    w, lvl, levels, n_fine = _hgrn_tables(c)
    assert seq % (c * unroll) == 0

    def head(slab):
        return pl.BlockSpec((1, 1, seq, HEAD_W), lambda b, h: (b, slab * HEADS + h, 0, 0))

    return pl.pallas_call(
        functools.partial(_hgrn_kernel, c=c, levels=levels, n_fine=n_fine, seq=seq, unroll=unroll),
        out_shape=jax.ShapeDtypeStruct((batch * seq, MIX_W), BF16),
        grid=(batch, HEADS),
        in_specs=[
            pl.BlockSpec(w.shape, lambda b, h: (0, 0)),
            pl.BlockSpec((c, c), lambda b, h: (0, 0)),
            head(0), head(1), head(2), head(3), head(0),
            pl.BlockSpec((1, HEAD_W), lambda b, h: (0, 0)),
        ],
        out_specs=pl.BlockSpec((seq, HEAD_W), lambda b, h: (b, h)),
        compiler_params=_cparams(("parallel", "parallel")),
        name="hgrn",
    )(w, lvl, hin, hin, hin, hin, logf, gain.reshape(1, HEAD_W))


def _merge_kernel(x_ref, ya_ref, yb_ref, gt_ref, wa_ref, wb_ref, wo_ref, o_ref):
    d = x_ref.shape[1]
    gt = gt_ref[...].astype(F32)
    merged = gt[:, 0:d] * _dot(ya_ref[...], wa_ref[...]) + gt[:, d:2 * d] * _dot(yb_ref[...], wb_ref[...])
    o_ref[...] = x_ref[...] + _dot(merged.astype(BF16), wo_ref[...])


def _resident(shape):
    return pl.BlockSpec(shape, lambda i: (0,) * len(shape), pipeline_mode=pl.Buffered(1))


def _merge(x, ya, yb, gates, wa, wb, wo, *, tm):
    m, d = x.shape
    return pl.pallas_call(
        _merge_kernel,
        out_shape=jax.ShapeDtypeStruct((m, d), F32),
        grid=(m // tm,),
        in_specs=[
            pl.BlockSpec((tm, d), lambda i: (i, 0)),
            pl.BlockSpec((tm, MIX_W), lambda i: (i, 0)),
            pl.BlockSpec((tm, MIX_W), lambda i: (i, 0)),
            pl.BlockSpec((tm, 2 * d), lambda i: (i, 0)),
            _resident(wa.shape), _resident(wb.shape), _resident(wo.shape),
        ],
        out_specs=pl.BlockSpec((tm, d), lambda i: (i, 0)),
        compiler_params=_cparams(("parallel",)),
        name="merge",
    )(x, ya, yb, gates, wa, wb, wo)


def _ple_kernel(x_ref, p_ref, gg_ref, pg_ref, wg_ref, wp_ref, o_ref):
    x = x_ref[...]
    gate = _sigmoid(_dot(_rms(x, gg_ref[...]).astype(BF16), wg_ref[...]))
    ple = _rms(_dot(p_ref[...].astype(BF16), wp_ref[...]), pg_ref[...])
    o_ref[...] = x + gate * ple


def _ple(x, p, gate_gain, post_gain, w_gate, w_proj, *, tm):
    m, d = x.shape
    pd = p.shape[1]
    return pl.pallas_call(
        _ple_kernel,
        out_shape=jax.ShapeDtypeStruct((m, d), F32),
        grid=(m // tm,),
        in_specs=[
            pl.BlockSpec((tm, d), lambda i: (i, 0)),
            pl.BlockSpec((tm, pd), lambda i: (i, 0)),
            pl.BlockSpec((1, d), lambda i: (0, 0)),
            pl.BlockSpec((1, d), lambda i: (0, 0)),
            _resident(w_gate.shape), _resident(w_proj.shape),
        ],
        out_specs=pl.BlockSpec((tm, d), lambda i: (i, 0)),
        compiler_params=_cparams(("parallel",)),
        name="ple",
    )(x, p, gate_gain.reshape(1, d), post_gain.reshape(1, d), w_gate, w_proj)


def _tiles(seq, d, f):
    def pick(n, choices):
        return next(c for c in choices if n % c == 0)
    return dict(
        tm=pick(seq, (512, 256, 128)),
        tfm=pick(seq, (1024, 512, 256, 128)),
        tf=pick(f, (512, 256, 128)),
        tp=pick(seq, (1024, 512, 256, 128)),
        tn=pick(math.gcd(MIX_W, 2 * d), (1024, 512, 256, 128)),
        ta=pick(seq, (512, 256, 128)),
        c=128,
        cu=16,
    )


def kernel(x, p, ffn1_norm, ffn1_w_gate, ffn1_w_up, ffn1_w_down, mix_norm, w_in, q_norm, k_norm, lambda_q1, lambda_k1, lambda_q2, lambda_k2, diff_subln, rel_bias, hgrn_lb_logits, hgrn_norm, w_branch_a, w_branch_b, w_out, ffn2_norm, ffn2_w_gate, ffn2_w_up, ffn2_w_down, ple_gate_norm, w_ple_gate, w_ple_proj, ple_post_norm):
    batch, seq, d = x.shape
    depth = ffn1_norm.shape[0]
    assert depth == 1
    m = batch * seq
    tl = _tiles(seq, d, ffn1_w_gate.shape[2])
    bf = lambda w: w.astype(BF16)

    lower_bounds = jnp.cumsum(jax.nn.softmax(hgrn_lb_logits.astype(F32), axis=0), axis=0)
    lb = lower_bounds[0].reshape(HEADS, HEAD_W)
    lam = (jnp.exp(jnp.sum(lambda_q1[0].astype(F32) * lambda_k1[0].astype(F32)))
           - jnp.exp(jnp.sum(lambda_q2[0].astype(F32) * lambda_k2[0].astype(F32)))
           + LAMBDA_INIT)
    scale = QK_DIM ** -0.5
    qg = jnp.tile(q_norm[0].astype(F32), 2).reshape(1, HEAD_W) * (scale * LOG2E)
    kg = jnp.tile(k_norm[0].astype(F32), 2).reshape(1, HEAD_W)
    qk_bound = (QK_DIM * scale * jnp.max(jnp.abs(q_norm[0].astype(F32)))
                * jnp.max(jnp.abs(k_norm[0].astype(F32))))

    x2 = x.reshape(m, d)
    x2 = _ffn(x2, ffn1_norm[0], bf(ffn1_w_gate[0]), bf(ffn1_w_up[0]), bf(ffn1_w_down[0]),
              tm=tl["tfm"], tf=tl["tf"])

    qkv, hin, logf, gates = _proj(x2, mix_norm[0], bf(w_in[0]), qg, kg, lb,
                                  batch=batch, seq=seq, tm=tl["tp"], tn=tl["tn"])
    ya = _attn(qkv, lam, rel_bias, qk_bound, diff_subln[0], batch=batch, seq=seq, t=tl["ta"])
    yb = _hgrn(hin, logf, hgrn_norm[0], batch=batch, seq=seq, c=tl["c"], unroll=tl["cu"])
    x2 = _merge(x2, ya, yb, gates, bf(w_branch_a[0]), bf(w_branch_b[0]), bf(w_out[0]), tm=tl["tm"])

    x2 = _ffn(x2, ffn2_norm[0], bf(ffn2_w_gate[0]), bf(ffn2_w_up[0]), bf(ffn2_w_down[0]),
              tm=tl["tfm"], tf=tl["tf"])
    x2 = _ple(x2, p[0].reshape(m, -1), ple_gate_norm[0], ple_post_norm[0],
              bf(w_ple_gate[0]), bf(w_ple_proj[0]), tm=tl["tm"])
    return x2.reshape(batch, seq, d)
```

```python
import functools
import math

import jax
import jax.numpy as jnp
import numpy as np
from jax import lax
from jax.experimental import pallas as pl
from jax.experimental.pallas import tpu as pltpu

F32 = jnp.float32
BF16 = jnp.bfloat16

EPS = 1e-6
HEADS = 8
HEAD_W = 128
QK_DIM = 64
MIX_W = HEADS * HEAD_W
REL_BUCKETS = 32
REL_MAX_DIST = 128
LAMBDA_INIT = 0.8 - 0.6 * math.exp(-0.3 * 0)
LOG2E = math.log2(math.e)
MAX_STATIC_SOFTMAX_SPREAD = 60.0

V7X_MXU_WIDTH = 256
V7X_VMEM_BYTES = 64 * 1024 * 1024
VMEM_LIMIT = 56 * 1024 * 1024


def _cparams(sem, flags=None):
    return pltpu.CompilerParams(dimension_semantics=sem, vmem_limit_bytes=VMEM_LIMIT, flags=flags)


def _sigmoid(x):
    return 1.0 / (1.0 + jnp.exp(-x))


def _rms(x, gain):
    ms = jnp.mean(x * x, axis=-1, keepdims=True)
    return x * lax.rsqrt(ms + EPS) * gain


def _dot(a, b):
    return jnp.dot(a, b, preferred_element_type=F32)


def _dot_nt(a, b):
    return lax.dot_general(a, b, (((1,), (1,)), ((), ())), preferred_element_type=F32)


def _dot_tn(a, b):
    return lax.dot_general(a, b, (((0,), (0,)), ((), ())), preferred_element_type=F32)


BF16_ROW_TILE = 16


class _SideCast:
    def __init__(self, arrays, steps, step_index):
        self.arrays = list(arrays)
        self.n = len(self.arrays)
        self.step_index = step_index
        self.chunks = []
        for w in self.arrays:
            rows = w.shape[0]
            self.chunks.append(max(c for c in range(1, steps + 1)
                                   if rows % c == 0 and (rows // c) % BF16_ROW_TILE == 0))

    def _specs(self):
        def spec(w, c):
            return pl.BlockSpec((w.shape[0] // c, w.shape[1]),
                                lambda *ids: (jnp.minimum(self.step_index(*ids), c - 1), 0))
        return [spec(w, c) for w, c in zip(self.arrays, self.chunks)]

    in_specs = property(_specs)
    out_specs = property(_specs)

    @property
    def out_shapes(self):
        return [jax.ShapeDtypeStruct(w.shape, BF16) for w in self.arrays]

    @staticmethod
    def run(in_refs, out_refs):
        for src, dst in zip(in_refs, out_refs):
            dst[...] = src[...].astype(BF16)


def _row_tile_fetch(x_hbm, xs_ref, sem, tm):
    i = pl.program_id(0)
    j = pl.program_id(1)

    def copy(row_tile):
        return pltpu.make_async_copy(x_hbm.at[pl.ds(row_tile * tm, tm), :], xs_ref, sem)

    @pl.when((i == 0) & (j == 0))
    def _():
        copy(0).start()

    @pl.when((j == 1) & (i + 1 < pl.num_programs(0)))
    def _():
        copy(i + 1).start()

    return copy(i)


def _ffn_kernel(x_hbm, g_ref, wg_ref, wu_ref, wd_ref, *rest, tm, n_side):
    side_in, (o_ref,), side_out = rest[:n_side], rest[n_side:n_side + 1], rest[n_side + 1:2 * n_side + 1]
    xs_ref, h_ref, sem = rest[2 * n_side + 1:]
    j = pl.program_id(1)
    x_copy = _row_tile_fetch(x_hbm, xs_ref, sem, tm)

    @pl.when(j == 0)
    def _():
        x_copy.wait()
        x = xs_ref[...]
        h_ref[...] = _rms(x, g_ref[...]).astype(BF16)
        o_ref[...] = x

    h = h_ref[...]
    gate = _dot(h, wg_ref[...])
    up = _dot(h, wu_ref[...])
    act = (gate * _sigmoid(gate) * (0.5 * up)).astype(BF16)
    o_ref[...] += _dot(act, wd_ref[...])
    _SideCast.run(side_in, side_out)


def _ffn(x, gain, w_gate, w_up, w_down, side_weights=(), *, tm, tf):
    m, d = x.shape
    f = w_gate.shape[1]
    nj = f // tf
    assert nj >= 2
    side = _SideCast(side_weights, (m // tm) * nj, lambda i, j: i * nj + j)
    out = pl.pallas_call(
        functools.partial(_ffn_kernel, tm=tm, n_side=side.n),
        out_shape=[jax.ShapeDtypeStruct((m, d), F32)] + side.out_shapes,
        grid=(m // tm, nj),
        in_specs=[
            pl.BlockSpec(memory_space=pl.ANY),
            pl.BlockSpec((1, d), lambda i, j: (0, 0)),
            pl.BlockSpec((d, tf), lambda i, j: (0, j)),
            pl.BlockSpec((d, tf), lambda i, j: (0, j)),
            pl.BlockSpec((tf, d), lambda i, j: (j, 0)),
        ] + side.in_specs,
        out_specs=[pl.BlockSpec((tm, d), lambda i, j: (i, 0))] + side.out_specs,
        scratch_shapes=[pltpu.VMEM((tm, d), F32), pltpu.VMEM((tm, d), BF16),
                        pltpu.SemaphoreType.DMA(())],
        compiler_params=_cparams(("arbitrary", "arbitrary")),
        name="ffn",
    )(x, gain.reshape(1, d), w_gate, w_up, w_down, *side.arrays)
    return out[0], out[1:]


def _proj_kernel(x_hbm, g_ref, w_ref, qg_ref, kg_ref, lb_ref, gm_ref, *rest, tm, tn, nj, n_side):
    side_in, rest = rest[:n_side], rest[n_side:]
    (oqkv_ref, oh_ref, olf_ref, og_ref), rest = rest[:4], rest[4:]
    side_out, (xs_ref, h_ref, acc_ref, sem) = rest[:n_side], rest[n_side:]
    j = pl.program_id(1)
    jp = j - 1
    tps = MIX_W // tn
    hpt = tn // HEAD_W

    def heads(vals):
        return [vals[:, s * HEAD_W:(s + 1) * HEAD_W] for s in range(hpt)]

    def epi_qk(acc):
        gain = jnp.where(jp < tps, qg_ref[...], kg_ref[...])
        pw = gm_ref.shape[0]
        for s in range(tn // pw):
            a = acc[:, s * pw:(s + 1) * pw]
            ms = _dot((a * a).astype(BF16), gm_ref[...])
            y = a * lax.rsqrt(ms + EPS)
            for u in range(pw // HEAD_W):
                oqkv_ref[0, s * (pw // HEAD_W) + u] = (
                    y[:, u * HEAD_W:(u + 1) * HEAD_W] * gain).astype(BF16)

    def epi_v(acc):
        for s, a in enumerate(heads(acc)):
            oqkv_ref[0, s] = a.astype(BF16)

    def epi_silu(acc):
        for s, a in enumerate(heads(acc * _sigmoid(acc))):
            oh_ref[0, s] = a.astype(BF16)

    def epi_forget(acc):
        jj = jp - 4 * tps
        for s, a in enumerate(heads(acc)):
            lb = lb_ref[pl.ds(jj * hpt + s, 1), :]
            forget = lb + (1.0 - lb) * _sigmoid(a)
            oh_ref[0, s] = (1.0 - forget).astype(BF16)
            olf_ref[0, s] = jnp.log2(forget)

    def epi_hv(acc):
        for s, a in enumerate(heads(acc)):
            oh_ref[0, s] = a.astype(BF16)

    def epi_gates(acc):
        og_ref[...] = _sigmoid(acc).astype(BF16)

    x_copy = _row_tile_fetch(x_hbm, xs_ref, sem, tm)

    @pl.when(j == 0)
    def _():
        x_copy.wait()
        h_ref[...] = _rms(xs_ref[...], g_ref[...]).astype(BF16)
        acc_ref[...] = _dot(h_ref[...], w_ref[...])
        _SideCast.run(side_in, side_out)

    def fused(cond, epi):
        @pl.when(cond & (j >= 1) & (j < nj))
        def _():
            acc = acc_ref[...]
            epi(acc)
            acc_ref[...] = _dot(h_ref[...], w_ref[...])
            _SideCast.run(side_in, side_out)

    fused(jp < 2 * tps, epi_qk)
    fused((jp >= 2 * tps) & (jp < 3 * tps), epi_v)
    fused(((jp >= 3 * tps) & (jp < 4 * tps)) | ((jp >= 6 * tps) & (jp < 7 * tps)), epi_silu)
    fused((jp >= 4 * tps) & (jp < 5 * tps), epi_forget)
    fused((jp >= 5 * tps) & (jp < 6 * tps), epi_hv)
    fused(jp >= 7 * tps, epi_gates)

    @pl.when(j == nj)
    def _():
        epi_gates(acc_ref[...])
        _SideCast.run(side_in, side_out)


def _proj(x, gain, w_in, qg, kg, lb, side_weights=(), *, batch, seq, tm, tn):
    m, d = x.shape
    n_in = w_in.shape[1]
    tps = MIX_W // tn
    hpt = tn // HEAD_W
    spb = seq // tm
    ng = 2 * d // tn
    nj = 7 * tps + ng
    assert n_in == 7 * MIX_W + 2 * d and MIX_W % tn == 0 and (2 * d) % tn == 0

    pw = math.gcd(tn, V7X_MXU_WIDTH)
    lane = np.arange(pw)
    gm = (lane[:, None] // QK_DIM == lane[None, :] // QK_DIM).astype(np.float32) / QK_DIM

    def hm_block(lo, n):
        return lambda i, j: (i // spb, jnp.clip(j - 1 - lo, 0, n - 1), i % spb, 0)

    side = _SideCast(side_weights, (m // tm) * (nj + 1), lambda i, j: i * (nj + 1) + j)
    out = pl.pallas_call(
        functools.partial(_proj_kernel, tm=tm, tn=tn, nj=nj, n_side=side.n),
        out_shape=[
            jax.ShapeDtypeStruct((batch, 3 * HEADS, seq, HEAD_W), BF16),
            jax.ShapeDtypeStruct((batch, 4 * HEADS, seq, HEAD_W), BF16),
            jax.ShapeDtypeStruct((batch, HEADS, seq, HEAD_W), F32),
            jax.ShapeDtypeStruct((m, 2 * d), BF16),
        ] + side.out_shapes,
        grid=(m // tm, nj + 1),
        in_specs=[
            pl.BlockSpec(memory_space=pl.ANY),
            pl.BlockSpec((1, d), lambda i, j: (0, 0)),
            pl.BlockSpec((d, tn), lambda i, j: (0, jnp.minimum(j, nj - 1))),
            pl.BlockSpec((1, HEAD_W), lambda i, j: (0, 0)),
            pl.BlockSpec((1, HEAD_W), lambda i, j: (0, 0)),
            pl.BlockSpec((HEADS, HEAD_W), lambda i, j: (0, 0)),
            pl.BlockSpec((pw, pw), lambda i, j: (0, 0)),
        ] + side.in_specs,
        out_specs=[
            pl.BlockSpec((1, hpt, tm, HEAD_W), hm_block(0, 3 * tps)),
            pl.BlockSpec((1, hpt, tm, HEAD_W), hm_block(3 * tps, 4 * tps)),
            pl.BlockSpec((1, hpt, tm, HEAD_W), hm_block(4 * tps, tps)),
            pl.BlockSpec((tm, tn), lambda i, j: (i, jnp.clip(j - 1 - 7 * tps, 0, ng - 1))),
        ] + side.out_specs,
        scratch_shapes=[pltpu.VMEM((tm, d), F32), pltpu.VMEM((tm, d), BF16),
                        pltpu.VMEM((tm, tn), F32), pltpu.SemaphoreType.DMA(())],
        compiler_params=_cparams(("arbitrary", "arbitrary")),
        name="proj",
    )(x, gain.reshape(1, d), w_in, qg, kg, lb, jnp.asarray(gm, BF16), *side.arrays)
    return out[0], out[1], out[2], out[3], out[4:]


def _attn_kernel(lam_ref, off_ref, q_ref, k_ref, v_ref, w_ref, sg_ref, o_ref,
                 vt_ref, bias_ref, *, t, nkb, online):
    h = pl.program_id(0)
    b = pl.program_id(1)
    i = pl.program_id(2)
    lam = lam_ref[0]
    far_off = off_ref[h]

    @pl.when((b == 0) & (i == 0))
    def _():
        for dd in range(2):
            rows = jnp.broadcast_to(w_ref[0, dd:dd + 1, :], (t, 2 * t))
            bias_ref[dd] = pltpu.roll(rows, 0, 1, stride=1, stride_axis=0)[:, 0:t]

    @pl.when(i == 0)
    def _():
        for jb in range(nkb):
            vt_ref[jb] = v_ref[0, 0, jb * t:(jb + 1) * t, :].astype(F32).T.astype(BF16)

    def colsum8(p):
        return jnp.sum(p.reshape(t // 8, 8, 2 * t), axis=0)

    def both(x):
        return jnp.concatenate([x, x], axis=1)

    def query_block(iq):
        q = q_ref[0, 0]
        lane = lax.broadcasted_iota(jnp.int32, (t, HEAD_W), 1)
        zero = jnp.zeros_like(q)
        qs = jnp.concatenate([jnp.where(lane < QK_DIM, q, zero),
                              jnp.where(lane >= QK_DIM, q, zero)], axis=0)
        offsets = [far_off] * max(iq - 1, 0)
        if iq >= 1:
            offsets.append(both(bias_ref[1]))
        offsets.append(both(bias_ref[0]))
        m = l = acc = None
        for j, badd in enumerate(offsets):
            s = _dot_nt(k_ref[0, 0, j * t:(j + 1) * t, :], qs) + badd
            if online:
                bmax = jnp.max(s, axis=0, keepdims=True)
                m_new = bmax if m is None else jnp.maximum(m, bmax)
                p = jnp.exp2(s - m_new)
            else:
                p = jnp.exp2(s)
            psum = colsum8(p)
            pv = _dot(vt_ref[j], p.astype(BF16))
            if l is None:
                l, acc = psum, pv
            elif online:
                alpha = jnp.exp2(m - m_new)
                l, acc = alpha * l + psum, alpha * acc + pv
            else:
                l, acc = l + psum, acc + pv
            if online:
                m = m_new
        l = jnp.sum(l, axis=0, keepdims=True)
        o_t = acc[:, 0:t] / l[:, 0:t] - lam * (acc[:, t:2 * t] / l[:, t:2 * t])
        o_ref[...] = (_rms(o_t.T, sg_ref[...]) * (1.0 - LAMBDA_INIT)).astype(BF16)

    for iq in range(nkb):
        pl.when(i == iq)(functools.partial(query_block, iq))


def _t5_bucket(dist):
    n = jnp.maximum(dist, 0)
    max_exact = REL_BUCKETS // 2
    nf = jnp.maximum(n, 1).astype(F32)
    large = max_exact + (jnp.log(nf / max_exact) / math.log(REL_MAX_DIST / max_exact)
                         * (REL_BUCKETS - max_exact)).astype(jnp.int32)
    large = jnp.minimum(large, REL_BUCKETS - 1)
    return jnp.where(n < max_exact, n, large)


def _bias_generators(f, t):
    hole = jnp.full((HEADS, 1), -jnp.inf, F32)
    w0 = jnp.concatenate([f[:, 0:t], jnp.full((HEADS, t), -jnp.inf, F32)], axis=1)
    w1 = jnp.concatenate([f[:, t:2 * t], hole, f[:, 1:t]], axis=1)
    return jnp.stack([w0, w1], axis=1)


def _attn(qkv, lam, rel_bias, qk_bound, subln, *, batch, seq, t):
    assert t >= REL_MAX_DIST
    nq = seq // t
    rb = rel_bias.astype(F32)
    spread = 2.0 * qk_bound + jnp.max(jnp.max(rb, axis=0) - jnp.min(rb, axis=0))
    static_ok = spread <= MAX_STATIC_SOFTMAX_SPREAD
    upper = jnp.where(static_ok, qk_bound + jnp.max(rb, axis=0), 0.0)
    f = rb[_t5_bucket(jnp.arange(2 * t))].T - upper[:, None]
    bias = _bias_generators(f * LOG2E, t)
    far_off = (rb[REL_BUCKETS - 1] - upper) * LOG2E

    def call(online):
        return pl.pallas_call(
            functools.partial(_attn_kernel, t=t, nkb=nq, online=online),
            out_shape=jax.ShapeDtypeStruct((batch * seq, MIX_W), BF16),
            grid=(HEADS, batch, nq),
            in_specs=[
                pl.BlockSpec(memory_space=pltpu.SMEM),
                pl.BlockSpec(memory_space=pltpu.SMEM),
                pl.BlockSpec((1, 1, t, HEAD_W), lambda h, b, i: (b, h, i, 0)),
                pl.BlockSpec((1, 1, seq, HEAD_W), lambda h, b, i: (b, HEADS + h, 0, 0)),
                pl.BlockSpec((1, 1, seq, HEAD_W), lambda h, b, i: (b, 2 * HEADS + h, 0, 0)),
                pl.BlockSpec((1, 2, 2 * t), lambda h, b, i: (h, 0, 0)),
                pl.BlockSpec((1, HEAD_W), lambda h, b, i: (0, 0)),
            ],
            out_specs=pl.BlockSpec((t, HEAD_W), lambda h, b, i: (b * nq + i, h)),
            scratch_shapes=[pltpu.VMEM((nq, HEAD_W, t), BF16),
                            pltpu.VMEM((2, t, t), F32)],
            compiler_params=_cparams(("parallel", "arbitrary", "arbitrary")),
            name="attn_online" if online else "attn",
        )(lam.reshape(1), far_off, qkv, qkv, qkv, bias, subln.reshape(1, HEAD_W))

    return lax.cond(static_ok, lambda: call(False), lambda: call(True))


SUBLANES = 8


def _hgrn_tables(c):
    levels = int(math.log2(c))
    assert 2 ** levels == c
    t = np.arange(c)[:, None]
    s = np.arange(c)[None, :]
    blocks = [(s <= t).astype(np.float32)]
    level_id = np.where(t == s, 0, -1)
    n_fine = 0
    for l in range(1, levels + 1):
        g, half = 2 ** l, 2 ** (l - 1)
        mid = (t // g) * g + half - 1
        upper = (t % g) >= half
        if half < SUBLANES:
            above = upper & (s > mid) & (s <= t)
            below = (~upper) & (s > t) & (s <= mid)
            blocks.append((above | below).astype(np.float32))
            n_fine += 1
        pair = (t // g == s // g) & upper & ((s % g) < half)
        level_id = np.where(pair, l, level_id)
    w = np.concatenate(blocks, axis=0)
    return jnp.asarray(w, BF16), jnp.asarray(level_id, jnp.int32), levels, n_fine


def _hgrn_kernel(w_ref, lvl_ref, q_ref, k_ref, v_ref, og_ref, lf_ref, gn_ref, o_ref, *,
                 c, levels, n_fine, seq, unroll):
    gain = gn_ref[...]

    def cumsums(ci):
        rows = pl.ds(pl.multiple_of(ci * c, c), c)
        g = lf_ref[0, 0, rows, :]
        g_hi = g.astype(BF16)
        g_lo = (g - g_hi.astype(F32)).astype(BF16)
        e2 = _dot(w_ref[...], jnp.concatenate([g_hi, g_lo], axis=1))
        return rows, e2[:, 0:HEAD_W] + e2[:, HEAD_W:2 * HEAD_W]

    def scores(rows, e):
        qb = q_ref[0, 0, rows, :]
        kb = k_ref[0, 0, rows, :]
        q = qb.astype(F32)
        k = kb.astype(F32)
        b = e[0:c]
        lvl = lvl_ref[...]
        a = jnp.where(lvl == 0, _dot_nt(qb, kb), 0.0)
        for l in range(1, levels + 1):
            if l <= n_fine:
                d = e[l * c:(l + 1) * c]
            else:
                g, half = 2 ** l, 2 ** (l - 1)
                b3 = b.reshape(c // g, g, HEAD_W)
                mid = b3[:, half - 1:half, :]
                d = jnp.concatenate([mid - b3[:, 0:half, :], b3[:, half:g, :] - mid],
                                    axis=1).reshape(c, HEAD_W)
            x = jnp.exp2(d)
            a = jnp.where(lvl == l, _dot_nt((q * x).astype(BF16), (k * x).astype(BF16)), a)
        b_last = b[c - 1:c, :]
        q_in = (q * jnp.exp2(b)).astype(BF16)
        k_out = (k * jnp.exp2(b_last - b)).astype(BF16)
        return a.astype(BF16), q_in, k_out, jnp.exp2(b_last)

    def body(it, state_t):
        st1 = [cumsums(it * unroll + u) for u in range(unroll)]
        st2 = [scores(rows, e) for rows, e in st1]
        st3 = []
        for (rows, _), (a, q_in, k_out, decay) in zip(st1, st2):
            v = v_ref[0, 0, rows, :]
            st3.append((rows, _dot(a, v), q_in, _dot_tn(v, k_out), decay))
        for rows, o_intra, q_in, ds_t, decay in st3:
            o = o_intra + _dot_nt(q_in, state_t.astype(BF16))
            state_t = state_t * decay + ds_t
            y = _rms(o, gain) * og_ref[0, 0, rows, :].astype(F32)
            o_ref[rows, :] = y.astype(BF16)
        return state_t

    lax.fori_loop(0, seq // (c * unroll), body, jnp.zeros((HEAD_W, HEAD_W), F32))


def _hgrn(hin, logf, gain, *, batch, seq, c, unroll):
    w, lvl, levels, n_fine = _hgrn_tables(c)
    assert seq % (c * unroll) == 0

    def head(slab):
        return pl.BlockSpec((1, 1, seq, HEAD_W), lambda b, h: (b, slab * HEADS + h, 0, 0))

    return pl.pallas_call(
        functools.partial(_hgrn_kernel, c=c, levels=levels, n_fine=n_fine, seq=seq, unroll=unroll),
        out_shape=jax.ShapeDtypeStruct((batch * seq, MIX_W), BF16),
        grid=(batch, HEADS),
        in_specs=[
            pl.BlockSpec(w.shape, lambda b, h: (0, 0)),
            pl.BlockSpec((c, c), lambda b, h: (0, 0)),
            head(0), head(1), head(2), head(3), head(0),
            pl.BlockSpec((1, HEAD_W), lambda b, h: (0, 0)),
        ],
        out_specs=pl.BlockSpec((seq, HEAD_W), lambda b, h: (b, h)),
        compiler_params=_cparams(("parallel", "parallel")),
        name="hgrn",
    )(w, lvl, hin, hin, hin, hin, logf, gain.reshape(1, HEAD_W))


def _merge_kernel(x_ref, ya_ref, yb_ref, gt_ref, wa_ref, wb_ref, wo_ref, *rest, n_side):
    side_in, (o_ref,), side_out = rest[:n_side], rest[n_side:n_side + 1], rest[n_side + 1:]
    d = x_ref.shape[1]
    gt = gt_ref[...].astype(F32)
    merged = gt[:, 0:d] * _dot(ya_ref[...], wa_ref[...]) + gt[:, d:2 * d] * _dot(yb_ref[...], wb_ref[...])
    o_ref[...] = x_ref[...] + _dot(merged.astype(BF16), wo_ref[...])
    _SideCast.run(side_in, side_out)


def _resident(shape):
    return pl.BlockSpec(shape, lambda i: (0,) * len(shape), pipeline_mode=pl.Buffered(1))


def _merge(x, ya, yb, gates, wa, wb, wo, side_weights=(), *, tm):
    m, d = x.shape
    side = _SideCast(side_weights, m // tm, lambda i: i)
    out = pl.pallas_call(
        functools.partial(_merge_kernel, n_side=side.n),
        out_shape=[jax.ShapeDtypeStruct((m, d), F32)] + side.out_shapes,
        grid=(m // tm,),
        in_specs=[
            pl.BlockSpec((tm, d), lambda i: (i, 0)),
            pl.BlockSpec((tm, MIX_W), lambda i: (i, 0)),
            pl.BlockSpec((tm, MIX_W), lambda i: (i, 0)),
            pl.BlockSpec((tm, 2 * d), lambda i: (i, 0)),
            _resident(wa.shape), _resident(wb.shape), _resident(wo.shape),
        ] + side.in_specs,
        out_specs=[pl.BlockSpec((tm, d), lambda i: (i, 0))] + side.out_specs,
        compiler_params=_cparams(("arbitrary",)),
        name="merge",
    )(x, ya, yb, gates, wa, wb, wo, *side.arrays)
    return out[0], out[1:]


def _ple_kernel(x_ref, p_ref, gg_ref, pg_ref, wg_ref, wp_ref, o_ref):
    x = x_ref[...]
    gate = _sigmoid(_dot(_rms(x, gg_ref[...]).astype(BF16), wg_ref[...]))
    ple = _rms(_dot(p_ref[...].astype(BF16), wp_ref[...]), pg_ref[...])
    o_ref[...] = x + gate * ple


def _ple(x, p, gate_gain, post_gain, w_gate, w_proj, *, tm):
    m, d = x.shape
    pd = p.shape[1]
    return pl.pallas_call(
        _ple_kernel,
        out_shape=jax.ShapeDtypeStruct((m, d), F32),
        grid=(m // tm,),
        in_specs=[
            pl.BlockSpec((tm, d), lambda i: (i, 0)),
            pl.BlockSpec((tm, pd), lambda i: (i, 0)),
            pl.BlockSpec((1, d), lambda i: (0, 0)),
            pl.BlockSpec((1, d), lambda i: (0, 0)),
            _resident(w_gate.shape), _resident(w_proj.shape),
        ],
        out_specs=pl.BlockSpec((tm, d), lambda i: (i, 0)),
        compiler_params=_cparams(("parallel",)),
        name="ple",
    )(x, p, gate_gain.reshape(1, d), post_gain.reshape(1, d), w_gate, w_proj)


def _tiles(seq, d, f):
    def pick(n, choices):
        return next(c for c in choices if n % c == 0)
    return dict(
        tm=pick(seq, (512, 256, 128)),
        tfm=pick(seq, (1024, 512, 256, 128)),
        tf=pick(f, (512, 256, 128)),
        tp=pick(seq, (1024, 512, 256, 128)),
        tn=pick(math.gcd(MIX_W, 2 * d), (1024, 512, 256, 128)),
        ta=pick(seq, (512, 256, 128)),
        c=128,
        cu=16,
    )


def kernel(x, p, ffn1_norm, ffn1_w_gate, ffn1_w_up, ffn1_w_down, mix_norm, w_in, q_norm, k_norm, lambda_q1, lambda_k1, lambda_q2, lambda_k2, diff_subln, rel_bias, hgrn_lb_logits, hgrn_norm, w_branch_a, w_branch_b, w_out, ffn2_norm, ffn2_w_gate, ffn2_w_up, ffn2_w_down, ple_gate_norm, w_ple_gate, w_ple_proj, ple_post_norm):
    batch, seq, d = x.shape
    depth = ffn1_norm.shape[0]
    assert depth == 1
    m = batch * seq
    tl = _tiles(seq, d, ffn1_w_gate.shape[2])
    bf = lambda w: w.astype(BF16)

    lower_bounds = jnp.cumsum(jax.nn.softmax(hgrn_lb_logits.astype(F32), axis=0), axis=0)
    lb = lower_bounds[0].reshape(HEADS, HEAD_W)
    lam = (jnp.exp(jnp.sum(lambda_q1[0].astype(F32) * lambda_k1[0].astype(F32)))
           - jnp.exp(jnp.sum(lambda_q2[0].astype(F32) * lambda_k2[0].astype(F32)))
           + LAMBDA_INIT)
    scale = QK_DIM ** -0.5
    qg = jnp.tile(q_norm[0].astype(F32), 2).reshape(1, HEAD_W) * (scale * LOG2E)
    kg = jnp.tile(k_norm[0].astype(F32), 2).reshape(1, HEAD_W)
    qk_bound = (QK_DIM * scale * jnp.max(jnp.abs(q_norm[0].astype(F32)))
                * jnp.max(jnp.abs(k_norm[0].astype(F32))))

    x2 = x.reshape(m, d)
    x2, (w_in_b, wa_b, wb_b, wo_b) = _ffn(
        x2, ffn1_norm[0], bf(ffn1_w_gate[0]), bf(ffn1_w_up[0]), bf(ffn1_w_down[0]),
        (w_in[0], w_branch_a[0], w_branch_b[0], w_out[0]), tm=tl["tfm"], tf=tl["tf"])

    qkv, hin, logf, gates, (wg2_b, wu2_b, wd2_b) = _proj(
        x2, mix_norm[0], w_in_b, qg, kg, lb, (ffn2_w_gate[0], ffn2_w_up[0], ffn2_w_down[0]),
        batch=batch, seq=seq, tm=tl["tp"], tn=tl["tn"])
    ya = _attn(qkv, lam, rel_bias, qk_bound, diff_subln[0], batch=batch, seq=seq, t=tl["ta"])
    yb = _hgrn(hin, logf, hgrn_norm[0], batch=batch, seq=seq, c=tl["c"], unroll=tl["cu"])
    x2, (wpg_b, wpp_b) = _merge(x2, ya, yb, gates, wa_b, wb_b, wo_b,
                                (w_ple_gate[0], w_ple_proj[0]), tm=tl["tm"])

    x2, _ = _ffn(x2, ffn2_norm[0], wg2_b, wu2_b, wd2_b, tm=tl["tfm"], tf=tl["tf"])
    x2 = _ple(x2, p[0].reshape(m, -1), ple_gate_norm[0], ple_post_norm[0], wpg_b, wpp_b, tm=tl["tm"])
    return x2.reshape(batch, seq, d)
```

```python
import functools
import math

import jax
import jax.numpy as jnp
import numpy as np
from jax import lax
from jax.experimental import pallas as pl
from jax.experimental.pallas import tpu as pltpu

F32 = jnp.float32
BF16 = jnp.bfloat16

EPS = 1e-6
HEADS = 8
HEAD_W = 128
QK_DIM = 64
MIX_W = HEADS * HEAD_W
REL_BUCKETS = 32
REL_MAX_DIST = 128
LAMBDA_INIT = 0.8 - 0.6 * math.exp(-0.3 * 0)
LOG2E = math.log2(math.e)
MAX_STATIC_SOFTMAX_SPREAD = 60.0

V7X_MXU_WIDTH = 256
V7X_VMEM_BYTES = 64 * 1024 * 1024
VMEM_LIMIT = 56 * 1024 * 1024


def _cparams(sem, flags=None):
    return pltpu.CompilerParams(dimension_semantics=sem, vmem_limit_bytes=VMEM_LIMIT, flags=flags)


def _sigmoid(x):
    return 1.0 / (1.0 + jnp.exp(-x))


def _rms(x, gain):
    ms = jnp.mean(x * x, axis=-1, keepdims=True)
    return x * lax.rsqrt(ms + EPS) * gain


def _dot(a, b):
    return jnp.dot(a, b, preferred_element_type=F32)


def _dot_nt(a, b):
    return lax.dot_general(a, b, (((1,), (1,)), ((), ())), preferred_element_type=F32)


def _dot_tn(a, b):
    return lax.dot_general(a, b, (((0,), (0,)), ((), ())), preferred_element_type=F32)


BF16_ROW_TILE = 16


class _SideCast:
    def __init__(self, arrays, steps, step_index):
        self.arrays = list(arrays)
        self.n = len(self.arrays)
        self.step_index = step_index
        self.chunks = []
        for w in self.arrays:
            rows = w.shape[0]
            self.chunks.append(max(c for c in range(1, steps + 1)
                                   if rows % c == 0 and (rows // c) % BF16_ROW_TILE == 0))

    def _specs(self):
        def spec(w, c):
            return pl.BlockSpec((w.shape[0] // c, w.shape[1]),
                                lambda *ids: (jnp.minimum(self.step_index(*ids), c - 1), 0))
        return [spec(w, c) for w, c in zip(self.arrays, self.chunks)]

    in_specs = property(_specs)
    out_specs = property(_specs)

    @property
    def out_shapes(self):
        return [jax.ShapeDtypeStruct(w.shape, BF16) for w in self.arrays]

    @staticmethod
    def run(in_refs, out_refs):
        for src, dst in zip(in_refs, out_refs):
            dst[...] = src[...].astype(BF16)


def _row_tile_fetch(x_hbm, xs_ref, sem, tm):
    i = pl.program_id(0)
    j = pl.program_id(1)

    def copy(row_tile):
        return pltpu.make_async_copy(x_hbm.at[pl.ds(row_tile * tm, tm), :], xs_ref, sem)

    @pl.when((i == 0) & (j == 0))
    def _():
        copy(0).start()

    @pl.when((j == 1) & (i + 1 < pl.num_programs(0)))
    def _():
        copy(i + 1).start()

    return copy(i)


def _ffn_kernel(x_hbm, g_ref, wg_ref, wu_ref, wd_ref, *rest, tm, n_side):
    side_in, (o_ref,), side_out = rest[:n_side], rest[n_side:n_side + 1], rest[n_side + 1:2 * n_side + 1]
    xs_ref, h_ref, sem = rest[2 * n_side + 1:]
    j = pl.program_id(1)
    x_copy = _row_tile_fetch(x_hbm, xs_ref, sem, tm)

    @pl.when(j == 0)
    def _():
        x_copy.wait()
        x = xs_ref[...]
        h_ref[...] = _rms(x, g_ref[...]).astype(BF16)
        o_ref[...] = x

    h = h_ref[...]
    gate = _dot(h, wg_ref[...])
    up = _dot(h, wu_ref[...])
    act = (gate * _sigmoid(gate) * (0.5 * up)).astype(BF16)
    o_ref[...] += _dot(act, wd_ref[...])
    _SideCast.run(side_in, side_out)


def _ffn(x, gain, w_gate, w_up, w_down, side_weights=(), *, tm, tf):
    m, d = x.shape
    f = w_gate.shape[1]
    nj = f // tf
    assert nj >= 2
    side = _SideCast(side_weights, (m // tm) * nj, lambda i, j: i * nj + j)
    out = pl.pallas_call(
        functools.partial(_ffn_kernel, tm=tm, n_side=side.n),
        out_shape=[jax.ShapeDtypeStruct((m, d), F32)] + side.out_shapes,
        grid=(m // tm, nj),
        in_specs=[
            pl.BlockSpec(memory_space=pl.ANY),
            pl.BlockSpec((1, d), lambda i, j: (0, 0)),
            pl.BlockSpec((d, tf), lambda i, j: (0, j)),
            pl.BlockSpec((d, tf), lambda i, j: (0, j)),
            pl.BlockSpec((tf, d), lambda i, j: (j, 0)),
        ] + side.in_specs,
        out_specs=[pl.BlockSpec((tm, d), lambda i, j: (i, 0))] + side.out_specs,
        scratch_shapes=[pltpu.VMEM((tm, d), F32), pltpu.VMEM((tm, d), BF16),
                        pltpu.SemaphoreType.DMA(())],
        compiler_params=_cparams(("arbitrary", "arbitrary")),
        name="ffn",
    )(x, gain.reshape(1, d), w_gate, w_up, w_down, *side.arrays)
    return out[0], out[1:]


def _proj_kernel(x_hbm, g_ref, w_ref, qg_ref, kg_ref, lb_ref, gm_ref, *rest, tm, tn, nj, n_side):
    side_in, rest = rest[:n_side], rest[n_side:]
    (oqkv_ref, oh_ref, olf_ref, og_ref), rest = rest[:4], rest[4:]
    side_out, (xs_ref, h_ref, acc_ref, sem) = rest[:n_side], rest[n_side:]
    j = pl.program_id(1)
    jp = j - 1
    tps = MIX_W // tn
    hpt = tn // HEAD_W

    def heads(vals):
        return [vals[:, s * HEAD_W:(s + 1) * HEAD_W] for s in range(hpt)]

    def epi_qk(acc):
        gain = jnp.where(jp < tps, qg_ref[...], kg_ref[...])
        pw = gm_ref.shape[0]
        for s in range(tn // pw):
            a = acc[:, s * pw:(s + 1) * pw]
            ms = _dot((a * a).astype(BF16), gm_ref[...])
            y = a * lax.rsqrt(ms + EPS)
            for u in range(pw // HEAD_W):
                oqkv_ref[0, s * (pw // HEAD_W) + u] = (
                    y[:, u * HEAD_W:(u + 1) * HEAD_W] * gain).astype(BF16)

    def epi_v(acc):
        for s, a in enumerate(heads(acc)):
            oqkv_ref[0, s] = a.astype(BF16)

    def epi_silu(acc):
        for s, a in enumerate(heads(acc * _sigmoid(acc))):
            oh_ref[0, s] = a.astype(BF16)

    def epi_forget(acc):
        jj = jp - 4 * tps
        for s, a in enumerate(heads(acc)):
            lb = lb_ref[pl.ds(jj * hpt + s, 1), :]
            forget = lb + (1.0 - lb) * _sigmoid(a)
            oh_ref[0, s] = (1.0 - forget).astype(BF16)
            olf_ref[0, s] = jnp.log2(forget)

    def epi_hv(acc):
        for s, a in enumerate(heads(acc)):
            oh_ref[0, s] = a.astype(BF16)

    def epi_gates(acc):
        og_ref[...] = _sigmoid(acc).astype(BF16)

    x_copy = _row_tile_fetch(x_hbm, xs_ref, sem, tm)

    @pl.when(j == 0)
    def _():
        x_copy.wait()
        h_ref[...] = _rms(xs_ref[...], g_ref[...]).astype(BF16)
        acc_ref[...] = _dot(h_ref[...], w_ref[...])
        _SideCast.run(side_in, side_out)

    def fused(cond, epi):
        @pl.when(cond & (j >= 1) & (j < nj))
        def _():
            acc = acc_ref[...]
            epi(acc)
            acc_ref[...] = _dot(h_ref[...], w_ref[...])
            _SideCast.run(side_in, side_out)

    fused(jp < 2 * tps, epi_qk)
    fused((jp >= 2 * tps) & (jp < 3 * tps), epi_v)
    fused(((jp >= 3 * tps) & (jp < 4 * tps)) | ((jp >= 6 * tps) & (jp < 7 * tps)), epi_silu)
    fused((jp >= 4 * tps) & (jp < 5 * tps), epi_forget)
    fused((jp >= 5 * tps) & (jp < 6 * tps), epi_hv)
    fused(jp >= 7 * tps, epi_gates)

    @pl.when(j == nj)
    def _():
        epi_gates(acc_ref[...])
        _SideCast.run(side_in, side_out)


def _proj(x, gain, w_in, qg, kg, lb, side_weights=(), *, batch, seq, tm, tn):
    m, d = x.shape
    n_in = w_in.shape[1]
    tps = MIX_W // tn
    hpt = tn // HEAD_W
    spb = seq // tm
    ng = 2 * d // tn
    nj = 7 * tps + ng
    assert n_in == 7 * MIX_W + 2 * d and MIX_W % tn == 0 and (2 * d) % tn == 0

    pw = math.gcd(tn, V7X_MXU_WIDTH)
    lane = np.arange(pw)
    gm = (lane[:, None] // QK_DIM == lane[None, :] // QK_DIM).astype(np.float32) / QK_DIM

    def hm_block(lo, n):
        return lambda i, j: (i // spb, jnp.clip(j - 1 - lo, 0, n - 1), i % spb, 0)

    side = _SideCast(side_weights, (m // tm) * (nj + 1), lambda i, j: i * (nj + 1) + j)
    out = pl.pallas_call(
        functools.partial(_proj_kernel, tm=tm, tn=tn, nj=nj, n_side=side.n),
        out_shape=[
            jax.ShapeDtypeStruct((batch, 3 * HEADS, seq, HEAD_W), BF16),
            jax.ShapeDtypeStruct((batch, 4 * HEADS, seq, HEAD_W), BF16),
            jax.ShapeDtypeStruct((batch, HEADS, seq, HEAD_W), F32),
            jax.ShapeDtypeStruct((m, 2 * d), BF16),
        ] + side.out_shapes,
        grid=(m // tm, nj + 1),
        in_specs=[
            pl.BlockSpec(memory_space=pl.ANY),
            pl.BlockSpec((1, d), lambda i, j: (0, 0)),
            pl.BlockSpec((d, tn), lambda i, j: (0, jnp.minimum(j, nj - 1))),
            pl.BlockSpec((1, HEAD_W), lambda i, j: (0, 0)),
            pl.BlockSpec((1, HEAD_W), lambda i, j: (0, 0)),
            pl.BlockSpec((HEADS, HEAD_W), lambda i, j: (0, 0)),
            pl.BlockSpec((pw, pw), lambda i, j: (0, 0)),
        ] + side.in_specs,
        out_specs=[
            pl.BlockSpec((1, hpt, tm, HEAD_W), hm_block(0, 3 * tps)),
            pl.BlockSpec((1, hpt, tm, HEAD_W), hm_block(3 * tps, 4 * tps)),
            pl.BlockSpec((1, hpt, tm, HEAD_W), hm_block(4 * tps, tps)),
            pl.BlockSpec((tm, tn), lambda i, j: (i, jnp.clip(j - 1 - 7 * tps, 0, ng - 1))),
        ] + side.out_specs,
        scratch_shapes=[pltpu.VMEM((tm, d), F32), pltpu.VMEM((tm, d), BF16),
                        pltpu.VMEM((tm, tn), F32), pltpu.SemaphoreType.DMA(())],
        compiler_params=_cparams(("arbitrary", "arbitrary")),
        name="proj",
    )(x, gain.reshape(1, d), w_in, qg, kg, lb, jnp.asarray(gm, BF16), *side.arrays)
    return out[0], out[1], out[2], out[3], out[4:]


def _attn_kernel(lam_ref, off_ref, q_ref, k_ref, v_ref, w_ref, sg_ref, *rest, t, nkb, online, n_side):
    side_in, (o_ref,), side_out = rest[:n_side], rest[n_side:n_side + 1], rest[n_side + 1:2 * n_side + 1]
    vt_ref, bias_ref = rest[2 * n_side + 1:]
    h = pl.program_id(0)
    b = pl.program_id(1)
    i = pl.program_id(2)
    lam = lam_ref[0]
    far_off = off_ref[h]

    @pl.when((b == 0) & (i == 0))
    def _():
        for dd in range(2):
            rows = jnp.broadcast_to(w_ref[0, dd:dd + 1, :], (t, 2 * t))
            bias_ref[dd] = pltpu.roll(rows, 0, 1, stride=1, stride_axis=0)[:, 0:t]

    @pl.when(i == 0)
    def _():
        for jb in range(nkb):
            vt_ref[jb] = v_ref[0, 0, jb * t:(jb + 1) * t, :].astype(F32).T.astype(BF16)

    def colsum8(p):
        return jnp.sum(p.reshape(t // 8, 8, 2 * t), axis=0)

    def both(x):
        return jnp.concatenate([x, x], axis=1)

    def query_block(iq):
        q = q_ref[0, 0]
        lane = lax.broadcasted_iota(jnp.int32, (t, HEAD_W), 1)
        zero = jnp.zeros_like(q)
        qs = jnp.concatenate([jnp.where(lane < QK_DIM, q, zero),
                              jnp.where(lane >= QK_DIM, q, zero)], axis=0)
        offsets = [far_off] * max(iq - 1, 0)
        if iq >= 1:
            offsets.append(both(bias_ref[1]))
        offsets.append(both(bias_ref[0]))
        m = l = acc = None
        for j, badd in enumerate(offsets):
            s = _dot_nt(k_ref[0, 0, j * t:(j + 1) * t, :], qs) + badd
            if online:
                bmax = jnp.max(s, axis=0, keepdims=True)
                m_new = bmax if m is None else jnp.maximum(m, bmax)
                p = jnp.exp2(s - m_new)
            else:
                p = jnp.exp2(s)
            psum = colsum8(p)
            pv = _dot(vt_ref[j], p.astype(BF16))
            if l is None:
                l, acc = psum, pv
            elif online:
                alpha = jnp.exp2(m - m_new)
                l, acc = alpha * l + psum, alpha * acc + pv
            else:
                l, acc = l + psum, acc + pv
            if online:
                m = m_new
        l = jnp.sum(l, axis=0, keepdims=True)
        o_t = acc[:, 0:t] / l[:, 0:t] - lam * (acc[:, t:2 * t] / l[:, t:2 * t])
        o_ref[...] = (_rms(o_t.T, sg_ref[...]) * (1.0 - LAMBDA_INIT)).astype(BF16)
        _SideCast.run(side_in, side_out)

    for iq in range(nkb):
        pl.when(i == iq)(functools.partial(query_block, iq))


def _t5_bucket(dist):
    n = jnp.maximum(dist, 0)
    max_exact = REL_BUCKETS // 2
    nf = jnp.maximum(n, 1).astype(F32)
    large = max_exact + (jnp.log(nf / max_exact) / math.log(REL_MAX_DIST / max_exact)
                         * (REL_BUCKETS - max_exact)).astype(jnp.int32)
    large = jnp.minimum(large, REL_BUCKETS - 1)
    return jnp.where(n < max_exact, n, large)


def _bias_generators(f, t):
    hole = jnp.full((HEADS, 1), -jnp.inf, F32)
    w0 = jnp.concatenate([f[:, 0:t], jnp.full((HEADS, t), -jnp.inf, F32)], axis=1)
    w1 = jnp.concatenate([f[:, t:2 * t], hole, f[:, 1:t]], axis=1)
    return jnp.stack([w0, w1], axis=1)


def _attn(qkv, lam, rel_bias, qk_bound, subln, side_weights=(), *, batch, seq, t):
    assert t >= REL_MAX_DIST
    nq = seq // t
    rb = rel_bias.astype(F32)
    spread = 2.0 * qk_bound + jnp.max(jnp.max(rb, axis=0) - jnp.min(rb, axis=0))
    static_ok = spread <= MAX_STATIC_SOFTMAX_SPREAD
    upper = jnp.where(static_ok, qk_bound + jnp.max(rb, axis=0), 0.0)
    f = rb[_t5_bucket(jnp.arange(2 * t))].T - upper[:, None]
    bias = _bias_generators(f * LOG2E, t)
    far_off = (rb[REL_BUCKETS - 1] - upper) * LOG2E

    side = _SideCast(side_weights, HEADS * batch * nq, lambda h, b, i: (h * batch + b) * nq + i)

    def call(online):
        out = pl.pallas_call(
            functools.partial(_attn_kernel, t=t, nkb=nq, online=online, n_side=side.n),
            out_shape=[jax.ShapeDtypeStruct((batch * seq, MIX_W), BF16)] + side.out_shapes,
            grid=(HEADS, batch, nq),
            in_specs=[
                pl.BlockSpec(memory_space=pltpu.SMEM),
                pl.BlockSpec(memory_space=pltpu.SMEM),
                pl.BlockSpec((1, 1, t, HEAD_W), lambda h, b, i: (b, h, i, 0)),
                pl.BlockSpec((1, 1, seq, HEAD_W), lambda h, b, i: (b, HEADS + h, 0, 0)),
                pl.BlockSpec((1, 1, seq, HEAD_W), lambda h, b, i: (b, 2 * HEADS + h, 0, 0)),
                pl.BlockSpec((1, 2, 2 * t), lambda h, b, i: (h, 0, 0)),
                pl.BlockSpec((1, HEAD_W), lambda h, b, i: (0, 0)),
            ] + side.in_specs,
            out_specs=[pl.BlockSpec((t, HEAD_W), lambda h, b, i: (b * nq + i, h))] + side.out_specs,
            scratch_shapes=[pltpu.VMEM((nq, HEAD_W, t), BF16),
                            pltpu.VMEM((2, t, t), F32)],
            compiler_params=_cparams(("arbitrary", "arbitrary", "arbitrary")),
            name="attn_online" if online else "attn",
        )(lam.reshape(1), far_off, qkv, qkv, qkv, bias, subln.reshape(1, HEAD_W), *side.arrays)
        return out[0], tuple(out[1:])

    return lax.cond(static_ok, lambda: call(False), lambda: call(True))


SUBLANES = 8


def _hgrn_tables(c):
    levels = int(math.log2(c))
    assert 2 ** levels == c
    t = np.arange(c)[:, None]
    s = np.arange(c)[None, :]
    blocks = [(s <= t).astype(np.float32)]
    level_id = np.where(t == s, 0, -1)
    n_fine = 0
    for l in range(1, levels + 1):
        g, half = 2 ** l, 2 ** (l - 1)
        mid = (t // g) * g + half - 1
        upper = (t % g) >= half
        if half < SUBLANES:
            above = upper & (s > mid) & (s <= t)
            below = (~upper) & (s > t) & (s <= mid)
            blocks.append((above | below).astype(np.float32))
            n_fine += 1
        pair = (t // g == s // g) & upper & ((s % g) < half)
        level_id = np.where(pair, l, level_id)
    w = np.concatenate(blocks, axis=0)
    return jnp.asarray(w, BF16), jnp.asarray(level_id, jnp.int32), levels, n_fine


def _hgrn_kernel(w_ref, lvl_ref, q_ref, k_ref, v_ref, og_ref, lf_ref, gn_ref, o_ref, *,
                 c, levels, n_fine, seq, unroll):
    gain = gn_ref[...]

    def cumsums(ci):
        rows = pl.ds(pl.multiple_of(ci * c, c), c)
        g = lf_ref[0, 0, rows, :]
        g_hi = g.astype(BF16)
        g_lo = (g - g_hi.astype(F32)).astype(BF16)
        e2 = _dot(w_ref[...], jnp.concatenate([g_hi, g_lo], axis=1))
        return rows, e2[:, 0:HEAD_W] + e2[:, HEAD_W:2 * HEAD_W]

    def scores(rows, e):
        qb = q_ref[0, 0, rows, :]
        kb = k_ref[0, 0, rows, :]
        q = qb.astype(F32)
        k = kb.astype(F32)
        b = e[0:c]
        lvl = lvl_ref[...]
        a = jnp.where(lvl == 0, _dot_nt(qb, kb), 0.0)
        for l in range(1, levels + 1):
            if l <= n_fine:
                d = e[l * c:(l + 1) * c]
            else:
                g, half = 2 ** l, 2 ** (l - 1)
                b3 = b.reshape(c // g, g, HEAD_W)
                mid = b3[:, half - 1:half, :]
                d = jnp.concatenate([mid - b3[:, 0:half, :], b3[:, half:g, :] - mid],
                                    axis=1).reshape(c, HEAD_W)
            x = jnp.exp2(d)
            a = jnp.where(lvl == l, _dot_nt((q * x).astype(BF16), (k * x).astype(BF16)), a)
        b_last = b[c - 1:c, :]
        q_in = (q * jnp.exp2(b)).astype(BF16)
        k_out = (k * jnp.exp2(b_last - b)).astype(BF16)
        return a.astype(BF16), q_in, k_out, jnp.exp2(b_last)

    def body(it, state_t):
        st1 = [cumsums(it * unroll + u) for u in range(unroll)]
        st2 = [scores(rows, e) for rows, e in st1]
        st3 = []
        for (rows, _), (a, q_in, k_out, decay) in zip(st1, st2):
            v = v_ref[0, 0, rows, :]
            st3.append((rows, _dot(a, v), q_in, _dot_tn(v, k_out), decay))
        for rows, o_intra, q_in, ds_t, decay in st3:
            o = o_intra + _dot_nt(q_in, state_t.astype(BF16))
            state_t = state_t * decay + ds_t
            y = _rms(o, gain) * og_ref[0, 0, rows, :].astype(F32)
            o_ref[rows, :] = y.astype(BF16)
        return state_t

    lax.fori_loop(0, seq // (c * unroll), body, jnp.zeros((HEAD_W, HEAD_W), F32))


def _hgrn(hin, logf, gain, *, batch, seq, c, unroll):
    w, lvl, levels, n_fine = _hgrn_tables(c)
    assert seq % (c * unroll) == 0

    def head(slab):
        return pl.BlockSpec((1, 1, seq, HEAD_W), lambda b, h: (b, slab * HEADS + h, 0, 0))

    return pl.pallas_call(
        functools.partial(_hgrn_kernel, c=c, levels=levels, n_fine=n_fine, seq=seq, unroll=unroll),
        out_shape=jax.ShapeDtypeStruct((batch * seq, MIX_W), BF16),
        grid=(batch, HEADS),
        in_specs=[
            pl.BlockSpec(w.shape, lambda b, h: (0, 0)),
            pl.BlockSpec((c, c), lambda b, h: (0, 0)),
            head(0), head(1), head(2), head(3), head(0),
            pl.BlockSpec((1, HEAD_W), lambda b, h: (0, 0)),
        ],
        out_specs=pl.BlockSpec((seq, HEAD_W), lambda b, h: (b, h)),
        compiler_params=_cparams(("parallel", "parallel")),
        name="hgrn",
    )(w, lvl, hin, hin, hin, hin, logf, gain.reshape(1, HEAD_W))


def _merge_kernel(x_ref, ya_ref, yb_ref, gt_ref, wa_ref, wb_ref, wo_ref, *rest, n_side):
    side_in, (o_ref,), side_out = rest[:n_side], rest[n_side:n_side + 1], rest[n_side + 1:]
    d = x_ref.shape[1]
    gt = gt_ref[...].astype(F32)
    merged = gt[:, 0:d] * _dot(ya_ref[...], wa_ref[...]) + gt[:, d:2 * d] * _dot(yb_ref[...], wb_ref[...])
    o_ref[...] = x_ref[...] + _dot(merged.astype(BF16), wo_ref[...])
    _SideCast.run(side_in, side_out)


def _resident(shape):
    return pl.BlockSpec(shape, lambda i: (0,) * len(shape), pipeline_mode=pl.Buffered(1))


def _merge(x, ya, yb, gates, wa, wb, wo, side_weights=(), *, tm):
    m, d = x.shape
    side = _SideCast(side_weights, m // tm, lambda i: i)
    out = pl.pallas_call(
        functools.partial(_merge_kernel, n_side=side.n),
        out_shape=[jax.ShapeDtypeStruct((m, d), F32)] + side.out_shapes,
        grid=(m // tm,),
        in_specs=[
            pl.BlockSpec((tm, d), lambda i: (i, 0)),
            pl.BlockSpec((tm, MIX_W), lambda i: (i, 0)),
            pl.BlockSpec((tm, MIX_W), lambda i: (i, 0)),
            pl.BlockSpec((tm, 2 * d), lambda i: (i, 0)),
            _resident(wa.shape), _resident(wb.shape), _resident(wo.shape),
        ] + side.in_specs,
        out_specs=[pl.BlockSpec((tm, d), lambda i: (i, 0))] + side.out_specs,
        compiler_params=_cparams(("arbitrary",)),
        name="merge",
    )(x, ya, yb, gates, wa, wb, wo, *side.arrays)
    return out[0], out[1:]


def _ple_kernel(x_ref, p_ref, gg_ref, pg_ref, wg_ref, wp_ref, o_ref):
    x = x_ref[...]
    gate = _sigmoid(_dot(_rms(x, gg_ref[...]).astype(BF16), wg_ref[...]))
    ple = _rms(_dot(p_ref[...].astype(BF16), wp_ref[...]), pg_ref[...])
    o_ref[...] = x + gate * ple


def _ple(x, p, gate_gain, post_gain, w_gate, w_proj, *, tm):
    m, d = x.shape
    pd = p.shape[1]
    return pl.pallas_call(
        _ple_kernel,
        out_shape=jax.ShapeDtypeStruct((m, d), F32),
        grid=(m // tm,),
        in_specs=[
            pl.BlockSpec((tm, d), lambda i: (i, 0)),
            pl.BlockSpec((tm, pd), lambda i: (i, 0)),
            pl.BlockSpec((1, d), lambda i: (0, 0)),
            pl.BlockSpec((1, d), lambda i: (0, 0)),
            _resident(w_gate.shape), _resident(w_proj.shape),
        ],
        out_specs=pl.BlockSpec((tm, d), lambda i: (i, 0)),
        compiler_params=_cparams(("parallel",)),
        name="ple",
    )(x, p, gate_gain.reshape(1, d), post_gain.reshape(1, d), w_gate, w_proj)


def _tiles(seq, d, f):
    def pick(n, choices):
        return next(c for c in choices if n % c == 0)
    return dict(
        tm=pick(seq, (512, 256, 128)),
        tfm=pick(seq, (1024, 512, 256, 128)),
        tf=pick(f, (512, 256, 128)),
        tp=pick(seq, (1024, 512, 256, 128)),
        tn=pick(math.gcd(MIX_W, 2 * d), (1024, 512, 256, 128)),
        ta=pick(seq, (512, 256, 128)),
        c=128,
        cu=16,
    )


def kernel(x, p, ffn1_norm, ffn1_w_gate, ffn1_w_up, ffn1_w_down, mix_norm, w_in, q_norm, k_norm, lambda_q1, lambda_k1, lambda_q2, lambda_k2, diff_subln, rel_bias, hgrn_lb_logits, hgrn_norm, w_branch_a, w_branch_b, w_out, ffn2_norm, ffn2_w_gate, ffn2_w_up, ffn2_w_down, ple_gate_norm, w_ple_gate, w_ple_proj, ple_post_norm):
    batch, seq, d = x.shape
    depth = ffn1_norm.shape[0]
    assert depth == 1
    m = batch * seq
    tl = _tiles(seq, d, ffn1_w_gate.shape[2])
    bf = lambda w: w.astype(BF16)

    lower_bounds = jnp.cumsum(jax.nn.softmax(hgrn_lb_logits.astype(F32), axis=0), axis=0)
    lb = lower_bounds[0].reshape(HEADS, HEAD_W)
    lam = (jnp.exp(jnp.sum(lambda_q1[0].astype(F32) * lambda_k1[0].astype(F32)))
           - jnp.exp(jnp.sum(lambda_q2[0].astype(F32) * lambda_k2[0].astype(F32)))
           + LAMBDA_INIT)
    scale = QK_DIM ** -0.5
    qg = jnp.tile(q_norm[0].astype(F32), 2).reshape(1, HEAD_W) * (scale * LOG2E)
    kg = jnp.tile(k_norm[0].astype(F32), 2).reshape(1, HEAD_W)
    qk_bound = (QK_DIM * scale * jnp.max(jnp.abs(q_norm[0].astype(F32)))
                * jnp.max(jnp.abs(k_norm[0].astype(F32))))

    x2 = x.reshape(m, d)
    x2, (w_in_b, wa_b, wb_b, wo_b) = _ffn(
        x2, ffn1_norm[0], bf(ffn1_w_gate[0]), bf(ffn1_w_up[0]), bf(ffn1_w_down[0]),
        (w_in[0], w_branch_a[0], w_branch_b[0], w_out[0]), tm=tl["tfm"], tf=tl["tf"])

    qkv, hin, logf, gates, _ = _proj(x2, mix_norm[0], w_in_b, qg, kg, lb,
                                     batch=batch, seq=seq, tm=tl["tp"], tn=tl["tn"])
    ya, (wg2_b, wu2_b, wd2_b) = _attn(
        qkv, lam, rel_bias, qk_bound, diff_subln[0], (ffn2_w_gate[0], ffn2_w_up[0], ffn2_w_down[0]),
        batch=batch, seq=seq, t=tl["ta"])
    yb = _hgrn(hin, logf, hgrn_norm[0], batch=batch, seq=seq, c=tl["c"], unroll=tl["cu"])
    x2, (wpg_b, wpp_b) = _merge(x2, ya, yb, gates, wa_b, wb_b, wo_b,
                                (w_ple_gate[0], w_ple_proj[0]), tm=tl["tm"])

    x2, _ = _ffn(x2, ffn2_norm[0], wg2_b, wu2_b, wd2_b, tm=tl["tfm"], tf=tl["tf"])
    x2 = _ple(x2, p[0].reshape(m, -1), ple_gate_norm[0], ple_post_norm[0], wpg_b, wpp_b, tm=tl["tm"])
    return x2.reshape(batch, seq, d)
```

```python
import functools
import math

import jax
import jax.numpy as jnp
import numpy as np
from jax import lax
from jax.experimental import pallas as pl
from jax.experimental.pallas import tpu as pltpu

F32 = jnp.float32
BF16 = jnp.bfloat16

EPS = 1e-6
HEADS = 8
HEAD_W = 128
QK_DIM = 64
MIX_W = HEADS * HEAD_W
REL_BUCKETS = 32
REL_MAX_DIST = 128
LAMBDA_INIT = 0.8 - 0.6 * math.exp(-0.3 * 0)
LOG2E = math.log2(math.e)
MAX_STATIC_SOFTMAX_SPREAD = 60.0

V7X_MXU_WIDTH = 256
V7X_VMEM_BYTES = 64 * 1024 * 1024
VMEM_LIMIT = 56 * 1024 * 1024


def _cparams(sem, flags=None):
    return pltpu.CompilerParams(dimension_semantics=sem, vmem_limit_bytes=VMEM_LIMIT, flags=flags)


def _sigmoid(x):
    return 1.0 / (1.0 + jnp.exp(-x))


def _rms(x, gain):
    ms = jnp.mean(x * x, axis=-1, keepdims=True)
    return x * lax.rsqrt(ms + EPS) * gain


def _dot(a, b):
    return jnp.dot(a, b, preferred_element_type=F32)


def _dot_nt(a, b):
    return lax.dot_general(a, b, (((1,), (1,)), ((), ())), preferred_element_type=F32)


def _dot_tn(a, b):
    return lax.dot_general(a, b, (((0,), (0,)), ((), ())), preferred_element_type=F32)


BF16_ROW_TILE = 16


class _SideCast:
    def __init__(self, arrays, steps, step_index):
        self.arrays = list(arrays)
        self.n = len(self.arrays)
        self.step_index = step_index
        self.chunks = []
        for w in self.arrays:
            rows = w.shape[0]
            self.chunks.append(max(c for c in range(1, steps + 1)
                                   if rows % c == 0 and (rows // c) % BF16_ROW_TILE == 0))

    def _specs(self):
        def spec(w, c):
            return pl.BlockSpec((w.shape[0] // c, w.shape[1]),
                                lambda *ids: (jnp.minimum(self.step_index(*ids), c - 1), 0))
        return [spec(w, c) for w, c in zip(self.arrays, self.chunks)]

    in_specs = property(_specs)
    out_specs = property(_specs)

    @property
    def out_shapes(self):
        return [jax.ShapeDtypeStruct(w.shape, BF16) for w in self.arrays]

    @staticmethod
    def run(in_refs, out_refs):
        for src, dst in zip(in_refs, out_refs):
            dst[...] = src[...].astype(BF16)


def _row_tile_fetch(x_hbm, xs_ref, sem, tm):
    i = pl.program_id(0)
    j = pl.program_id(1)

    def copy(row_tile):
        return pltpu.make_async_copy(x_hbm.at[pl.ds(row_tile * tm, tm), :], xs_ref, sem)

    @pl.when((i == 0) & (j == 0))
    def _():
        copy(0).start()

    @pl.when((j == 1) & (i + 1 < pl.num_programs(0)))
    def _():
        copy(i + 1).start()

    return copy(i)


def _ffn_kernel(x_hbm, g_ref, wg_ref, wu_ref, wd_ref, *rest, tm, n_side):
    side_in, (o_ref,), side_out = rest[:n_side], rest[n_side:n_side + 1], rest[n_side + 1:2 * n_side + 1]
    xs_ref, hn_ref, h_ref, sem = rest[2 * n_side + 1:]
    i = pl.program_id(0)
    j = pl.program_id(1)
    has_next = i + 1 < pl.num_programs(0)

    def x_copy(row_tile):
        return pltpu.make_async_copy(x_hbm.at[pl.ds(row_tile * tm, tm), :], xs_ref, sem)

    def norm_to_hn():
        hn_ref[...] = _rms(xs_ref[...], g_ref[...]).astype(BF16)

    def body(h, first):
        gate = _dot(h, wg_ref[...])
        up = _dot(h, wu_ref[...])
        act = (gate * _sigmoid(gate) * (0.5 * up)).astype(BF16)
        down = _dot(act, wd_ref[...])
        if first:
            o_ref[...] = xs_ref[...] + down
        else:
            o_ref[...] += down
        _SideCast.run(side_in, side_out)

    @pl.when((i == 0) & (j == 0))
    def _():
        first_copy = x_copy(0)
        first_copy.start()
        first_copy.wait()
        norm_to_hn()

    @pl.when(j == 0)
    def _():
        h = hn_ref[...]
        h_ref[...] = h
        body(h, True)

    @pl.when((j == 1) & has_next)
    def _():
        x_copy(i + 1).start()

    @pl.when((j == 2) & has_next)
    def _():
        x_copy(i + 1).wait()
        norm_to_hn()
        body(h_ref[...], False)

    @pl.when((j != 0) & jnp.logical_not((j == 2) & has_next))
    def _():
        body(h_ref[...], False)


def _ffn(x, gain, w_gate, w_up, w_down, side_weights=(), *, tm, tf):
    m, d = x.shape
    f = w_gate.shape[1]
    nj = f // tf
    assert nj >= 3
    side = _SideCast(side_weights, (m // tm) * nj, lambda i, j: i * nj + j)
    out = pl.pallas_call(
        functools.partial(_ffn_kernel, tm=tm, n_side=side.n),
        out_shape=[jax.ShapeDtypeStruct((m, d), F32)] + side.out_shapes,
        grid=(m // tm, nj),
        in_specs=[
            pl.BlockSpec(memory_space=pl.ANY),
            pl.BlockSpec((1, d), lambda i, j: (0, 0)),
            pl.BlockSpec((d, tf), lambda i, j: (0, j)),
            pl.BlockSpec((d, tf), lambda i, j: (0, j)),
            pl.BlockSpec((tf, d), lambda i, j: (j, 0)),
        ] + side.in_specs,
        out_specs=[pl.BlockSpec((tm, d), lambda i, j: (i, 0))] + side.out_specs,
        scratch_shapes=[pltpu.VMEM((tm, d), F32), pltpu.VMEM((tm, d), BF16),
                        pltpu.VMEM((tm, d), BF16), pltpu.SemaphoreType.DMA(())],
        compiler_params=_cparams(("arbitrary", "arbitrary")),
        name="ffn",
    )(x, gain.reshape(1, d), w_gate, w_up, w_down, *side.arrays)
    return out[0], out[1:]


def _proj_kernel(x_hbm, g_ref, w_ref, qg_ref, kg_ref, lb_ref, gm_ref, *rest, tm, tn, nj, n_side):
    side_in, rest = rest[:n_side], rest[n_side:]
    (oqkv_ref, oh_ref, olf_ref, og_ref), rest = rest[:4], rest[4:]
    side_out, (xs_ref, h_ref, acc_ref, sem) = rest[:n_side], rest[n_side:]
    j = pl.program_id(1)
    jp = j - 1
    tps = MIX_W // tn
    hpt = tn // HEAD_W

    def heads(vals):
        return [vals[:, s * HEAD_W:(s + 1) * HEAD_W] for s in range(hpt)]

    def epi_qk(acc):
        gain = jnp.where(jp < tps, qg_ref[...], kg_ref[...])
        pw = gm_ref.shape[0]
        for s in range(tn // pw):
            a = acc[:, s * pw:(s + 1) * pw]
            ms = _dot((a * a).astype(BF16), gm_ref[...])
            y = a * lax.rsqrt(ms + EPS)
            for u in range(pw // HEAD_W):
                oqkv_ref[0, s * (pw // HEAD_W) + u] = (
                    y[:, u * HEAD_W:(u + 1) * HEAD_W] * gain).astype(BF16)

    def epi_v(acc):
        for s, a in enumerate(heads(acc)):
            oqkv_ref[0, s] = a.astype(BF16)

    def epi_silu(acc):
        for s, a in enumerate(heads(acc * _sigmoid(acc))):
            oh_ref[0, s] = a.astype(BF16)

    def epi_forget(acc):
        jj = jp - 4 * tps
        for s, a in enumerate(heads(acc)):
            lb = lb_ref[pl.ds(jj * hpt + s, 1), :]
            forget = lb + (1.0 - lb) * _sigmoid(a)
            oh_ref[0, s] = (1.0 - forget).astype(BF16)
            olf_ref[0, s] = jnp.log2(forget)

    def epi_hv(acc):
        for s, a in enumerate(heads(acc)):
            oh_ref[0, s] = a.astype(BF16)

    def epi_gates(acc):
        og_ref[...] = _sigmoid(acc).astype(BF16)

    x_copy = _row_tile_fetch(x_hbm, xs_ref, sem, tm)

    @pl.when(j == 0)
    def _():
        x_copy.wait()
        h_ref[...] = _rms(xs_ref[...], g_ref[...]).astype(BF16)
        acc_ref[...] = _dot(h_ref[...], w_ref[...])
        _SideCast.run(side_in, side_out)

    def fused(cond, epi):
        @pl.when(cond & (j >= 1) & (j < nj))
        def _():
            acc = acc_ref[...]
            epi(acc)
            acc_ref[...] = _dot(h_ref[...], w_ref[...])
            _SideCast.run(side_in, side_out)

    fused(jp < 2 * tps, epi_qk)
    fused((jp >= 2 * tps) & (jp < 3 * tps), epi_v)
    fused(((jp >= 3 * tps) & (jp < 4 * tps)) | ((jp >= 6 * tps) & (jp < 7 * tps)), epi_silu)
    fused((jp >= 4 * tps) & (jp < 5 * tps), epi_forget)
    fused((jp >= 5 * tps) & (jp < 6 * tps), epi_hv)
    fused(jp >= 7 * tps, epi_gates)

    @pl.when(j == nj)
    def _():
        epi_gates(acc_ref[...])
        _SideCast.run(side_in, side_out)


def _proj(x, gain, w_in, qg, kg, lb, side_weights=(), *, batch, seq, tm, tn):
    m, d = x.shape
    n_in = w_in.shape[1]
    tps = MIX_W // tn
    hpt = tn // HEAD_W
    spb = seq // tm
    ng = 2 * d // tn
    nj = 7 * tps + ng
    assert n_in == 7 * MIX_W + 2 * d and MIX_W % tn == 0 and (2 * d) % tn == 0

    pw = math.gcd(tn, V7X_MXU_WIDTH)
    lane = np.arange(pw)
    gm = (lane[:, None] // QK_DIM == lane[None, :] // QK_DIM).astype(np.float32) / QK_DIM

    def hm_block(lo, n):
        return lambda i, j: (i // spb, jnp.clip(j - 1 - lo, 0, n - 1), i % spb, 0)

    side = _SideCast(side_weights, (m // tm) * (nj + 1), lambda i, j: i * (nj + 1) + j)
    out = pl.pallas_call(
        functools.partial(_proj_kernel, tm=tm, tn=tn, nj=nj, n_side=side.n),
        out_shape=[
            jax.ShapeDtypeStruct((batch, 3 * HEADS, seq, HEAD_W), BF16),
            jax.ShapeDtypeStruct((batch, 4 * HEADS, seq, HEAD_W), BF16),
            jax.ShapeDtypeStruct((batch, HEADS, seq, HEAD_W), F32),
            jax.ShapeDtypeStruct((m, 2 * d), BF16),
        ] + side.out_shapes,
        grid=(m // tm, nj + 1),
        in_specs=[
            pl.BlockSpec(memory_space=pl.ANY),
            pl.BlockSpec((1, d), lambda i, j: (0, 0)),
            pl.BlockSpec((d, tn), lambda i, j: (0, jnp.minimum(j, nj - 1))),
            pl.BlockSpec((1, HEAD_W), lambda i, j: (0, 0)),
            pl.BlockSpec((1, HEAD_W), lambda i, j: (0, 0)),
            pl.BlockSpec((HEADS, HEAD_W), lambda i, j: (0, 0)),
            pl.BlockSpec((pw, pw), lambda i, j: (0, 0)),
        ] + side.in_specs,
        out_specs=[
            pl.BlockSpec((1, hpt, tm, HEAD_W), hm_block(0, 3 * tps)),
            pl.BlockSpec((1, hpt, tm, HEAD_W), hm_block(3 * tps, 4 * tps)),
            pl.BlockSpec((1, hpt, tm, HEAD_W), hm_block(4 * tps, tps)),
            pl.BlockSpec((tm, tn), lambda i, j: (i, jnp.clip(j - 1 - 7 * tps, 0, ng - 1))),
        ] + side.out_specs,
        scratch_shapes=[pltpu.VMEM((tm, d), F32), pltpu.VMEM((tm, d), BF16),
                        pltpu.VMEM((tm, tn), F32), pltpu.SemaphoreType.DMA(())],
        compiler_params=_cparams(("arbitrary", "arbitrary")),
        name="proj",
    )(x, gain.reshape(1, d), w_in, qg, kg, lb, jnp.asarray(gm, BF16), *side.arrays)
    return out[0], out[1], out[2], out[3], out[4:]


def _attn_kernel(lam_ref, off_ref, q_ref, k_ref, v_ref, w_ref, sg_ref, *rest, t, nkb, online, n_side):
    side_in, (o_ref,), side_out = rest[:n_side], rest[n_side:n_side + 1], rest[n_side + 1:2 * n_side + 1]
    vt_ref, bias_ref = rest[2 * n_side + 1:]
    h = pl.program_id(0)
    b = pl.program_id(1)
    i = pl.program_id(2)
    lam = lam_ref[0]
    far_off = off_ref[h]

    @pl.when((b == 0) & (i == 0))
    def _():
        for dd in range(2):
            rows = jnp.broadcast_to(w_ref[0, dd:dd + 1, :], (t, 2 * t))
            bias_ref[dd] = pltpu.roll(rows, 0, 1, stride=1, stride_axis=0)[:, 0:t]

    @pl.when(i == 0)
    def _():
        for jb in range(nkb):
            vt_ref[jb] = v_ref[0, 0, jb * t:(jb + 1) * t, :].astype(F32).T.astype(BF16)

    def colsum8(p):
        return jnp.sum(p.reshape(t // 8, 8, 2 * t), axis=0)

    def both(x):
        return jnp.concatenate([x, x], axis=1)

    def query_block(iq):
        q = q_ref[0, 0]
        lane = lax.broadcasted_iota(jnp.int32, (t, HEAD_W), 1)
        zero = jnp.zeros_like(q)
        qs = jnp.concatenate([jnp.where(lane < QK_DIM, q, zero),
                              jnp.where(lane >= QK_DIM, q, zero)], axis=0)
        offsets = [far_off] * max(iq - 1, 0)
        if iq >= 1:
            offsets.append(both(bias_ref[1]))
        offsets.append(both(bias_ref[0]))
        m = l = acc = None
        for j, badd in enumerate(offsets):
            s = _dot_nt(k_ref[0, 0, j * t:(j + 1) * t, :], qs) + badd
            if online:
                bmax = jnp.max(s, axis=0, keepdims=True)
                m_new = bmax if m is None else jnp.maximum(m, bmax)
                p = jnp.exp2(s - m_new)
            else:
                p = jnp.exp2(s)
            psum = colsum8(p)
            pv = _dot(vt_ref[j], p.astype(BF16))
            if l is None:
                l, acc = psum, pv
            elif online:
                alpha = jnp.exp2(m - m_new)
                l, acc = alpha * l + psum, alpha * acc + pv
            else:
                l, acc = l + psum, acc + pv
            if online:
                m = m_new
        l = jnp.sum(l, axis=0, keepdims=True)
        o_t = acc[:, 0:t] / l[:, 0:t] - lam * (acc[:, t:2 * t] / l[:, t:2 * t])
        o_ref[...] = (_rms(o_t.T, sg_ref[...]) * (1.0 - LAMBDA_INIT)).astype(BF16)
        _SideCast.run(side_in, side_out)

    for iq in range(nkb):
        pl.when(i == iq)(functools.partial(query_block, iq))


def _t5_bucket(dist):
    n = jnp.maximum(dist, 0)
    max_exact = REL_BUCKETS // 2
    nf = jnp.maximum(n, 1).astype(F32)
    large = max_exact + (jnp.log(nf / max_exact) / math.log(REL_MAX_DIST / max_exact)
                         * (REL_BUCKETS - max_exact)).astype(jnp.int32)
    large = jnp.minimum(large, REL_BUCKETS - 1)
    return jnp.where(n < max_exact, n, large)


def _bias_generators(f, t):
    hole = jnp.full((HEADS, 1), -jnp.inf, F32)
    w0 = jnp.concatenate([f[:, 0:t], jnp.full((HEADS, t), -jnp.inf, F32)], axis=1)
    w1 = jnp.concatenate([f[:, t:2 * t], hole, f[:, 1:t]], axis=1)
    return jnp.stack([w0, w1], axis=1)


def _attn(qkv, lam, rel_bias, qk_bound, subln, side_weights=(), *, batch, seq, t):
    assert t >= REL_MAX_DIST
    nq = seq // t
    rb = rel_bias.astype(F32)
    spread = 2.0 * qk_bound + jnp.max(jnp.max(rb, axis=0) - jnp.min(rb, axis=0))
    static_ok = spread <= MAX_STATIC_SOFTMAX_SPREAD
    upper = jnp.where(static_ok, qk_bound + jnp.max(rb, axis=0), 0.0)
    f = rb[_t5_bucket(jnp.arange(2 * t))].T - upper[:, None]
    bias = _bias_generators(f * LOG2E, t)
    far_off = (rb[REL_BUCKETS - 1] - upper) * LOG2E

    side = _SideCast(side_weights, HEADS * batch * nq, lambda h, b, i: (h * batch + b) * nq + i)

    def call(online):
        out = pl.pallas_call(
            functools.partial(_attn_kernel, t=t, nkb=nq, online=online, n_side=side.n),
            out_shape=[jax.ShapeDtypeStruct((batch * seq, MIX_W), BF16)] + side.out_shapes,
            grid=(HEADS, batch, nq),
            in_specs=[
                pl.BlockSpec(memory_space=pltpu.SMEM),
                pl.BlockSpec(memory_space=pltpu.SMEM),
                pl.BlockSpec((1, 1, t, HEAD_W), lambda h, b, i: (b, h, i, 0)),
                pl.BlockSpec((1, 1, seq, HEAD_W), lambda h, b, i: (b, HEADS + h, 0, 0)),
                pl.BlockSpec((1, 1, seq, HEAD_W), lambda h, b, i: (b, 2 * HEADS + h, 0, 0)),
                pl.BlockSpec((1, 2, 2 * t), lambda h, b, i: (h, 0, 0)),
                pl.BlockSpec((1, HEAD_W), lambda h, b, i: (0, 0)),
            ] + side.in_specs,
            out_specs=[pl.BlockSpec((t, HEAD_W), lambda h, b, i: (b * nq + i, h))] + side.out_specs,
            scratch_shapes=[pltpu.VMEM((nq, HEAD_W, t), BF16),
                            pltpu.VMEM((2, t, t), F32)],
            compiler_params=_cparams(("arbitrary", "arbitrary", "arbitrary")),
            name="attn_online" if online else "attn",
        )(lam.reshape(1), far_off, qkv, qkv, qkv, bias, subln.reshape(1, HEAD_W), *side.arrays)
        return out[0], tuple(out[1:])

    return lax.cond(static_ok, lambda: call(False), lambda: call(True))


SUBLANES = 8


def _hgrn_tables(c):
    levels = int(math.log2(c))
    assert 2 ** levels == c
    t = np.arange(c)[:, None]
    s = np.arange(c)[None, :]
    blocks = [(s <= t).astype(np.float32)]
    level_id = np.where(t == s, 0, -1)
    n_fine = 0
    for l in range(1, levels + 1):
        g, half = 2 ** l, 2 ** (l - 1)
        mid = (t // g) * g + half - 1
        upper = (t % g) >= half
        if half < SUBLANES:
            above = upper & (s > mid) & (s <= t)
            below = (~upper) & (s > t) & (s <= mid)
            blocks.append((above | below).astype(np.float32))
            n_fine += 1
        pair = (t // g == s // g) & upper & ((s % g) < half)
        level_id = np.where(pair, l, level_id)
    w = np.concatenate(blocks, axis=0)
    return jnp.asarray(w, BF16), jnp.asarray(level_id, jnp.int32), levels, n_fine


def _hgrn_kernel(w_ref, lvl_ref, q_ref, k_ref, v_ref, og_ref, lf_ref, gn_ref, o_ref, *,
                 c, levels, n_fine, seq, unroll):
    gain = gn_ref[...]

    def cumsums(ci):
        rows = pl.ds(pl.multiple_of(ci * c, c), c)
        g = lf_ref[0, 0, rows, :]
        g_hi = g.astype(BF16)
        g_lo = (g - g_hi.astype(F32)).astype(BF16)
        e2 = _dot(w_ref[...], jnp.concatenate([g_hi, g_lo], axis=1))
        return rows, e2[:, 0:HEAD_W] + e2[:, HEAD_W:2 * HEAD_W]

    def scores(rows, e):
        qb = q_ref[0, 0, rows, :]
        kb = k_ref[0, 0, rows, :]
        q = qb.astype(F32)
        k = kb.astype(F32)
        b = e[0:c]
        lvl = lvl_ref[...]
        a = jnp.where(lvl == 0, _dot_nt(qb, kb), 0.0)
        for l in range(1, levels + 1):
            if l <= n_fine:
                d = e[l * c:(l + 1) * c]
            else:
                g, half = 2 ** l, 2 ** (l - 1)
                b3 = b.reshape(c // g, g, HEAD_W)
                mid = b3[:, half - 1:half, :]
                d = jnp.concatenate([mid - b3[:, 0:half, :], b3[:, half:g, :] - mid],
                                    axis=1).reshape(c, HEAD_W)
            x = jnp.exp2(d)
            a = jnp.where(lvl == l, _dot_nt((q * x).astype(BF16), (k * x).astype(BF16)), a)
        b_last = b[c - 1:c, :]
        q_in = (q * jnp.exp2(b)).astype(BF16)
        k_out = (k * jnp.exp2(b_last - b)).astype(BF16)
        return a.astype(BF16), q_in, k_out, jnp.exp2(b_last)

    def body(it, state_t):
        st1 = [cumsums(it * unroll + u) for u in range(unroll)]
        st2 = [scores(rows, e) for rows, e in st1]
        st3 = []
        for (rows, _), (a, q_in, k_out, decay) in zip(st1, st2):
            v = v_ref[0, 0, rows, :]
            st3.append((rows, _dot(a, v), q_in, _dot_tn(v, k_out), decay))
        for rows, o_intra, q_in, ds_t, decay in st3:
            o = o_intra + _dot_nt(q_in, state_t.astype(BF16))
            state_t = state_t * decay + ds_t
            y = _rms(o, gain) * og_ref[0, 0, rows, :].astype(F32)
            o_ref[rows, :] = y.astype(BF16)
        return state_t

    lax.fori_loop(0, seq // (c * unroll), body, jnp.zeros((HEAD_W, HEAD_W), F32))


def _hgrn(hin, logf, gain, *, batch, seq, c, unroll):
    w, lvl, levels, n_fine = _hgrn_tables(c)
    assert seq % (c * unroll) == 0

    def head(slab):
        return pl.BlockSpec((1, 1, seq, HEAD_W), lambda b, h: (b, slab * HEADS + h, 0, 0))

    return pl.pallas_call(
        functools.partial(_hgrn_kernel, c=c, levels=levels, n_fine=n_fine, seq=seq, unroll=unroll),
        out_shape=jax.ShapeDtypeStruct((batch * seq, MIX_W), BF16),
        grid=(batch, HEADS),
        in_specs=[
            pl.BlockSpec(w.shape, lambda b, h: (0, 0)),
            pl.BlockSpec((c, c), lambda b, h: (0, 0)),
            head(0), head(1), head(2), head(3), head(0),
            pl.BlockSpec((1, HEAD_W), lambda b, h: (0, 0)),
        ],
        out_specs=pl.BlockSpec((seq, HEAD_W), lambda b, h: (b, h)),
        compiler_params=_cparams(("parallel", "parallel")),
        name="hgrn",
    )(w, lvl, hin, hin, hin, hin, logf, gain.reshape(1, HEAD_W))


def _merge_kernel(x_ref, ya_ref, yb_ref, gt_ref, wa_ref, wb_ref, wo_ref, *rest, n_side):
    side_in, (o_ref,), side_out = rest[:n_side], rest[n_side:n_side + 1], rest[n_side + 1:]
    d = x_ref.shape[1]
    gt = gt_ref[...].astype(F32)
    merged = gt[:, 0:d] * _dot(ya_ref[...], wa_ref[...]) + gt[:, d:2 * d] * _dot(yb_ref[...], wb_ref[...])
    o_ref[...] = x_ref[...] + _dot(merged.astype(BF16), wo_ref[...])
    _SideCast.run(side_in, side_out)


def _resident(shape):
    return pl.BlockSpec(shape, lambda i: (0,) * len(shape), pipeline_mode=pl.Buffered(1))


def _merge(x, ya, yb, gates, wa, wb, wo, side_weights=(), *, tm):
    m, d = x.shape
    side = _SideCast(side_weights, m // tm, lambda i: i)
    out = pl.pallas_call(
        functools.partial(_merge_kernel, n_side=side.n),
        out_shape=[jax.ShapeDtypeStruct((m, d), F32)] + side.out_shapes,
        grid=(m // tm,),
        in_specs=[
            pl.BlockSpec((tm, d), lambda i: (i, 0)),
            pl.BlockSpec((tm, MIX_W), lambda i: (i, 0)),
            pl.BlockSpec((tm, MIX_W), lambda i: (i, 0)),
            pl.BlockSpec((tm, 2 * d), lambda i: (i, 0)),
            _resident(wa.shape), _resident(wb.shape), _resident(wo.shape),
        ] + side.in_specs,
        out_specs=[pl.BlockSpec((tm, d), lambda i: (i, 0))] + side.out_specs,
        compiler_params=_cparams(("arbitrary",)),
        name="merge",
    )(x, ya, yb, gates, wa, wb, wo, *side.arrays)
    return out[0], out[1:]


def _ple_kernel(x_ref, p_ref, gg_ref, pg_ref, wg_ref, wp_ref, o_ref):
    x = x_ref[...]
    gate = _sigmoid(_dot(_rms(x, gg_ref[...]).astype(BF16), wg_ref[...]))
    ple = _rms(_dot(p_ref[...].astype(BF16), wp_ref[...]), pg_ref[...])
    o_ref[...] = x + gate * ple


def _ple(x, p, gate_gain, post_gain, w_gate, w_proj, *, tm):
    m, d = x.shape
    pd = p.shape[1]
    return pl.pallas_call(
        _ple_kernel,
        out_shape=jax.ShapeDtypeStruct((m, d), F32),
        grid=(m // tm,),
        in_specs=[
            pl.BlockSpec((tm, d), lambda i: (i, 0)),
            pl.BlockSpec((tm, pd), lambda i: (i, 0)),
            pl.BlockSpec((1, d), lambda i: (0, 0)),
            pl.BlockSpec((1, d), lambda i: (0, 0)),
            _resident(w_gate.shape), _resident(w_proj.shape),
        ],
        out_specs=pl.BlockSpec((tm, d), lambda i: (i, 0)),
        compiler_params=_cparams(("parallel",)),
        name="ple",
    )(x, p, gate_gain.reshape(1, d), post_gain.reshape(1, d), w_gate, w_proj)


def _tiles(seq, d, f):
    def pick(n, choices):
        return next(c for c in choices if n % c == 0)
    return dict(
        tm=pick(seq, (512, 256, 128)),
        tfm=pick(seq, (1024, 512, 256, 128)),
        tf=pick(f, (512, 256, 128)),
        tp=pick(seq, (1024, 512, 256, 128)),
        tn=pick(math.gcd(MIX_W, 2 * d), (1024, 512, 256, 128)),
        ta=pick(seq, (512, 256, 128)),
        c=128,
        cu=16,
    )


def kernel(x, p, ffn1_norm, ffn1_w_gate, ffn1_w_up, ffn1_w_down, mix_norm, w_in, q_norm, k_norm, lambda_q1, lambda_k1, lambda_q2, lambda_k2, diff_subln, rel_bias, hgrn_lb_logits, hgrn_norm, w_branch_a, w_branch_b, w_out, ffn2_norm, ffn2_w_gate, ffn2_w_up, ffn2_w_down, ple_gate_norm, w_ple_gate, w_ple_proj, ple_post_norm):
    batch, seq, d = x.shape
    depth = ffn1_norm.shape[0]
    assert depth == 1
    m = batch * seq
    tl = _tiles(seq, d, ffn1_w_gate.shape[2])
    bf = lambda w: w.astype(BF16)

    lower_bounds = jnp.cumsum(jax.nn.softmax(hgrn_lb_logits.astype(F32), axis=0), axis=0)
    lb = lower_bounds[0].reshape(HEADS, HEAD_W)
    lam = (jnp.exp(jnp.sum(lambda_q1[0].astype(F32) * lambda_k1[0].astype(F32)))
           - jnp.exp(jnp.sum(lambda_q2[0].astype(F32) * lambda_k2[0].astype(F32)))
           + LAMBDA_INIT)
    scale = QK_DIM ** -0.5
    qg = jnp.tile(q_norm[0].astype(F32), 2).reshape(1, HEAD_W) * (scale * LOG2E)
    kg = jnp.tile(k_norm[0].astype(F32), 2).reshape(1, HEAD_W)
    qk_bound = (QK_DIM * scale * jnp.max(jnp.abs(q_norm[0].astype(F32)))
                * jnp.max(jnp.abs(k_norm[0].astype(F32))))

    x2 = x.reshape(m, d)
    x2, (w_in_b, wa_b, wb_b, wo_b) = _ffn(
        x2, ffn1_norm[0], bf(ffn1_w_gate[0]), bf(ffn1_w_up[0]), bf(ffn1_w_down[0]),
        (w_in[0], w_branch_a[0], w_branch_b[0], w_out[0]), tm=tl["tfm"], tf=tl["tf"])

    qkv, hin, logf, gates, (wg2_b, wu2_b, wd2_b) = _proj(
        x2, mix_norm[0], w_in_b, qg, kg, lb, (ffn2_w_gate[0], ffn2_w_up[0], ffn2_w_down[0]),
        batch=batch, seq=seq, tm=tl["tp"], tn=tl["tn"])
    ya, _ = _attn(qkv, lam, rel_bias, qk_bound, diff_subln[0], batch=batch, seq=seq, t=tl["ta"])
    yb = _hgrn(hin, logf, hgrn_norm[0], batch=batch, seq=seq, c=tl["c"], unroll=tl["cu"])
    x2, (wpg_b, wpp_b) = _merge(x2, ya, yb, gates, wa_b, wb_b, wo_b,
                                (w_ple_gate[0], w_ple_proj[0]), tm=tl["tm"])

    x2, _ = _ffn(x2, ffn2_norm[0], wg2_b, wu2_b, wd2_b, tm=tl["tfm"], tf=tl["tf"])
    x2 = _ple(x2, p[0].reshape(m, -1), ple_gate_norm[0], ple_post_norm[0], wpg_b, wpp_b, tm=tl["tm"])
    return x2.reshape(batch, seq, d)
```

```python
import functools
import math

import jax
import jax.numpy as jnp
import numpy as np
from jax import lax
from jax.experimental import pallas as pl
from jax.experimental.pallas import tpu as pltpu

F32 = jnp.float32
BF16 = jnp.bfloat16

EPS = 1e-6
HEADS = 8
HEAD_W = 128
QK_DIM = 64
MIX_W = HEADS * HEAD_W
REL_BUCKETS = 32
REL_MAX_DIST = 128
LAMBDA_INIT = 0.8 - 0.6 * math.exp(-0.3 * 0)
LOG2E = math.log2(math.e)
MAX_STATIC_SOFTMAX_SPREAD = 60.0

V7X_MXU_WIDTH = 256
V7X_VMEM_BYTES = 64 * 1024 * 1024
VMEM_LIMIT = 56 * 1024 * 1024


def _cparams(sem, flags=None):
    return pltpu.CompilerParams(dimension_semantics=sem, vmem_limit_bytes=VMEM_LIMIT, flags=flags)


def _sigmoid(x):
    return 1.0 / (1.0 + jnp.exp(-x))


def _rms(x, gain):
    ms = jnp.mean(x * x, axis=-1, keepdims=True)
    return x * lax.rsqrt(ms + EPS) * gain


def _dot(a, b):
    return jnp.dot(a, b, preferred_element_type=F32)


def _dot_nt(a, b):
    return lax.dot_general(a, b, (((1,), (1,)), ((), ())), preferred_element_type=F32)


def _dot_tn(a, b):
    return lax.dot_general(a, b, (((0,), (0,)), ((), ())), preferred_element_type=F32)


BF16_ROW_TILE = 16


class _SideCast:
    def __init__(self, arrays, steps, step_index):
        self.arrays = list(arrays)
        self.n = len(self.arrays)
        self.step_index = step_index
        self.chunks = []
        for w in self.arrays:
            rows = w.shape[0]
            self.chunks.append(max(c for c in range(1, steps + 1)
                                   if rows % c == 0 and (rows // c) % BF16_ROW_TILE == 0))

    def _specs(self):
        def spec(w, c):
            return pl.BlockSpec((w.shape[0] // c, w.shape[1]),
                                lambda *ids: (jnp.minimum(self.step_index(*ids), c - 1), 0))
        return [spec(w, c) for w, c in zip(self.arrays, self.chunks)]

    in_specs = property(_specs)
    out_specs = property(_specs)

    @property
    def out_shapes(self):
        return [jax.ShapeDtypeStruct(w.shape, BF16) for w in self.arrays]

    @staticmethod
    def run(in_refs, out_refs):
        for src, dst in zip(in_refs, out_refs):
            dst[...] = src[...].astype(BF16)


def _row_tile_fetch(x_hbm, xs_ref, sem, tm):
    i = pl.program_id(0)
    j = pl.program_id(1)

    def copy(row_tile):
        return pltpu.make_async_copy(x_hbm.at[pl.ds(row_tile * tm, tm), :], xs_ref, sem)

    @pl.when((i == 0) & (j == 0))
    def _():
        copy(0).start()

    @pl.when((j == 1) & (i + 1 < pl.num_programs(0)))
    def _():
        copy(i + 1).start()

    return copy(i)


def _ffn_kernel(x_hbm, g_ref, wg_ref, wu_ref, wd_ref, *rest, tm, n_side):
    side_in, (o_ref,), side_out = rest[:n_side], rest[n_side:n_side + 1], rest[n_side + 1:2 * n_side + 1]
    xs_ref, h_ref, sem = rest[2 * n_side + 1:]
    j = pl.program_id(1)
    x_copy = _row_tile_fetch(x_hbm, xs_ref, sem, tm)

    @pl.when(j == 0)
    def _():
        x_copy.wait()
        x = xs_ref[...]
        h_ref[...] = _rms(x, g_ref[...]).astype(BF16)
        o_ref[...] = x

    h = h_ref[...]
    gate = _dot(h, wg_ref[...])
    up = _dot(h, wu_ref[...])
    act = (gate * _sigmoid(gate) * (0.5 * up)).astype(BF16)
    o_ref[...] += _dot(act, wd_ref[...])
    _SideCast.run(side_in, side_out)


def _ffn(x, gain, w_gate, w_up, w_down, side_weights=(), *, tm, tf):
    m, d = x.shape
    f = w_gate.shape[1]
    nj = f // tf
    assert nj >= 2
    side = _SideCast(side_weights, (m // tm) * nj, lambda i, j: i * nj + j)
    out = pl.pallas_call(
        functools.partial(_ffn_kernel, tm=tm, n_side=side.n),
        out_shape=[jax.ShapeDtypeStruct((m, d), F32)] + side.out_shapes,
        grid=(m // tm, nj),
        in_specs=[
            pl.BlockSpec(memory_space=pl.ANY),
            pl.BlockSpec((1, d), lambda i, j: (0, 0)),
            pl.BlockSpec((d, tf), lambda i, j: (0, j)),
            pl.BlockSpec((d, tf), lambda i, j: (0, j)),
            pl.BlockSpec((tf, d), lambda i, j: (j, 0)),
        ] + side.in_specs,
        out_specs=[pl.BlockSpec((tm, d), lambda i, j: (i, 0))] + side.out_specs,
        scratch_shapes=[pltpu.VMEM((tm, d), F32), pltpu.VMEM((tm, d), BF16),
                        pltpu.SemaphoreType.DMA(())],
        compiler_params=_cparams(("arbitrary", "arbitrary")),
        name="ffn",
    )(x, gain.reshape(1, d), w_gate, w_up, w_down, *side.arrays)
    return out[0], out[1:]


def _proj_kernel(x_hbm, g_ref, w_ref, qg_ref, kg_ref, lb_ref, gm_ref, *rest, tm, tn, nj, n_side):
    side_in, rest = rest[:n_side], rest[n_side:]
    (oqkv_ref, oh_ref, olf_ref, og_ref), rest = rest[:4], rest[4:]
    side_out, (xs_ref, h_ref, acc_ref, sem) = rest[:n_side], rest[n_side:]
    j = pl.program_id(1)
    jp = j - 1
    tps = MIX_W // tn
    hpt = tn // HEAD_W

    def heads(vals):
        return [vals[:, s * HEAD_W:(s + 1) * HEAD_W] for s in range(hpt)]

    def epi_qk(acc):
        gain = jnp.where(jp < tps, qg_ref[...], kg_ref[...])
        pw = gm_ref.shape[0]
        for s in range(tn // pw):
            a = acc[:, s * pw:(s + 1) * pw]
            ms = _dot((a * a).astype(BF16), gm_ref[...])
            y = a * lax.rsqrt(ms + EPS)
            for u in range(pw // HEAD_W):
                oqkv_ref[0, s * (pw // HEAD_W) + u] = (
                    y[:, u * HEAD_W:(u + 1) * HEAD_W] * gain).astype(BF16)

    def epi_v(acc):
        for s, a in enumerate(heads(acc)):
            oqkv_ref[0, s] = a.astype(BF16)

    def epi_silu(acc):
        for s, a in enumerate(heads(acc * _sigmoid(acc))):
            oh_ref[0, s] = a.astype(BF16)

    def epi_forget(acc):
        jj = jp - 4 * tps
        for s, a in enumerate(heads(acc)):
            lb = lb_ref[pl.ds(jj * hpt + s, 1), :]
            forget = lb + (1.0 - lb) * _sigmoid(a)
            oh_ref[0, s] = (1.0 - forget).astype(BF16)
            olf_ref[0, s] = jnp.log2(forget)

    def epi_hv(acc):
        for s, a in enumerate(heads(acc)):
            oh_ref[0, s] = a.astype(BF16)

    def epi_gates(acc):
        og_ref[...] = _sigmoid(acc).astype(BF16)

    x_copy = _row_tile_fetch(x_hbm, xs_ref, sem, tm)

    @pl.when(j == 0)
    def _():
        x_copy.wait()
        h_ref[...] = _rms(xs_ref[...], g_ref[...]).astype(BF16)
        acc_ref[...] = _dot(h_ref[...], w_ref[...])
        _SideCast.run(side_in, side_out)

    def fused(cond, epi):
        @pl.when(cond & (j >= 1) & (j < nj))
        def _():
            acc = acc_ref[...]
            epi(acc)
            acc_ref[...] = _dot(h_ref[...], w_ref[...])
            _SideCast.run(side_in, side_out)

    fused(jp < 2 * tps, epi_qk)
    fused((jp >= 2 * tps) & (jp < 3 * tps), epi_v)
    fused(((jp >= 3 * tps) & (jp < 4 * tps)) | ((jp >= 6 * tps) & (jp < 7 * tps)), epi_silu)
    fused((jp >= 4 * tps) & (jp < 5 * tps), epi_forget)
    fused((jp >= 5 * tps) & (jp < 6 * tps), epi_hv)
    fused(jp >= 7 * tps, epi_gates)

    @pl.when(j == nj)
    def _():
        epi_gates(acc_ref[...])
        _SideCast.run(side_in, side_out)


def _proj(x, gain, w_in, qg, kg, lb, side_weights=(), *, batch, seq, tm, tn):
    m, d = x.shape
    n_in = w_in.shape[1]
    tps = MIX_W // tn
    hpt = tn // HEAD_W
    spb = seq // tm
    ng = 2 * d // tn
    nj = 7 * tps + ng
    assert n_in == 7 * MIX_W + 2 * d and MIX_W % tn == 0 and (2 * d) % tn == 0

    pw = math.gcd(tn, V7X_MXU_WIDTH)
    lane = np.arange(pw)
    gm = (lane[:, None] // QK_DIM == lane[None, :] // QK_DIM).astype(np.float32) / QK_DIM

    def hm_block(lo, n):
        return lambda i, j: (i // spb, jnp.clip(j - 1 - lo, 0, n - 1), i % spb, 0)

    side = _SideCast(side_weights, (m // tm) * (nj + 1), lambda i, j: i * (nj + 1) + j)
    out = pl.pallas_call(
        functools.partial(_proj_kernel, tm=tm, tn=tn, nj=nj, n_side=side.n),
        out_shape=[
            jax.ShapeDtypeStruct((batch, 3 * HEADS, seq, HEAD_W), BF16),
            jax.ShapeDtypeStruct((batch, 4 * HEADS, seq, HEAD_W), BF16),
            jax.ShapeDtypeStruct((batch, HEADS, seq, HEAD_W), F32),
            jax.ShapeDtypeStruct((m, 2 * d), BF16),
        ] + side.out_shapes,
        grid=(m // tm, nj + 1),
        in_specs=[
            pl.BlockSpec(memory_space=pl.ANY),
            pl.BlockSpec((1, d), lambda i, j: (0, 0)),
            pl.BlockSpec((d, tn), lambda i, j: (0, jnp.minimum(j, nj - 1))),
            pl.BlockSpec((1, HEAD_W), lambda i, j: (0, 0)),
            pl.BlockSpec((1, HEAD_W), lambda i, j: (0, 0)),
            pl.BlockSpec((HEADS, HEAD_W), lambda i, j: (0, 0)),
            pl.BlockSpec((pw, pw), lambda i, j: (0, 0)),
        ] + side.in_specs,
        out_specs=[
            pl.BlockSpec((1, hpt, tm, HEAD_W), hm_block(0, 3 * tps)),
            pl.BlockSpec((1, hpt, tm, HEAD_W), hm_block(3 * tps, 4 * tps)),
            pl.BlockSpec((1, hpt, tm, HEAD_W), hm_block(4 * tps, tps)),
            pl.BlockSpec((tm, tn), lambda i, j: (i, jnp.clip(j - 1 - 7 * tps, 0, ng - 1))),
        ] + side.out_specs,
        scratch_shapes=[pltpu.VMEM((tm, d), F32), pltpu.VMEM((tm, d), BF16),
                        pltpu.VMEM((tm, tn), F32), pltpu.SemaphoreType.DMA(())],
        compiler_params=_cparams(("arbitrary", "arbitrary")),
        name="proj",
    )(x, gain.reshape(1, d), w_in, qg, kg, lb, jnp.asarray(gm, BF16), *side.arrays)
    return out[0], out[1], out[2], out[3], out[4:]


def _attn_kernel(lam_ref, off_ref, q_ref, k_ref, v_ref, w_ref, sg_ref, *rest, t, nkb, online, n_side):
    side_in, (o_ref,), side_out = rest[:n_side], rest[n_side:n_side + 1], rest[n_side + 1:2 * n_side + 1]
    vt_ref, bias_ref = rest[2 * n_side + 1:]
    h = pl.program_id(0)
    b = pl.program_id(1)
    i = pl.program_id(2)
    lam = lam_ref[0]
    far_off = off_ref[h]

    @pl.when((b == 0) & (i == 0))
    def _():
        for dd in range(2):
            rows = jnp.broadcast_to(w_ref[0, dd:dd + 1, :], (t, 2 * t))
            bias_ref[dd] = pltpu.roll(rows, 0, 1, stride=1, stride_axis=0)[:, 0:t]

    @pl.when(i == 0)
    def _():
        for jb in range(nkb):
            vt_ref[jb] = v_ref[0, 0, jb * t:(jb + 1) * t, :].astype(F32).T.astype(BF16)

    def colsum8(p):
        return jnp.sum(p.reshape(t // 8, 8, 2 * t), axis=0)

    def both(x):
        return jnp.concatenate([x, x], axis=1)

    def query_block(iq):
        q = q_ref[0, 0]
        lane = lax.broadcasted_iota(jnp.int32, (t, HEAD_W), 1)
        zero = jnp.zeros_like(q)
        qs = jnp.concatenate([jnp.where(lane < QK_DIM, q, zero),
                              jnp.where(lane >= QK_DIM, q, zero)], axis=0)
        offsets = [far_off] * max(iq - 1, 0)
        if iq >= 1:
            offsets.append(both(bias_ref[1]))
        offsets.append(both(bias_ref[0]))
        m = l = acc = None
        for j, badd in enumerate(offsets):
            s = _dot_nt(k_ref[0, 0, j * t:(j + 1) * t, :], qs) + badd
            if online:
                bmax = jnp.max(s, axis=0, keepdims=True)
                m_new = bmax if m is None else jnp.maximum(m, bmax)
                p = jnp.exp2(s - m_new)
            else:
                p = jnp.exp2(s)
            psum = colsum8(p)
            pv = _dot(vt_ref[j], p.astype(BF16))
            if l is None:
                l, acc = psum, pv
            elif online:
                alpha = jnp.exp2(m - m_new)
                l, acc = alpha * l + psum, alpha * acc + pv
            else:
                l, acc = l + psum, acc + pv
            if online:
                m = m_new
        l = jnp.sum(l, axis=0, keepdims=True)
        o_t = acc[:, 0:t] / l[:, 0:t] - lam * (acc[:, t:2 * t] / l[:, t:2 * t])
        o_ref[...] = (_rms(o_t.T, sg_ref[...]) * (1.0 - LAMBDA_INIT)).astype(BF16)
        _SideCast.run(side_in, side_out)

    for iq in range(nkb):
        pl.when(i == iq)(functools.partial(query_block, iq))


def _t5_bucket(dist):
    n = jnp.maximum(dist, 0)
    max_exact = REL_BUCKETS // 2
    nf = jnp.maximum(n, 1).astype(F32)
    large = max_exact + (jnp.log(nf / max_exact) / math.log(REL_MAX_DIST / max_exact)
                         * (REL_BUCKETS - max_exact)).astype(jnp.int32)
    large = jnp.minimum(large, REL_BUCKETS - 1)
    return jnp.where(n < max_exact, n, large)


def _bias_generators(f, t):
    hole = jnp.full((HEADS, 1), -jnp.inf, F32)
    w0 = jnp.concatenate([f[:, 0:t], jnp.full((HEADS, t), -jnp.inf, F32)], axis=1)
    w1 = jnp.concatenate([f[:, t:2 * t], hole, f[:, 1:t]], axis=1)
    return jnp.stack([w0, w1], axis=1)


def _attn(qkv, lam, rel_bias, qk_bound, subln, side_weights=(), *, batch, seq, t):
    assert t >= REL_MAX_DIST
    nq = seq // t
    rb = rel_bias.astype(F32)
    spread = 2.0 * qk_bound + jnp.max(jnp.max(rb, axis=0) - jnp.min(rb, axis=0))
    static_ok = spread <= MAX_STATIC_SOFTMAX_SPREAD
    upper = jnp.where(static_ok, qk_bound + jnp.max(rb, axis=0), 0.0)
    f = rb[_t5_bucket(jnp.arange(2 * t))].T - upper[:, None]
    bias = _bias_generators(f * LOG2E, t)
    far_off = (rb[REL_BUCKETS - 1] - upper) * LOG2E

    side = _SideCast(side_weights, HEADS * batch * nq, lambda h, b, i: (h * batch + b) * nq + i)

    def call(online):
        out = pl.pallas_call(
            functools.partial(_attn_kernel, t=t, nkb=nq, online=online, n_side=side.n),
            out_shape=[jax.ShapeDtypeStruct((batch * seq, MIX_W), BF16)] + side.out_shapes,
            grid=(HEADS, batch, nq),
            in_specs=[
                pl.BlockSpec(memory_space=pltpu.SMEM),
                pl.BlockSpec(memory_space=pltpu.SMEM),
                pl.BlockSpec((1, 1, t, HEAD_W), lambda h, b, i: (b, h, i, 0)),
                pl.BlockSpec((1, 1, seq, HEAD_W), lambda h, b, i: (b, HEADS + h, 0, 0)),
                pl.BlockSpec((1, 1, seq, HEAD_W), lambda h, b, i: (b, 2 * HEADS + h, 0, 0)),
                pl.BlockSpec((1, 2, 2 * t), lambda h, b, i: (h, 0, 0)),
                pl.BlockSpec((1, HEAD_W), lambda h, b, i: (0, 0)),
            ] + side.in_specs,
            out_specs=[pl.BlockSpec((t, HEAD_W), lambda h, b, i: (b * nq + i, h))] + side.out_specs,
            scratch_shapes=[pltpu.VMEM((nq, HEAD_W, t), BF16),
                            pltpu.VMEM((2, t, t), F32)],
            compiler_params=_cparams(("arbitrary", "arbitrary", "arbitrary")),
            name="attn_online" if online else "attn",
        )(lam.reshape(1), far_off, qkv, qkv, qkv, bias, subln.reshape(1, HEAD_W), *side.arrays)
        return out[0], tuple(out[1:])

    return lax.cond(static_ok, lambda: call(False), lambda: call(True))


SUBLANES = 8


def _hgrn_tables(c):
    levels = int(math.log2(c))
    assert 2 ** levels == c
    t = np.arange(c)[:, None]
    s = np.arange(c)[None, :]
    blocks = [(s <= t).astype(np.float32)]
    level_id = np.where(t == s, 0, -1)
    n_fine = 0
    for l in range(1, levels + 1):
        g, half = 2 ** l, 2 ** (l - 1)
        mid = (t // g) * g + half - 1
        upper = (t % g) >= half
        if half < SUBLANES:
            above = upper & (s > mid) & (s <= t)
            below = (~upper) & (s > t) & (s <= mid)
            blocks.append((above | below).astype(np.float32))
            n_fine += 1
        pair = (t // g == s // g) & upper & ((s % g) < half)
        level_id = np.where(pair, l, level_id)
    w = np.concatenate(blocks, axis=0)
    return jnp.asarray(w, BF16), jnp.asarray(level_id, jnp.int32), levels, n_fine


def _hgrn_kernel(w_ref, lvl_ref, q_ref, k_ref, v_ref, og_ref, lf_ref, gn_ref, o_ref, *,
                 c, levels, n_fine, seq, unroll):
    gain = gn_ref[...]

    def cumsums(ci):
        rows = pl.ds(pl.multiple_of(ci * c, c), c)
        g = lf_ref[0, 0, rows, :]
        g_hi = g.astype(BF16)
        g_lo = (g - g_hi.astype(F32)).astype(BF16)
        e2 = _dot(w_ref[...], jnp.concatenate([g_hi, g_lo], axis=1))
        return rows, e2[:, 0:HEAD_W] + e2[:, HEAD_W:2 * HEAD_W]

    def scores(rows, e):
        qb = q_ref[0, 0, rows, :]
        kb = k_ref[0, 0, rows, :]
        q = qb.astype(F32)
        k = kb.astype(F32)
        b = e[0:c]
        lvl = lvl_ref[...]
        pairs = [(qb, kb)]
        for l in range(1, levels + 1):
            if l <= n_fine:
                d = e[l * c:(l + 1) * c]
            else:
                g, half = 2 ** l, 2 ** (l - 1)
                b3 = b.reshape(c // g, g, HEAD_W)
                mid = b3[:, half - 1:half, :]
                d = jnp.concatenate([mid - b3[:, 0:half, :], b3[:, half:g, :] - mid],
                                    axis=1).reshape(c, HEAD_W)
            x = jnp.exp2(d)
            pairs.append(((q * x).astype(BF16), (k * x).astype(BF16)))
        a = jnp.zeros((c, c), F32)
        zero = jnp.zeros((c, HEAD_W), BF16)
        for l in range(0, levels + 1, 2):
            (qa, ka), (qc, kc) = pairs[l], pairs[l + 1]
            keys = jnp.concatenate([jnp.concatenate([ka, zero], axis=1),
                                    jnp.concatenate([zero, kc], axis=1)], axis=0)
            r = _dot_nt(jnp.concatenate([qa, qc], axis=1), keys)
            a = jnp.where(lvl == l, r[:, 0:c], a)
            a = jnp.where(lvl == l + 1, r[:, c:2 * c], a)
        b_last = b[c - 1:c, :]
        q_in = (q * jnp.exp2(b)).astype(BF16)
        k_out = (k * jnp.exp2(b_last - b)).astype(BF16)
        return a.astype(BF16), q_in, k_out, jnp.exp2(b_last)

    def outputs(rows, a, q_in, k_out, decay):
        v = v_ref[0, 0, rows, :]
        return rows, _dot(a, v), q_in, _dot_tn(v, k_out), decay

    def body(it, state_t):
        st1, st2, st3 = {}, {}, []
        for u in range(unroll + 2):
            if u < unroll:
                st1[u] = cumsums(it * unroll + u)
            if 0 <= u - 1 < unroll:
                st2[u - 1] = scores(*st1[u - 1])
            if 0 <= u - 2 < unroll:
                st3.append(outputs(st1[u - 2][0], *st2[u - 2]))
        for rows, o_intra, q_in, ds_t, decay in st3:
            o = o_intra + _dot_nt(q_in, state_t.astype(BF16))
            state_t = state_t * decay + ds_t
            y = _rms(o, gain) * og_ref[0, 0, rows, :].astype(F32)
            o_ref[rows, :] = y.astype(BF16)
        return state_t

    lax.fori_loop(0, seq // (c * unroll), body, jnp.zeros((HEAD_W, HEAD_W), F32))


def _hgrn(hin, logf, gain, *, batch, seq, c, unroll):
    w, lvl, levels, n_fine = _hgrn_tables(c)
    assert seq % (c * unroll) == 0

    def head(slab):
        return pl.BlockSpec((1, 1, seq, HEAD_W), lambda b, h: (b, slab * HEADS + h, 0, 0))

    return pl.pallas_call(
        functools.partial(_hgrn_kernel, c=c, levels=levels, n_fine=n_fine, seq=seq, unroll=unroll),
        out_shape=jax.ShapeDtypeStruct((batch * seq, MIX_W), BF16),
        grid=(batch, HEADS),
        in_specs=[
            pl.BlockSpec(w.shape, lambda b, h: (0, 0)),
            pl.BlockSpec((c, c), lambda b, h: (0, 0)),
            head(0), head(1), head(2), head(3), head(0),
            pl.BlockSpec((1, HEAD_W), lambda b, h: (0, 0)),
        ],
        out_specs=pl.BlockSpec((seq, HEAD_W), lambda b, h: (b, h)),
        compiler_params=_cparams(("parallel", "parallel")),
        name="hgrn",
    )(w, lvl, hin, hin, hin, hin, logf, gain.reshape(1, HEAD_W))


def _merge_kernel(x_ref, ya_ref, yb_ref, gt_ref, wa_ref, wb_ref, wo_ref, *rest, n_side):
    side_in, (o_ref,), side_out = rest[:n_side], rest[n_side:n_side + 1], rest[n_side + 1:]
    d = x_ref.shape[1]
    gt = gt_ref[...].astype(F32)
    merged = gt[:, 0:d] * _dot(ya_ref[...], wa_ref[...]) + gt[:, d:2 * d] * _dot(yb_ref[...], wb_ref[...])
    o_ref[...] = x_ref[...] + _dot(merged.astype(BF16), wo_ref[...])
    _SideCast.run(side_in, side_out)


def _resident(shape):
    return pl.BlockSpec(shape, lambda i: (0,) * len(shape), pipeline_mode=pl.Buffered(1))


def _merge(x, ya, yb, gates, wa, wb, wo, side_weights=(), *, tm):
    m, d = x.shape
    side = _SideCast(side_weights, m // tm, lambda i: i)
    out = pl.pallas_call(
        functools.partial(_merge_kernel, n_side=side.n),
        out_shape=[jax.ShapeDtypeStruct((m, d), F32)] + side.out_shapes,
        grid=(m // tm,),
        in_specs=[
            pl.BlockSpec((tm, d), lambda i: (i, 0)),
            pl.BlockSpec((tm, MIX_W), lambda i: (i, 0)),
            pl.BlockSpec((tm, MIX_W), lambda i: (i, 0)),
            pl.BlockSpec((tm, 2 * d), lambda i: (i, 0)),
            _resident(wa.shape), _resident(wb.shape), _resident(wo.shape),
        ] + side.in_specs,
        out_specs=[pl.BlockSpec((tm, d), lambda i: (i, 0))] + side.out_specs,
        compiler_params=_cparams(("arbitrary",)),
        name="merge",
    )(x, ya, yb, gates, wa, wb, wo, *side.arrays)
    return out[0], out[1:]


def _ple_kernel(x_ref, p_ref, gg_ref, pg_ref, wg_ref, wp_ref, o_ref):
    ple = _rms(_dot(p_ref[...].astype(BF16), wp_ref[...]), pg_ref[...])
    x = x_ref[...]
    gate = _sigmoid(_dot(_rms(x, gg_ref[...]).astype(BF16), wg_ref[...]))
    o_ref[...] = x + gate * ple


def _ple(x, p, gate_gain, post_gain, w_gate, w_proj, *, tm):
    m, d = x.shape
    pd = p.shape[1]
    return pl.pallas_call(
        _ple_kernel,
        out_shape=jax.ShapeDtypeStruct((m, d), F32),
        grid=(m // tm,),
        in_specs=[
            pl.BlockSpec((tm, d), lambda i: (i, 0)),
            pl.BlockSpec((tm, pd), lambda i: (i, 0)),
            pl.BlockSpec((1, d), lambda i: (0, 0)),
            pl.BlockSpec((1, d), lambda i: (0, 0)),
            _resident(w_gate.shape), _resident(w_proj.shape),
        ],
        out_specs=pl.BlockSpec((tm, d), lambda i: (i, 0)),
        compiler_params=_cparams(("parallel",)),
        name="ple",
    )(x, p, gate_gain.reshape(1, d), post_gain.reshape(1, d), w_gate, w_proj)


def _tiles(seq, d, f):
    def pick(n, choices):
        return next(c for c in choices if n % c == 0)
    return dict(
        tm=pick(seq, (512, 256, 128)),
        tfm=pick(seq, (1024, 512, 256, 128)),
        tf=pick(f, (512, 256, 128)),
        tp=pick(seq, (1024, 512, 256, 128)),
        tn=pick(math.gcd(MIX_W, 2 * d), (1024, 512, 256, 128)),
        ta=pick(seq, (512, 256, 128)),
        c=128,
        cu=16,
    )


def kernel(x, p, ffn1_norm, ffn1_w_gate, ffn1_w_up, ffn1_w_down, mix_norm, w_in, q_norm, k_norm, lambda_q1, lambda_k1, lambda_q2, lambda_k2, diff_subln, rel_bias, hgrn_lb_logits, hgrn_norm, w_branch_a, w_branch_b, w_out, ffn2_norm, ffn2_w_gate, ffn2_w_up, ffn2_w_down, ple_gate_norm, w_ple_gate, w_ple_proj, ple_post_norm):
    batch, seq, d = x.shape
    depth = ffn1_norm.shape[0]
    assert depth == 1
    m = batch * seq
    tl = _tiles(seq, d, ffn1_w_gate.shape[2])
    bf = lambda w: w.astype(BF16)

    lower_bounds = jnp.cumsum(jax.nn.softmax(hgrn_lb_logits.astype(F32), axis=0), axis=0)
    lb = lower_bounds[0].reshape(HEADS, HEAD_W)
    lam = (jnp.exp(jnp.sum(lambda_q1[0].astype(F32) * lambda_k1[0].astype(F32)))
           - jnp.exp(jnp.sum(lambda_q2[0].astype(F32) * lambda_k2[0].astype(F32)))
           + LAMBDA_INIT)
    scale = QK_DIM ** -0.5
    qg = jnp.tile(q_norm[0].astype(F32), 2).reshape(1, HEAD_W) * (scale * LOG2E)
    kg = jnp.tile(k_norm[0].astype(F32), 2).reshape(1, HEAD_W)
    qk_bound = (QK_DIM * scale * jnp.max(jnp.abs(q_norm[0].astype(F32)))
                * jnp.max(jnp.abs(k_norm[0].astype(F32))))

    x2 = x.reshape(m, d)
    x2, (w_in_b, wa_b, wb_b, wo_b) = _ffn(
        x2, ffn1_norm[0], bf(ffn1_w_gate[0]), bf(ffn1_w_up[0]), bf(ffn1_w_down[0]),
        (w_in[0], w_branch_a[0], w_branch_b[0], w_out[0]), tm=tl["tfm"], tf=tl["tf"])

    qkv, hin, logf, gates, (wg2_b, wu2_b, wd2_b) = _proj(
        x2, mix_norm[0], w_in_b, qg, kg, lb, (ffn2_w_gate[0], ffn2_w_up[0], ffn2_w_down[0]),
        batch=batch, seq=seq, tm=tl["tp"], tn=tl["tn"])
    ya, _ = _attn(qkv, lam, rel_bias, qk_bound, diff_subln[0], batch=batch, seq=seq, t=tl["ta"])
    yb = _hgrn(hin, logf, hgrn_norm[0], batch=batch, seq=seq, c=tl["c"], unroll=tl["cu"])
    x2, (wpg_b, wpp_b) = _merge(x2, ya, yb, gates, wa_b, wb_b, wo_b,
                                (w_ple_gate[0], w_ple_proj[0]), tm=tl["tm"])

    x2, _ = _ffn(x2, ffn2_norm[0], wg2_b, wu2_b, wd2_b, tm=tl["tfm"], tf=tl["tf"])
    x2 = _ple(x2, p[0].reshape(m, -1), ple_gate_norm[0], ple_post_norm[0], wpg_b, wpp_b, tm=tl["tm"])
    return x2.reshape(batch, seq, d)
```

```python
import functools
import math

import jax
import jax.numpy as jnp
import numpy as np
from jax import lax
from jax.experimental import pallas as pl
from jax.experimental.pallas import tpu as pltpu

F32 = jnp.float32
BF16 = jnp.bfloat16

EPS = 1e-6
HEADS = 8
HEAD_W = 128
QK_DIM = 64
MIX_W = HEADS * HEAD_W
REL_BUCKETS = 32
REL_MAX_DIST = 128
LAMBDA_INIT = 0.8 - 0.6 * math.exp(-0.3 * 0)
LOG2E = math.log2(math.e)
MAX_STATIC_SOFTMAX_SPREAD = 60.0

V7X_MXU_WIDTH = 256
V7X_VMEM_BYTES = 64 * 1024 * 1024
VMEM_LIMIT = 56 * 1024 * 1024


def _cparams(sem, flags=None):
    return pltpu.CompilerParams(dimension_semantics=sem, vmem_limit_bytes=VMEM_LIMIT, flags=flags)


def _sigmoid(x):
    return 1.0 / (1.0 + jnp.exp(-x))


def _rms(x, gain):
    ms = jnp.mean(x * x, axis=-1, keepdims=True)
    return x * lax.rsqrt(ms + EPS) * gain


def _dot(a, b):
    return jnp.dot(a, b, preferred_element_type=F32)


def _dot_nt(a, b):
    return lax.dot_general(a, b, (((1,), (1,)), ((), ())), preferred_element_type=F32)


def _dot_tn(a, b):
    return lax.dot_general(a, b, (((0,), (0,)), ((), ())), preferred_element_type=F32)


BF16_ROW_TILE = 16


class _SideCast:
    def __init__(self, arrays, steps, step_index):
        self.arrays = list(arrays)
        self.n = len(self.arrays)
        self.step_index = step_index
        self.chunks = []
        for w in self.arrays:
            rows = w.shape[0]
            self.chunks.append(max(c for c in range(1, steps + 1)
                                   if rows % c == 0 and (rows // c) % BF16_ROW_TILE == 0))

    def _specs(self):
        def spec(w, c):
            return pl.BlockSpec((w.shape[0] // c, w.shape[1]),
                                lambda *ids: (jnp.minimum(self.step_index(*ids), c - 1), 0))
        return [spec(w, c) for w, c in zip(self.arrays, self.chunks)]

    in_specs = property(_specs)
    out_specs = property(_specs)

    @property
    def out_shapes(self):
        return [jax.ShapeDtypeStruct(w.shape, BF16) for w in self.arrays]

    @staticmethod
    def run(in_refs, out_refs):
        for src, dst in zip(in_refs, out_refs):
            dst[...] = src[...].astype(BF16)


def _row_tile_fetch(x_hbm, xs_ref, sem, tm):
    i = pl.program_id(0)
    j = pl.program_id(1)

    def copy(row_tile):
        return pltpu.make_async_copy(x_hbm.at[pl.ds(row_tile * tm, tm), :], xs_ref, sem)

    @pl.when((i == 0) & (j == 0))
    def _():
        copy(0).start()

    @pl.when((j == 1) & (i + 1 < pl.num_programs(0)))
    def _():
        copy(i + 1).start()

    return copy(i)


def _ffn_kernel(x_hbm, g_ref, wg_ref, wu_ref, wd_ref, *rest, tm, n_side):
    side_in, (o_ref,), side_out = rest[:n_side], rest[n_side:n_side + 1], rest[n_side + 1:2 * n_side + 1]
    xs_ref, h_ref, sem = rest[2 * n_side + 1:]
    j = pl.program_id(1)
    x_copy = _row_tile_fetch(x_hbm, xs_ref, sem, tm)

    @pl.when(j == 0)
    def _():
        x_copy.wait()
        x = xs_ref[...]
        h_ref[...] = _rms(x, g_ref[...]).astype(BF16)
        o_ref[...] = x

    h = h_ref[...]
    gate = _dot(h, wg_ref[...])
    up = _dot(h, wu_ref[...])
    act = (gate * _sigmoid(gate) * (0.5 * up)).astype(BF16)
    o_ref[...] += _dot(act, wd_ref[...])
    _SideCast.run(side_in, side_out)


def _ffn(x, gain, w_gate, w_up, w_down, side_weights=(), *, tm, tf):
    m, d = x.shape
    f = w_gate.shape[1]
    nj = f // tf
    assert nj >= 2
    side = _SideCast(side_weights, (m // tm) * nj, lambda i, j: i * nj + j)
    out = pl.pallas_call(
        functools.partial(_ffn_kernel, tm=tm, n_side=side.n),
        out_shape=[jax.ShapeDtypeStruct((m, d), F32)] + side.out_shapes,
        grid=(m // tm, nj),
        in_specs=[
            pl.BlockSpec(memory_space=pl.ANY),
            pl.BlockSpec((1, d), lambda i, j: (0, 0)),
            pl.BlockSpec((d, tf), lambda i, j: (0, j)),
            pl.BlockSpec((d, tf), lambda i, j: (0, j)),
            pl.BlockSpec((tf, d), lambda i, j: (j, 0)),
        ] + side.in_specs,
        out_specs=[pl.BlockSpec((tm, d), lambda i, j: (i, 0))] + side.out_specs,
        scratch_shapes=[pltpu.VMEM((tm, d), F32), pltpu.VMEM((tm, d), BF16),
                        pltpu.SemaphoreType.DMA(())],
        compiler_params=_cparams(("arbitrary", "arbitrary")),
        name="ffn",
    )(x, gain.reshape(1, d), w_gate, w_up, w_down, *side.arrays)
    return out[0], out[1:]


def _proj_kernel(x_hbm, g_ref, w_ref, qg_ref, kg_ref, lb_ref, gm_ref, *rest, tm, tn, nj, n_side):
    side_in, rest = rest[:n_side], rest[n_side:]
    (oqkv_ref, oh_ref, olf_ref, og_ref), rest = rest[:4], rest[4:]
    side_out, (xs_ref, h_ref, acc_ref, sem) = rest[:n_side], rest[n_side:]
    j = pl.program_id(1)
    jp = j - 1
    tps = MIX_W // tn
    hpt = tn // HEAD_W

    def heads(vals):
        return [vals[:, s * HEAD_W:(s + 1) * HEAD_W] for s in range(hpt)]

    def epi_qk(acc):
        gain = jnp.where(jp < tps, qg_ref[...], kg_ref[...])
        pw = gm_ref.shape[0]
        for s in range(tn // pw):
            a = acc[:, s * pw:(s + 1) * pw]
            ms = _dot((a * a).astype(BF16), gm_ref[...])
            y = a * lax.rsqrt(ms + EPS)
            for u in range(pw // HEAD_W):
                oqkv_ref[0, s * (pw // HEAD_W) + u] = (
                    y[:, u * HEAD_W:(u + 1) * HEAD_W] * gain).astype(BF16)

    def epi_v(acc):
        for s, a in enumerate(heads(acc)):
            oqkv_ref[0, s] = a.astype(BF16)

    def epi_silu(acc):
        for s, a in enumerate(heads(acc * _sigmoid(acc))):
            oh_ref[0, s] = a.astype(BF16)

    def epi_forget(acc):
        jj = jp - 4 * tps
        for s, a in enumerate(heads(acc)):
            lb = lb_ref[pl.ds(jj * hpt + s, 1), :]
            forget = lb + (1.0 - lb) * _sigmoid(a)
            oh_ref[0, s] = (1.0 - forget).astype(BF16)
            olf_ref[0, s] = jnp.log2(forget)

    def epi_hv(acc):
        for s, a in enumerate(heads(acc)):
            oh_ref[0, s] = a.astype(BF16)

    def epi_gates(acc):
        og_ref[...] = _sigmoid(acc).astype(BF16)

    x_copy = _row_tile_fetch(x_hbm, xs_ref, sem, tm)

    @pl.when(j == 0)
    def _():
        x_copy.wait()
        h_ref[...] = _rms(xs_ref[...], g_ref[...]).astype(BF16)
        acc_ref[...] = _dot(h_ref[...], w_ref[...])
        _SideCast.run(side_in, side_out)

    def fused(cond, epi):
        @pl.when(cond & (j >= 1) & (j < nj))
        def _():
            acc = acc_ref[...]
            epi(acc)
            acc_ref[...] = _dot(h_ref[...], w_ref[...])
            _SideCast.run(side_in, side_out)

    fused(jp < 2 * tps, epi_qk)
    fused((jp >= 2 * tps) & (jp < 3 * tps), epi_v)
    fused(((jp >= 3 * tps) & (jp < 4 * tps)) | ((jp >= 6 * tps) & (jp < 7 * tps)), epi_silu)
    fused((jp >= 4 * tps) & (jp < 5 * tps), epi_forget)
    fused((jp >= 5 * tps) & (jp < 6 * tps), epi_hv)
    fused(jp >= 7 * tps, epi_gates)

    @pl.when(j == nj)
    def _():
        epi_gates(acc_ref[...])
        _SideCast.run(side_in, side_out)


def _proj(x, gain, w_in, qg, kg, lb, side_weights=(), *, batch, seq, tm, tn):
    m, d = x.shape
    n_in = w_in.shape[1]
    tps = MIX_W // tn
    hpt = tn // HEAD_W
    spb = seq // tm
    ng = 2 * d // tn
    nj = 7 * tps + ng
    assert n_in == 7 * MIX_W + 2 * d and MIX_W % tn == 0 and (2 * d) % tn == 0

    pw = math.gcd(tn, V7X_MXU_WIDTH)
    lane = np.arange(pw)
    gm = (lane[:, None] // QK_DIM == lane[None, :] // QK_DIM).astype(np.float32) / QK_DIM

    def hm_block(lo, n):
        return lambda i, j: (i // spb, jnp.clip(j - 1 - lo, 0, n - 1), i % spb, 0)

    side = _SideCast(side_weights, (m // tm) * (nj + 1), lambda i, j: i * (nj + 1) + j)
    out = pl.pallas_call(
        functools.partial(_proj_kernel, tm=tm, tn=tn, nj=nj, n_side=side.n),
        out_shape=[
            jax.ShapeDtypeStruct((batch, 3 * HEADS, seq, HEAD_W), BF16),
            jax.ShapeDtypeStruct((batch, 4 * HEADS, seq, HEAD_W), BF16),
            jax.ShapeDtypeStruct((batch, HEADS, seq, HEAD_W), F32),
            jax.ShapeDtypeStruct((m, 2 * d), BF16),
        ] + side.out_shapes,
        grid=(m // tm, nj + 1),
        in_specs=[
            pl.BlockSpec(memory_space=pl.ANY),
            pl.BlockSpec((1, d), lambda i, j: (0, 0)),
            pl.BlockSpec((d, tn), lambda i, j: (0, jnp.minimum(j, nj - 1))),
            pl.BlockSpec((1, HEAD_W), lambda i, j: (0, 0)),
            pl.BlockSpec((1, HEAD_W), lambda i, j: (0, 0)),
            pl.BlockSpec((HEADS, HEAD_W), lambda i, j: (0, 0)),
            pl.BlockSpec((pw, pw), lambda i, j: (0, 0)),
        ] + side.in_specs,
        out_specs=[
            pl.BlockSpec((1, hpt, tm, HEAD_W), hm_block(0, 3 * tps)),
            pl.BlockSpec((1, hpt, tm, HEAD_W), hm_block(3 * tps, 4 * tps)),
            pl.BlockSpec((1, hpt, tm, HEAD_W), hm_block(4 * tps, tps)),
            pl.BlockSpec((tm, tn), lambda i, j: (i, jnp.clip(j - 1 - 7 * tps, 0, ng - 1))),
        ] + side.out_specs,
        scratch_shapes=[pltpu.VMEM((tm, d), F32), pltpu.VMEM((tm, d), BF16),
                        pltpu.VMEM((tm, tn), F32), pltpu.SemaphoreType.DMA(())],
        compiler_params=_cparams(("arbitrary", "arbitrary")),
        name="proj",
    )(x, gain.reshape(1, d), w_in, qg, kg, lb, jnp.asarray(gm, BF16), *side.arrays)
    return out[0], out[1], out[2], out[3], out[4:]


def _attn_kernel(lam_ref, off_ref, q_ref, k_ref, v_ref, w_ref, sg_ref, *rest, t, nkb, qpg, online, n_side):
    side_in, (o_ref,), side_out = rest[:n_side], rest[n_side:n_side + 1], rest[n_side + 1:2 * n_side + 1]
    vt_ref, bias_ref = rest[2 * n_side + 1:]
    h = pl.program_id(0)
    b = pl.program_id(1)
    i = pl.program_id(2)
    lam = lam_ref[0]
    far_off = off_ref[h]

    @pl.when((b == 0) & (i == 0))
    def _():
        for dd in range(2):
            rows = jnp.broadcast_to(w_ref[0, dd:dd + 1, :], (t, 2 * t))
            bias_ref[dd] = pltpu.roll(rows, 0, 1, stride=1, stride_axis=0)[:, 0:t]

    @pl.when(i == 0)
    def _():
        for jb in range(nkb):
            vt_ref[jb] = v_ref[0, 0, jb * t:(jb + 1) * t, :].astype(F32).T.astype(BF16)

    def colsum8(p):
        return jnp.sum(p.reshape(t // 8, 8, 2 * t), axis=0)

    def both(x):
        return jnp.concatenate([x, x], axis=1)

    def query_block(iq, row0):
        q = q_ref[0, 0, row0:row0 + t, :]
        lane = lax.broadcasted_iota(jnp.int32, (t, HEAD_W), 1)
        zero = jnp.zeros_like(q)
        qs = jnp.concatenate([jnp.where(lane < QK_DIM, q, zero),
                              jnp.where(lane >= QK_DIM, q, zero)], axis=0)
        offsets = [far_off] * max(iq - 1, 0)
        if iq >= 1:
            offsets.append(both(bias_ref[1]))
        offsets.append(both(bias_ref[0]))
        m = l = acc = None
        for j, badd in enumerate(offsets):
            s = _dot_nt(k_ref[0, 0, j * t:(j + 1) * t, :], qs) + badd
            if online:
                bmax = jnp.max(s, axis=0, keepdims=True)
                m_new = bmax if m is None else jnp.maximum(m, bmax)
                p = jnp.exp2(s - m_new)
            else:
                p = jnp.exp2(s)
            psum = colsum8(p)
            pv = _dot(vt_ref[j], p.astype(BF16))
            if l is None:
                l, acc = psum, pv
            elif online:
                alpha = jnp.exp2(m - m_new)
                l, acc = alpha * l + psum, alpha * acc + pv
            else:
                l, acc = l + psum, acc + pv
            if online:
                m = m_new
        l = jnp.sum(l, axis=0, keepdims=True)
        o_t = acc[:, 0:t] / l[:, 0:t] - lam * (acc[:, t:2 * t] / l[:, t:2 * t])
        o_ref[row0:row0 + t, :] = (_rms(o_t.T, sg_ref[...]) * (1.0 - LAMBDA_INIT)).astype(BF16)

    def query_group(ig):
        for u in range(qpg):
            query_block(ig * qpg + u, u * t)
        _SideCast.run(side_in, side_out)

    for ig in range(nkb // qpg):
        pl.when(i == ig)(functools.partial(query_group, ig))


def _t5_bucket(dist):
    n = jnp.maximum(dist, 0)
    max_exact = REL_BUCKETS // 2
    nf = jnp.maximum(n, 1).astype(F32)
    large = max_exact + (jnp.log(nf / max_exact) / math.log(REL_MAX_DIST / max_exact)
                         * (REL_BUCKETS - max_exact)).astype(jnp.int32)
    large = jnp.minimum(large, REL_BUCKETS - 1)
    return jnp.where(n < max_exact, n, large)


def _bias_generators(f, t):
    hole = jnp.full((HEADS, 1), -jnp.inf, F32)
    w0 = jnp.concatenate([f[:, 0:t], jnp.full((HEADS, t), -jnp.inf, F32)], axis=1)
    w1 = jnp.concatenate([f[:, t:2 * t], hole, f[:, 1:t]], axis=1)
    return jnp.stack([w0, w1], axis=1)


def _attn(qkv, lam, rel_bias, qk_bound, subln, side_weights=(), *, batch, seq, t, qpg):
    assert t >= REL_MAX_DIST
    nq = seq // t
    rb = rel_bias.astype(F32)
    spread = 2.0 * qk_bound + jnp.max(jnp.max(rb, axis=0) - jnp.min(rb, axis=0))
    static_ok = spread <= MAX_STATIC_SOFTMAX_SPREAD
    upper = jnp.where(static_ok, qk_bound + jnp.max(rb, axis=0), 0.0)
    f = rb[_t5_bucket(jnp.arange(2 * t))].T - upper[:, None]
    bias = _bias_generators(f * LOG2E, t)
    far_off = (rb[REL_BUCKETS - 1] - upper) * LOG2E

    assert nq % qpg == 0
    ng = nq // qpg
    side = _SideCast(side_weights, HEADS * batch * ng, lambda h, b, i: (h * batch + b) * ng + i)

    def call(online):
        out = pl.pallas_call(
            functools.partial(_attn_kernel, t=t, nkb=nq, qpg=qpg, online=online, n_side=side.n),
            out_shape=[jax.ShapeDtypeStruct((batch * seq, MIX_W), BF16)] + side.out_shapes,
            grid=(HEADS, batch, ng),
            in_specs=[
                pl.BlockSpec(memory_space=pltpu.SMEM),
                pl.BlockSpec(memory_space=pltpu.SMEM),
                pl.BlockSpec((1, 1, qpg * t, HEAD_W), lambda h, b, i: (b, h, i, 0)),
                pl.BlockSpec((1, 1, seq, HEAD_W), lambda h, b, i: (b, HEADS + h, 0, 0)),
                pl.BlockSpec((1, 1, seq, HEAD_W), lambda h, b, i: (b, 2 * HEADS + h, 0, 0)),
                pl.BlockSpec((1, 2, 2 * t), lambda h, b, i: (h, 0, 0)),
                pl.BlockSpec((1, HEAD_W), lambda h, b, i: (0, 0)),
            ] + side.in_specs,
            out_specs=[pl.BlockSpec((qpg * t, HEAD_W), lambda h, b, i: (b * ng + i, h))] + side.out_specs,
            scratch_shapes=[pltpu.VMEM((nq, HEAD_W, t), BF16),
                            pltpu.VMEM((2, t, t), F32)],
            compiler_params=_cparams(("arbitrary", "arbitrary", "arbitrary")),
            name="attn_online" if online else "attn",
        )(lam.reshape(1), far_off, qkv, qkv, qkv, bias, subln.reshape(1, HEAD_W), *side.arrays)
        return out[0], tuple(out[1:])

    return lax.cond(static_ok, lambda: call(False), lambda: call(True))


SUBLANES = 8


def _hgrn_tables(c):
    levels = int(math.log2(c))
    assert 2 ** levels == c
    t = np.arange(c)[:, None]
    s = np.arange(c)[None, :]
    blocks = [(s <= t).astype(np.float32)]
    level_id = np.where(t == s, 0, -1)
    n_fine = 0
    for l in range(1, levels + 1):
        g, half = 2 ** l, 2 ** (l - 1)
        mid = (t // g) * g + half - 1
        upper = (t % g) >= half
        if half < SUBLANES:
            above = upper & (s > mid) & (s <= t)
            below = (~upper) & (s > t) & (s <= mid)
            blocks.append((above | below).astype(np.float32))
            n_fine += 1
        pair = (t // g == s // g) & upper & ((s % g) < half)
        level_id = np.where(pair, l, level_id)
    w = np.concatenate(blocks, axis=0)
    return jnp.asarray(w, BF16), jnp.asarray(level_id, jnp.int32), levels, n_fine


def _hgrn_kernel(w_ref, lvl_ref, q_ref, k_ref, v_ref, og_ref, lf_ref, gn_ref, o_ref, *,
                 c, levels, n_fine, seq, unroll):
    gain = gn_ref[...]

    def cumsums(ci):
        rows = pl.ds(pl.multiple_of(ci * c, c), c)
        g = lf_ref[0, 0, rows, :]
        g_hi = g.astype(BF16)
        g_lo = (g - g_hi.astype(F32)).astype(BF16)
        e2 = _dot(w_ref[...], jnp.concatenate([g_hi, g_lo], axis=1))
        return rows, e2[:, 0:HEAD_W] + e2[:, HEAD_W:2 * HEAD_W]

    def scores(rows, e):
        qb = q_ref[0, 0, rows, :]
        kb = k_ref[0, 0, rows, :]
        q = qb.astype(F32)
        k = kb.astype(F32)
        b = e[0:c]
        lvl = lvl_ref[...]
        pairs = [(qb, kb)]
        for l in range(1, levels + 1):
            if l <= n_fine:
                d = e[l * c:(l + 1) * c]
            else:
                g, half = 2 ** l, 2 ** (l - 1)
                b3 = b.reshape(c // g, g, HEAD_W)
                mid = b3[:, half - 1:half, :]
                d = jnp.concatenate([mid - b3[:, 0:half, :], b3[:, half:g, :] - mid],
                                    axis=1).reshape(c, HEAD_W)
            x = jnp.exp2(d)
            pairs.append(((q * x).astype(BF16), (k * x).astype(BF16)))
        a = jnp.zeros((c, c), F32)
        zero = jnp.zeros((c, HEAD_W), BF16)
        for l in range(0, levels + 1, 2):
            (qa, ka), (qc, kc) = pairs[l], pairs[l + 1]
            keys = jnp.concatenate([jnp.concatenate([ka, zero], axis=1),
                                    jnp.concatenate([zero, kc], axis=1)], axis=0)
            r = _dot_nt(jnp.concatenate([qa, qc], axis=1), keys)
            a = jnp.where(lvl == l, r[:, 0:c], a)
            a = jnp.where(lvl == l + 1, r[:, c:2 * c], a)
        b_last = b[c - 1:c, :]
        q_in = (q * jnp.exp2(b)).astype(BF16)
        k_out = (k * jnp.exp2(b_last - b)).astype(BF16)
        return a.astype(BF16), q_in, k_out, jnp.exp2(b_last)

    def outputs(rows, a, q_in, k_out, decay):
        v = v_ref[0, 0, rows, :]
        return rows, _dot(a, v), q_in, _dot_tn(v, k_out), decay

    def body(it, state_t):
        st1, st2, st3 = {}, {}, []
        for u in range(unroll + 2):
            if u < unroll:
                st1[u] = cumsums(it * unroll + u)
            if 0 <= u - 1 < unroll:
                st2[u - 1] = scores(*st1[u - 1])
            if 0 <= u - 2 < unroll:
                st3.append(outputs(st1[u - 2][0], *st2[u - 2]))
        for rows, o_intra, q_in, ds_t, decay in st3:
            o = o_intra + _dot_nt(q_in, state_t.astype(BF16))
            state_t = state_t * decay + ds_t
            y = _rms(o, gain) * og_ref[0, 0, rows, :].astype(F32)
            o_ref[rows, :] = y.astype(BF16)
        return state_t

    lax.fori_loop(0, seq // (c * unroll), body, jnp.zeros((HEAD_W, HEAD_W), F32))


def _hgrn(hin, logf, gain, *, batch, seq, c, unroll):
    w, lvl, levels, n_fine = _hgrn_tables(c)
    assert seq % (c * unroll) == 0

    def head(slab):
        return pl.BlockSpec((1, 1, seq, HEAD_W), lambda b, h: (b, slab * HEADS + h, 0, 0))

    return pl.pallas_call(
        functools.partial(_hgrn_kernel, c=c, levels=levels, n_fine=n_fine, seq=seq, unroll=unroll),
        out_shape=jax.ShapeDtypeStruct((batch * seq, MIX_W), BF16),
        grid=(batch, HEADS),
        in_specs=[
            pl.BlockSpec(w.shape, lambda b, h: (0, 0)),
            pl.BlockSpec((c, c), lambda b, h: (0, 0)),
            head(0), head(1), head(2), head(3), head(0),
            pl.BlockSpec((1, HEAD_W), lambda b, h: (0, 0)),
        ],
        out_specs=pl.BlockSpec((seq, HEAD_W), lambda b, h: (b, h)),
        compiler_params=_cparams(("parallel", "parallel")),
        name="hgrn",
    )(w, lvl, hin, hin, hin, hin, logf, gain.reshape(1, HEAD_W))


def _merge_kernel(x_ref, ya_ref, yb_ref, gt_ref, wa_ref, wb_ref, wo_ref, *rest, n_side):
    side_in, (o_ref,), side_out = rest[:n_side], rest[n_side:n_side + 1], rest[n_side + 1:]
    d = x_ref.shape[1]
    gt = gt_ref[...].astype(F32)
    merged = gt[:, 0:d] * _dot(ya_ref[...], wa_ref[...]) + gt[:, d:2 * d] * _dot(yb_ref[...], wb_ref[...])
    o_ref[...] = x_ref[...] + _dot(merged.astype(BF16), wo_ref[...])
    _SideCast.run(side_in, side_out)


def _resident(shape):
    return pl.BlockSpec(shape, lambda i: (0,) * len(shape), pipeline_mode=pl.Buffered(1))


def _merge(x, ya, yb, gates, wa, wb, wo, side_weights=(), *, tm):
    m, d = x.shape
    side = _SideCast(side_weights, m // tm, lambda i: i)
    out = pl.pallas_call(
        functools.partial(_merge_kernel, n_side=side.n),
        out_shape=[jax.ShapeDtypeStruct((m, d), F32)] + side.out_shapes,
        grid=(m // tm,),
        in_specs=[
            pl.BlockSpec((tm, d), lambda i: (i, 0)),
            pl.BlockSpec((tm, MIX_W), lambda i: (i, 0)),
            pl.BlockSpec((tm, MIX_W), lambda i: (i, 0)),
            pl.BlockSpec((tm, 2 * d), lambda i: (i, 0)),
            _resident(wa.shape), _resident(wb.shape), _resident(wo.shape),
        ] + side.in_specs,
        out_specs=[pl.BlockSpec((tm, d), lambda i: (i, 0))] + side.out_specs,
        compiler_params=_cparams(("arbitrary",)),
        name="merge",
    )(x, ya, yb, gates, wa, wb, wo, *side.arrays)
    return out[0], out[1:]


def _ple_kernel(x_ref, p_ref, gg_ref, pg_ref, wg_ref, wp_ref, o_ref):
    ple = _rms(_dot(p_ref[...].astype(BF16), wp_ref[...]), pg_ref[...])
    x = x_ref[...]
    gate = _sigmoid(_dot(_rms(x, gg_ref[...]).astype(BF16), wg_ref[...]))
    o_ref[...] = x + gate * ple


def _ple(x, p, gate_gain, post_gain, w_gate, w_proj, *, tm):
    m, d = x.shape
    pd = p.shape[1]
    return pl.pallas_call(
        _ple_kernel,
        out_shape=jax.ShapeDtypeStruct((m, d), F32),
        grid=(m // tm,),
        in_specs=[
            pl.BlockSpec((tm, d), lambda i: (i, 0)),
            pl.BlockSpec((tm, pd), lambda i: (i, 0)),
            pl.BlockSpec((1, d), lambda i: (0, 0)),
            pl.BlockSpec((1, d), lambda i: (0, 0)),
            _resident(w_gate.shape), _resident(w_proj.shape),
        ],
        out_specs=pl.BlockSpec((tm, d), lambda i: (i, 0)),
        compiler_params=_cparams(("parallel",)),
        name="ple",
    )(x, p, gate_gain.reshape(1, d), post_gain.reshape(1, d), w_gate, w_proj)


def _tiles(seq, d, f):
    def pick(n, choices):
        return next(c for c in choices if n % c == 0)
    return dict(
        tm=pick(seq, (512, 256, 128)),
        tfm=pick(seq, (1024, 512, 256, 128)),
        tf=pick(f, (512, 256, 128)),
        tp=pick(seq, (1024, 512, 256, 128)),
        tn=pick(math.gcd(MIX_W, 2 * d), (1024, 512, 256, 128)),
        ta=pick(seq, (512, 256, 128)),
        qpg=4,
        c=128,
        cu=16,
    )


def kernel(x, p, ffn1_norm, ffn1_w_gate, ffn1_w_up, ffn1_w_down, mix_norm, w_in, q_norm, k_norm, lambda_q1, lambda_k1, lambda_q2, lambda_k2, diff_subln, rel_bias, hgrn_lb_logits, hgrn_norm, w_branch_a, w_branch_b, w_out, ffn2_norm, ffn2_w_gate, ffn2_w_up, ffn2_w_down, ple_gate_norm, w_ple_gate, w_ple_proj, ple_post_norm):
    batch, seq, d = x.shape
    depth = ffn1_norm.shape[0]
    assert depth == 1
    m = batch * seq
    tl = _tiles(seq, d, ffn1_w_gate.shape[2])
    bf = lambda w: w.astype(BF16)

    lower_bounds = jnp.cumsum(jax.nn.softmax(hgrn_lb_logits.astype(F32), axis=0), axis=0)
    lb = lower_bounds[0].reshape(HEADS, HEAD_W)
    lam = (jnp.exp(jnp.sum(lambda_q1[0].astype(F32) * lambda_k1[0].astype(F32)))
           - jnp.exp(jnp.sum(lambda_q2[0].astype(F32) * lambda_k2[0].astype(F32)))
           + LAMBDA_INIT)
    scale = QK_DIM ** -0.5
    qg = jnp.tile(q_norm[0].astype(F32), 2).reshape(1, HEAD_W) * (scale * LOG2E)
    kg = jnp.tile(k_norm[0].astype(F32), 2).reshape(1, HEAD_W)
    qk_bound = (QK_DIM * scale * jnp.max(jnp.abs(q_norm[0].astype(F32)))
                * jnp.max(jnp.abs(k_norm[0].astype(F32))))

    x2 = x.reshape(m, d)
    x2, (w_in_b, wa_b, wb_b, wo_b) = _ffn(
        x2, ffn1_norm[0], bf(ffn1_w_gate[0]), bf(ffn1_w_up[0]), bf(ffn1_w_down[0]),
        (w_in[0], w_branch_a[0], w_branch_b[0], w_out[0]), tm=tl["tfm"], tf=tl["tf"])

    qkv, hin, logf, gates, (wg2_b, wu2_b, wd2_b) = _proj(
        x2, mix_norm[0], w_in_b, qg, kg, lb, (ffn2_w_gate[0], ffn2_w_up[0], ffn2_w_down[0]),
        batch=batch, seq=seq, tm=tl["tp"], tn=tl["tn"])
    ya, _ = _attn(qkv, lam, rel_bias, qk_bound, diff_subln[0], batch=batch, seq=seq,
                  t=tl["ta"], qpg=tl["qpg"])
    yb = _hgrn(hin, logf, hgrn_norm[0], batch=batch, seq=seq, c=tl["c"], unroll=tl["cu"])
    x2, (wpg_b, wpp_b) = _merge(x2, ya, yb, gates, wa_b, wb_b, wo_b,
                                (w_ple_gate[0], w_ple_proj[0]), tm=tl["tm"])

    x2, _ = _ffn(x2, ffn2_norm[0], wg2_b, wu2_b, wd2_b, tm=tl["tfm"], tf=tl["tf"])
    x2 = _ple(x2, p[0].reshape(m, -1), ple_gate_norm[0], ple_post_norm[0], wpg_b, wpp_b, tm=tl["tm"])
    return x2.reshape(batch, seq, d)
```

```python
import functools
import math

import jax
import jax.numpy as jnp
import numpy as np
from jax import lax
from jax.experimental import pallas as pl
from jax.experimental.pallas import tpu as pltpu

F32 = jnp.float32
BF16 = jnp.bfloat16

EPS = 1e-6
HEADS = 8
HEAD_W = 128
QK_DIM = 64
MIX_W = HEADS * HEAD_W
REL_BUCKETS = 32
REL_MAX_DIST = 128
LAMBDA_INIT = 0.8 - 0.6 * math.exp(-0.3 * 0)
LOG2E = math.log2(math.e)
MAX_STATIC_SOFTMAX_SPREAD = 60.0

V7X_MXU_WIDTH = 256
V7X_VMEM_BYTES = 64 * 1024 * 1024
VMEM_LIMIT = 56 * 1024 * 1024


def _cparams(sem, flags=None):
    return pltpu.CompilerParams(dimension_semantics=sem, vmem_limit_bytes=VMEM_LIMIT, flags=flags)


def _sigmoid(x):
    return 1.0 / (1.0 + jnp.exp(-x))


def _rms(x, gain):
    ms = jnp.mean(x * x, axis=-1, keepdims=True)
    return x * lax.rsqrt(ms + EPS) * gain


def _dot(a, b):
    return jnp.dot(a, b, preferred_element_type=F32)


def _dot_nt(a, b):
    return lax.dot_general(a, b, (((1,), (1,)), ((), ())), preferred_element_type=F32)


def _dot_tn(a, b):
    return lax.dot_general(a, b, (((0,), (0,)), ((), ())), preferred_element_type=F32)


BF16_ROW_TILE = 16


class _SideCast:
    def __init__(self, arrays, steps, step_index):
        self.arrays = list(arrays)
        self.n = len(self.arrays)
        self.step_index = step_index
        self.chunks = []
        for w in self.arrays:
            rows = w.shape[0]
            self.chunks.append(max(c for c in range(1, steps + 1)
                                   if rows % c == 0 and (rows // c) % BF16_ROW_TILE == 0))

    def _specs(self):
        def spec(w, c):
            return pl.BlockSpec((w.shape[0] // c, w.shape[1]),
                                lambda *ids: (jnp.minimum(self.step_index(*ids), c - 1), 0))
        return [spec(w, c) for w, c in zip(self.arrays, self.chunks)]

    in_specs = property(_specs)
    out_specs = property(_specs)

    @property
    def out_shapes(self):
        return [jax.ShapeDtypeStruct(w.shape, BF16) for w in self.arrays]

    @staticmethod
    def run(in_refs, out_refs):
        for src, dst in zip(in_refs, out_refs):
            dst[...] = src[...].astype(BF16)


def _row_tile_fetch(x_hbm, xs_ref, sem, tm):
    i = pl.program_id(0)
    j = pl.program_id(1)

    def copy(row_tile):
        return pltpu.make_async_copy(x_hbm.at[pl.ds(row_tile * tm, tm), :], xs_ref, sem)

    @pl.when((i == 0) & (j == 0))
    def _():
        copy(0).start()

    @pl.when((j == 1) & (i + 1 < pl.num_programs(0)))
    def _():
        copy(i + 1).start()

    return copy(i)


def _ffn_kernel(x_hbm, g_ref, wg_ref, wu_ref, wd_ref, *rest, tm, n_side):
    side_in, (o_ref,), side_out = rest[:n_side], rest[n_side:n_side + 1], rest[n_side + 1:2 * n_side + 1]
    xs_ref, h_ref, sem = rest[2 * n_side + 1:]
    j = pl.program_id(1)
    x_copy = _row_tile_fetch(x_hbm, xs_ref, sem, tm)

    @pl.when(j == 0)
    def _():
        x_copy.wait()
        x = xs_ref[...]
        h_ref[...] = _rms(x, g_ref[...]).astype(BF16)
        o_ref[...] = x

    h = h_ref[...]
    gate = _dot(h, wg_ref[...])
    up = _dot(h, wu_ref[...])
    act = (gate * _sigmoid(gate) * (0.5 * up)).astype(BF16)
    o_ref[...] += _dot(act, wd_ref[...])
    _SideCast.run(side_in, side_out)


def _ffn(x, gain, w_gate, w_up, w_down, side_weights=(), *, tm, tf):
    m, d = x.shape
    f = w_gate.shape[1]
    nj = f // tf
    assert nj >= 2
    side = _SideCast(side_weights, (m // tm) * nj, lambda i, j: i * nj + j)
    out = pl.pallas_call(
        functools.partial(_ffn_kernel, tm=tm, n_side=side.n),
        out_shape=[jax.ShapeDtypeStruct((m, d), F32)] + side.out_shapes,
        grid=(m // tm, nj),
        in_specs=[
            pl.BlockSpec(memory_space=pl.ANY),
            pl.BlockSpec((1, d), lambda i, j: (0, 0)),
            pl.BlockSpec((d, tf), lambda i, j: (0, j)),
            pl.BlockSpec((d, tf), lambda i, j: (0, j)),
            pl.BlockSpec((tf, d), lambda i, j: (j, 0)),
        ] + side.in_specs,
        out_specs=[pl.BlockSpec((tm, d), lambda i, j: (i, 0))] + side.out_specs,
        scratch_shapes=[pltpu.VMEM((tm, d), F32), pltpu.VMEM((tm, d), BF16),
                        pltpu.SemaphoreType.DMA(())],
        compiler_params=_cparams(("arbitrary", "arbitrary")),
        name="ffn",
    )(x, gain.reshape(1, d), w_gate, w_up, w_down, *side.arrays)
    return out[0], out[1:]


def _proj_kernel(x_hbm, g_ref, w_ref, qg_ref, kg_ref, lb_ref, gm_ref, *rest, tm, tn, nj, n_side):
    side_in, rest = rest[:n_side], rest[n_side:]
    (oqkv_ref, oh_ref, olf_ref, og_ref), rest = rest[:4], rest[4:]
    side_out, (xs_ref, h_ref, acc_ref, sem) = rest[:n_side], rest[n_side:]
    j = pl.program_id(1)
    jp = j - 1
    tps = MIX_W // tn
    hpt = tn // HEAD_W

    def heads(vals):
        return [vals[:, s * HEAD_W:(s + 1) * HEAD_W] for s in range(hpt)]

    def epi_qk(acc):
        gain = jnp.where(jp < tps, qg_ref[...], kg_ref[...])
        pw = gm_ref.shape[0]
        for s in range(tn // pw):
            a = acc[:, s * pw:(s + 1) * pw]
            ms = _dot((a * a).astype(BF16), gm_ref[...])
            y = a * lax.rsqrt(ms + EPS)
            for u in range(pw // HEAD_W):
                oqkv_ref[0, s * (pw // HEAD_W) + u] = (
                    y[:, u * HEAD_W:(u + 1) * HEAD_W] * gain).astype(BF16)

    def epi_v(acc):
        for s, a in enumerate(heads(acc)):
            oqkv_ref[0, s] = a.astype(BF16)

    def epi_silu(acc):
        for s, a in enumerate(heads(acc * _sigmoid(acc))):
            oh_ref[0, s] = a.astype(BF16)

    def epi_forget(acc):
        jj = jp - 4 * tps
        for s, a in enumerate(heads(acc)):
            lb = lb_ref[pl.ds(jj * hpt + s, 1), :]
            forget = lb + (1.0 - lb) * _sigmoid(a)
            oh_ref[0, s] = (1.0 - forget).astype(BF16)
            olf_ref[0, s] = jnp.log2(forget)

    def epi_hv(acc):
        for s, a in enumerate(heads(acc)):
            oh_ref[0, s] = a.astype(BF16)

    def epi_gates(acc):
        og_ref[...] = _sigmoid(acc).astype(BF16)

    x_copy = _row_tile_fetch(x_hbm, xs_ref, sem, tm)

    @pl.when(j == 0)
    def _():
        x_copy.wait()
        h_ref[...] = _rms(xs_ref[...], g_ref[...]).astype(BF16)
        acc_ref[...] = _dot(h_ref[...], w_ref[...])
        _SideCast.run(side_in, side_out)

    def fused(cond, epi):
        @pl.when(cond & (j >= 1) & (j < nj))
        def _():
            acc = acc_ref[...]
            epi(acc)
            acc_ref[...] = _dot(h_ref[...], w_ref[...])
            _SideCast.run(side_in, side_out)

    fused(jp < 2 * tps, epi_qk)
    fused((jp >= 2 * tps) & (jp < 3 * tps), epi_v)
    fused(((jp >= 3 * tps) & (jp < 4 * tps)) | ((jp >= 6 * tps) & (jp < 7 * tps)), epi_silu)
    fused((jp >= 4 * tps) & (jp < 5 * tps), epi_forget)
    fused((jp >= 5 * tps) & (jp < 6 * tps), epi_hv)
    fused(jp >= 7 * tps, epi_gates)

    @pl.when(j == nj)
    def _():
        epi_gates(acc_ref[...])
        _SideCast.run(side_in, side_out)


def _proj(x, gain, w_in, qg, kg, lb, side_weights=(), *, batch, seq, tm, tn):
    m, d = x.shape
    n_in = w_in.shape[1]
    tps = MIX_W // tn
    hpt = tn // HEAD_W
    spb = seq // tm
    ng = 2 * d // tn
    nj = 7 * tps + ng
    assert n_in == 7 * MIX_W + 2 * d and MIX_W % tn == 0 and (2 * d) % tn == 0

    pw = math.gcd(tn, V7X_MXU_WIDTH)
    lane = np.arange(pw)
    gm = (lane[:, None] // QK_DIM == lane[None, :] // QK_DIM).astype(np.float32) / QK_DIM

    def hm_block(lo, n):
        return lambda i, j: (i // spb, jnp.clip(j - 1 - lo, 0, n - 1), i % spb, 0)

    side = _SideCast(side_weights, (m // tm) * (nj + 1), lambda i, j: i * (nj + 1) + j)
    out = pl.pallas_call(
        functools.partial(_proj_kernel, tm=tm, tn=tn, nj=nj, n_side=side.n),
        out_shape=[
            jax.ShapeDtypeStruct((batch, 3 * HEADS, seq, HEAD_W), BF16),
            jax.ShapeDtypeStruct((batch, 4 * HEADS, seq, HEAD_W), BF16),
            jax.ShapeDtypeStruct((batch, HEADS, seq, HEAD_W), F32),
            jax.ShapeDtypeStruct((m, 2 * d), BF16),
        ] + side.out_shapes,
        grid=(m // tm, nj + 1),
        in_specs=[
            pl.BlockSpec(memory_space=pl.ANY),
            pl.BlockSpec((1, d), lambda i, j: (0, 0)),
            pl.BlockSpec((d, tn), lambda i, j: (0, jnp.minimum(j, nj - 1))),
            pl.BlockSpec((1, HEAD_W), lambda i, j: (0, 0)),
            pl.BlockSpec((1, HEAD_W), lambda i, j: (0, 0)),
            pl.BlockSpec((HEADS, HEAD_W), lambda i, j: (0, 0)),
            pl.BlockSpec((pw, pw), lambda i, j: (0, 0)),
        ] + side.in_specs,
        out_specs=[
            pl.BlockSpec((1, hpt, tm, HEAD_W), hm_block(0, 3 * tps)),
            pl.BlockSpec((1, hpt, tm, HEAD_W), hm_block(3 * tps, 4 * tps)),
            pl.BlockSpec((1, hpt, tm, HEAD_W), hm_block(4 * tps, tps)),
            pl.BlockSpec((tm, tn), lambda i, j: (i, jnp.clip(j - 1 - 7 * tps, 0, ng - 1))),
        ] + side.out_specs,
        scratch_shapes=[pltpu.VMEM((tm, d), F32), pltpu.VMEM((tm, d), BF16),
                        pltpu.VMEM((tm, tn), F32), pltpu.SemaphoreType.DMA(())],
        compiler_params=_cparams(("arbitrary", "arbitrary")),
        name="proj",
    )(x, gain.reshape(1, d), w_in, qg, kg, lb, jnp.asarray(gm, BF16), *side.arrays)
    return out[0], out[1], out[2], out[3], out[4:]


def _attn_kernel(lam_ref, off_ref, q_ref, k_ref, v_ref, w_ref, sg_ref, *rest, t, nkb, qpg, hps, online,
                 n_side):
    side_in, (o_ref,), side_out = rest[:n_side], rest[n_side:n_side + 1], rest[n_side + 1:2 * n_side + 1]
    vt_ref, bias_ref = rest[2 * n_side + 1:]
    hg = pl.program_id(0)
    b = pl.program_id(1)
    i = pl.program_id(2)
    lam = lam_ref[0]

    @pl.when((b == 0) & (i == 0))
    def _():
        for hh in range(hps):
            for dd in range(2):
                rows = jnp.broadcast_to(w_ref[hh, dd:dd + 1, :], (t, 2 * t))
                bias_ref[hh, dd] = pltpu.roll(rows, 0, 1, stride=1, stride_axis=0)[:, 0:t]

    @pl.when(i == 0)
    def _():
        for hh in range(hps):
            for jb in range(nkb):
                vt_ref[hh, jb] = v_ref[0, hh, jb * t:(jb + 1) * t, :].astype(F32).T.astype(BF16)

    def colsum8(p):
        return jnp.sum(p.reshape(t // 8, 8, 2 * t), axis=0)

    def both(x):
        return jnp.concatenate([x, x], axis=1)

    def query_block(hh, iq, row0):
        far_off = off_ref[hg * hps + hh]
        q = q_ref[0, hh, row0:row0 + t, :]
        lane = lax.broadcasted_iota(jnp.int32, (t, HEAD_W), 1)
        zero = jnp.zeros_like(q)
        qs = jnp.concatenate([jnp.where(lane < QK_DIM, q, zero),
                              jnp.where(lane >= QK_DIM, q, zero)], axis=0)
        offsets = [far_off] * max(iq - 1, 0)
        if iq >= 1:
            offsets.append(both(bias_ref[hh, 1]))
        offsets.append(both(bias_ref[hh, 0]))
        m = l = acc = None
        for j, badd in enumerate(offsets):
            s = _dot_nt(k_ref[0, hh, j * t:(j + 1) * t, :], qs) + badd
            if online:
                bmax = jnp.max(s, axis=0, keepdims=True)
                m_new = bmax if m is None else jnp.maximum(m, bmax)
                p = jnp.exp2(s - m_new)
            else:
                p = jnp.exp2(s)
            psum = colsum8(p)
            pv = _dot(vt_ref[hh, j], p.astype(BF16))
            if l is None:
                l, acc = psum, pv
            elif online:
                alpha = jnp.exp2(m - m_new)
                l, acc = alpha * l + psum, alpha * acc + pv
            else:
                l, acc = l + psum, acc + pv
            if online:
                m = m_new
        l = jnp.sum(l, axis=0, keepdims=True)
        o_t = acc[:, 0:t] / l[:, 0:t] - lam * (acc[:, t:2 * t] / l[:, t:2 * t])
        o_ref[row0:row0 + t, hh * HEAD_W:(hh + 1) * HEAD_W] = (
            _rms(o_t.T, sg_ref[...]) * (1.0 - LAMBDA_INIT)).astype(BF16)

    def query_group(ig):
        for hh in range(hps):
            for u in range(qpg):
                query_block(hh, ig * qpg + u, u * t)
        _SideCast.run(side_in, side_out)

    for ig in range(nkb // qpg):
        pl.when(i == ig)(functools.partial(query_group, ig))


def _t5_bucket(dist):
    n = jnp.maximum(dist, 0)
    max_exact = REL_BUCKETS // 2
    nf = jnp.maximum(n, 1).astype(F32)
    large = max_exact + (jnp.log(nf / max_exact) / math.log(REL_MAX_DIST / max_exact)
                         * (REL_BUCKETS - max_exact)).astype(jnp.int32)
    large = jnp.minimum(large, REL_BUCKETS - 1)
    return jnp.where(n < max_exact, n, large)


def _bias_generators(f, t):
    hole = jnp.full((HEADS, 1), -jnp.inf, F32)
    w0 = jnp.concatenate([f[:, 0:t], jnp.full((HEADS, t), -jnp.inf, F32)], axis=1)
    w1 = jnp.concatenate([f[:, t:2 * t], hole, f[:, 1:t]], axis=1)
    return jnp.stack([w0, w1], axis=1)


def _attn(qkv, lam, rel_bias, qk_bound, subln, side_weights=(), *, batch, seq, t, qpg, hps):
    assert t >= REL_MAX_DIST
    nq = seq // t
    rb = rel_bias.astype(F32)
    spread = 2.0 * qk_bound + jnp.max(jnp.max(rb, axis=0) - jnp.min(rb, axis=0))
    static_ok = spread <= MAX_STATIC_SOFTMAX_SPREAD
    upper = jnp.where(static_ok, qk_bound + jnp.max(rb, axis=0), 0.0)
    f = rb[_t5_bucket(jnp.arange(2 * t))].T - upper[:, None]
    bias = _bias_generators(f * LOG2E, t)
    far_off = (rb[REL_BUCKETS - 1] - upper) * LOG2E

    assert nq % qpg == 0 and HEADS % hps == 0
    ng = nq // qpg
    hgs = HEADS // hps
    side = _SideCast(side_weights, hgs * batch * ng, lambda h, b, i: (h * batch + b) * ng + i)

    def call(online):
        out = pl.pallas_call(
            functools.partial(_attn_kernel, t=t, nkb=nq, qpg=qpg, hps=hps, online=online,
                              n_side=side.n),
            out_shape=[jax.ShapeDtypeStruct((batch * seq, MIX_W), BF16)] + side.out_shapes,
            grid=(hgs, batch, ng),
            in_specs=[
                pl.BlockSpec(memory_space=pltpu.SMEM),
                pl.BlockSpec(memory_space=pltpu.SMEM),
                pl.BlockSpec((1, hps, qpg * t, HEAD_W), lambda h, b, i: (b, h, i, 0)),
                pl.BlockSpec((1, hps, seq, HEAD_W), lambda h, b, i: (b, hgs + h, 0, 0)),
                pl.BlockSpec((1, hps, seq, HEAD_W), lambda h, b, i: (b, 2 * hgs + h, 0, 0)),
                pl.BlockSpec((hps, 2, 2 * t), lambda h, b, i: (h, 0, 0)),
                pl.BlockSpec((1, HEAD_W), lambda h, b, i: (0, 0)),
            ] + side.in_specs,
            out_specs=[pl.BlockSpec((qpg * t, hps * HEAD_W), lambda h, b, i: (b * ng + i, h))]
            + side.out_specs,
            scratch_shapes=[pltpu.VMEM((hps, nq, HEAD_W, t), BF16),
                            pltpu.VMEM((hps, 2, t, t), F32)],
            compiler_params=_cparams(("arbitrary", "arbitrary", "arbitrary")),
            name="attn_online" if online else "attn",
        )(lam.reshape(1), far_off, qkv, qkv, qkv, bias, subln.reshape(1, HEAD_W), *side.arrays)
        return out[0], tuple(out[1:])

    return lax.cond(static_ok, lambda: call(False), lambda: call(True))


SUBLANES = 8


def _hgrn_tables(c):
    levels = int(math.log2(c))
    assert 2 ** levels == c
    t = np.arange(c)[:, None]
    s = np.arange(c)[None, :]
    blocks = [(s <= t).astype(np.float32)]
    level_id = np.where(t == s, 0, -1)
    n_fine = 0
    for l in range(1, levels + 1):
        g, half = 2 ** l, 2 ** (l - 1)
        mid = (t // g) * g + half - 1
        upper = (t % g) >= half
        if half < SUBLANES:
            above = upper & (s > mid) & (s <= t)
            below = (~upper) & (s > t) & (s <= mid)
            blocks.append((above | below).astype(np.float32))
            n_fine += 1
        pair = (t // g == s // g) & upper & ((s % g) < half)
        level_id = np.where(pair, l, level_id)
    w = np.concatenate(blocks, axis=0)
    return jnp.asarray(w, BF16), jnp.asarray(level_id, jnp.int32), levels, n_fine


def _hgrn_kernel(w_ref, lvl_ref, q_ref, k_ref, v_ref, og_ref, lf_ref, gn_ref, o_ref, *,
                 c, levels, n_fine, seq, unroll, hpg):
    gain = gn_ref[...]

    def cumsums(hh, ci):
        rows = pl.ds(pl.multiple_of(ci * c, c), c)
        g = lf_ref[0, hh, rows, :]
        g_hi = g.astype(BF16)
        g_lo = (g - g_hi.astype(F32)).astype(BF16)
        e2 = _dot(w_ref[...], jnp.concatenate([g_hi, g_lo], axis=1))
        return hh, rows, e2[:, 0:HEAD_W] + e2[:, HEAD_W:2 * HEAD_W]

    def scores(hh, rows, e):
        qb = q_ref[0, hh, rows, :]
        kb = k_ref[0, hh, rows, :]
        q = qb.astype(F32)
        k = kb.astype(F32)
        b = e[0:c]
        lvl = lvl_ref[...]
        pairs = [(qb, kb)]
        for l in range(1, levels + 1):
            if l <= n_fine:
                d = e[l * c:(l + 1) * c]
            else:
                g, half = 2 ** l, 2 ** (l - 1)
                b3 = b.reshape(c // g, g, HEAD_W)
                mid = b3[:, half - 1:half, :]
                d = jnp.concatenate([mid - b3[:, 0:half, :], b3[:, half:g, :] - mid],
                                    axis=1).reshape(c, HEAD_W)
            x = jnp.exp2(d)
            pairs.append(((q * x).astype(BF16), (k * x).astype(BF16)))
        a = jnp.zeros((c, c), F32)
        zero = jnp.zeros((c, HEAD_W), BF16)
        for l in range(0, levels + 1, 2):
            (qa, ka), (qc, kc) = pairs[l], pairs[l + 1]
            keys = jnp.concatenate([jnp.concatenate([ka, zero], axis=1),
                                    jnp.concatenate([zero, kc], axis=1)], axis=0)
            r = _dot_nt(jnp.concatenate([qa, qc], axis=1), keys)
            a = jnp.where(lvl == l, r[:, 0:c], a)
            a = jnp.where(lvl == l + 1, r[:, c:2 * c], a)
        b_last = b[c - 1:c, :]
        q_in = (q * jnp.exp2(b)).astype(BF16)
        k_out = (k * jnp.exp2(b_last - b)).astype(BF16)
        return a.astype(BF16), q_in, k_out, jnp.exp2(b_last)

    def outputs(hh, rows, a, q_in, k_out, decay):
        v = v_ref[0, hh, rows, :]
        return hh, rows, _dot(a, v), q_in, _dot_tn(v, k_out), decay

    def body(it, states):
        units = [(hh, it * unroll + u) for u in range(unroll) for hh in range(hpg)]
        st1, st2, st3 = {}, {}, []
        for n in range(len(units) + 2):
            if n < len(units):
                st1[n] = cumsums(*units[n])
            if 0 <= n - 1 < len(units):
                st2[n - 1] = scores(*st1[n - 1])
            if 0 <= n - 2 < len(units):
                st3.append(outputs(*st1[n - 2][0:2], *st2[n - 2]))
        states = list(states)
        for hh, rows, o_intra, q_in, ds_t, decay in st3:
            o = o_intra + _dot_nt(q_in, states[hh].astype(BF16))
            states[hh] = states[hh] * decay + ds_t
            y = _rms(o, gain) * og_ref[0, hh, rows, :].astype(F32)
            o_ref[rows, hh * HEAD_W:(hh + 1) * HEAD_W] = y.astype(BF16)
        return tuple(states)

    lax.fori_loop(0, seq // (c * unroll), body,
                  tuple(jnp.zeros((HEAD_W, HEAD_W), F32) for _ in range(hpg)))


def _hgrn(hin, logf, gain, *, batch, seq, c, unroll, hpg):
    w, lvl, levels, n_fine = _hgrn_tables(c)
    assert seq % (c * unroll) == 0 and HEADS % hpg == 0
    groups = HEADS // hpg

    def head(slab):
        return pl.BlockSpec((1, hpg, seq, HEAD_W), lambda b, h: (b, slab * groups + h, 0, 0))

    return pl.pallas_call(
        functools.partial(_hgrn_kernel, c=c, levels=levels, n_fine=n_fine, seq=seq, unroll=unroll,
                          hpg=hpg),
        out_shape=jax.ShapeDtypeStruct((batch * seq, MIX_W), BF16),
        grid=(batch, groups),
        in_specs=[
            pl.BlockSpec(w.shape, lambda b, h: (0, 0)),
            pl.BlockSpec((c, c), lambda b, h: (0, 0)),
            head(0), head(1), head(2), head(3), head(0),
            pl.BlockSpec((1, HEAD_W), lambda b, h: (0, 0)),
        ],
        out_specs=pl.BlockSpec((seq, hpg * HEAD_W), lambda b, h: (b, h)),
        compiler_params=_cparams(("parallel", "parallel")),
        name="hgrn",
    )(w, lvl, hin, hin, hin, hin, logf, gain.reshape(1, HEAD_W))


def _merge_kernel(x_ref, ya_ref, yb_ref, gt_ref, wa_ref, wb_ref, wo_ref, *rest, n_side):
    side_in, (o_ref,), side_out = rest[:n_side], rest[n_side:n_side + 1], rest[n_side + 1:]
    d = x_ref.shape[1]
    gt = gt_ref[...].astype(F32)
    merged = gt[:, 0:d] * _dot(ya_ref[...], wa_ref[...]) + gt[:, d:2 * d] * _dot(yb_ref[...], wb_ref[...])
    o_ref[...] = x_ref[...] + _dot(merged.astype(BF16), wo_ref[...])
    _SideCast.run(side_in, side_out)


def _resident(shape):
    return pl.BlockSpec(shape, lambda i: (0,) * len(shape), pipeline_mode=pl.Buffered(1))


def _merge(x, ya, yb, gates, wa, wb, wo, side_weights=(), *, tm):
    m, d = x.shape
    side = _SideCast(side_weights, m // tm, lambda i: i)
    out = pl.pallas_call(
        functools.partial(_merge_kernel, n_side=side.n),
        out_shape=[jax.ShapeDtypeStruct((m, d), F32)] + side.out_shapes,
        grid=(m // tm,),
        in_specs=[
            pl.BlockSpec((tm, d), lambda i: (i, 0)),
            pl.BlockSpec((tm, MIX_W), lambda i: (i, 0)),
            pl.BlockSpec((tm, MIX_W), lambda i: (i, 0)),
            pl.BlockSpec((tm, 2 * d), lambda i: (i, 0)),
            _resident(wa.shape), _resident(wb.shape), _resident(wo.shape),
        ] + side.in_specs,
        out_specs=[pl.BlockSpec((tm, d), lambda i: (i, 0))] + side.out_specs,
        compiler_params=_cparams(("arbitrary",)),
        name="merge",
    )(x, ya, yb, gates, wa, wb, wo, *side.arrays)
    return out[0], out[1:]


def _ple_kernel(x_ref, p_ref, gg_ref, pg_ref, wg_ref, wp_ref, o_ref):
    ple = _rms(_dot(p_ref[...].astype(BF16), wp_ref[...]), pg_ref[...])
    x = x_ref[...]
    gate = _sigmoid(_dot(_rms(x, gg_ref[...]).astype(BF16), wg_ref[...]))
    o_ref[...] = x + gate * ple


def _ple(x, p, gate_gain, post_gain, w_gate, w_proj, *, tm):
    m, d = x.shape
    pd = p.shape[1]
    return pl.pallas_call(
        _ple_kernel,
        out_shape=jax.ShapeDtypeStruct((m, d), F32),
        grid=(m // tm,),
        in_specs=[
            pl.BlockSpec((tm, d), lambda i: (i, 0)),
            pl.BlockSpec((tm, pd), lambda i: (i, 0)),
            pl.BlockSpec((1, d), lambda i: (0, 0)),
            pl.BlockSpec((1, d), lambda i: (0, 0)),
            _resident(w_gate.shape), _resident(w_proj.shape),
        ],
        out_specs=pl.BlockSpec((tm, d), lambda i: (i, 0)),
        compiler_params=_cparams(("parallel",)),
        name="ple",
    )(x, p, gate_gain.reshape(1, d), post_gain.reshape(1, d), w_gate, w_proj)


def _tiles(seq, d, f):
    def pick(n, choices):
        return next(c for c in choices if n % c == 0)
    return dict(
        tm=pick(seq, (512, 256, 128)),
        tfm=pick(seq, (1024, 512, 256, 128)),
        tf=pick(f, (512, 256, 128)),
        tp=pick(seq, (1024, 512, 256, 128)),
        tn=pick(math.gcd(MIX_W, 2 * d), (1024, 512, 256, 128)),
        ta=pick(seq, (512, 256, 128)),
        qpg=4,
        hps=2,
        c=128,
        cu=16,
        hpg=2,
    )


def kernel(x, p, ffn1_norm, ffn1_w_gate, ffn1_w_up, ffn1_w_down, mix_norm, w_in, q_norm, k_norm, lambda_q1, lambda_k1, lambda_q2, lambda_k2, diff_subln, rel_bias, hgrn_lb_logits, hgrn_norm, w_branch_a, w_branch_b, w_out, ffn2_norm, ffn2_w_gate, ffn2_w_up, ffn2_w_down, ple_gate_norm, w_ple_gate, w_ple_proj, ple_post_norm):
    batch, seq, d = x.shape
    depth = ffn1_norm.shape[0]
    assert depth == 1
    m = batch * seq
    tl = _tiles(seq, d, ffn1_w_gate.shape[2])
    bf = lambda w: w.astype(BF16)

    lower_bounds = jnp.cumsum(jax.nn.softmax(hgrn_lb_logits.astype(F32), axis=0), axis=0)
    lb = lower_bounds[0].reshape(HEADS, HEAD_W)
    lam = (jnp.exp(jnp.sum(lambda_q1[0].astype(F32) * lambda_k1[0].astype(F32)))
           - jnp.exp(jnp.sum(lambda_q2[0].astype(F32) * lambda_k2[0].astype(F32)))
           + LAMBDA_INIT)
    scale = QK_DIM ** -0.5
    qg = jnp.tile(q_norm[0].astype(F32), 2).reshape(1, HEAD_W) * (scale * LOG2E)
    kg = jnp.tile(k_norm[0].astype(F32), 2).reshape(1, HEAD_W)
    qk_bound = (QK_DIM * scale * jnp.max(jnp.abs(q_norm[0].astype(F32)))
                * jnp.max(jnp.abs(k_norm[0].astype(F32))))

    x2 = x.reshape(m, d)
    x2, (w_in_b, wa_b, wb_b, wo_b) = _ffn(
        x2, ffn1_norm[0], bf(ffn1_w_gate[0]), bf(ffn1_w_up[0]), bf(ffn1_w_down[0]),
        (w_in[0], w_branch_a[0], w_branch_b[0], w_out[0]), tm=tl["tfm"], tf=tl["tf"])

    qkv, hin, logf, gates, (wg2_b, wu2_b, wd2_b) = _proj(
        x2, mix_norm[0], w_in_b, qg, kg, lb, (ffn2_w_gate[0], ffn2_w_up[0], ffn2_w_down[0]),
        batch=batch, seq=seq, tm=tl["tp"], tn=tl["tn"])
    ya, _ = _attn(qkv, lam, rel_bias, qk_bound, diff_subln[0], batch=batch, seq=seq,
                  t=tl["ta"], qpg=tl["qpg"], hps=tl["hps"])
    yb = _hgrn(hin, logf, hgrn_norm[0], batch=batch, seq=seq, c=tl["c"], unroll=tl["cu"],
               hpg=tl["hpg"])
    x2, (wpg_b, wpp_b) = _merge(x2, ya, yb, gates, wa_b, wb_b, wo_b,
                                (w_ple_gate[0], w_ple_proj[0]), tm=tl["tm"])

    x2, _ = _ffn(x2, ffn2_norm[0], wg2_b, wu2_b, wd2_b, tm=tl["tfm"], tf=tl["tf"])
    x2 = _ple(x2, p[0].reshape(m, -1), ple_gate_norm[0], ple_post_norm[0], wpg_b, wpp_b, tm=tl["tm"])
    return x2.reshape(batch, seq, d)
```

```python
import functools
import math

import jax
import jax.numpy as jnp
import numpy as np
from jax import lax
from jax.experimental import pallas as pl
from jax.experimental.pallas import tpu as pltpu

F32 = jnp.float32
BF16 = jnp.bfloat16

EPS = 1e-6
HEADS = 8
HEAD_W = 128
QK_DIM = 64
MIX_W = HEADS * HEAD_W
REL_BUCKETS = 32
REL_MAX_DIST = 128
LAMBDA_INIT = 0.8 - 0.6 * math.exp(-0.3 * 0)
LOG2E = math.log2(math.e)
MAX_STATIC_SOFTMAX_SPREAD = 60.0

V7X_MXU_WIDTH = 256
V7X_VMEM_BYTES = 64 * 1024 * 1024
VMEM_LIMIT = 56 * 1024 * 1024


def _cparams(sem, flags=None):
    return pltpu.CompilerParams(dimension_semantics=sem, vmem_limit_bytes=VMEM_LIMIT, flags=flags)


def _sigmoid(x):
    return 1.0 / (1.0 + jnp.exp(-x))


def _rms(x, gain):
    ms = jnp.mean(x * x, axis=-1, keepdims=True)
    return x * lax.rsqrt(ms + EPS) * gain


def _dot(a, b):
    return jnp.dot(a, b, preferred_element_type=F32)


def _dot_nt(a, b):
    return lax.dot_general(a, b, (((1,), (1,)), ((), ())), preferred_element_type=F32)


def _dot_tn(a, b):
    return lax.dot_general(a, b, (((0,), (0,)), ((), ())), preferred_element_type=F32)


BF16_ROW_TILE = 16


class _SideCast:
    def __init__(self, arrays, steps, step_index):
        self.arrays = list(arrays)
        self.n = len(self.arrays)
        self.step_index = step_index
        self.chunks = []
        for w in self.arrays:
            rows = w.shape[0]
            self.chunks.append(max(c for c in range(1, steps + 1)
                                   if rows % c == 0 and (rows // c) % BF16_ROW_TILE == 0))

    def _specs(self):
        def spec(w, c):
            return pl.BlockSpec((w.shape[0] // c, w.shape[1]),
                                lambda *ids: (jnp.minimum(self.step_index(*ids), c - 1), 0))
        return [spec(w, c) for w, c in zip(self.arrays, self.chunks)]

    in_specs = property(_specs)
    out_specs = property(_specs)

    @property
    def out_shapes(self):
        return [jax.ShapeDtypeStruct(w.shape, BF16) for w in self.arrays]

    @staticmethod
    def run(in_refs, out_refs):
        for src, dst in zip(in_refs, out_refs):
            dst[...] = src[...].astype(BF16)


def _row_tile_fetch(x_hbm, xs_ref, sem, tm):
    i = pl.program_id(0)
    j = pl.program_id(1)

    def copy(row_tile):
        return pltpu.make_async_copy(x_hbm.at[pl.ds(row_tile * tm, tm), :], xs_ref, sem)

    @pl.when((i == 0) & (j == 0))
    def _():
        copy(0).start()

    @pl.when((j == 1) & (i + 1 < pl.num_programs(0)))
    def _():
        copy(i + 1).start()

    return copy(i)


def _ffn_kernel(x_hbm, g_ref, wg_ref, wu_ref, wd_ref, *rest, tm, n_side):
    side_in, (o_ref,), side_out = rest[:n_side], rest[n_side:n_side + 1], rest[n_side + 1:2 * n_side + 1]
    xs_ref, h_ref, sem = rest[2 * n_side + 1:]
    j = pl.program_id(1)
    x_copy = _row_tile_fetch(x_hbm, xs_ref, sem, tm)

    @pl.when(j == 0)
    def _():
        x_copy.wait()
        x = xs_ref[...]
        h_ref[...] = _rms(x, g_ref[...]).astype(BF16)
        o_ref[...] = x

    h = h_ref[...]
    gate = _dot(h, wg_ref[...])
    up = _dot(h, wu_ref[...])
    act = (gate * _sigmoid(gate) * (0.5 * up)).astype(BF16)
    o_ref[...] += _dot(act, wd_ref[...])
    _SideCast.run(side_in, side_out)


def _ffn(x, gain, w_gate, w_up, w_down, side_weights=(), *, tm, tf):
    m, d = x.shape
    f = w_gate.shape[1]
    nj = f // tf
    assert nj >= 2
    side = _SideCast(side_weights, (m // tm) * nj, lambda i, j: i * nj + j)
    out = pl.pallas_call(
        functools.partial(_ffn_kernel, tm=tm, n_side=side.n),
        out_shape=[jax.ShapeDtypeStruct((m, d), F32)] + side.out_shapes,
        grid=(m // tm, nj),
        in_specs=[
            pl.BlockSpec(memory_space=pl.ANY),
            pl.BlockSpec((1, d), lambda i, j: (0, 0)),
            pl.BlockSpec((d, tf), lambda i, j: (0, j)),
            pl.BlockSpec((d, tf), lambda i, j: (0, j)),
            pl.BlockSpec((tf, d), lambda i, j: (j, 0)),
        ] + side.in_specs,
        out_specs=[pl.BlockSpec((tm, d), lambda i, j: (i, 0))] + side.out_specs,
        scratch_shapes=[pltpu.VMEM((tm, d), F32), pltpu.VMEM((tm, d), BF16),
                        pltpu.SemaphoreType.DMA(())],
        compiler_params=_cparams(("arbitrary", "arbitrary")),
        name="ffn",
    )(x, gain.reshape(1, d), w_gate, w_up, w_down, *side.arrays)
    return out[0], out[1:]


def _proj_kernel(x_hbm, g_ref, w_ref, qg_ref, kg_ref, lb_ref, gm_ref, *rest, tm, tn, nj, n_side):
    side_in, rest = rest[:n_side], rest[n_side:]
    (oqkv_ref, oh_ref, olf_ref, og_ref), rest = rest[:4], rest[4:]
    side_out, (xs_ref, h_ref, acc_ref, sem) = rest[:n_side], rest[n_side:]
    j = pl.program_id(1)
    jp = j - 1
    tps = MIX_W // tn
    hpt = tn // HEAD_W

    def heads(vals):
        return [vals[:, s * HEAD_W:(s + 1) * HEAD_W] for s in range(hpt)]

    def epi_qk(acc):
        gain = jnp.where(jp < tps, qg_ref[...], kg_ref[...])
        pw = gm_ref.shape[0]
        for s in range(tn // pw):
            a = acc[:, s * pw:(s + 1) * pw]
            ms = _dot((a * a).astype(BF16), gm_ref[...])
            y = a * lax.rsqrt(ms + EPS)
            for u in range(pw // HEAD_W):
                oqkv_ref[0, s * (pw // HEAD_W) + u] = (
                    y[:, u * HEAD_W:(u + 1) * HEAD_W] * gain).astype(BF16)

    def epi_v(acc):
        for s, a in enumerate(heads(acc)):
            oqkv_ref[0, s] = a.astype(BF16)

    def epi_silu(acc):
        for s, a in enumerate(heads(acc * _sigmoid(acc))):
            oh_ref[0, s] = a.astype(BF16)

    def epi_forget(acc):
        jj = jp - 4 * tps
        for s, a in enumerate(heads(acc)):
            lb = lb_ref[pl.ds(jj * hpt + s, 1), :]
            forget = lb + (1.0 - lb) * _sigmoid(a)
            oh_ref[0, s] = (1.0 - forget).astype(BF16)
            olf_ref[0, s] = jnp.log2(forget)

    def epi_hv(acc):
        for s, a in enumerate(heads(acc)):
            oh_ref[0, s] = a.astype(BF16)

    def epi_gates(acc):
        og_ref[...] = acc.astype(BF16)

    x_copy = _row_tile_fetch(x_hbm, xs_ref, sem, tm)

    @pl.when(j == 0)
    def _():
        x_copy.wait()
        h_ref[...] = _rms(xs_ref[...], g_ref[...]).astype(BF16)
        acc_ref[...] = _dot(h_ref[...], w_ref[...])
        _SideCast.run(side_in, side_out)

    def fused(cond, epi):
        @pl.when(cond & (j >= 1) & (j < nj))
        def _():
            acc = acc_ref[...]
            epi(acc)
            acc_ref[...] = _dot(h_ref[...], w_ref[...])
            _SideCast.run(side_in, side_out)

    fused(jp < 2 * tps, epi_qk)
    fused((jp >= 2 * tps) & (jp < 3 * tps), epi_v)
    fused(((jp >= 3 * tps) & (jp < 4 * tps)) | ((jp >= 6 * tps) & (jp < 7 * tps)), epi_silu)
    fused((jp >= 4 * tps) & (jp < 5 * tps), epi_forget)
    fused((jp >= 5 * tps) & (jp < 6 * tps), epi_hv)
    fused(jp >= 7 * tps, epi_gates)

    @pl.when(j == nj)
    def _():
        epi_gates(acc_ref[...])
        _SideCast.run(side_in, side_out)


def _proj(x, gain, w_in, qg, kg, lb, side_weights=(), *, batch, seq, tm, tn):
    m, d = x.shape
    n_in = w_in.shape[1]
    tps = MIX_W // tn
    hpt = tn // HEAD_W
    spb = seq // tm
    ng = 2 * d // tn
    nj = 7 * tps + ng
    assert n_in == 7 * MIX_W + 2 * d and MIX_W % tn == 0 and (2 * d) % tn == 0

    pw = math.gcd(tn, V7X_MXU_WIDTH)
    lane = np.arange(pw)
    gm = (lane[:, None] // QK_DIM == lane[None, :] // QK_DIM).astype(np.float32) / QK_DIM

    def hm_block(lo, n):
        return lambda i, j: (i // spb, jnp.clip(j - 1 - lo, 0, n - 1), i % spb, 0)

    side = _SideCast(side_weights, (m // tm) * (nj + 1), lambda i, j: i * (nj + 1) + j)
    out = pl.pallas_call(
        functools.partial(_proj_kernel, tm=tm, tn=tn, nj=nj, n_side=side.n),
        out_shape=[
            jax.ShapeDtypeStruct((batch, 3 * HEADS, seq, HEAD_W), BF16),
            jax.ShapeDtypeStruct((batch, 4 * HEADS, seq, HEAD_W), BF16),
            jax.ShapeDtypeStruct((batch, HEADS, seq, HEAD_W), F32),
            jax.ShapeDtypeStruct((m, 2 * d), BF16),
        ] + side.out_shapes,
        grid=(m // tm, nj + 1),
        in_specs=[
            pl.BlockSpec(memory_space=pl.ANY),
            pl.BlockSpec((1, d), lambda i, j: (0, 0)),
            pl.BlockSpec((d, tn), lambda i, j: (0, jnp.minimum(j, nj - 1))),
            pl.BlockSpec((1, HEAD_W), lambda i, j: (0, 0)),
            pl.BlockSpec((1, HEAD_W), lambda i, j: (0, 0)),
            pl.BlockSpec((HEADS, HEAD_W), lambda i, j: (0, 0)),
            pl.BlockSpec((pw, pw), lambda i, j: (0, 0)),
        ] + side.in_specs,
        out_specs=[
            pl.BlockSpec((1, hpt, tm, HEAD_W), hm_block(0, 3 * tps)),
            pl.BlockSpec((1, hpt, tm, HEAD_W), hm_block(3 * tps, 4 * tps)),
            pl.BlockSpec((1, hpt, tm, HEAD_W), hm_block(4 * tps, tps)),
            pl.BlockSpec((tm, tn), lambda i, j: (i, jnp.clip(j - 1 - 7 * tps, 0, ng - 1))),
        ] + side.out_specs,
        scratch_shapes=[pltpu.VMEM((tm, d), F32), pltpu.VMEM((tm, d), BF16),
                        pltpu.VMEM((tm, tn), F32), pltpu.SemaphoreType.DMA(())],
        compiler_params=_cparams(("arbitrary", "arbitrary")),
        name="proj",
    )(x, gain.reshape(1, d), w_in, qg, kg, lb, jnp.asarray(gm, BF16), *side.arrays)
    return out[0], out[1], out[2], out[3], out[4:]


def _attn_kernel(lam_ref, off_ref, q_ref, k_ref, v_ref, w_ref, sg_ref, *rest, t, nkb, qpg, hps, online,
                 n_side):
    side_in, (o_ref,), side_out = rest[:n_side], rest[n_side:n_side + 1], rest[n_side + 1:2 * n_side + 1]
    vt_ref, bias_ref = rest[2 * n_side + 1:]
    hg = pl.program_id(0)
    b = pl.program_id(1)
    i = pl.program_id(2)
    lam = lam_ref[0]

    @pl.when((b == 0) & (i == 0))
    def _():
        for hh in range(hps):
            for dd in range(2):
                rows = jnp.broadcast_to(w_ref[hh, dd:dd + 1, :], (t, 2 * t))
                bias_ref[hh, dd] = pltpu.roll(rows, 0, 1, stride=1, stride_axis=0)[:, 0:t]

    @pl.when(i == 0)
    def _():
        for hh in range(hps):
            for jb in range(nkb):
                vt_ref[hh, jb] = v_ref[0, hh, jb * t:(jb + 1) * t, :].astype(F32).T.astype(BF16)

    def colsum8(p):
        return jnp.sum(p.reshape(t // 8, 8, 2 * t), axis=0)

    def both(x):
        return jnp.concatenate([x, x], axis=1)

    def query_block(hh, iq, row0):
        far_off = off_ref[hg * hps + hh]
        q = q_ref[0, hh, row0:row0 + t, :]
        lane = lax.broadcasted_iota(jnp.int32, (t, HEAD_W), 1)
        zero = jnp.zeros_like(q)
        qs = jnp.concatenate([jnp.where(lane < QK_DIM, q, zero),
                              jnp.where(lane >= QK_DIM, q, zero)], axis=0)
        offsets = [far_off] * max(iq - 1, 0)
        if iq >= 1:
            offsets.append(both(bias_ref[hh, 1]))
        offsets.append(both(bias_ref[hh, 0]))
        m = l = acc = None
        for j, badd in enumerate(offsets):
            s = _dot_nt(k_ref[0, hh, j * t:(j + 1) * t, :], qs) + badd
            if online:
                bmax = jnp.max(s, axis=0, keepdims=True)
                m_new = bmax if m is None else jnp.maximum(m, bmax)
                p = jnp.exp2(s - m_new)
            else:
                p = jnp.exp2(s)
            psum = colsum8(p)
            pv = _dot(vt_ref[hh, j], p.astype(BF16))
            if l is None:
                l, acc = psum, pv
            elif online:
                alpha = jnp.exp2(m - m_new)
                l, acc = alpha * l + psum, alpha * acc + pv
            else:
                l, acc = l + psum, acc + pv
            if online:
                m = m_new
        l = jnp.sum(l, axis=0, keepdims=True)
        o_t = acc[:, 0:t] / l[:, 0:t] - lam * (acc[:, t:2 * t] / l[:, t:2 * t])
        o_ref[row0:row0 + t, hh * HEAD_W:(hh + 1) * HEAD_W] = (
            _rms(o_t.T, sg_ref[...]) * (1.0 - LAMBDA_INIT)).astype(BF16)

    def query_group(ig):
        for hh in range(hps):
            for u in range(qpg):
                query_block(hh, ig * qpg + u, u * t)
        _SideCast.run(side_in, side_out)

    for ig in range(nkb // qpg):
        pl.when(i == ig)(functools.partial(query_group, ig))


def _t5_bucket(dist):
    n = jnp.maximum(dist, 0)
    max_exact = REL_BUCKETS // 2
    nf = jnp.maximum(n, 1).astype(F32)
    large = max_exact + (jnp.log(nf / max_exact) / math.log(REL_MAX_DIST / max_exact)
                         * (REL_BUCKETS - max_exact)).astype(jnp.int32)
    large = jnp.minimum(large, REL_BUCKETS - 1)
    return jnp.where(n < max_exact, n, large)


def _bias_generators(f, t):
    hole = jnp.full((HEADS, 1), -jnp.inf, F32)
    w0 = jnp.concatenate([f[:, 0:t], jnp.full((HEADS, t), -jnp.inf, F32)], axis=1)
    w1 = jnp.concatenate([f[:, t:2 * t], hole, f[:, 1:t]], axis=1)
    return jnp.stack([w0, w1], axis=1)


def _attn(qkv, lam, rel_bias, qk_bound, subln, side_weights=(), *, batch, seq, t, qpg, hps):
    assert t >= REL_MAX_DIST
    nq = seq // t
    rb = rel_bias.astype(F32)
    spread = 2.0 * qk_bound + jnp.max(jnp.max(rb, axis=0) - jnp.min(rb, axis=0))
    static_ok = spread <= MAX_STATIC_SOFTMAX_SPREAD
    upper = jnp.where(static_ok, qk_bound + jnp.max(rb, axis=0), 0.0)
    f = rb[_t5_bucket(jnp.arange(2 * t))].T - upper[:, None]
    bias = _bias_generators(f * LOG2E, t)
    far_off = (rb[REL_BUCKETS - 1] - upper) * LOG2E

    assert nq % qpg == 0 and HEADS % hps == 0
    ng = nq // qpg
    hgs = HEADS // hps
    side = _SideCast(side_weights, hgs * batch * ng, lambda h, b, i: (h * batch + b) * ng + i)

    def call(online):
        out = pl.pallas_call(
            functools.partial(_attn_kernel, t=t, nkb=nq, qpg=qpg, hps=hps, online=online,
                              n_side=side.n),
            out_shape=[jax.ShapeDtypeStruct((batch * seq, MIX_W), BF16)] + side.out_shapes,
            grid=(hgs, batch, ng),
            in_specs=[
                pl.BlockSpec(memory_space=pltpu.SMEM),
                pl.BlockSpec(memory_space=pltpu.SMEM),
                pl.BlockSpec((1, hps, qpg * t, HEAD_W), lambda h, b, i: (b, h, i, 0)),
                pl.BlockSpec((1, hps, seq, HEAD_W), lambda h, b, i: (b, hgs + h, 0, 0)),
                pl.BlockSpec((1, hps, seq, HEAD_W), lambda h, b, i: (b, 2 * hgs + h, 0, 0)),
                pl.BlockSpec((hps, 2, 2 * t), lambda h, b, i: (h, 0, 0)),
                pl.BlockSpec((1, HEAD_W), lambda h, b, i: (0, 0)),
            ] + side.in_specs,
            out_specs=[pl.BlockSpec((qpg * t, hps * HEAD_W), lambda h, b, i: (b * ng + i, h))]
            + side.out_specs,
            scratch_shapes=[pltpu.VMEM((hps, nq, HEAD_W, t), BF16),
                            pltpu.VMEM((hps, 2, t, t), F32)],
            compiler_params=_cparams(("arbitrary", "arbitrary", "arbitrary")),
            name="attn_online" if online else "attn",
        )(lam.reshape(1), far_off, qkv, qkv, qkv, bias, subln.reshape(1, HEAD_W), *side.arrays)
        return out[0], tuple(out[1:])

    return lax.cond(static_ok, lambda: call(False), lambda: call(True))


SUBLANES = 8


def _hgrn_tables(c):
    levels = int(math.log2(c))
    assert 2 ** levels == c
    t = np.arange(c)[:, None]
    s = np.arange(c)[None, :]
    blocks = [(s <= t).astype(np.float32)]
    level_id = np.where(t == s, 0, -1)
    n_fine = 0
    for l in range(1, levels + 1):
        g, half = 2 ** l, 2 ** (l - 1)
        mid = (t // g) * g + half - 1
        upper = (t % g) >= half
        if half < SUBLANES:
            above = upper & (s > mid) & (s <= t)
            below = (~upper) & (s > t) & (s <= mid)
            blocks.append((above | below).astype(np.float32))
            n_fine += 1
        pair = (t // g == s // g) & upper & ((s % g) < half)
        level_id = np.where(pair, l, level_id)
    w = np.concatenate(blocks, axis=0)
    return jnp.asarray(w, BF16), jnp.asarray(level_id, jnp.int32), levels, n_fine


def _hgrn_kernel(w_ref, lvl_ref, q_ref, k_ref, v_ref, og_ref, lf_ref, gn_ref, o_ref, *,
                 c, levels, n_fine, seq, unroll, hpg):
    gain = gn_ref[...]

    def cumsums(hh, ci):
        rows = pl.ds(pl.multiple_of(ci * c, c), c)
        g = lf_ref[0, hh, rows, :]
        g_hi = g.astype(BF16)
        g_lo = (g - g_hi.astype(F32)).astype(BF16)
        e2 = _dot(w_ref[...], jnp.concatenate([g_hi, g_lo], axis=1))
        return hh, rows, e2[:, 0:HEAD_W] + e2[:, HEAD_W:2 * HEAD_W]

    def scores(hh, rows, e):
        qb = q_ref[0, hh, rows, :]
        kb = k_ref[0, hh, rows, :]
        q = qb.astype(F32)
        k = kb.astype(F32)
        b = e[0:c]
        lvl = lvl_ref[...]
        pairs = [(qb, kb)]
        for l in range(1, levels + 1):
            if l <= n_fine:
                d = e[l * c:(l + 1) * c]
            else:
                g, half = 2 ** l, 2 ** (l - 1)
                b3 = b.reshape(c // g, g, HEAD_W)
                mid = b3[:, half - 1:half, :]
                d = jnp.concatenate([mid - b3[:, 0:half, :], b3[:, half:g, :] - mid],
                                    axis=1).reshape(c, HEAD_W)
            x = jnp.exp2(d)
            pairs.append(((q * x).astype(BF16), (k * x).astype(BF16)))
        a = jnp.zeros((c, c), F32)
        zero = jnp.zeros((c, HEAD_W), BF16)
        for l in range(0, levels + 1, 2):
            (qa, ka), (qc, kc) = pairs[l], pairs[l + 1]
            keys = jnp.concatenate([jnp.concatenate([ka, zero], axis=1),
                                    jnp.concatenate([zero, kc], axis=1)], axis=0)
            r = _dot_nt(jnp.concatenate([qa, qc], axis=1), keys)
            a = jnp.where(lvl == l, r[:, 0:c], a)
            a = jnp.where(lvl == l + 1, r[:, c:2 * c], a)
        b_last = b[c - 1:c, :]
        q_in = (q * jnp.exp2(b)).astype(BF16)
        k_out = (k * jnp.exp2(b_last - b)).astype(BF16)
        return a.astype(BF16), q_in, k_out, jnp.exp2(b_last)

    def outputs(hh, rows, a, q_in, k_out, decay):
        v = v_ref[0, hh, rows, :]
        return hh, rows, _dot(a, v), q_in, _dot_tn(v, k_out), decay

    def body(it, states):
        units = [(hh, it * unroll + u) for u in range(unroll) for hh in range(hpg)]
        st1, st2, st3 = {}, {}, []
        for n in range(len(units) + 2):
            if n < len(units):
                st1[n] = cumsums(*units[n])
            if 0 <= n - 1 < len(units):
                st2[n - 1] = scores(*st1[n - 1])
            if 0 <= n - 2 < len(units):
                st3.append(outputs(*st1[n - 2][0:2], *st2[n - 2]))
        states = list(states)
        for hh, rows, o_intra, q_in, ds_t, decay in st3:
            o = o_intra + _dot_nt(q_in, states[hh].astype(BF16))
            states[hh] = states[hh] * decay + ds_t
            y = _rms(o, gain) * og_ref[0, hh, rows, :].astype(F32)
            o_ref[rows, hh * HEAD_W:(hh + 1) * HEAD_W] = y.astype(BF16)
        return tuple(states)

    lax.fori_loop(0, seq // (c * unroll), body,
                  tuple(jnp.zeros((HEAD_W, HEAD_W), F32) for _ in range(hpg)))


def _hgrn(hin, logf, gain, *, batch, seq, c, unroll, hpg):
    w, lvl, levels, n_fine = _hgrn_tables(c)
    assert seq % (c * unroll) == 0 and HEADS % hpg == 0
    groups = HEADS // hpg

    def head(slab):
        return pl.BlockSpec((1, hpg, seq, HEAD_W), lambda b, h: (b, slab * groups + h, 0, 0))

    return pl.pallas_call(
        functools.partial(_hgrn_kernel, c=c, levels=levels, n_fine=n_fine, seq=seq, unroll=unroll,
                          hpg=hpg),
        out_shape=jax.ShapeDtypeStruct((batch * seq, MIX_W), BF16),
        grid=(batch, groups),
        in_specs=[
            pl.BlockSpec(w.shape, lambda b, h: (0, 0)),
            pl.BlockSpec((c, c), lambda b, h: (0, 0)),
            head(0), head(1), head(2), head(3), head(0),
            pl.BlockSpec((1, HEAD_W), lambda b, h: (0, 0)),
        ],
        out_specs=pl.BlockSpec((seq, hpg * HEAD_W), lambda b, h: (b, h)),
        compiler_params=_cparams(("parallel", "parallel")),
        name="hgrn",
    )(w, lvl, hin, hin, hin, hin, logf, gain.reshape(1, HEAD_W))


def _merge_kernel(x_ref, ya_ref, yb_ref, gt_ref, wa_ref, wb_ref, wo_ref, *rest, n_side):
    side_in, (o_ref,), side_out = rest[:n_side], rest[n_side:n_side + 1], rest[n_side + 1:]
    d = x_ref.shape[1]
    gt = _sigmoid(gt_ref[...].astype(F32))
    merged = gt[:, 0:d] * _dot(ya_ref[...], wa_ref[...]) + gt[:, d:2 * d] * _dot(yb_ref[...], wb_ref[...])
    o_ref[...] = x_ref[...] + _dot(merged.astype(BF16), wo_ref[...])
    _SideCast.run(side_in, side_out)


def _resident(shape):
    return pl.BlockSpec(shape, lambda i: (0,) * len(shape), pipeline_mode=pl.Buffered(1))


def _merge(x, ya, yb, gates, wa, wb, wo, side_weights=(), *, tm):
    m, d = x.shape
    side = _SideCast(side_weights, m // tm, lambda i: i)
    out = pl.pallas_call(
        functools.partial(_merge_kernel, n_side=side.n),
        out_shape=[jax.ShapeDtypeStruct((m, d), F32)] + side.out_shapes,
        grid=(m // tm,),
        in_specs=[
            pl.BlockSpec((tm, d), lambda i: (i, 0)),
            pl.BlockSpec((tm, MIX_W), lambda i: (i, 0)),
            pl.BlockSpec((tm, MIX_W), lambda i: (i, 0)),
            pl.BlockSpec((tm, 2 * d), lambda i: (i, 0)),
            _resident(wa.shape), _resident(wb.shape), _resident(wo.shape),
        ] + side.in_specs,
        out_specs=[pl.BlockSpec((tm, d), lambda i: (i, 0))] + side.out_specs,
        compiler_params=_cparams(("arbitrary",)),
        name="merge",
    )(x, ya, yb, gates, wa, wb, wo, *side.arrays)
    return out[0], out[1:]


def _ple_kernel(x_ref, p_ref, gg_ref, pg_ref, wg_ref, wp_ref, o_ref):
    ple = _rms(_dot(p_ref[...].astype(BF16), wp_ref[...]), pg_ref[...])
    x = x_ref[...]
    gate = _sigmoid(_dot(_rms(x, gg_ref[...]).astype(BF16), wg_ref[...]))
    o_ref[...] = x + gate * ple


def _ple(x, p, gate_gain, post_gain, w_gate, w_proj, *, tm):
    m, d = x.shape
    pd = p.shape[1]
    return pl.pallas_call(
        _ple_kernel,
        out_shape=jax.ShapeDtypeStruct((m, d), F32),
        grid=(m // tm,),
        in_specs=[
            pl.BlockSpec((tm, d), lambda i: (i, 0)),
            pl.BlockSpec((tm, pd), lambda i: (i, 0)),
            pl.BlockSpec((1, d), lambda i: (0, 0)),
            pl.BlockSpec((1, d), lambda i: (0, 0)),
            _resident(w_gate.shape), _resident(w_proj.shape),
        ],
        out_specs=pl.BlockSpec((tm, d), lambda i: (i, 0)),
        compiler_params=_cparams(("parallel",)),
        name="ple",
    )(x, p, gate_gain.reshape(1, d), post_gain.reshape(1, d), w_gate, w_proj)


def _tiles(seq, d, f):
    def pick(n, choices):
        return next(c for c in choices if n % c == 0)
    return dict(
        tm=pick(seq, (512, 256, 128)),
        tfm=pick(seq, (1024, 512, 256, 128)),
        tf=pick(f, (512, 256, 128)),
        tp=pick(seq, (1024, 512, 256, 128)),
        tn=pick(math.gcd(MIX_W, 2 * d), (1024, 512, 256, 128)),
        ta=pick(seq, (512, 256, 128)),
        qpg=4,
        hps=2,
        c=128,
        cu=16,
        hpg=2,
    )


def kernel(x, p, ffn1_norm, ffn1_w_gate, ffn1_w_up, ffn1_w_down, mix_norm, w_in, q_norm, k_norm, lambda_q1, lambda_k1, lambda_q2, lambda_k2, diff_subln, rel_bias, hgrn_lb_logits, hgrn_norm, w_branch_a, w_branch_b, w_out, ffn2_norm, ffn2_w_gate, ffn2_w_up, ffn2_w_down, ple_gate_norm, w_ple_gate, w_ple_proj, ple_post_norm):
    batch, seq, d = x.shape
    depth = ffn1_norm.shape[0]
    assert depth == 1
    m = batch * seq
    tl = _tiles(seq, d, ffn1_w_gate.shape[2])
    bf = lambda w: w.astype(BF16)

    lower_bounds = jnp.cumsum(jax.nn.softmax(hgrn_lb_logits.astype(F32), axis=0), axis=0)
    lb = lower_bounds[0].reshape(HEADS, HEAD_W)
    lam = (jnp.exp(jnp.sum(lambda_q1[0].astype(F32) * lambda_k1[0].astype(F32)))
           - jnp.exp(jnp.sum(lambda_q2[0].astype(F32) * lambda_k2[0].astype(F32)))
           + LAMBDA_INIT)
    scale = QK_DIM ** -0.5
    qg = jnp.tile(q_norm[0].astype(F32), 2).reshape(1, HEAD_W) * (scale * LOG2E)
    kg = jnp.tile(k_norm[0].astype(F32), 2).reshape(1, HEAD_W)
    qk_bound = (QK_DIM * scale * jnp.max(jnp.abs(q_norm[0].astype(F32)))
                * jnp.max(jnp.abs(k_norm[0].astype(F32))))

    x2 = x.reshape(m, d)
    x2, (w_in_b, wa_b, wb_b, wo_b) = _ffn(
        x2, ffn1_norm[0], bf(ffn1_w_gate[0]), bf(ffn1_w_up[0]), bf(ffn1_w_down[0]),
        (w_in[0], w_branch_a[0], w_branch_b[0], w_out[0]), tm=tl["tfm"], tf=tl["tf"])

    qkv, hin, logf, gates, (wg2_b, wu2_b, wd2_b) = _proj(
        x2, mix_norm[0], w_in_b, qg, kg, lb, (ffn2_w_gate[0], ffn2_w_up[0], ffn2_w_down[0]),
        batch=batch, seq=seq, tm=tl["tp"], tn=tl["tn"])
    ya, _ = _attn(qkv, lam, rel_bias, qk_bound, diff_subln[0], batch=batch, seq=seq,
                  t=tl["ta"], qpg=tl["qpg"], hps=tl["hps"])
    yb = _hgrn(hin, logf, hgrn_norm[0], batch=batch, seq=seq, c=tl["c"], unroll=tl["cu"],
               hpg=tl["hpg"])
    x2, (wpg_b, wpp_b) = _merge(x2, ya, yb, gates, wa_b, wb_b, wo_b,
                                (w_ple_gate[0], w_ple_proj[0]), tm=tl["tm"])

    x2, _ = _ffn(x2, ffn2_norm[0], wg2_b, wu2_b, wd2_b, tm=tl["tfm"], tf=tl["tf"])
    x2 = _ple(x2, p[0].reshape(m, -1), ple_gate_norm[0], ple_post_norm[0], wpg_b, wpp_b, tm=tl["tm"])
    return x2.reshape(batch, seq, d)
```

```python
import functools
import math

import jax
import jax.numpy as jnp
import numpy as np
from jax import lax
from jax.experimental import pallas as pl
from jax.experimental.pallas import tpu as pltpu

F32 = jnp.float32
BF16 = jnp.bfloat16

EPS = 1e-6
HEADS = 8
HEAD_W = 128
QK_DIM = 64
MIX_W = HEADS * HEAD_W
REL_BUCKETS = 32
REL_MAX_DIST = 128
LAMBDA_INIT = 0.8 - 0.6 * math.exp(-0.3 * 0)
LOG2E = math.log2(math.e)
MAX_STATIC_SOFTMAX_SPREAD = 60.0

V7X_MXU_WIDTH = 256
V7X_VMEM_BYTES = 64 * 1024 * 1024
VMEM_LIMIT = V7X_VMEM_BYTES * 7 // 8


def _cparams(sem):
    return pltpu.CompilerParams(dimension_semantics=sem, vmem_limit_bytes=VMEM_LIMIT)


def _sigmoid(x):
    return 1.0 / (1.0 + jnp.exp(-x))


def _rms(x, gain):
    ms = jnp.mean(x * x, axis=-1, keepdims=True)
    return x * lax.rsqrt(ms + EPS) * gain


def _dot(a, b):
    return jnp.dot(a, b, preferred_element_type=F32)


def _dot_nt(a, b):
    return lax.dot_general(a, b, (((1,), (1,)), ((), ())), preferred_element_type=F32)


def _dot_tn(a, b):
    return lax.dot_general(a, b, (((0,), (0,)), ((), ())), preferred_element_type=F32)


BF16_ROW_TILE = 16


class _SideCast:
    def __init__(self, arrays, steps, step_index):
        self.arrays = list(arrays)
        self.n = len(self.arrays)
        self.step_index = step_index
        self.chunks = []
        for w in self.arrays:
            rows = w.shape[0]
            self.chunks.append(max(c for c in range(1, steps + 1)
                                   if rows % c == 0 and (rows // c) % BF16_ROW_TILE == 0))

    def _specs(self):
        def spec(w, c):
            return pl.BlockSpec((w.shape[0] // c, w.shape[1]),
                                lambda *ids: (jnp.minimum(self.step_index(*ids), c - 1), 0))
        return [spec(w, c) for w, c in zip(self.arrays, self.chunks)]

    in_specs = property(_specs)
    out_specs = property(_specs)

    @property
    def out_shapes(self):
        return [jax.ShapeDtypeStruct(w.shape, BF16) for w in self.arrays]

    @staticmethod
    def run(in_refs, out_refs):
        for src, dst in zip(in_refs, out_refs):
            dst[...] = src[...].astype(BF16)


def _row_tile_fetch(x_hbm, xs_ref, sem, tm):
    i = pl.program_id(0)
    j = pl.program_id(1)

    def copy(row_tile):
        return pltpu.make_async_copy(x_hbm.at[pl.ds(row_tile * tm, tm), :], xs_ref, sem)

    @pl.when((i == 0) & (j == 0))
    def _():
        copy(0).start()

    @pl.when((j == 1) & (i + 1 < pl.num_programs(0)))
    def _():
        copy(i + 1).start()

    return copy(i)


def _ffn_kernel(x_hbm, g_ref, wg_ref, wu_ref, wd_ref, *rest, tm, n_side):
    side_in, (o_ref,), side_out = rest[:n_side], rest[n_side:n_side + 1], rest[n_side + 1:2 * n_side + 1]
    xs_ref, h_ref, sem = rest[2 * n_side + 1:]
    j = pl.program_id(1)
    x_copy = _row_tile_fetch(x_hbm, xs_ref, sem, tm)

    @pl.when(j == 0)
    def _():
        x_copy.wait()
        x = xs_ref[...]
        h_ref[...] = _rms(x, g_ref[...]).astype(BF16)
        o_ref[...] = x

    h = h_ref[...]
    gate = _dot(h, wg_ref[...])
    up = _dot(h, wu_ref[...])
    act = (gate * _sigmoid(gate) * (0.5 * up)).astype(BF16)
    o_ref[...] += _dot(act, wd_ref[...])
    _SideCast.run(side_in, side_out)


def _ffn(x, gain, w_gate, w_up, w_down, side_weights=(), *, tm, tf):
    m, d = x.shape
    f = w_gate.shape[1]
    nj = f // tf
    assert nj >= 2
    side = _SideCast(side_weights, (m // tm) * nj, lambda i, j: i * nj + j)
    out = pl.pallas_call(
        functools.partial(_ffn_kernel, tm=tm, n_side=side.n),
        out_shape=[jax.ShapeDtypeStruct((m, d), F32)] + side.out_shapes,
        grid=(m // tm, nj),
        in_specs=[
            pl.BlockSpec(memory_space=pl.ANY),
            pl.BlockSpec((1, d), lambda i, j: (0, 0)),
            pl.BlockSpec((d, tf), lambda i, j: (0, j)),
            pl.BlockSpec((d, tf), lambda i, j: (0, j)),
            pl.BlockSpec((tf, d), lambda i, j: (j, 0)),
        ] + side.in_specs,
        out_specs=[pl.BlockSpec((tm, d), lambda i, j: (i, 0))] + side.out_specs,
        scratch_shapes=[pltpu.VMEM((tm, d), F32), pltpu.VMEM((tm, d), BF16),
                        pltpu.SemaphoreType.DMA(())],
        compiler_params=_cparams(("arbitrary", "arbitrary")),
        name="ffn",
    )(x, gain.reshape(1, d), w_gate, w_up, w_down, *side.arrays)
    return out[0], out[1:]


def _proj_kernel(x_hbm, g_ref, w_ref, qg_ref, kg_ref, lb_ref, gm_ref, *rest, tm, tn, nj, n_side):
    side_in, rest = rest[:n_side], rest[n_side:]
    (oqkv_ref, oh_ref, olf_ref, og_ref), rest = rest[:4], rest[4:]
    side_out, (xs_ref, h_ref, acc_ref, sem) = rest[:n_side], rest[n_side:]
    j = pl.program_id(1)
    jp = j - 1
    tps = MIX_W // tn
    hpt = tn // HEAD_W

    def heads(vals):
        return [vals[:, s * HEAD_W:(s + 1) * HEAD_W] for s in range(hpt)]

    def epi_qk(acc):
        gain = jnp.where(jp < tps, qg_ref[...], kg_ref[...])
        pw = gm_ref.shape[0]
        for s in range(tn // pw):
            a = acc[:, s * pw:(s + 1) * pw]
            ms = _dot((a * a).astype(BF16), gm_ref[...])
            y = a * lax.rsqrt(ms + EPS)
            for u in range(pw // HEAD_W):
                oqkv_ref[0, s * (pw // HEAD_W) + u] = (
                    y[:, u * HEAD_W:(u + 1) * HEAD_W] * gain).astype(BF16)

    def epi_v(acc):
        for s, a in enumerate(heads(acc)):
            oqkv_ref[0, s] = a.astype(BF16)

    def epi_silu(acc):
        for s, a in enumerate(heads(acc * _sigmoid(acc))):
            oh_ref[0, s] = a.astype(BF16)

    def epi_forget(acc):
        jj = jp - 4 * tps
        for s, a in enumerate(heads(acc)):
            lb = lb_ref[pl.ds(jj * hpt + s, 1), :]
            forget = lb + (1.0 - lb) * _sigmoid(a)
            oh_ref[0, s] = (1.0 - forget).astype(BF16)
            olf_ref[0, s] = jnp.log2(forget)

    def epi_hv(acc):
        for s, a in enumerate(heads(acc)):
            oh_ref[0, s] = a.astype(BF16)

    def epi_gates(acc):
        og_ref[...] = acc.astype(BF16)

    x_copy = _row_tile_fetch(x_hbm, xs_ref, sem, tm)

    @pl.when(j == 0)
    def _():
        x_copy.wait()
        h_ref[...] = _rms(xs_ref[...], g_ref[...]).astype(BF16)
        acc_ref[...] = _dot(h_ref[...], w_ref[...])
        _SideCast.run(side_in, side_out)

    def fused(cond, epi):
        @pl.when(cond & (j >= 1) & (j < nj))
        def _():
            acc = acc_ref[...]
            epi(acc)
            acc_ref[...] = _dot(h_ref[...], w_ref[...])
            _SideCast.run(side_in, side_out)

    fused(jp < 2 * tps, epi_qk)
    fused((jp >= 2 * tps) & (jp < 3 * tps), epi_v)
    fused(((jp >= 3 * tps) & (jp < 4 * tps)) | ((jp >= 6 * tps) & (jp < 7 * tps)), epi_silu)
    fused((jp >= 4 * tps) & (jp < 5 * tps), epi_forget)
    fused((jp >= 5 * tps) & (jp < 6 * tps), epi_hv)
    fused(jp >= 7 * tps, epi_gates)

    @pl.when(j == nj)
    def _():
        epi_gates(acc_ref[...])
        _SideCast.run(side_in, side_out)


def _proj(x, gain, w_in, qg, kg, lb, side_weights=(), *, batch, seq, tm, tn):
    m, d = x.shape
    n_in = w_in.shape[1]
    tps = MIX_W // tn
    hpt = tn // HEAD_W
    spb = seq // tm
    ng = 2 * d // tn
    nj = 7 * tps + ng
    assert n_in == 7 * MIX_W + 2 * d and MIX_W % tn == 0 and (2 * d) % tn == 0

    pw = math.gcd(tn, V7X_MXU_WIDTH)
    lane = np.arange(pw)
    gm = (lane[:, None] // QK_DIM == lane[None, :] // QK_DIM).astype(np.float32) / QK_DIM

    def hm_block(lo, n):
        return lambda i, j: (i // spb, jnp.clip(j - 1 - lo, 0, n - 1), i % spb, 0)

    side = _SideCast(side_weights, (m // tm) * (nj + 1), lambda i, j: i * (nj + 1) + j)
    out = pl.pallas_call(
        functools.partial(_proj_kernel, tm=tm, tn=tn, nj=nj, n_side=side.n),
        out_shape=[
            jax.ShapeDtypeStruct((batch, 3 * HEADS, seq, HEAD_W), BF16),
            jax.ShapeDtypeStruct((batch, 4 * HEADS, seq, HEAD_W), BF16),
            jax.ShapeDtypeStruct((batch, HEADS, seq, HEAD_W), F32),
            jax.ShapeDtypeStruct((m, 2 * d), BF16),
        ] + side.out_shapes,
        grid=(m // tm, nj + 1),
        in_specs=[
            pl.BlockSpec(memory_space=pl.ANY),
            pl.BlockSpec((1, d), lambda i, j: (0, 0)),
            pl.BlockSpec((d, tn), lambda i, j: (0, jnp.minimum(j, nj - 1))),
            pl.BlockSpec((1, HEAD_W), lambda i, j: (0, 0)),
            pl.BlockSpec((1, HEAD_W), lambda i, j: (0, 0)),
            pl.BlockSpec((HEADS, HEAD_W), lambda i, j: (0, 0)),
            pl.BlockSpec((pw, pw), lambda i, j: (0, 0)),
        ] + side.in_specs,
        out_specs=[
            pl.BlockSpec((1, hpt, tm, HEAD_W), hm_block(0, 3 * tps)),
            pl.BlockSpec((1, hpt, tm, HEAD_W), hm_block(3 * tps, 4 * tps)),
            pl.BlockSpec((1, hpt, tm, HEAD_W), hm_block(4 * tps, tps)),
            pl.BlockSpec((tm, tn), lambda i, j: (i, jnp.clip(j - 1 - 7 * tps, 0, ng - 1))),
        ] + side.out_specs,
        scratch_shapes=[pltpu.VMEM((tm, d), F32), pltpu.VMEM((tm, d), BF16),
                        pltpu.VMEM((tm, tn), F32), pltpu.SemaphoreType.DMA(())],
        compiler_params=_cparams(("arbitrary", "arbitrary")),
        name="proj",
    )(x, gain.reshape(1, d), w_in, qg, kg, lb, jnp.asarray(gm, BF16), *side.arrays)
    return out[0], out[1], out[2], out[3], out[4:]


def _attn_kernel(lam_ref, off_ref, q_ref, k_ref, v_ref, w_ref, sg_ref, o_ref, vt_ref, bias_ref, *,
                 t, nkb, qpg, hps, online):
    hg = pl.program_id(0)
    b = pl.program_id(1)
    i = pl.program_id(2)
    lam = lam_ref[0]

    @pl.when((b == 0) & (i == 0))
    def _():
        for hh in range(hps):
            for dd in range(2):
                rows = jnp.broadcast_to(w_ref[hh, dd:dd + 1, :], (t, 2 * t))
                bias_ref[hh, dd] = pltpu.roll(rows, 0, 1, stride=1, stride_axis=0)[:, 0:t]

    @pl.when(i == 0)
    def _():
        for hh in range(hps):
            for jb in range(nkb):
                vt_ref[hh, jb] = v_ref[0, hh, jb * t:(jb + 1) * t, :].astype(F32).T.astype(BF16)

    def colsum8(p):
        return jnp.sum(p.reshape(t // 8, 8, 2 * t), axis=0)

    def both(x):
        return jnp.concatenate([x, x], axis=1)

    def query_block(hh, iq, row0):
        far_off = off_ref[hg * hps + hh]
        q = q_ref[0, hh, row0:row0 + t, :]
        lane = lax.broadcasted_iota(jnp.int32, (t, HEAD_W), 1)
        zero = jnp.zeros_like(q)
        qs = jnp.concatenate([jnp.where(lane < QK_DIM, q, zero),
                              jnp.where(lane >= QK_DIM, q, zero)], axis=0)
        offsets = [far_off] * max(iq - 1, 0)
        if iq >= 1:
            offsets.append(both(bias_ref[hh, 1]))
        offsets.append(both(bias_ref[hh, 0]))
        m = l = acc = None
        for j, badd in enumerate(offsets):
            s = _dot_nt(k_ref[0, hh, j * t:(j + 1) * t, :], qs) + badd
            if online:
                bmax = jnp.max(s, axis=0, keepdims=True)
                m_new = bmax if m is None else jnp.maximum(m, bmax)
                p = jnp.exp2(s - m_new)
            else:
                p = jnp.exp2(s)
            psum = colsum8(p)
            pv = _dot(vt_ref[hh, j], p.astype(BF16))
            if l is None:
                l, acc = psum, pv
            elif online:
                alpha = jnp.exp2(m - m_new)
                l, acc = alpha * l + psum, alpha * acc + pv
            else:
                l, acc = l + psum, acc + pv
            if online:
                m = m_new
        l = jnp.sum(l, axis=0, keepdims=True)
        o_t = acc[:, 0:t] / l[:, 0:t] - lam * (acc[:, t:2 * t] / l[:, t:2 * t])
        o_ref[row0:row0 + t, hh * HEAD_W:(hh + 1) * HEAD_W] = (
            _rms(o_t.T, sg_ref[...]) * (1.0 - LAMBDA_INIT)).astype(BF16)

    def query_group(ig):
        for hh in range(hps):
            for u in range(qpg):
                query_block(hh, ig * qpg + u, u * t)

    for ig in range(nkb // qpg):
        pl.when(i == ig)(functools.partial(query_group, ig))


def _t5_bucket(dist):
    n = jnp.maximum(dist, 0)
    max_exact = REL_BUCKETS // 2
    nf = jnp.maximum(n, 1).astype(F32)
    large = max_exact + (jnp.log(nf / max_exact) / math.log(REL_MAX_DIST / max_exact)
                         * (REL_BUCKETS - max_exact)).astype(jnp.int32)
    large = jnp.minimum(large, REL_BUCKETS - 1)
    return jnp.where(n < max_exact, n, large)


def _bias_generators(f, t):
    hole = jnp.full((HEADS, 1), -jnp.inf, F32)
    w0 = jnp.concatenate([f[:, 0:t], jnp.full((HEADS, t), -jnp.inf, F32)], axis=1)
    w1 = jnp.concatenate([f[:, t:2 * t], hole, f[:, 1:t]], axis=1)
    return jnp.stack([w0, w1], axis=1)


def _attn(qkv, lam, rel_bias, qk_bound, subln, *, batch, seq, t, qpg, hps):
    assert t >= REL_MAX_DIST
    nq = seq // t
    rb = rel_bias.astype(F32)
    spread = 2.0 * qk_bound + jnp.max(jnp.max(rb, axis=0) - jnp.min(rb, axis=0))
    static_ok = spread <= MAX_STATIC_SOFTMAX_SPREAD
    upper = jnp.where(static_ok, qk_bound + jnp.max(rb, axis=0), 0.0)
    f = rb[_t5_bucket(jnp.arange(2 * t))].T - upper[:, None]
    bias = _bias_generators(f * LOG2E, t)
    far_off = (rb[REL_BUCKETS - 1] - upper) * LOG2E

    assert nq % qpg == 0 and HEADS % hps == 0
    ng = nq // qpg
    hgs = HEADS // hps

    def call(online):
        return pl.pallas_call(
            functools.partial(_attn_kernel, t=t, nkb=nq, qpg=qpg, hps=hps, online=online),
            out_shape=jax.ShapeDtypeStruct((batch * seq, MIX_W), BF16),
            grid=(hgs, batch, ng),
            in_specs=[
                pl.BlockSpec(memory_space=pltpu.SMEM),
                pl.BlockSpec(memory_space=pltpu.SMEM),
                pl.BlockSpec((1, hps, qpg * t, HEAD_W), lambda h, b, i: (b, h, i, 0)),
                pl.BlockSpec((1, hps, seq, HEAD_W), lambda h, b, i: (b, hgs + h, 0, 0)),
                pl.BlockSpec((1, hps, seq, HEAD_W), lambda h, b, i: (b, 2 * hgs + h, 0, 0)),
                pl.BlockSpec((hps, 2, 2 * t), lambda h, b, i: (h, 0, 0)),
                pl.BlockSpec((1, HEAD_W), lambda h, b, i: (0, 0)),
            ],
            out_specs=pl.BlockSpec((qpg * t, hps * HEAD_W), lambda h, b, i: (b * ng + i, h)),
            scratch_shapes=[pltpu.VMEM((hps, nq, HEAD_W, t), BF16),
                            pltpu.VMEM((hps, 2, t, t), F32)],
            compiler_params=_cparams(("arbitrary", "arbitrary", "arbitrary")),
            name="attn_online" if online else "attn",
        )(lam.reshape(1), far_off, qkv, qkv, qkv, bias, subln.reshape(1, HEAD_W))

    return lax.cond(static_ok, lambda: call(False), lambda: call(True))


SUBLANES = 8


def _hgrn_tables(c):
    levels = int(math.log2(c))
    assert 2 ** levels == c
    t = np.arange(c)[:, None]
    s = np.arange(c)[None, :]
    blocks = [(s <= t).astype(np.float32)]
    level_id = np.where(t == s, 0, -1)
    n_fine = 0
    for l in range(1, levels + 1):
        g, half = 2 ** l, 2 ** (l - 1)
        mid = (t // g) * g + half - 1
        upper = (t % g) >= half
        if half < SUBLANES:
            above = upper & (s > mid) & (s <= t)
            below = (~upper) & (s > t) & (s <= mid)
            blocks.append((above | below).astype(np.float32))
            n_fine += 1
        pair = (t // g == s // g) & upper & ((s % g) < half)
        level_id = np.where(pair, l, level_id)
    w = np.concatenate(blocks, axis=0)
    return jnp.asarray(w, BF16), jnp.asarray(level_id, jnp.int32), levels, n_fine


def _hgrn_kernel(w_ref, lvl_ref, q_ref, k_ref, v_ref, og_ref, lf_ref, gn_ref, o_ref, *,
                 c, levels, n_fine, seq, unroll, hpg):
    gain = gn_ref[...]

    def cumsums(hh, ci):
        rows = pl.ds(pl.multiple_of(ci * c, c), c)
        g = lf_ref[0, hh, rows, :]
        g_hi = g.astype(BF16)
        g_lo = (g - g_hi.astype(F32)).astype(BF16)
        e2 = _dot(w_ref[...], jnp.concatenate([g_hi, g_lo], axis=1))
        return hh, rows, e2[:, 0:HEAD_W] + e2[:, HEAD_W:2 * HEAD_W]

    def scores(hh, rows, e):
        qb = q_ref[0, hh, rows, :]
        kb = k_ref[0, hh, rows, :]
        q = qb.astype(F32)
        k = kb.astype(F32)
        b = e[0:c]
        lvl = lvl_ref[...]
        pairs = [(qb, kb)]
        for l in range(1, levels + 1):
            if l <= n_fine:
                d = e[l * c:(l + 1) * c]
            else:
                g, half = 2 ** l, 2 ** (l - 1)
                b3 = b.reshape(c // g, g, HEAD_W)
                mid = b3[:, half - 1:half, :]
                d = jnp.concatenate([mid - b3[:, 0:half, :], b3[:, half:g, :] - mid],
                                    axis=1).reshape(c, HEAD_W)
            x = jnp.exp2(d)
            pairs.append(((q * x).astype(BF16), (k * x).astype(BF16)))
        a = jnp.zeros((c, c), F32)
        zero = jnp.zeros((c, HEAD_W), BF16)
        for l in range(0, levels + 1, 2):
            (qa, ka), (qc, kc) = pairs[l], pairs[l + 1]
            keys = jnp.concatenate([jnp.concatenate([ka, zero], axis=1),
                                    jnp.concatenate([zero, kc], axis=1)], axis=0)
            r = _dot_nt(jnp.concatenate([qa, qc], axis=1), keys)
            a = jnp.where(lvl == l, r[:, 0:c], a)
            a = jnp.where(lvl == l + 1, r[:, c:2 * c], a)
        b_last = b[c - 1:c, :]
        q_in = (q * jnp.exp2(b)).astype(BF16)
        k_out = (k * jnp.exp2(b_last - b)).astype(BF16)
        return a.astype(BF16), q_in, k_out, jnp.exp2(b_last)

    def outputs(hh, rows, a, q_in, k_out, decay):
        v = v_ref[0, hh, rows, :]
        return hh, rows, _dot(a, v), q_in, _dot_tn(v, k_out), decay

    def body(it, states):
        units = [(hh, it * unroll + u) for u in range(unroll) for hh in range(hpg)]
        st1, st2, st3 = {}, {}, []
        for n in range(len(units) + 2):
            if n < len(units):
                st1[n] = cumsums(*units[n])
            if 0 <= n - 1 < len(units):
                st2[n - 1] = scores(*st1[n - 1])
            if 0 <= n - 2 < len(units):
                st3.append(outputs(*st1[n - 2][0:2], *st2[n - 2]))
        states = list(states)
        for hh, rows, o_intra, q_in, ds_t, decay in st3:
            o = o_intra + _dot_nt(q_in, states[hh].astype(BF16))
            states[hh] = states[hh] * decay + ds_t
            y = _rms(o, gain) * og_ref[0, hh, rows, :].astype(F32)
            o_ref[rows, hh * HEAD_W:(hh + 1) * HEAD_W] = y.astype(BF16)
        return tuple(states)

    lax.fori_loop(0, seq // (c * unroll), body,
                  tuple(jnp.zeros((HEAD_W, HEAD_W), F32) for _ in range(hpg)))


def _hgrn(hin, logf, gain, *, batch, seq, c, unroll, hpg):
    w, lvl, levels, n_fine = _hgrn_tables(c)
    assert seq % (c * unroll) == 0 and HEADS % hpg == 0
    groups = HEADS // hpg

    def head(slab):
        return pl.BlockSpec((1, hpg, seq, HEAD_W), lambda b, h: (b, slab * groups + h, 0, 0))

    return pl.pallas_call(
        functools.partial(_hgrn_kernel, c=c, levels=levels, n_fine=n_fine, seq=seq, unroll=unroll,
                          hpg=hpg),
        out_shape=jax.ShapeDtypeStruct((batch * seq, MIX_W), BF16),
        grid=(batch, groups),
        in_specs=[
            pl.BlockSpec(w.shape, lambda b, h: (0, 0)),
            pl.BlockSpec((c, c), lambda b, h: (0, 0)),
            head(0), head(1), head(2), head(3), head(0),
            pl.BlockSpec((1, HEAD_W), lambda b, h: (0, 0)),
        ],
        out_specs=pl.BlockSpec((seq, hpg * HEAD_W), lambda b, h: (b, h)),
        compiler_params=_cparams(("parallel", "parallel")),
        name="hgrn",
    )(w, lvl, hin, hin, hin, hin, logf, gain.reshape(1, HEAD_W))


def _merge_kernel(x_ref, ya_ref, yb_ref, gt_ref, wa_ref, wb_ref, wo_ref, *rest, n_side):
    side_in, (o_ref,), side_out = rest[:n_side], rest[n_side:n_side + 1], rest[n_side + 1:]
    d = x_ref.shape[1]
    gt = _sigmoid(gt_ref[...].astype(F32))
    merged = gt[:, 0:d] * _dot(ya_ref[...], wa_ref[...]) + gt[:, d:2 * d] * _dot(yb_ref[...], wb_ref[...])
    o_ref[...] = x_ref[...] + _dot(merged.astype(BF16), wo_ref[...])
    _SideCast.run(side_in, side_out)


def _resident(shape):
    return pl.BlockSpec(shape, lambda i: (0,) * len(shape), pipeline_mode=pl.Buffered(1))


def _merge(x, ya, yb, gates, wa, wb, wo, side_weights=(), *, tm):
    m, d = x.shape
    side = _SideCast(side_weights, m // tm, lambda i: i)
    out = pl.pallas_call(
        functools.partial(_merge_kernel, n_side=side.n),
        out_shape=[jax.ShapeDtypeStruct((m, d), F32)] + side.out_shapes,
        grid=(m // tm,),
        in_specs=[
            pl.BlockSpec((tm, d), lambda i: (i, 0)),
            pl.BlockSpec((tm, MIX_W), lambda i: (i, 0)),
            pl.BlockSpec((tm, MIX_W), lambda i: (i, 0)),
            pl.BlockSpec((tm, 2 * d), lambda i: (i, 0)),
            _resident(wa.shape), _resident(wb.shape), _resident(wo.shape),
        ] + side.in_specs,
        out_specs=[pl.BlockSpec((tm, d), lambda i: (i, 0))] + side.out_specs,
        compiler_params=_cparams(("arbitrary",)),
        name="merge",
    )(x, ya, yb, gates, wa, wb, wo, *side.arrays)
    return out[0], out[1:]


def _ple_kernel(x_ref, p_ref, gg_ref, pg_ref, wg_ref, wp_ref, o_ref):
    ple = _rms(_dot(p_ref[...].astype(BF16), wp_ref[...]), pg_ref[...])
    x = x_ref[...]
    gate = _sigmoid(_dot(_rms(x, gg_ref[...]).astype(BF16), wg_ref[...]))
    o_ref[...] = x + gate * ple


def _ple(x, p, gate_gain, post_gain, w_gate, w_proj, *, tm):
    m, d = x.shape
    pd = p.shape[1]
    return pl.pallas_call(
        _ple_kernel,
        out_shape=jax.ShapeDtypeStruct((m, d), F32),
        grid=(m // tm,),
        in_specs=[
            pl.BlockSpec((tm, d), lambda i: (i, 0)),
            pl.BlockSpec((tm, pd), lambda i: (i, 0)),
            pl.BlockSpec((1, d), lambda i: (0, 0)),
            pl.BlockSpec((1, d), lambda i: (0, 0)),
            _resident(w_gate.shape), _resident(w_proj.shape),
        ],
        out_specs=pl.BlockSpec((tm, d), lambda i: (i, 0)),
        compiler_params=_cparams(("parallel",)),
        name="ple",
    )(x, p, gate_gain.reshape(1, d), post_gain.reshape(1, d), w_gate, w_proj)


def _tiles(seq, d, f):
    def pick(n, choices):
        return next(c for c in choices if n % c == 0)
    return dict(
        tm=pick(seq, (512, 256, 128)),
        tfm=pick(seq, (1024, 512, 256, 128)),
        tf=pick(f, (512, 256, 128)),
        tp=pick(seq, (1024, 512, 256, 128)),
        tn=pick(math.gcd(MIX_W, 2 * d), (1024, 512, 256, 128)),
        ta=pick(seq, (512, 256, 128)),
        qpg=4,
        hps=2,
        c=128,
        cu=16,
        hpg=2,
    )


def kernel(x, p, ffn1_norm, ffn1_w_gate, ffn1_w_up, ffn1_w_down, mix_norm, w_in, q_norm, k_norm, lambda_q1, lambda_k1, lambda_q2, lambda_k2, diff_subln, rel_bias, hgrn_lb_logits, hgrn_norm, w_branch_a, w_branch_b, w_out, ffn2_norm, ffn2_w_gate, ffn2_w_up, ffn2_w_down, ple_gate_norm, w_ple_gate, w_ple_proj, ple_post_norm):
    batch, seq, d = x.shape
    depth = ffn1_norm.shape[0]
    assert depth == 1
    m = batch * seq
    tl = _tiles(seq, d, ffn1_w_gate.shape[2])
    bf = lambda w: w.astype(BF16)

    lower_bounds = jnp.cumsum(jax.nn.softmax(hgrn_lb_logits.astype(F32), axis=0), axis=0)
    lb = lower_bounds[0].reshape(HEADS, HEAD_W)
    lam = (jnp.exp(jnp.sum(lambda_q1[0].astype(F32) * lambda_k1[0].astype(F32)))
           - jnp.exp(jnp.sum(lambda_q2[0].astype(F32) * lambda_k2[0].astype(F32)))
           + LAMBDA_INIT)
    scale = QK_DIM ** -0.5
    qg = jnp.tile(q_norm[0].astype(F32), 2).reshape(1, HEAD_W) * (scale * LOG2E)
    kg = jnp.tile(k_norm[0].astype(F32), 2).reshape(1, HEAD_W)
    qk_bound = (QK_DIM * scale * jnp.max(jnp.abs(q_norm[0].astype(F32)))
                * jnp.max(jnp.abs(k_norm[0].astype(F32))))

    x2 = x.reshape(m, d)
    x2, (w_in_b, wa_b, wb_b, wo_b) = _ffn(
        x2, ffn1_norm[0], bf(ffn1_w_gate[0]), bf(ffn1_w_up[0]), bf(ffn1_w_down[0]),
        (w_in[0], w_branch_a[0], w_branch_b[0], w_out[0]), tm=tl["tfm"], tf=tl["tf"])

    qkv, hin, logf, gates, (wg2_b, wu2_b, wd2_b) = _proj(
        x2, mix_norm[0], w_in_b, qg, kg, lb, (ffn2_w_gate[0], ffn2_w_up[0], ffn2_w_down[0]),
        batch=batch, seq=seq, tm=tl["tp"], tn=tl["tn"])
    ya = _attn(qkv, lam, rel_bias, qk_bound, diff_subln[0], batch=batch, seq=seq,
               t=tl["ta"], qpg=tl["qpg"], hps=tl["hps"])
    yb = _hgrn(hin, logf, hgrn_norm[0], batch=batch, seq=seq, c=tl["c"], unroll=tl["cu"],
               hpg=tl["hpg"])
    x2, (wpg_b, wpp_b) = _merge(x2, ya, yb, gates, wa_b, wb_b, wo_b,
                                (w_ple_gate[0], w_ple_proj[0]), tm=tl["tm"])

    x2, _ = _ffn(x2, ffn2_norm[0], wg2_b, wu2_b, wd2_b, tm=tl["tfm"], tf=tl["tf"])
    x2 = _ple(x2, p[0].reshape(m, -1), ple_gate_norm[0], ple_post_norm[0], wpg_b, wpp_b, tm=tl["tm"])
    return x2.reshape(batch, seq, d)
```

```python
import functools
import math

import jax
import jax.numpy as jnp
import numpy as np
from jax import lax
from jax.experimental import pallas as pl
from jax.experimental.pallas import tpu as pltpu

F32 = jnp.float32
BF16 = jnp.bfloat16

EPS = 1e-6
HEADS = 8
HEAD_W = 128
QK_DIM = 64
MIX_W = HEADS * HEAD_W
REL_BUCKETS = 32
REL_MAX_DIST = 128
LAMBDA_INIT = 0.8 - 0.6 * math.exp(-0.3 * 0)
LOG2E = math.log2(math.e)
MAX_STATIC_SOFTMAX_SPREAD = 60.0

V7X_MXU_WIDTH = 256
V7X_VMEM_BYTES = 64 * 1024 * 1024
VMEM_LIMIT = V7X_VMEM_BYTES * 7 // 8


def _cparams(sem):
    return pltpu.CompilerParams(dimension_semantics=sem, vmem_limit_bytes=VMEM_LIMIT)


def _sigmoid(x):
    return 1.0 / (1.0 + jnp.exp(-x))


def _rms(x, gain):
    ms = jnp.mean(x * x, axis=-1, keepdims=True)
    return x * lax.rsqrt(ms + EPS) * gain


def _dot(a, b):
    return jnp.dot(a, b, preferred_element_type=F32)


def _dot_nt(a, b):
    return lax.dot_general(a, b, (((1,), (1,)), ((), ())), preferred_element_type=F32)


def _dot_tn(a, b):
    return lax.dot_general(a, b, (((0,), (0,)), ((), ())), preferred_element_type=F32)


BF16_ROW_TILE = 16


class _SideCast:
    def __init__(self, arrays, steps, step_index):
        self.arrays = list(arrays)
        self.n = len(self.arrays)
        self.step_index = step_index
        self.chunks = []
        for w in self.arrays:
            rows = w.shape[0]
            self.chunks.append(max(c for c in range(1, steps + 1)
                                   if rows % c == 0 and (rows // c) % BF16_ROW_TILE == 0))

    def _specs(self):
        def spec(w, c):
            return pl.BlockSpec((w.shape[0] // c, w.shape[1]),
                                lambda *ids: (jnp.minimum(self.step_index(*ids), c - 1), 0))
        return [spec(w, c) for w, c in zip(self.arrays, self.chunks)]

    in_specs = property(_specs)
    out_specs = property(_specs)

    @property
    def out_shapes(self):
        return [jax.ShapeDtypeStruct(w.shape, BF16) for w in self.arrays]

    @staticmethod
    def run(in_refs, out_refs):
        for src, dst in zip(in_refs, out_refs):
            dst[...] = src[...].astype(BF16)


def _row_tile_fetch(x_hbm, xs_ref, sem, tm):
    i = pl.program_id(0)
    j = pl.program_id(1)

    def copy(row_tile):
        return pltpu.make_async_copy(x_hbm.at[pl.ds(row_tile * tm, tm), :], xs_ref, sem)

    @pl.when((i == 0) & (j == 0))
    def _():
        copy(0).start()

    @pl.when((j == 1) & (i + 1 < pl.num_programs(0)))
    def _():
        copy(i + 1).start()

    return copy(i)


def _ffn_kernel(x_hbm, g_ref, wg_ref, wu_ref, wd_ref, *rest, tm, n_side):
    side_in, (o_ref,), side_out = rest[:n_side], rest[n_side:n_side + 1], rest[n_side + 1:2 * n_side + 1]
    xs_ref, h_ref, sem = rest[2 * n_side + 1:]
    j = pl.program_id(1)
    x_copy = _row_tile_fetch(x_hbm, xs_ref, sem, tm)

    @pl.when(j == 0)
    def _():
        x_copy.wait()
        x = xs_ref[...]
        h_ref[...] = _rms(x, g_ref[...]).astype(BF16)
        o_ref[...] = x

    h = h_ref[...]
    gate = _dot(h, wg_ref[...])
    up = _dot(h, wu_ref[...])
    act = (gate * _sigmoid(gate) * (0.5 * up)).astype(BF16)
    o_ref[...] += _dot(act, wd_ref[...])
    _SideCast.run(side_in, side_out)


def _ffn(x, gain, w_gate, w_up, w_down, side_weights=(), *, tm, tf):
    m, d = x.shape
    f = w_gate.shape[1]
    nj = f // tf
    assert nj >= 2
    side = _SideCast(side_weights, (m // tm) * nj, lambda i, j: i * nj + j)
    out = pl.pallas_call(
        functools.partial(_ffn_kernel, tm=tm, n_side=side.n),
        out_shape=[jax.ShapeDtypeStruct((m, d), F32)] + side.out_shapes,
        grid=(m // tm, nj),
        in_specs=[
            pl.BlockSpec(memory_space=pl.ANY),
            pl.BlockSpec((1, d), lambda i, j: (0, 0)),
            pl.BlockSpec((d, tf), lambda i, j: (0, j)),
            pl.BlockSpec((d, tf), lambda i, j: (0, j)),
            pl.BlockSpec((tf, d), lambda i, j: (j, 0)),
        ] + side.in_specs,
        out_specs=[pl.BlockSpec((tm, d), lambda i, j: (i, 0))] + side.out_specs,
        scratch_shapes=[pltpu.VMEM((tm, d), F32), pltpu.VMEM((tm, d), BF16),
                        pltpu.SemaphoreType.DMA(())],
        compiler_params=_cparams(("arbitrary", "arbitrary")),
        name="ffn",
    )(x, gain.reshape(1, d), w_gate, w_up, w_down, *side.arrays)
    return out[0], out[1:]


def _proj_kernel(x_hbm, g_ref, w_ref, qg_ref, kg_ref, lb_ref, gm_ref, *rest, tm, tn, nj, n_side):
    side_in, rest = rest[:n_side], rest[n_side:]
    (oqkv_ref, oh_ref, olf_ref, og_ref), rest = rest[:4], rest[4:]
    side_out, (xs_ref, h_ref, acc_ref, sem) = rest[:n_side], rest[n_side:]
    j = pl.program_id(1)
    jp = j - 1
    tps = MIX_W // tn
    hpt = tn // HEAD_W

    def heads(vals):
        return [vals[:, s * HEAD_W:(s + 1) * HEAD_W] for s in range(hpt)]

    def epi_qk(acc):
        gain = jnp.where(jp < tps, qg_ref[...], kg_ref[...])
        pw = gm_ref.shape[0]
        for s in range(tn // pw):
            a = acc[:, s * pw:(s + 1) * pw]
            ms = _dot((a * a).astype(BF16), gm_ref[...])
            y = a * lax.rsqrt(ms + EPS)
            for u in range(pw // HEAD_W):
                oqkv_ref[0, s * (pw // HEAD_W) + u] = (
                    y[:, u * HEAD_W:(u + 1) * HEAD_W] * gain).astype(BF16)

    def epi_v(acc):
        for s, a in enumerate(heads(acc)):
            oqkv_ref[0, s] = a.astype(BF16)

    def epi_silu(acc):
        for s, a in enumerate(heads(acc * _sigmoid(acc))):
            oh_ref[0, s] = a.astype(BF16)

    def epi_forget(acc):
        jj = jp - 4 * tps
        for s, a in enumerate(heads(acc)):
            lb = lb_ref[pl.ds(jj * hpt + s, 1), :]
            forget = lb + (1.0 - lb) * _sigmoid(a)
            oh_ref[0, s] = (1.0 - forget).astype(BF16)
            olf_ref[0, s] = jnp.log2(forget)

    def epi_hv(acc):
        for s, a in enumerate(heads(acc)):
            oh_ref[0, s] = a.astype(BF16)

    def epi_gates(acc):
        og_ref[...] = _sigmoid(acc).astype(BF16)

    x_copy = _row_tile_fetch(x_hbm, xs_ref, sem, tm)

    @pl.when(j == 0)
    def _():
        x_copy.wait()
        h_ref[...] = _rms(xs_ref[...], g_ref[...]).astype(BF16)
        acc_ref[...] = _dot(h_ref[...], w_ref[...])
        _SideCast.run(side_in, side_out)

    def fused(cond, epi):
        @pl.when(cond & (j >= 1) & (j < nj))
        def _():
            acc = acc_ref[...]
            epi(acc)
            acc_ref[...] = _dot(h_ref[...], w_ref[...])
            _SideCast.run(side_in, side_out)

    fused(jp < 2 * tps, epi_qk)
    fused((jp >= 2 * tps) & (jp < 3 * tps), epi_v)
    fused(((jp >= 3 * tps) & (jp < 4 * tps)) | ((jp >= 6 * tps) & (jp < 7 * tps)), epi_silu)
    fused((jp >= 4 * tps) & (jp < 5 * tps), epi_forget)
    fused((jp >= 5 * tps) & (jp < 6 * tps), epi_hv)
    fused(jp >= 7 * tps, epi_gates)

    @pl.when(j == nj)
    def _():
        epi_gates(acc_ref[...])
        _SideCast.run(side_in, side_out)


def _proj(x, gain, w_in, qg, kg, lb, side_weights=(), *, batch, seq, tm, tn):
    m, d = x.shape
    n_in = w_in.shape[1]
    tps = MIX_W // tn
    hpt = tn // HEAD_W
    spb = seq // tm
    ng = 2 * d // tn
    nj = 7 * tps + ng
    assert n_in == 7 * MIX_W + 2 * d and MIX_W % tn == 0 and (2 * d) % tn == 0

    pw = math.gcd(tn, V7X_MXU_WIDTH)
    lane = np.arange(pw)
    gm = (lane[:, None] // QK_DIM == lane[None, :] // QK_DIM).astype(np.float32) / QK_DIM

    def hm_block(lo, n):
        return lambda i, j: (i // spb, jnp.clip(j - 1 - lo, 0, n - 1), i % spb, 0)

    side = _SideCast(side_weights, (m // tm) * (nj + 1), lambda i, j: i * (nj + 1) + j)
    out = pl.pallas_call(
        functools.partial(_proj_kernel, tm=tm, tn=tn, nj=nj, n_side=side.n),
        out_shape=[
            jax.ShapeDtypeStruct((batch, 3 * HEADS, seq, HEAD_W), BF16),
            jax.ShapeDtypeStruct((batch, 4 * HEADS, seq, HEAD_W), BF16),
            jax.ShapeDtypeStruct((batch, HEADS, seq, HEAD_W), F32),
            jax.ShapeDtypeStruct((m, 2 * d), BF16),
        ] + side.out_shapes,
        grid=(m // tm, nj + 1),
        in_specs=[
            pl.BlockSpec(memory_space=pl.ANY),
            pl.BlockSpec((1, d), lambda i, j: (0, 0)),
            pl.BlockSpec((d, tn), lambda i, j: (0, jnp.minimum(j, nj - 1))),
            pl.BlockSpec((1, HEAD_W), lambda i, j: (0, 0)),
            pl.BlockSpec((1, HEAD_W), lambda i, j: (0, 0)),
            pl.BlockSpec((HEADS, HEAD_W), lambda i, j: (0, 0)),
            pl.BlockSpec((pw, pw), lambda i, j: (0, 0)),
        ] + side.in_specs,
        out_specs=[
            pl.BlockSpec((1, hpt, tm, HEAD_W), hm_block(0, 3 * tps)),
            pl.BlockSpec((1, hpt, tm, HEAD_W), hm_block(3 * tps, 4 * tps)),
            pl.BlockSpec((1, hpt, tm, HEAD_W), hm_block(4 * tps, tps)),
            pl.BlockSpec((tm, tn), lambda i, j: (i, jnp.clip(j - 1 - 7 * tps, 0, ng - 1))),
        ] + side.out_specs,
        scratch_shapes=[pltpu.VMEM((tm, d), F32), pltpu.VMEM((tm, d), BF16),
                        pltpu.VMEM((tm, tn), F32), pltpu.SemaphoreType.DMA(())],
        compiler_params=_cparams(("arbitrary", "arbitrary")),
        name="proj",
    )(x, gain.reshape(1, d), w_in, qg, kg, lb, jnp.asarray(gm, BF16), *side.arrays)
    return out[0], out[1], out[2], out[3], out[4:]


def _attn_kernel(lam_ref, off_ref, q_ref, k_ref, v_ref, w_ref, sg_ref, o_ref, vt_ref, bias_ref, *,
                 t, nkb, qpg, hps, online):
    hg = pl.program_id(0)
    b = pl.program_id(1)
    i = pl.program_id(2)
    lam = lam_ref[0]

    @pl.when((b == 0) & (i == 0))
    def _():
        for hh in range(hps):
            for dd in range(2):
                rows = jnp.broadcast_to(w_ref[hh, dd:dd + 1, :], (t, 2 * t))
                bias_ref[hh, dd] = pltpu.roll(rows, 0, 1, stride=1, stride_axis=0)[:, 0:t]

    @pl.when(i == 0)
    def _():
        for hh in range(hps):
            for jb in range(nkb):
                vt_ref[hh, jb] = v_ref[0, hh, jb * t:(jb + 1) * t, :].astype(F32).T.astype(BF16)

    def colsum8(p):
        return jnp.sum(p.reshape(t // 8, 8, 2 * t), axis=0)

    def both(x):
        return jnp.concatenate([x, x], axis=1)

    def query_block(hh, iq, row0):
        far_off = off_ref[hg * hps + hh]
        q = q_ref[0, hh, row0:row0 + t, :]
        lane = lax.broadcasted_iota(jnp.int32, (t, HEAD_W), 1)
        zero = jnp.zeros_like(q)
        qs = jnp.concatenate([jnp.where(lane < QK_DIM, q, zero),
                              jnp.where(lane >= QK_DIM, q, zero)], axis=0)
        offsets = [far_off] * max(iq - 1, 0)
        if iq >= 1:
            offsets.append(both(bias_ref[hh, 1]))
        offsets.append(both(bias_ref[hh, 0]))
        m = l = acc = None
        for j, badd in enumerate(offsets):
            s = _dot_nt(k_ref[0, hh, j * t:(j + 1) * t, :], qs) + badd
            if online:
                bmax = jnp.max(s, axis=0, keepdims=True)
                m_new = bmax if m is None else jnp.maximum(m, bmax)
                p = jnp.exp2(s - m_new)
            else:
                p = jnp.exp2(s)
            psum = colsum8(p)
            pv = _dot(vt_ref[hh, j], p.astype(BF16))
            if l is None:
                l, acc = psum, pv
            elif online:
                alpha = jnp.exp2(m - m_new)
                l, acc = alpha * l + psum, alpha * acc + pv
            else:
                l, acc = l + psum, acc + pv
            if online:
                m = m_new
        l = jnp.sum(l, axis=0, keepdims=True)
        o_t = acc[:, 0:t] / l[:, 0:t] - lam * (acc[:, t:2 * t] / l[:, t:2 * t])
        o_ref[row0:row0 + t, hh * HEAD_W:(hh + 1) * HEAD_W] = (
            _rms(o_t.T, sg_ref[...]) * (1.0 - LAMBDA_INIT)).astype(BF16)

    def query_group(ig):
        for hh in range(hps):
            for u in range(qpg):
                query_block(hh, ig * qpg + u, u * t)

    for ig in range(nkb // qpg):
        pl.when(i == ig)(functools.partial(query_group, ig))


def _t5_bucket(dist):
    n = jnp.maximum(dist, 0)
    max_exact = REL_BUCKETS // 2
    nf = jnp.maximum(n, 1).astype(F32)
    large = max_exact + (jnp.log(nf / max_exact) / math.log(REL_MAX_DIST / max_exact)
                         * (REL_BUCKETS - max_exact)).astype(jnp.int32)
    large = jnp.minimum(large, REL_BUCKETS - 1)
    return jnp.where(n < max_exact, n, large)


def _bias_generators(f, t):
    hole = jnp.full((HEADS, 1), -jnp.inf, F32)
    w0 = jnp.concatenate([f[:, 0:t], jnp.full((HEADS, t), -jnp.inf, F32)], axis=1)
    w1 = jnp.concatenate([f[:, t:2 * t], hole, f[:, 1:t]], axis=1)
    return jnp.stack([w0, w1], axis=1)


def _attn(qkv, lam, rel_bias, qk_bound, subln, *, batch, seq, t, qpg, hps):
    assert t >= REL_MAX_DIST
    nq = seq // t
    rb = rel_bias.astype(F32)
    spread = 2.0 * qk_bound + jnp.max(jnp.max(rb, axis=0) - jnp.min(rb, axis=0))
    static_ok = spread <= MAX_STATIC_SOFTMAX_SPREAD
    upper = jnp.where(static_ok, qk_bound + jnp.max(rb, axis=0), 0.0)
    f = rb[_t5_bucket(jnp.arange(2 * t))].T - upper[:, None]
    bias = _bias_generators(f * LOG2E, t)
    far_off = (rb[REL_BUCKETS - 1] - upper) * LOG2E

    assert nq % qpg == 0 and HEADS % hps == 0
    ng = nq // qpg
    hgs = HEADS // hps

    def call(online):
        return pl.pallas_call(
            functools.partial(_attn_kernel, t=t, nkb=nq, qpg=qpg, hps=hps, online=online),
            out_shape=jax.ShapeDtypeStruct((batch * seq, MIX_W), BF16),
            grid=(hgs, batch, ng),
            in_specs=[
                pl.BlockSpec(memory_space=pltpu.SMEM),
                pl.BlockSpec(memory_space=pltpu.SMEM),
                pl.BlockSpec((1, hps, qpg * t, HEAD_W), lambda h, b, i: (b, h, i, 0)),
                pl.BlockSpec((1, hps, seq, HEAD_W), lambda h, b, i: (b, hgs + h, 0, 0)),
                pl.BlockSpec((1, hps, seq, HEAD_W), lambda h, b, i: (b, 2 * hgs + h, 0, 0)),
                pl.BlockSpec((hps, 2, 2 * t), lambda h, b, i: (h, 0, 0)),
                pl.BlockSpec((1, HEAD_W), lambda h, b, i: (0, 0)),
            ],
            out_specs=pl.BlockSpec((qpg * t, hps * HEAD_W), lambda h, b, i: (b * ng + i, h)),
            scratch_shapes=[pltpu.VMEM((hps, nq, HEAD_W, t), BF16),
                            pltpu.VMEM((hps, 2, t, t), F32)],
            compiler_params=_cparams(("arbitrary", "arbitrary", "arbitrary")),
            name="attn_online" if online else "attn",
        )(lam.reshape(1), far_off, qkv, qkv, qkv, bias, subln.reshape(1, HEAD_W))

    return lax.cond(static_ok, lambda: call(False), lambda: call(True))


SUBLANES = 8


def _hgrn_tables(c):
    levels = int(math.log2(c))
    assert 2 ** levels == c
    t = np.arange(c)[:, None]
    s = np.arange(c)[None, :]
    blocks = [(s <= t).astype(np.float32)]
    level_id = np.where(t == s, 0, -1)
    n_fine = 0
    for l in range(1, levels + 1):
        g, half = 2 ** l, 2 ** (l - 1)
        mid = (t // g) * g + half - 1
        upper = (t % g) >= half
        if half < SUBLANES:
            above = upper & (s > mid) & (s <= t)
            below = (~upper) & (s > t) & (s <= mid)
            blocks.append((above | below).astype(np.float32))
            n_fine += 1
        pair = (t // g == s // g) & upper & ((s % g) < half)
        level_id = np.where(pair, l, level_id)
    w = np.concatenate(blocks, axis=0)
    return jnp.asarray(w, BF16), jnp.asarray(level_id, jnp.int32), levels, n_fine


def _hgrn_kernel(w_ref, lvl_ref, q_ref, k_ref, v_ref, og_ref, lf_ref, gn_ref, o_ref, *,
                 c, levels, n_fine, seq, unroll, hpg):
    gain = gn_ref[...]

    def cumsums(hh, ci):
        rows = pl.ds(pl.multiple_of(ci * c, c), c)
        g = lf_ref[0, hh, rows, :]
        g_hi = g.astype(BF16)
        g_lo = (g - g_hi.astype(F32)).astype(BF16)
        e2 = _dot(w_ref[...], jnp.concatenate([g_hi, g_lo], axis=1))
        return hh, rows, e2[:, 0:HEAD_W] + e2[:, HEAD_W:2 * HEAD_W]

    def scores(hh, rows, e):
        qb = q_ref[0, hh, rows, :]
        kb = k_ref[0, hh, rows, :]
        q = qb.astype(F32)
        k = kb.astype(F32)
        b = e[0:c]
        lvl = lvl_ref[...]
        pairs = [(qb, kb)]
        for l in range(1, levels + 1):
            if l <= n_fine:
                d = e[l * c:(l + 1) * c]
            else:
                g, half = 2 ** l, 2 ** (l - 1)
                b3 = b.reshape(c // g, g, HEAD_W)
                mid = b3[:, half - 1:half, :]
                d = jnp.concatenate([mid - b3[:, 0:half, :], b3[:, half:g, :] - mid],
                                    axis=1).reshape(c, HEAD_W)
            x = jnp.exp2(d)
            pairs.append(((q * x).astype(BF16), (k * x).astype(BF16)))
        a = jnp.zeros((c, c), F32)
        zero = jnp.zeros((c, HEAD_W), BF16)
        for l in range(0, levels + 1, 2):
            (qa, ka), (qc, kc) = pairs[l], pairs[l + 1]
            keys = jnp.concatenate([jnp.concatenate([ka, zero], axis=1),
                                    jnp.concatenate([zero, kc], axis=1)], axis=0)
            r = _dot_nt(jnp.concatenate([qa, qc], axis=1), keys)
            a = jnp.where(lvl == l, r[:, 0:c], a)
            a = jnp.where(lvl == l + 1, r[:, c:2 * c], a)
        b_last = b[c - 1:c, :]
        q_in = (q * jnp.exp2(b)).astype(BF16)
        k_out = (k * jnp.exp2(b_last - b)).astype(BF16)
        return a.astype(BF16), q_in, k_out, jnp.exp2(b_last)

    def outputs(hh, rows, a, q_in, k_out, decay):
        v = v_ref[0, hh, rows, :]
        return hh, rows, _dot(a, v), q_in, _dot_tn(v, k_out), decay

    def body(it, states):
        units = [(hh, it * unroll + u) for u in range(unroll) for hh in range(hpg)]
        st1, st2, st3 = {}, {}, []
        for n in range(len(units) + 2):
            if n < len(units):
                st1[n] = cumsums(*units[n])
            if 0 <= n - 1 < len(units):
                st2[n - 1] = scores(*st1[n - 1])
            if 0 <= n - 2 < len(units):
                st3.append(outputs(*st1[n - 2][0:2], *st2[n - 2]))
        states = list(states)
        for hh, rows, o_intra, q_in, ds_t, decay in st3:
            o = o_intra + _dot_nt(q_in, states[hh].astype(BF16))
            states[hh] = states[hh] * decay + ds_t
            y = _rms(o, gain) * og_ref[0, hh, rows, :].astype(F32)
            o_ref[rows, hh * HEAD_W:(hh + 1) * HEAD_W] = y.astype(BF16)
        return tuple(states)

    lax.fori_loop(0, seq // (c * unroll), body,
                  tuple(jnp.zeros((HEAD_W, HEAD_W), F32) for _ in range(hpg)))


def _hgrn(hin, logf, gain, *, batch, seq, c, unroll, hpg):
    w, lvl, levels, n_fine = _hgrn_tables(c)
    assert seq % (c * unroll) == 0 and HEADS % hpg == 0
    groups = HEADS // hpg

    def head(slab):
        return pl.BlockSpec((1, hpg, seq, HEAD_W), lambda b, h: (b, slab * groups + h, 0, 0))

    return pl.pallas_call(
        functools.partial(_hgrn_kernel, c=c, levels=levels, n_fine=n_fine, seq=seq, unroll=unroll,
                          hpg=hpg),
        out_shape=jax.ShapeDtypeStruct((batch * seq, MIX_W), BF16),
        grid=(batch, groups),
        in_specs=[
            pl.BlockSpec(w.shape, lambda b, h: (0, 0)),
            pl.BlockSpec((c, c), lambda b, h: (0, 0)),
            head(0), head(1), head(2), head(3), head(0),
            pl.BlockSpec((1, HEAD_W), lambda b, h: (0, 0)),
        ],
        out_specs=pl.BlockSpec((seq, hpg * HEAD_W), lambda b, h: (b, h)),
        compiler_params=_cparams(("parallel", "parallel")),
        name="hgrn",
    )(w, lvl, hin, hin, hin, hin, logf, gain.reshape(1, HEAD_W))


def _merge_kernel(x_ref, ya_ref, yb_ref, gt_ref, wa_ref, wb_ref, wo_ref, *rest, n_side):
    side_in, (o_ref,), side_out = rest[:n_side], rest[n_side:n_side + 1], rest[n_side + 1:]
    d = x_ref.shape[1]
    gt = gt_ref[...].astype(F32)
    merged = gt[:, 0:d] * _dot(ya_ref[...], wa_ref[...]) + gt[:, d:2 * d] * _dot(yb_ref[...], wb_ref[...])
    o_ref[...] = x_ref[...] + _dot(merged.astype(BF16), wo_ref[...])
    _SideCast.run(side_in, side_out)


def _resident(shape):
    return pl.BlockSpec(shape, lambda i: (0,) * len(shape), pipeline_mode=pl.Buffered(1))


def _merge(x, ya, yb, gates, wa, wb, wo, side_weights=(), *, tm):
    m, d = x.shape
    side = _SideCast(side_weights, m // tm, lambda i: i)
    out = pl.pallas_call(
        functools.partial(_merge_kernel, n_side=side.n),
        out_shape=[jax.ShapeDtypeStruct((m, d), F32)] + side.out_shapes,
        grid=(m // tm,),
        in_specs=[
            pl.BlockSpec((tm, d), lambda i: (i, 0)),
            pl.BlockSpec((tm, MIX_W), lambda i: (i, 0)),
            pl.BlockSpec((tm, MIX_W), lambda i: (i, 0)),
            pl.BlockSpec((tm, 2 * d), lambda i: (i, 0)),
            _resident(wa.shape), _resident(wb.shape), _resident(wo.shape),
        ] + side.in_specs,
        out_specs=[pl.BlockSpec((tm, d), lambda i: (i, 0))] + side.out_specs,
        compiler_params=_cparams(("arbitrary",)),
        name="merge",
    )(x, ya, yb, gates, wa, wb, wo, *side.arrays)
    return out[0], out[1:]


def _ple_kernel(x_ref, p_ref, gg_ref, pg_ref, wg_ref, wp_ref, o_ref):
    ple = _rms(_dot(p_ref[...].astype(BF16), wp_ref[...]), pg_ref[...])
    x = x_ref[...]
    gate = _sigmoid(_dot(_rms(x, gg_ref[...]).astype(BF16), wg_ref[...]))
    o_ref[...] = x + gate * ple


def _ple(x, p, gate_gain, post_gain, w_gate, w_proj, *, tm):
    m, d = x.shape
    pd = p.shape[1]
    return pl.pallas_call(
        _ple_kernel,
        out_shape=jax.ShapeDtypeStruct((m, d), F32),
        grid=(m // tm,),
        in_specs=[
            pl.BlockSpec((tm, d), lambda i: (i, 0)),
            pl.BlockSpec((tm, pd), lambda i: (i, 0)),
            pl.BlockSpec((1, d), lambda i: (0, 0)),
            pl.BlockSpec((1, d), lambda i: (0, 0)),
            _resident(w_gate.shape), _resident(w_proj.shape),
        ],
        out_specs=pl.BlockSpec((tm, d), lambda i: (i, 0)),
        compiler_params=_cparams(("parallel",)),
        name="ple",
    )(x, p, gate_gain.reshape(1, d), post_gain.reshape(1, d), w_gate, w_proj)


def _tiles(seq, d, f):
    def pick(n, choices):
        return next(c for c in choices if n % c == 0)
    return dict(
        tm=pick(seq, (512, 256, 128)),
        tfm=pick(seq, (1024, 512, 256, 128)),
        tf=pick(f, (512, 256, 128)),
        tp=pick(seq, (1024, 512, 256, 128)),
        tn=pick(math.gcd(MIX_W, 2 * d), (1024, 512, 256, 128)),
        ta=pick(seq, (512, 256, 128)),
        qpg=4,
        hps=4,
        c=128,
        cu=16,
        hpg=4,
    )


def kernel(x, p, ffn1_norm, ffn1_w_gate, ffn1_w_up, ffn1_w_down, mix_norm, w_in, q_norm, k_norm, lambda_q1, lambda_k1, lambda_q2, lambda_k2, diff_subln, rel_bias, hgrn_lb_logits, hgrn_norm, w_branch_a, w_branch_b, w_out, ffn2_norm, ffn2_w_gate, ffn2_w_up, ffn2_w_down, ple_gate_norm, w_ple_gate, w_ple_proj, ple_post_norm):
    batch, seq, d = x.shape
    depth = ffn1_norm.shape[0]
    assert depth == 1
    m = batch * seq
    tl = _tiles(seq, d, ffn1_w_gate.shape[2])
    bf = lambda w: w.astype(BF16)

    lower_bounds = jnp.cumsum(jax.nn.softmax(hgrn_lb_logits.astype(F32), axis=0), axis=0)
    lb = lower_bounds[0].reshape(HEADS, HEAD_W)
    lam = (jnp.exp(jnp.sum(lambda_q1[0].astype(F32) * lambda_k1[0].astype(F32)))
           - jnp.exp(jnp.sum(lambda_q2[0].astype(F32) * lambda_k2[0].astype(F32)))
           + LAMBDA_INIT)
    scale = QK_DIM ** -0.5
    qg = jnp.tile(q_norm[0].astype(F32), 2).reshape(1, HEAD_W) * (scale * LOG2E)
    kg = jnp.tile(k_norm[0].astype(F32), 2).reshape(1, HEAD_W)
    qk_bound = (QK_DIM * scale * jnp.max(jnp.abs(q_norm[0].astype(F32)))
                * jnp.max(jnp.abs(k_norm[0].astype(F32))))

    x2 = x.reshape(m, d)
    x2, (w_in_b, wa_b, wb_b, wo_b) = _ffn(
        x2, ffn1_norm[0], bf(ffn1_w_gate[0]), bf(ffn1_w_up[0]), bf(ffn1_w_down[0]),
        (w_in[0], w_branch_a[0], w_branch_b[0], w_out[0]), tm=tl["tfm"], tf=tl["tf"])

    qkv, hin, logf, gates, (wg2_b, wu2_b, wd2_b) = _proj(
        x2, mix_norm[0], w_in_b, qg, kg, lb, (ffn2_w_gate[0], ffn2_w_up[0], ffn2_w_down[0]),
        batch=batch, seq=seq, tm=tl["tp"], tn=tl["tn"])
    ya = _attn(qkv, lam, rel_bias, qk_bound, diff_subln[0], batch=batch, seq=seq,
               t=tl["ta"], qpg=tl["qpg"], hps=tl["hps"])
    yb = _hgrn(hin, logf, hgrn_norm[0], batch=batch, seq=seq, c=tl["c"], unroll=tl["cu"],
               hpg=tl["hpg"])
    x2, (wpg_b, wpp_b) = _merge(x2, ya, yb, gates, wa_b, wb_b, wo_b,
                                (w_ple_gate[0], w_ple_proj[0]), tm=tl["tm"])

    x2, _ = _ffn(x2, ffn2_norm[0], wg2_b, wu2_b, wd2_b, tm=tl["tfm"], tf=tl["tf"])
    x2 = _ple(x2, p[0].reshape(m, -1), ple_gate_norm[0], ple_post_norm[0], wpg_b, wpp_b, tm=tl["tm"])
    return x2.reshape(batch, seq, d)
```

```python
import functools
import math

import jax
import jax.numpy as jnp
import numpy as np
from jax import lax
from jax.experimental import pallas as pl
from jax.experimental.pallas import tpu as pltpu

F32 = jnp.float32
BF16 = jnp.bfloat16

EPS = 1e-6
HEADS = 8
HEAD_W = 128
QK_DIM = 64
MIX_W = HEADS * HEAD_W
REL_BUCKETS = 32
REL_MAX_DIST = 128
LAMBDA_INIT = 0.8 - 0.6 * math.exp(-0.3 * 0)
LOG2E = math.log2(math.e)
MAX_STATIC_SOFTMAX_SPREAD = 60.0

V7X_MXU_WIDTH = 256
V7X_VMEM_BYTES = 64 * 1024 * 1024
VMEM_LIMIT = V7X_VMEM_BYTES * 7 // 8


def _cparams(sem):
    return pltpu.CompilerParams(dimension_semantics=sem, vmem_limit_bytes=VMEM_LIMIT)


def _sigmoid(x):
    return 1.0 / (1.0 + jnp.exp(-x))


def _rms(x, gain):
    ms = jnp.mean(x * x, axis=-1, keepdims=True)
    return x * lax.rsqrt(ms + EPS) * gain


def _dot(a, b):
    return jnp.dot(a, b, preferred_element_type=F32)


def _dot_nt(a, b):
    return lax.dot_general(a, b, (((1,), (1,)), ((), ())), preferred_element_type=F32)


def _dot_tn(a, b):
    return lax.dot_general(a, b, (((0,), (0,)), ((), ())), preferred_element_type=F32)


BF16_ROW_TILE = 16


class _SideCast:
    def __init__(self, arrays, steps, step_index):
        self.arrays = list(arrays)
        self.n = len(self.arrays)
        self.step_index = step_index
        self.chunks = []
        for w in self.arrays:
            rows = w.shape[0]
            self.chunks.append(max(c for c in range(1, steps + 1)
                                   if rows % c == 0 and (rows // c) % BF16_ROW_TILE == 0))

    def _specs(self):
        def spec(w, c):
            return pl.BlockSpec((w.shape[0] // c, w.shape[1]),
                                lambda *ids: (jnp.minimum(self.step_index(*ids), c - 1), 0))
        return [spec(w, c) for w, c in zip(self.arrays, self.chunks)]

    in_specs = property(_specs)
    out_specs = property(_specs)

    @property
    def out_shapes(self):
        return [jax.ShapeDtypeStruct(w.shape, BF16) for w in self.arrays]

    @staticmethod
    def run(in_refs, out_refs):
        for src, dst in zip(in_refs, out_refs):
            dst[...] = src[...].astype(BF16)


def _row_tile_fetch(x_hbm, xs_ref, sem, tm):
    i = pl.program_id(0)
    j = pl.program_id(1)

    def copy(row_tile):
        return pltpu.make_async_copy(x_hbm.at[pl.ds(row_tile * tm, tm), :], xs_ref, sem)

    @pl.when((i == 0) & (j == 0))
    def _():
        copy(0).start()

    @pl.when((j == 1) & (i + 1 < pl.num_programs(0)))
    def _():
        copy(i + 1).start()

    return copy(i)


def _ffn_kernel(x_hbm, g_ref, wg_ref, wu_ref, wd_ref, *rest, tm, n_side):
    side_in, (o_ref,), side_out = rest[:n_side], rest[n_side:n_side + 1], rest[n_side + 1:2 * n_side + 1]
    xs_ref, h_ref, sem = rest[2 * n_side + 1:]
    j = pl.program_id(1)
    x_copy = _row_tile_fetch(x_hbm, xs_ref, sem, tm)

    @pl.when(j == 0)
    def _():
        x_copy.wait()
        x = xs_ref[...]
        h_ref[...] = _rms(x, g_ref[...]).astype(BF16)
        o_ref[...] = x

    h = h_ref[...]
    gate = _dot(h, wg_ref[...])
    up = _dot(h, wu_ref[...])
    act = (gate * _sigmoid(gate) * (0.5 * up)).astype(BF16)
    o_ref[...] += _dot(act, wd_ref[...])
    _SideCast.run(side_in, side_out)


def _ffn(x, gain, w_gate, w_up, w_down, side_weights=(), *, tm, tf):
    m, d = x.shape
    f = w_gate.shape[1]
    nj = f // tf
    assert nj >= 2
    side = _SideCast(side_weights, (m // tm) * nj, lambda i, j: i * nj + j)
    out = pl.pallas_call(
        functools.partial(_ffn_kernel, tm=tm, n_side=side.n),
        out_shape=[jax.ShapeDtypeStruct((m, d), F32)] + side.out_shapes,
        grid=(m // tm, nj),
        in_specs=[
            pl.BlockSpec(memory_space=pl.ANY),
            pl.BlockSpec((1, d), lambda i, j: (0, 0)),
            pl.BlockSpec((d, tf), lambda i, j: (0, j)),
            pl.BlockSpec((d, tf), lambda i, j: (0, j)),
            pl.BlockSpec((tf, d), lambda i, j: (j, 0)),
        ] + side.in_specs,
        out_specs=[pl.BlockSpec((tm, d), lambda i, j: (i, 0))] + side.out_specs,
        scratch_shapes=[pltpu.VMEM((tm, d), F32), pltpu.VMEM((tm, d), BF16),
                        pltpu.SemaphoreType.DMA(())],
        compiler_params=_cparams(("arbitrary", "arbitrary")),
        name="ffn",
    )(x, gain.reshape(1, d), w_gate, w_up, w_down, *side.arrays)
    return out[0], out[1:]


def _proj_kernel(x_hbm, g_ref, w_ref, qg_ref, kg_ref, lb_ref, gm_ref, *rest, tm, tn, nj, n_side):
    side_in, rest = rest[:n_side], rest[n_side:]
    (oqkv_ref, oh_ref, olf_ref, og_ref), rest = rest[:4], rest[4:]
    side_out, (xs_ref, h_ref, acc_ref, sem) = rest[:n_side], rest[n_side:]
    j = pl.program_id(1)
    jp = j - 1
    tps = MIX_W // tn
    hpt = tn // HEAD_W

    def heads(vals):
        return [vals[:, s * HEAD_W:(s + 1) * HEAD_W] for s in range(hpt)]

    def epi_qk(acc):
        gain = jnp.where(jp < tps, qg_ref[...], kg_ref[...])
        pw = gm_ref.shape[0]
        for s in range(tn // pw):
            a = acc[:, s * pw:(s + 1) * pw]
            ms = _dot((a * a).astype(BF16), gm_ref[...])
            y = a * lax.rsqrt(ms + EPS)
            for u in range(pw // HEAD_W):
                oqkv_ref[0, s * (pw // HEAD_W) + u] = (
                    y[:, u * HEAD_W:(u + 1) * HEAD_W] * gain).astype(BF16)

    def epi_v(acc):
        for s, a in enumerate(heads(acc)):
            oqkv_ref[0, s] = a.astype(BF16)

    def epi_silu(acc):
        for s, a in enumerate(heads(acc * _sigmoid(acc))):
            oh_ref[0, s] = a.astype(BF16)

    def epi_forget(acc):
        jj = jp - 4 * tps
        for s, a in enumerate(heads(acc)):
            lb = lb_ref[pl.ds(jj * hpt + s, 1), :]
            forget = lb + (1.0 - lb) * _sigmoid(a)
            oh_ref[0, s] = (1.0 - forget).astype(BF16)
            olf_ref[0, s] = jnp.log2(forget)

    def epi_hv(acc):
        for s, a in enumerate(heads(acc)):
            oh_ref[0, s] = a.astype(BF16)

    def epi_gates(acc):
        og_ref[...] = _sigmoid(acc).astype(BF16)

    x_copy = _row_tile_fetch(x_hbm, xs_ref, sem, tm)

    @pl.when(j == 0)
    def _():
        x_copy.wait()
        h_ref[...] = _rms(xs_ref[...], g_ref[...]).astype(BF16)
        acc_ref[...] = _dot(h_ref[...], w_ref[...])
        _SideCast.run(side_in, side_out)

    def fused(cond, epi):
        @pl.when(cond & (j >= 1) & (j < nj))
        def _():
            acc = acc_ref[...]
            epi(acc)
            acc_ref[...] = _dot(h_ref[...], w_ref[...])
            _SideCast.run(side_in, side_out)

    fused(jp < 2 * tps, epi_qk)
    fused((jp >= 2 * tps) & (jp < 3 * tps), epi_v)
    fused(((jp >= 3 * tps) & (jp < 4 * tps)) | ((jp >= 6 * tps) & (jp < 7 * tps)), epi_silu)
    fused((jp >= 4 * tps) & (jp < 5 * tps), epi_forget)
    fused((jp >= 5 * tps) & (jp < 6 * tps), epi_hv)
    fused(jp >= 7 * tps, epi_gates)

    @pl.when(j == nj)
    def _():
        epi_gates(acc_ref[...])
        _SideCast.run(side_in, side_out)


def _proj(x, gain, w_in, qg, kg, lb, side_weights=(), *, batch, seq, tm, tn):
    m, d = x.shape
    n_in = w_in.shape[1]
    tps = MIX_W // tn
    hpt = tn // HEAD_W
    spb = seq // tm
    ng = 2 * d // tn
    nj = 7 * tps + ng
    assert n_in == 7 * MIX_W + 2 * d and MIX_W % tn == 0 and (2 * d) % tn == 0

    pw = math.gcd(tn, V7X_MXU_WIDTH)
    lane = np.arange(pw)
    gm = (lane[:, None] // QK_DIM == lane[None, :] // QK_DIM).astype(np.float32) / QK_DIM

    def hm_block(lo, n):
        return lambda i, j: (i // spb, jnp.clip(j - 1 - lo, 0, n - 1), i % spb, 0)

    side = _SideCast(side_weights, (m // tm) * (nj + 1), lambda i, j: i * (nj + 1) + j)
    out = pl.pallas_call(
        functools.partial(_proj_kernel, tm=tm, tn=tn, nj=nj, n_side=side.n),
        out_shape=[
            jax.ShapeDtypeStruct((batch, 3 * HEADS, seq, HEAD_W), BF16),
            jax.ShapeDtypeStruct((batch, 4 * HEADS, seq, HEAD_W), BF16),
            jax.ShapeDtypeStruct((batch, HEADS, seq, HEAD_W), F32),
            jax.ShapeDtypeStruct((m, 2 * d), BF16),
        ] + side.out_shapes,
        grid=(m // tm, nj + 1),
        in_specs=[
            pl.BlockSpec(memory_space=pl.ANY),
            pl.BlockSpec((1, d), lambda i, j: (0, 0)),
            pl.BlockSpec((d, tn), lambda i, j: (0, jnp.minimum(j, nj - 1))),
            pl.BlockSpec((1, HEAD_W), lambda i, j: (0, 0)),
            pl.BlockSpec((1, HEAD_W), lambda i, j: (0, 0)),
            pl.BlockSpec((HEADS, HEAD_W), lambda i, j: (0, 0)),
            pl.BlockSpec((pw, pw), lambda i, j: (0, 0)),
        ] + side.in_specs,
        out_specs=[
            pl.BlockSpec((1, hpt, tm, HEAD_W), hm_block(0, 3 * tps)),
            pl.BlockSpec((1, hpt, tm, HEAD_W), hm_block(3 * tps, 4 * tps)),
            pl.BlockSpec((1, hpt, tm, HEAD_W), hm_block(4 * tps, tps)),
            pl.BlockSpec((tm, tn), lambda i, j: (i, jnp.clip(j - 1 - 7 * tps, 0, ng - 1))),
        ] + side.out_specs,
        scratch_shapes=[pltpu.VMEM((tm, d), F32), pltpu.VMEM((tm, d), BF16),
                        pltpu.VMEM((tm, tn), F32), pltpu.SemaphoreType.DMA(())],
        compiler_params=_cparams(("arbitrary", "arbitrary")),
        name="proj",
    )(x, gain.reshape(1, d), w_in, qg, kg, lb, jnp.asarray(gm, BF16), *side.arrays)
    return out[0], out[1], out[2], out[3], out[4:]


def _attn_kernel(lam_ref, off_ref, shifted_ref, q_ref, k_ref, v_ref, w_ref, sg_ref, o_ref,
                 vt_ref, bias_ref, *, t, nkb, qpg, hps):
    hg = pl.program_id(0)
    b = pl.program_id(1)
    i = pl.program_id(2)
    lam = lam_ref[0]

    @pl.when((b == 0) & (i == 0))
    def _():
        for hh in range(hps):
            for dd in range(2):
                rows = jnp.broadcast_to(w_ref[hh, dd:dd + 1, :], (t, 2 * t))
                bias_ref[hh, dd] = pltpu.roll(rows, 0, 1, stride=1, stride_axis=0)[:, 0:t]

    @pl.when(i == 0)
    def _():
        for hh in range(hps):
            for jb in range(nkb):
                vt_ref[hh, jb] = v_ref[0, hh, jb * t:(jb + 1) * t, :].astype(F32).T.astype(BF16)

    def colsum8(p):
        return jnp.sum(p.reshape(t // 8, 8, 2 * t), axis=0)

    def both(x):
        return jnp.concatenate([x, x], axis=1)

    def query_block(hh, iq, row0, online):
        far_off = off_ref[hg * hps + hh]
        q = q_ref[0, hh, row0:row0 + t, :]
        lane = lax.broadcasted_iota(jnp.int32, (t, HEAD_W), 1)
        zero = jnp.zeros_like(q)
        qs = jnp.concatenate([jnp.where(lane < QK_DIM, q, zero),
                              jnp.where(lane >= QK_DIM, q, zero)], axis=0)
        offsets = [far_off] * max(iq - 1, 0)
        if iq >= 1:
            offsets.append(both(bias_ref[hh, 1]))
        offsets.append(both(bias_ref[hh, 0]))
        m = l = acc = None
        for j, badd in enumerate(offsets):
            s = _dot_nt(k_ref[0, hh, j * t:(j + 1) * t, :], qs) + badd
            if online:
                bmax = jnp.max(s, axis=0, keepdims=True)
                m_new = bmax if m is None else jnp.maximum(m, bmax)
                p = jnp.exp2(s - m_new)
            else:
                p = jnp.exp2(s)
            psum = colsum8(p)
            pv = _dot(vt_ref[hh, j], p.astype(BF16))
            if l is None:
                l, acc = psum, pv
            elif online:
                alpha = jnp.exp2(m - m_new)
                l, acc = alpha * l + psum, alpha * acc + pv
            else:
                l, acc = l + psum, acc + pv
            if online:
                m = m_new
        l = jnp.sum(l, axis=0, keepdims=True)
        o_t = acc[:, 0:t] / l[:, 0:t] - lam * (acc[:, t:2 * t] / l[:, t:2 * t])
        o_ref[row0:row0 + t, hh * HEAD_W:(hh + 1) * HEAD_W] = (
            _rms(o_t.T, sg_ref[...]) * (1.0 - LAMBDA_INIT)).astype(BF16)

    def query_group(ig, online):
        for hh in range(hps):
            for u in range(qpg):
                query_block(hh, ig * qpg + u, u * t, online)

    shifted = shifted_ref[0] != 0
    for ig in range(nkb // qpg):
        pl.when((i == ig) & shifted)(functools.partial(query_group, ig, False))
        pl.when((i == ig) & jnp.logical_not(shifted))(functools.partial(query_group, ig, True))


def _t5_bucket(dist):
    n = jnp.maximum(dist, 0)
    max_exact = REL_BUCKETS // 2
    nf = jnp.maximum(n, 1).astype(F32)
    large = max_exact + (jnp.log(nf / max_exact) / math.log(REL_MAX_DIST / max_exact)
                         * (REL_BUCKETS - max_exact)).astype(jnp.int32)
    large = jnp.minimum(large, REL_BUCKETS - 1)
    return jnp.where(n < max_exact, n, large)


def _bias_generators(f, t):
    hole = jnp.full((HEADS, 1), -jnp.inf, F32)
    w0 = jnp.concatenate([f[:, 0:t], jnp.full((HEADS, t), -jnp.inf, F32)], axis=1)
    w1 = jnp.concatenate([f[:, t:2 * t], hole, f[:, 1:t]], axis=1)
    return jnp.stack([w0, w1], axis=1)


def _attn(qkv, lam, rel_bias, qk_bound, subln, *, batch, seq, t, qpg, hps):
    assert t >= REL_MAX_DIST
    nq = seq // t
    rb = rel_bias.astype(F32)
    spread = 2.0 * qk_bound + jnp.max(jnp.max(rb, axis=0) - jnp.min(rb, axis=0))
    static_ok = spread <= MAX_STATIC_SOFTMAX_SPREAD
    upper = jnp.where(static_ok, qk_bound + jnp.max(rb, axis=0), 0.0)
    f = rb[_t5_bucket(jnp.arange(2 * t))].T - upper[:, None]
    bias = _bias_generators(f * LOG2E, t)
    far_off = (rb[REL_BUCKETS - 1] - upper) * LOG2E

    assert nq % qpg == 0 and HEADS % hps == 0
    ng = nq // qpg
    hgs = HEADS // hps

    return pl.pallas_call(
        functools.partial(_attn_kernel, t=t, nkb=nq, qpg=qpg, hps=hps),
        out_shape=jax.ShapeDtypeStruct((batch * seq, MIX_W), BF16),
        grid=(hgs, batch, ng),
        in_specs=[
            pl.BlockSpec(memory_space=pltpu.SMEM),
            pl.BlockSpec(memory_space=pltpu.SMEM),
            pl.BlockSpec(memory_space=pltpu.SMEM),
            pl.BlockSpec((1, hps, qpg * t, HEAD_W), lambda h, b, i: (b, h, i, 0)),
            pl.BlockSpec((1, hps, seq, HEAD_W), lambda h, b, i: (b, hgs + h, 0, 0)),
            pl.BlockSpec((1, hps, seq, HEAD_W), lambda h, b, i: (b, 2 * hgs + h, 0, 0)),
            pl.BlockSpec((hps, 2, 2 * t), lambda h, b, i: (h, 0, 0)),
            pl.BlockSpec((1, HEAD_W), lambda h, b, i: (0, 0)),
        ],
        out_specs=pl.BlockSpec((qpg * t, hps * HEAD_W), lambda h, b, i: (b * ng + i, h)),
        scratch_shapes=[pltpu.VMEM((hps, nq, HEAD_W, t), BF16),
                        pltpu.VMEM((hps, 2, t, t), F32)],
        compiler_params=_cparams(("arbitrary", "arbitrary", "arbitrary")),
        name="attn",
    )(lam.reshape(1), far_off, static_ok.astype(jnp.int32).reshape(1), qkv, qkv, qkv, bias,
      subln.reshape(1, HEAD_W))


SUBLANES = 8


def _hgrn_tables(c):
    levels = int(math.log2(c))
    assert 2 ** levels == c
    t = np.arange(c)[:, None]
    s = np.arange(c)[None, :]
    blocks = [(s <= t).astype(np.float32)]
    level_id = np.where(t == s, 0, -1)
    n_fine = 0
    for l in range(1, levels + 1):
        g, half = 2 ** l, 2 ** (l - 1)
        mid = (t // g) * g + half - 1
        upper = (t % g) >= half
        if half < SUBLANES:
            above = upper & (s > mid) & (s <= t)
            below = (~upper) & (s > t) & (s <= mid)
            blocks.append((above | below).astype(np.float32))
            n_fine += 1
        pair = (t // g == s // g) & upper & ((s % g) < half)
        level_id = np.where(pair, l, level_id)
    w = np.concatenate(blocks, axis=0)
    return jnp.asarray(w, BF16), jnp.asarray(level_id, jnp.int32), levels, n_fine


def _hgrn_kernel(w_ref, lvl_ref, q_ref, k_ref, v_ref, og_ref, lf_ref, gn_ref, o_ref, *,
                 c, levels, n_fine, seq, unroll, hpg):
    gain = gn_ref[...]

    def cumsums(hh, ci):
        rows = pl.ds(pl.multiple_of(ci * c, c), c)
        g = lf_ref[0, hh, rows, :]
        g_hi = g.astype(BF16)
        g_lo = (g - g_hi.astype(F32)).astype(BF16)
        e2 = _dot(w_ref[...], jnp.concatenate([g_hi, g_lo], axis=1))
        return hh, rows, e2[:, 0:HEAD_W] + e2[:, HEAD_W:2 * HEAD_W]

    def scores(hh, rows, e):
        qb = q_ref[0, hh, rows, :]
        kb = k_ref[0, hh, rows, :]
        q = qb.astype(F32)
        k = kb.astype(F32)
        b = e[0:c]
        lvl = lvl_ref[...]
        pairs = [(qb, kb)]
        for l in range(1, levels + 1):
            if l <= n_fine:
                d = e[l * c:(l + 1) * c]
            else:
                g, half = 2 ** l, 2 ** (l - 1)
                b3 = b.reshape(c // g, g, HEAD_W)
                mid = b3[:, half - 1:half, :]
                d = jnp.concatenate([mid - b3[:, 0:half, :], b3[:, half:g, :] - mid],
                                    axis=1).reshape(c, HEAD_W)
            x = jnp.exp2(d)
            pairs.append(((q * x).astype(BF16), (k * x).astype(BF16)))
        a = jnp.zeros((c, c), F32)
        zero = jnp.zeros((c, HEAD_W), BF16)
        for l in range(0, levels + 1, 2):
            (qa, ka), (qc, kc) = pairs[l], pairs[l + 1]
            keys = jnp.concatenate([jnp.concatenate([ka, zero], axis=1),
                                    jnp.concatenate([zero, kc], axis=1)], axis=0)
            r = _dot_nt(jnp.concatenate([qa, qc], axis=1), keys)
            a = jnp.where(lvl == l, r[:, 0:c], a)
            a = jnp.where(lvl == l + 1, r[:, c:2 * c], a)
        b_last = b[c - 1:c, :]
        q_in = (q * jnp.exp2(b)).astype(BF16)
        k_out = (k * jnp.exp2(b_last - b)).astype(BF16)
        return a.astype(BF16), q_in, k_out, jnp.exp2(b_last)

    def outputs(hh, rows, a, q_in, k_out, decay):
        v = v_ref[0, hh, rows, :]
        return hh, rows, _dot(a, v), q_in, _dot_tn(v, k_out), decay

    def body(it, states):
        units = [(hh, it * unroll + u) for u in range(unroll) for hh in range(hpg)]
        st1, st2, st3 = {}, {}, []
        for n in range(len(units) + 2):
            if n < len(units):
                st1[n] = cumsums(*units[n])
            if 0 <= n - 1 < len(units):
                st2[n - 1] = scores(*st1[n - 1])
            if 0 <= n - 2 < len(units):
                st3.append(outputs(*st1[n - 2][0:2], *st2[n - 2]))
        states = list(states)
        for hh, rows, o_intra, q_in, ds_t, decay in st3:
            o = o_intra + _dot_nt(q_in, states[hh].astype(BF16))
            states[hh] = states[hh] * decay + ds_t
            y = _rms(o, gain) * og_ref[0, hh, rows, :].astype(F32)
            o_ref[rows, hh * HEAD_W:(hh + 1) * HEAD_W] = y.astype(BF16)
        return tuple(states)

    lax.fori_loop(0, seq // (c * unroll), body,
                  tuple(jnp.zeros((HEAD_W, HEAD_W), F32) for _ in range(hpg)))


def _hgrn(hin, logf, gain, *, batch, seq, c, unroll, hpg):
    w, lvl, levels, n_fine = _hgrn_tables(c)
    assert seq % (c * unroll) == 0 and HEADS % hpg == 0
    groups = HEADS // hpg

    def head(slab):
        return pl.BlockSpec((1, hpg, seq, HEAD_W), lambda b, h: (b, slab * groups + h, 0, 0))

    return pl.pallas_call(
        functools.partial(_hgrn_kernel, c=c, levels=levels, n_fine=n_fine, seq=seq, unroll=unroll,
                          hpg=hpg),
        out_shape=jax.ShapeDtypeStruct((batch * seq, MIX_W), BF16),
        grid=(batch, groups),
        in_specs=[
            pl.BlockSpec(w.shape, lambda b, h: (0, 0)),
            pl.BlockSpec((c, c), lambda b, h: (0, 0)),
            head(0), head(1), head(2), head(3), head(0),
            pl.BlockSpec((1, HEAD_W), lambda b, h: (0, 0)),
        ],
        out_specs=pl.BlockSpec((seq, hpg * HEAD_W), lambda b, h: (b, h)),
        compiler_params=_cparams(("parallel", "parallel")),
        name="hgrn",
    )(w, lvl, hin, hin, hin, hin, logf, gain.reshape(1, HEAD_W))


def _merge_kernel(x_ref, ya_ref, yb_ref, gt_ref, wa_ref, wb_ref, wo_ref, *rest, n_side):
    side_in, (o_ref,), side_out = rest[:n_side], rest[n_side:n_side + 1], rest[n_side + 1:]
    d = x_ref.shape[1]
    gt = gt_ref[...].astype(F32)
    merged = gt[:, 0:d] * _dot(ya_ref[...], wa_ref[...]) + gt[:, d:2 * d] * _dot(yb_ref[...], wb_ref[...])
    o_ref[...] = x_ref[...] + _dot(merged.astype(BF16), wo_ref[...])
    _SideCast.run(side_in, side_out)


def _resident(shape):
    return pl.BlockSpec(shape, lambda i: (0,) * len(shape), pipeline_mode=pl.Buffered(1))


def _merge(x, ya, yb, gates, wa, wb, wo, side_weights=(), *, tm):
    m, d = x.shape
    side = _SideCast(side_weights, m // tm, lambda i: i)
    out = pl.pallas_call(
        functools.partial(_merge_kernel, n_side=side.n),
        out_shape=[jax.ShapeDtypeStruct((m, d), F32)] + side.out_shapes,
        grid=(m // tm,),
        in_specs=[
            pl.BlockSpec((tm, d), lambda i: (i, 0)),
            pl.BlockSpec((tm, MIX_W), lambda i: (i, 0)),
            pl.BlockSpec((tm, MIX_W), lambda i: (i, 0)),
            pl.BlockSpec((tm, 2 * d), lambda i: (i, 0)),
            _resident(wa.shape), _resident(wb.shape), _resident(wo.shape),
        ] + side.in_specs,
        out_specs=[pl.BlockSpec((tm, d), lambda i: (i, 0))] + side.out_specs,
        compiler_params=_cparams(("arbitrary",)),
        name="merge",
    )(x, ya, yb, gates, wa, wb, wo, *side.arrays)
    return out[0], out[1:]


def _ple_kernel(x_ref, p_ref, gg_ref, pg_ref, wg_ref, wp_ref, o_ref):
    ple = _rms(_dot(p_ref[...].astype(BF16), wp_ref[...]), pg_ref[...])
    x = x_ref[...]
    gate = _sigmoid(_dot(_rms(x, gg_ref[...]).astype(BF16), wg_ref[...]))
    o_ref[...] = x + gate * ple


def _ple(x, p, gate_gain, post_gain, w_gate, w_proj, *, tm):
    m, d = x.shape
    pd = p.shape[1]
    return pl.pallas_call(
        _ple_kernel,
        out_shape=jax.ShapeDtypeStruct((m, d), F32),
        grid=(m // tm,),
        in_specs=[
            pl.BlockSpec((tm, d), lambda i: (i, 0)),
            pl.BlockSpec((tm, pd), lambda i: (i, 0)),
            pl.BlockSpec((1, d), lambda i: (0, 0)),
            pl.BlockSpec((1, d), lambda i: (0, 0)),
            _resident(w_gate.shape), _resident(w_proj.shape),
        ],
        out_specs=pl.BlockSpec((tm, d), lambda i: (i, 0)),
        compiler_params=_cparams(("parallel",)),
        name="ple",
    )(x, p, gate_gain.reshape(1, d), post_gain.reshape(1, d), w_gate, w_proj)


def _tiles(seq, d, f):
    def pick(n, choices):
        return next(c for c in choices if n % c == 0)
    return dict(
        tm=pick(seq, (512, 256, 128)),
        tfm=pick(seq, (1024, 512, 256, 128)),
        tf=pick(f, (512, 256, 128)),
        tp=pick(seq, (1024, 512, 256, 128)),
        tn=pick(math.gcd(MIX_W, 2 * d), (1024, 512, 256, 128)),
        ta=pick(seq, (512, 256, 128)),
        qpg=4,
        hps=4,
        c=128,
        cu=16,
        hpg=4,
    )


def kernel(x, p, ffn1_norm, ffn1_w_gate, ffn1_w_up, ffn1_w_down, mix_norm, w_in, q_norm, k_norm, lambda_q1, lambda_k1, lambda_q2, lambda_k2, diff_subln, rel_bias, hgrn_lb_logits, hgrn_norm, w_branch_a, w_branch_b, w_out, ffn2_norm, ffn2_w_gate, ffn2_w_up, ffn2_w_down, ple_gate_norm, w_ple_gate, w_ple_proj, ple_post_norm):
    batch, seq, d = x.shape
    depth = ffn1_norm.shape[0]
    assert depth == 1
    m = batch * seq
    tl = _tiles(seq, d, ffn1_w_gate.shape[2])
    bf = lambda w: w.astype(BF16)

    lower_bounds = jnp.cumsum(jax.nn.softmax(hgrn_lb_logits.astype(F32), axis=0), axis=0)
    lb = lower_bounds[0].reshape(HEADS, HEAD_W)
    lam = (jnp.exp(jnp.sum(lambda_q1[0].astype(F32) * lambda_k1[0].astype(F32)))
           - jnp.exp(jnp.sum(lambda_q2[0].astype(F32) * lambda_k2[0].astype(F32)))
           + LAMBDA_INIT)
    scale = QK_DIM ** -0.5
    qg = jnp.tile(q_norm[0].astype(F32), 2).reshape(1, HEAD_W) * (scale * LOG2E)
    kg = jnp.tile(k_norm[0].astype(F32), 2).reshape(1, HEAD_W)
    qk_bound = (QK_DIM * scale * jnp.max(jnp.abs(q_norm[0].astype(F32)))
                * jnp.max(jnp.abs(k_norm[0].astype(F32))))

    x2 = x.reshape(m, d)
    x2, (w_in_b, wa_b, wb_b, wo_b) = _ffn(
        x2, ffn1_norm[0], bf(ffn1_w_gate[0]), bf(ffn1_w_up[0]), bf(ffn1_w_down[0]),
        (w_in[0], w_branch_a[0], w_branch_b[0], w_out[0]), tm=tl["tfm"], tf=tl["tf"])

    qkv, hin, logf, gates, (wg2_b, wu2_b, wd2_b) = _proj(
        x2, mix_norm[0], w_in_b, qg, kg, lb, (ffn2_w_gate[0], ffn2_w_up[0], ffn2_w_down[0]),
        batch=batch, seq=seq, tm=tl["tp"], tn=tl["tn"])
    ya = _attn(qkv, lam, rel_bias, qk_bound, diff_subln[0], batch=batch, seq=seq,
               t=tl["ta"], qpg=tl["qpg"], hps=tl["hps"])
    yb = _hgrn(hin, logf, hgrn_norm[0], batch=batch, seq=seq, c=tl["c"], unroll=tl["cu"],
               hpg=tl["hpg"])
    x2, (wpg_b, wpp_b) = _merge(x2, ya, yb, gates, wa_b, wb_b, wo_b,
                                (w_ple_gate[0], w_ple_proj[0]), tm=tl["tm"])

    x2, _ = _ffn(x2, ffn2_norm[0], wg2_b, wu2_b, wd2_b, tm=tl["tfm"], tf=tl["tf"])
    x2 = _ple(x2, p[0].reshape(m, -1), ple_gate_norm[0], ple_post_norm[0], wpg_b, wpp_b, tm=tl["tm"])
    return x2.reshape(batch, seq, d)
```

```python
import functools
import math

import jax
import jax.numpy as jnp
import numpy as np
from jax import lax
from jax.experimental import pallas as pl
from jax.experimental.pallas import tpu as pltpu

F32 = jnp.float32
BF16 = jnp.bfloat16

EPS = 1e-6
HEADS = 8
HEAD_W = 128
QK_DIM = 64
MIX_W = HEADS * HEAD_W
REL_BUCKETS = 32
REL_MAX_DIST = 128
LAMBDA_INIT = 0.8 - 0.6 * math.exp(-0.3 * 0)
LOG2E = math.log2(math.e)
MAX_STATIC_SOFTMAX_SPREAD = 60.0

V7X_MXU_WIDTH = 256
V7X_VMEM_BYTES = 64 * 1024 * 1024
VMEM_LIMIT = V7X_VMEM_BYTES * 7 // 8


def _cparams(sem):
    return pltpu.CompilerParams(dimension_semantics=sem, vmem_limit_bytes=VMEM_LIMIT)


def _sigmoid(x):
    return 1.0 / (1.0 + jnp.exp(-x))


def _rms(x, gain):
    ms = jnp.mean(x * x, axis=-1, keepdims=True)
    return x * lax.rsqrt(ms + EPS) * gain


def _dot(a, b):
    return jnp.dot(a, b, preferred_element_type=F32)


def _dot_nt(a, b):
    return lax.dot_general(a, b, (((1,), (1,)), ((), ())), preferred_element_type=F32)


def _dot_tn(a, b):
    return lax.dot_general(a, b, (((0,), (0,)), ((), ())), preferred_element_type=F32)


BF16_ROW_TILE = 16


class _SideCast:
    def __init__(self, arrays, steps, step_index):
        self.arrays = list(arrays)
        self.n = len(self.arrays)
        self.step_index = step_index
        self.chunks = []
        for w in self.arrays:
            rows = w.shape[0]
            self.chunks.append(max(c for c in range(1, steps + 1)
                                   if rows % c == 0 and (rows // c) % BF16_ROW_TILE == 0))

    def _specs(self):
        def spec(w, c):
            return pl.BlockSpec((w.shape[0] // c, w.shape[1]),
                                lambda *ids: (jnp.minimum(self.step_index(*ids), c - 1), 0))
        return [spec(w, c) for w, c in zip(self.arrays, self.chunks)]

    in_specs = property(_specs)
    out_specs = property(_specs)

    @property
    def out_shapes(self):
        return [jax.ShapeDtypeStruct(w.shape, BF16) for w in self.arrays]

    @staticmethod
    def run(in_refs, out_refs):
        for src, dst in zip(in_refs, out_refs):
            dst[...] = src[...].astype(BF16)


def _row_tile_fetch(x_hbm, xs_ref, sem, tm):
    i = pl.program_id(0)
    j = pl.program_id(1)

    def copy(row_tile):
        return pltpu.make_async_copy(x_hbm.at[pl.ds(row_tile * tm, tm), :], xs_ref, sem)

    @pl.when((i == 0) & (j == 0))
    def _():
        copy(0).start()

    @pl.when((j == 1) & (i + 1 < pl.num_programs(0)))
    def _():
        copy(i + 1).start()

    return copy(i)


def _ffn_kernel(x_hbm, g_ref, wg_ref, wu_ref, wd_ref, *rest, tm, n_side):
    side_in, (o_ref,), side_out = rest[:n_side], rest[n_side:n_side + 1], rest[n_side + 1:2 * n_side + 1]
    xs_ref, h_ref, sem = rest[2 * n_side + 1:]
    j = pl.program_id(1)
    x_copy = _row_tile_fetch(x_hbm, xs_ref, sem, tm)

    @pl.when(j == 0)
    def _():
        x_copy.wait()
        x = xs_ref[...]
        h_ref[...] = _rms(x, g_ref[...]).astype(BF16)
        o_ref[...] = x

    h = h_ref[...]
    gate = _dot(h, wg_ref[...])
    up = _dot(h, wu_ref[...])
    act = (gate * _sigmoid(gate) * (0.5 * up)).astype(BF16)
    o_ref[...] += _dot(act, wd_ref[...])
    _SideCast.run(side_in, side_out)


def _ffn(x, gain, w_gate, w_up, w_down, side_weights=(), *, tm, tf):
    m, d = x.shape
    f = w_gate.shape[1]
    nj = f // tf
    assert nj >= 2
    side = _SideCast(side_weights, (m // tm) * nj, lambda i, j: i * nj + j)
    out = pl.pallas_call(
        functools.partial(_ffn_kernel, tm=tm, n_side=side.n),
        out_shape=[jax.ShapeDtypeStruct((m, d), F32)] + side.out_shapes,
        grid=(m // tm, nj),
        in_specs=[
            pl.BlockSpec(memory_space=pl.ANY),
            pl.BlockSpec((1, d), lambda i, j: (0, 0)),
            pl.BlockSpec((d, tf), lambda i, j: (0, j)),
            pl.BlockSpec((d, tf), lambda i, j: (0, j)),
            pl.BlockSpec((tf, d), lambda i, j: (j, 0)),
        ] + side.in_specs,
        out_specs=[pl.BlockSpec((tm, d), lambda i, j: (i, 0))] + side.out_specs,
        scratch_shapes=[pltpu.VMEM((tm, d), F32), pltpu.VMEM((tm, d), BF16),
                        pltpu.SemaphoreType.DMA(())],
        compiler_params=_cparams(("arbitrary", "arbitrary")),
        name="ffn",
    )(x, gain.reshape(1, d), w_gate, w_up, w_down, *side.arrays)
    return out[0], out[1:]


def _proj_kernel(x_hbm, g_ref, w_ref, qg_ref, kg_ref, lb_ref, gm_ref, *rest, tm, tn, nj, n_side):
    side_in, rest = rest[:n_side], rest[n_side:]
    (oqkv_ref, oh_ref, olf_ref, og_ref), rest = rest[:4], rest[4:]
    side_out, (xs_ref, h_ref, acc_ref, sem) = rest[:n_side], rest[n_side:]
    j = pl.program_id(1)
    jp = j - 1
    tps = MIX_W // tn
    hpt = tn // HEAD_W

    def heads(vals):
        return [vals[:, s * HEAD_W:(s + 1) * HEAD_W] for s in range(hpt)]

    def epi_qk(acc):
        gain = jnp.where(jp < tps, qg_ref[...], kg_ref[...])
        pw = gm_ref.shape[0]
        for s in range(tn // pw):
            a = acc[:, s * pw:(s + 1) * pw]
            ms = _dot((a * a).astype(BF16), gm_ref[...])
            y = a * lax.rsqrt(ms + EPS)
            for u in range(pw // HEAD_W):
                oqkv_ref[0, s * (pw // HEAD_W) + u] = (
                    y[:, u * HEAD_W:(u + 1) * HEAD_W] * gain).astype(BF16)

    def epi_v(acc):
        for s, a in enumerate(heads(acc)):
            oqkv_ref[0, s] = a.astype(BF16)

    def epi_silu(acc):
        for s, a in enumerate(heads(acc * _sigmoid(acc))):
            oh_ref[0, s] = a.astype(BF16)

    def epi_forget(acc):
        jj = jp - 4 * tps
        for s, a in enumerate(heads(acc)):
            lb = lb_ref[pl.ds(jj * hpt + s, 1), :]
            forget = lb + (1.0 - lb) * _sigmoid(a)
            oh_ref[0, s] = (1.0 - forget).astype(BF16)
            olf_ref[0, s] = jnp.log2(forget)

    def epi_hv(acc):
        for s, a in enumerate(heads(acc)):
            oh_ref[0, s] = a.astype(BF16)

    def epi_gates(acc):
        og_ref[...] = _sigmoid(acc).astype(BF16)

    x_copy = _row_tile_fetch(x_hbm, xs_ref, sem, tm)

    @pl.when(j == 0)
    def _():
        x_copy.wait()
        h_ref[...] = _rms(xs_ref[...], g_ref[...]).astype(BF16)
        acc_ref[...] = _dot(h_ref[...], w_ref[...])
        _SideCast.run(side_in, side_out)

    def fused(cond, epi):
        @pl.when(cond & (j >= 1) & (j < nj))
        def _():
            acc = acc_ref[...]
            epi(acc)
            acc_ref[...] = _dot(h_ref[...], w_ref[...])
            _SideCast.run(side_in, side_out)

    fused(jp < 2 * tps, epi_qk)
    fused((jp >= 2 * tps) & (jp < 3 * tps), epi_v)
    fused(((jp >= 3 * tps) & (jp < 4 * tps)) | ((jp >= 6 * tps) & (jp < 7 * tps)), epi_silu)
    fused((jp >= 4 * tps) & (jp < 5 * tps), epi_forget)
    fused((jp >= 5 * tps) & (jp < 6 * tps), epi_hv)
    fused(jp >= 7 * tps, epi_gates)

    @pl.when(j == nj)
    def _():
        epi_gates(acc_ref[...])
        _SideCast.run(side_in, side_out)


def _proj(x, gain, w_in, qg, kg, lb, side_weights=(), *, batch, seq, tm, tn):
    m, d = x.shape
    n_in = w_in.shape[1]
    tps = MIX_W // tn
    hpt = tn // HEAD_W
    spb = seq // tm
    ng = 2 * d // tn
    nj = 7 * tps + ng
    assert n_in == 7 * MIX_W + 2 * d and MIX_W % tn == 0 and (2 * d) % tn == 0

    pw = math.gcd(tn, V7X_MXU_WIDTH)
    lane = np.arange(pw)
    gm = (lane[:, None] // QK_DIM == lane[None, :] // QK_DIM).astype(np.float32) / QK_DIM

    def hm_block(lo, n):
        return lambda i, j: (i // spb, jnp.clip(j - 1 - lo, 0, n - 1), i % spb, 0)

    side = _SideCast(side_weights, (m // tm) * (nj + 1), lambda i, j: i * (nj + 1) + j)
    out = pl.pallas_call(
        functools.partial(_proj_kernel, tm=tm, tn=tn, nj=nj, n_side=side.n),
        out_shape=[
            jax.ShapeDtypeStruct((batch, 3 * HEADS, seq, HEAD_W), BF16),
            jax.ShapeDtypeStruct((batch, 4 * HEADS, seq, HEAD_W), BF16),
            jax.ShapeDtypeStruct((batch, HEADS, seq, HEAD_W), F32),
            jax.ShapeDtypeStruct((m, 2 * d), BF16),
        ] + side.out_shapes,
        grid=(m // tm, nj + 1),
        in_specs=[
            pl.BlockSpec(memory_space=pl.ANY),
            pl.BlockSpec((1, d), lambda i, j: (0, 0)),
            pl.BlockSpec((d, tn), lambda i, j: (0, jnp.minimum(j, nj - 1))),
            pl.BlockSpec((1, HEAD_W), lambda i, j: (0, 0)),
            pl.BlockSpec((1, HEAD_W), lambda i, j: (0, 0)),
            pl.BlockSpec((HEADS, HEAD_W), lambda i, j: (0, 0)),
            pl.BlockSpec((pw, pw), lambda i, j: (0, 0)),
        ] + side.in_specs,
        out_specs=[
            pl.BlockSpec((1, hpt, tm, HEAD_W), hm_block(0, 3 * tps)),
            pl.BlockSpec((1, hpt, tm, HEAD_W), hm_block(3 * tps, 4 * tps)),
            pl.BlockSpec((1, hpt, tm, HEAD_W), hm_block(4 * tps, tps)),
            pl.BlockSpec((tm, tn), lambda i, j: (i, jnp.clip(j - 1 - 7 * tps, 0, ng - 1))),
        ] + side.out_specs,
        scratch_shapes=[pltpu.VMEM((tm, d), F32), pltpu.VMEM((tm, d), BF16),
                        pltpu.VMEM((tm, tn), F32), pltpu.SemaphoreType.DMA(())],
        compiler_params=_cparams(("arbitrary", "arbitrary")),
        name="proj",
    )(x, gain.reshape(1, d), w_in, qg, kg, lb, jnp.asarray(gm, BF16), *side.arrays)
    return out[0], out[1], out[2], out[3], out[4:]


def _attn_kernel(lam_ref, off_ref, q_ref, k_ref, v_ref, w_ref, sg_ref, o_ref, vt_ref, bias_ref, *,
                 t, nkb, qpg, hps, online):
    hg = pl.program_id(0)
    b = pl.program_id(1)
    i = pl.program_id(2)
    lam = lam_ref[0]

    @pl.when((b == 0) & (i == 0))
    def _():
        for hh in range(hps):
            for dd in range(2):
                rows = jnp.broadcast_to(w_ref[hh, dd:dd + 1, :], (t, 2 * t))
                bias_ref[hh, dd] = pltpu.roll(rows, 0, 1, stride=1, stride_axis=0)[:, 0:t]

    @pl.when(i == 0)
    def _():
        for hh in range(hps):
            for jb in range(nkb):
                vt_ref[hh, jb] = v_ref[0, hh, jb * t:(jb + 1) * t, :].astype(F32).T.astype(BF16)

    def colsum8(p):
        return jnp.sum(p.reshape(t // 8, 8, 2 * t), axis=0)

    def both(x):
        return jnp.concatenate([x, x], axis=1)

    def query_block(hh, iq, row0):
        far_off = off_ref[hg * hps + hh]
        q = q_ref[0, hh, row0:row0 + t, :]
        lane = lax.broadcasted_iota(jnp.int32, (t, HEAD_W), 1)
        zero = jnp.zeros_like(q)
        qs = jnp.concatenate([jnp.where(lane < QK_DIM, q, zero),
                              jnp.where(lane >= QK_DIM, q, zero)], axis=0)
        offsets = [far_off] * max(iq - 1, 0)
        if iq >= 1:
            offsets.append(both(bias_ref[hh, 1]))
        offsets.append(both(bias_ref[hh, 0]))
        m = l = acc = None
        for j, badd in enumerate(offsets):
            s = _dot_nt(k_ref[0, hh, j * t:(j + 1) * t, :], qs) + badd
            if online:
                bmax = jnp.max(s, axis=0, keepdims=True)
                m_new = bmax if m is None else jnp.maximum(m, bmax)
                p = jnp.exp2(s - m_new)
            else:
                p = jnp.exp2(s)
            psum = colsum8(p)
            pv = _dot(vt_ref[hh, j], p.astype(BF16))
            if l is None:
                l, acc = psum, pv
            elif online:
                alpha = jnp.exp2(m - m_new)
                l, acc = alpha * l + psum, alpha * acc + pv
            else:
                l, acc = l + psum, acc + pv
            if online:
                m = m_new
        l = jnp.sum(l, axis=0, keepdims=True)
        o_t = acc[:, 0:t] / l[:, 0:t] - lam * (acc[:, t:2 * t] / l[:, t:2 * t])
        o_ref[row0:row0 + t, hh * HEAD_W:(hh + 1) * HEAD_W] = (
            _rms(o_t.T, sg_ref[...]) * (1.0 - LAMBDA_INIT)).astype(BF16)

    def query_group(ig):
        for hh in range(hps):
            for u in range(qpg):
                query_block(hh, ig * qpg + u, u * t)

    for ig in range(nkb // qpg):
        pl.when(i == ig)(functools.partial(query_group, ig))


def _t5_bucket(dist):
    n = jnp.maximum(dist, 0)
    max_exact = REL_BUCKETS // 2
    nf = jnp.maximum(n, 1).astype(F32)
    large = max_exact + (jnp.log(nf / max_exact) / math.log(REL_MAX_DIST / max_exact)
                         * (REL_BUCKETS - max_exact)).astype(jnp.int32)
    large = jnp.minimum(large, REL_BUCKETS - 1)
    return jnp.where(n < max_exact, n, large)


def _bias_generators(f, t):
    hole = jnp.full((HEADS, 1), -jnp.inf, F32)
    w0 = jnp.concatenate([f[:, 0:t], jnp.full((HEADS, t), -jnp.inf, F32)], axis=1)
    w1 = jnp.concatenate([f[:, t:2 * t], hole, f[:, 1:t]], axis=1)
    return jnp.stack([w0, w1], axis=1)


def _attn(qkv, lam, rel_bias, qk_bound, subln, *, batch, seq, t, qpg, hps):
    assert t >= REL_MAX_DIST
    nq = seq // t
    rb = rel_bias.astype(F32)
    spread = 2.0 * qk_bound + jnp.max(jnp.max(rb, axis=0) - jnp.min(rb, axis=0))
    static_ok = spread <= MAX_STATIC_SOFTMAX_SPREAD
    upper = jnp.where(static_ok, qk_bound + jnp.max(rb, axis=0), 0.0)
    f = rb[_t5_bucket(jnp.arange(2 * t))].T - upper[:, None]
    bias = _bias_generators(f * LOG2E, t)
    far_off = (rb[REL_BUCKETS - 1] - upper) * LOG2E

    assert nq % qpg == 0 and HEADS % hps == 0
    ng = nq // qpg
    hgs = HEADS // hps

    def call(online):
        return pl.pallas_call(
            functools.partial(_attn_kernel, t=t, nkb=nq, qpg=qpg, hps=hps, online=online),
            out_shape=jax.ShapeDtypeStruct((batch * seq, MIX_W), BF16),
            grid=(hgs, batch, ng),
            in_specs=[
                pl.BlockSpec(memory_space=pltpu.SMEM),
                pl.BlockSpec(memory_space=pltpu.SMEM),
                pl.BlockSpec((1, hps, qpg * t, HEAD_W), lambda h, b, i: (b, h, i, 0)),
                pl.BlockSpec((1, hps, seq, HEAD_W), lambda h, b, i: (b, hgs + h, 0, 0)),
                pl.BlockSpec((1, hps, seq, HEAD_W), lambda h, b, i: (b, 2 * hgs + h, 0, 0)),
                pl.BlockSpec((hps, 2, 2 * t), lambda h, b, i: (h, 0, 0)),
                pl.BlockSpec((1, HEAD_W), lambda h, b, i: (0, 0)),
            ],
            out_specs=pl.BlockSpec((qpg * t, hps * HEAD_W), lambda h, b, i: (b * ng + i, h)),
            scratch_shapes=[pltpu.VMEM((hps, nq, HEAD_W, t), BF16),
                            pltpu.VMEM((hps, 2, t, t), F32)],
            compiler_params=_cparams(("arbitrary", "arbitrary", "arbitrary")),
            name="attn_online" if online else "attn",
        )(lam.reshape(1), far_off, qkv, qkv, qkv, bias, subln.reshape(1, HEAD_W))

    return lax.cond(static_ok, lambda: call(False), lambda: call(True))


SUBLANES = 8


def _hgrn_tables(c):
    levels = int(math.log2(c))
    assert 2 ** levels == c
    t = np.arange(c)[:, None]
    s = np.arange(c)[None, :]
    blocks = [(s <= t).astype(np.float32)]
    level_id = np.where(t == s, 0, -1)
    n_fine = 0
    for l in range(1, levels + 1):
        g, half = 2 ** l, 2 ** (l - 1)
        mid = (t // g) * g + half - 1
        upper = (t % g) >= half
        if half < SUBLANES:
            above = upper & (s > mid) & (s <= t)
            below = (~upper) & (s > t) & (s <= mid)
            blocks.append((above | below).astype(np.float32))
            n_fine += 1
        pair = (t // g == s // g) & upper & ((s % g) < half)
        level_id = np.where(pair, l, level_id)
    w = np.concatenate(blocks, axis=0)
    return jnp.asarray(w, BF16), jnp.asarray(level_id, jnp.int32), levels, n_fine


def _hgrn_kernel(w_ref, lvl_ref, q_ref, k_ref, v_ref, og_ref, lf_ref, gn_ref, o_ref, *,
                 c, levels, n_fine, seq, unroll, hpg):
    gain = gn_ref[...]

    def cumsums(hh, ci):
        rows = pl.ds(pl.multiple_of(ci * c, c), c)
        g = lf_ref[0, hh, rows, :]
        g_hi = g.astype(BF16)
        g_lo = (g - g_hi.astype(F32)).astype(BF16)
        e2 = _dot(w_ref[...], jnp.concatenate([g_hi, g_lo], axis=1))
        return hh, rows, e2[:, 0:HEAD_W] + e2[:, HEAD_W:2 * HEAD_W]

    def scores(hh, rows, e):
        qb = q_ref[0, hh, rows, :]
        kb = k_ref[0, hh, rows, :]
        q = qb.astype(F32)
        k = kb.astype(F32)
        b = e[0:c]
        lvl = lvl_ref[...]
        pairs = [(qb, kb)]
        for l in range(1, levels + 1):
            if l <= n_fine:
                d = e[l * c:(l + 1) * c]
            else:
                g, half = 2 ** l, 2 ** (l - 1)
                b3 = b.reshape(c // g, g, HEAD_W)
                mid = b3[:, half - 1:half, :]
                d = jnp.concatenate([mid - b3[:, 0:half, :], b3[:, half:g, :] - mid],
                                    axis=1).reshape(c, HEAD_W)
            x = jnp.exp2(d)
            pairs.append(((q * x).astype(BF16), (k * x).astype(BF16)))
        a = jnp.zeros((c, c), F32)
        zero = jnp.zeros((c, HEAD_W), BF16)
        for l in range(0, levels + 1, 2):
            (qa, ka), (qc, kc) = pairs[l], pairs[l + 1]
            keys = jnp.concatenate([jnp.concatenate([ka, zero], axis=1),
                                    jnp.concatenate([zero, kc], axis=1)], axis=0)
            r = _dot_nt(jnp.concatenate([qa, qc], axis=1), keys)
            a = jnp.where(lvl == l, r[:, 0:c], a)
            a = jnp.where(lvl == l + 1, r[:, c:2 * c], a)
        b_last = b[c - 1:c, :]
        q_in = (q * jnp.exp2(b)).astype(BF16)
        k_out = (k * jnp.exp2(b_last - b)).astype(BF16)
        return a.astype(BF16), q_in, k_out, jnp.exp2(b_last)

    def outputs(hh, rows, a, q_in, k_out, decay):
        v = v_ref[0, hh, rows, :]
        return hh, rows, _dot(a, v), q_in, _dot_tn(v, k_out), decay

    def body(it, states):
        units = [(hh, it * unroll + u) for u in range(unroll) for hh in range(hpg)]
        st1, st2, st3 = {}, {}, []
        for n in range(len(units) + 2):
            if n < len(units):
                st1[n] = cumsums(*units[n])
            if 0 <= n - 1 < len(units):
                st2[n - 1] = scores(*st1[n - 1])
            if 0 <= n - 2 < len(units):
                st3.append(outputs(*st1[n - 2][0:2], *st2[n - 2]))
        states = list(states)
        for hh, rows, o_intra, q_in, ds_t, decay in st3:
            o = o_intra + _dot_nt(q_in, states[hh].astype(BF16))
            states[hh] = states[hh] * decay + ds_t
            y = _rms(o, gain) * og_ref[0, hh, rows, :].astype(F32)
            o_ref[rows, hh * HEAD_W:(hh + 1) * HEAD_W] = y.astype(BF16)
        return tuple(states)

    lax.fori_loop(0, seq // (c * unroll), body,
                  tuple(jnp.zeros((HEAD_W, HEAD_W), F32) for _ in range(hpg)))


def _hgrn(hin, logf, gain, *, batch, seq, c, unroll, hpg):
    w, lvl, levels, n_fine = _hgrn_tables(c)
    assert seq % (c * unroll) == 0 and HEADS % hpg == 0
    groups = HEADS // hpg

    def head(slab):
        return pl.BlockSpec((1, hpg, seq, HEAD_W), lambda b, h: (b, slab * groups + h, 0, 0))

    return pl.pallas_call(
        functools.partial(_hgrn_kernel, c=c, levels=levels, n_fine=n_fine, seq=seq, unroll=unroll,
                          hpg=hpg),
        out_shape=jax.ShapeDtypeStruct((batch * seq, MIX_W), BF16),
        grid=(batch, groups),
        in_specs=[
            pl.BlockSpec(w.shape, lambda b, h: (0, 0)),
            pl.BlockSpec((c, c), lambda b, h: (0, 0)),
            head(0), head(1), head(2), head(3), head(0),
            pl.BlockSpec((1, HEAD_W), lambda b, h: (0, 0)),
        ],
        out_specs=pl.BlockSpec((seq, hpg * HEAD_W), lambda b, h: (b, h)),
        compiler_params=_cparams(("parallel", "parallel")),
        name="hgrn",
    )(w, lvl, hin, hin, hin, hin, logf, gain.reshape(1, HEAD_W))


def _merge_kernel(x_ref, ya_ref, yb_ref, gt_ref, wa_ref, wb_ref, wo_ref, *rest, n_side):
    side_in, (o_ref,), side_out = rest[:n_side], rest[n_side:n_side + 1], rest[n_side + 1:]
    d = x_ref.shape[1]
    gt = gt_ref[...].astype(F32)
    merged = gt[:, 0:d] * _dot(ya_ref[...], wa_ref[...]) + gt[:, d:2 * d] * _dot(yb_ref[...], wb_ref[...])
    o_ref[...] = x_ref[...] + _dot(merged.astype(BF16), wo_ref[...])
    _SideCast.run(side_in, side_out)


def _resident(shape):
    return pl.BlockSpec(shape, lambda i: (0,) * len(shape), pipeline_mode=pl.Buffered(1))


def _merge(x, ya, yb, gates, wa, wb, wo, side_weights=(), *, tm):
    m, d = x.shape
    side = _SideCast(side_weights, m // tm, lambda i: i)
    out = pl.pallas_call(
        functools.partial(_merge_kernel, n_side=side.n),
        out_shape=[jax.ShapeDtypeStruct((m, d), F32)] + side.out_shapes,
        grid=(m // tm,),
        in_specs=[
            pl.BlockSpec((tm, d), lambda i: (i, 0)),
            pl.BlockSpec((tm, MIX_W), lambda i: (i, 0)),
            pl.BlockSpec((tm, MIX_W), lambda i: (i, 0)),
            pl.BlockSpec((tm, 2 * d), lambda i: (i, 0)),
            _resident(wa.shape), _resident(wb.shape), _resident(wo.shape),
        ] + side.in_specs,
        out_specs=[pl.BlockSpec((tm, d), lambda i: (i, 0))] + side.out_specs,
        compiler_params=_cparams(("arbitrary",)),
        name="merge",
    )(x, ya, yb, gates, wa, wb, wo, *side.arrays)
    return out[0], out[1:]


def _ple_kernel(x_ref, p_ref, gg_ref, pg_ref, wg_ref, wp_ref, o_ref):
    ple = _rms(_dot(p_ref[...].astype(BF16), wp_ref[...]), pg_ref[...])
    x = x_ref[...]
    gate = _sigmoid(_dot(_rms(x, gg_ref[...]).astype(BF16), wg_ref[...]))
    o_ref[...] = x + gate * ple


def _ple(x, p, gate_gain, post_gain, w_gate, w_proj, *, tm):
    m, d = x.shape
    pd = p.shape[1]
    return pl.pallas_call(
        _ple_kernel,
        out_shape=jax.ShapeDtypeStruct((m, d), F32),
        grid=(m // tm,),
        in_specs=[
            pl.BlockSpec((tm, d), lambda i: (i, 0)),
            pl.BlockSpec((tm, pd), lambda i: (i, 0)),
            pl.BlockSpec((1, d), lambda i: (0, 0)),
            pl.BlockSpec((1, d), lambda i: (0, 0)),
            _resident(w_gate.shape), _resident(w_proj.shape),
        ],
        out_specs=pl.BlockSpec((tm, d), lambda i: (i, 0)),
        compiler_params=_cparams(("parallel",)),
        name="ple",
    )(x, p, gate_gain.reshape(1, d), post_gain.reshape(1, d), w_gate, w_proj)


def _tiles(seq, d, f):
    def pick(n, choices):
        return next(c for c in choices if n % c == 0)
    return dict(
        tm=pick(seq, (512, 256, 128)),
        tfm=pick(seq, (1024, 512, 256, 128)),
        tf=pick(f, (512, 256, 128)),
        tp=pick(seq, (1024, 512, 256, 128)),
        tn=pick(math.gcd(MIX_W, 2 * d), (512, 256, 128)),
        ta=pick(seq, (512, 256, 128)),
        qpg=4,
        hps=4,
        c=128,
        cu=16,
        hpg=4,
    )


def kernel(x, p, ffn1_norm, ffn1_w_gate, ffn1_w_up, ffn1_w_down, mix_norm, w_in, q_norm, k_norm, lambda_q1, lambda_k1, lambda_q2, lambda_k2, diff_subln, rel_bias, hgrn_lb_logits, hgrn_norm, w_branch_a, w_branch_b, w_out, ffn2_norm, ffn2_w_gate, ffn2_w_up, ffn2_w_down, ple_gate_norm, w_ple_gate, w_ple_proj, ple_post_norm):
    batch, seq, d = x.shape
    depth = ffn1_norm.shape[0]
    assert depth == 1
    m = batch * seq
    tl = _tiles(seq, d, ffn1_w_gate.shape[2])
    bf = lambda w: w.astype(BF16)

    lower_bounds = jnp.cumsum(jax.nn.softmax(hgrn_lb_logits.astype(F32), axis=0), axis=0)
    lb = lower_bounds[0].reshape(HEADS, HEAD_W)
    lam = (jnp.exp(jnp.sum(lambda_q1[0].astype(F32) * lambda_k1[0].astype(F32)))
           - jnp.exp(jnp.sum(lambda_q2[0].astype(F32) * lambda_k2[0].astype(F32)))
           + LAMBDA_INIT)
    scale = QK_DIM ** -0.5
    qg = jnp.tile(q_norm[0].astype(F32), 2).reshape(1, HEAD_W) * (scale * LOG2E)
    kg = jnp.tile(k_norm[0].astype(F32), 2).reshape(1, HEAD_W)
    qk_bound = (QK_DIM * scale * jnp.max(jnp.abs(q_norm[0].astype(F32)))
                * jnp.max(jnp.abs(k_norm[0].astype(F32))))

    x2 = x.reshape(m, d)
    x2, (w_in_b, wa_b, wb_b, wo_b) = _ffn(
        x2, ffn1_norm[0], bf(ffn1_w_gate[0]), bf(ffn1_w_up[0]), bf(ffn1_w_down[0]),
        (w_in[0], w_branch_a[0], w_branch_b[0], w_out[0]), tm=tl["tfm"], tf=tl["tf"])

    qkv, hin, logf, gates, (wg2_b, wu2_b, wd2_b) = _proj(
        x2, mix_norm[0], w_in_b, qg, kg, lb, (ffn2_w_gate[0], ffn2_w_up[0], ffn2_w_down[0]),
        batch=batch, seq=seq, tm=tl["tp"], tn=tl["tn"])
    ya = _attn(qkv, lam, rel_bias, qk_bound, diff_subln[0], batch=batch, seq=seq,
               t=tl["ta"], qpg=tl["qpg"], hps=tl["hps"])
    yb = _hgrn(hin, logf, hgrn_norm[0], batch=batch, seq=seq, c=tl["c"], unroll=tl["cu"],
               hpg=tl["hpg"])
    x2, (wpg_b, wpp_b) = _merge(x2, ya, yb, gates, wa_b, wb_b, wo_b,
                                (w_ple_gate[0], w_ple_proj[0]), tm=tl["tm"])

    x2, _ = _ffn(x2, ffn2_norm[0], wg2_b, wu2_b, wd2_b, tm=tl["tfm"], tf=tl["tf"])
    x2 = _ple(x2, p[0].reshape(m, -1), ple_gate_norm[0], ple_post_norm[0], wpg_b, wpp_b, tm=tl["tm"])
    return x2.reshape(batch, seq, d)
```

```python
import functools
import math

import jax
import jax.numpy as jnp
import numpy as np
from jax import lax
from jax.experimental import pallas as pl
from jax.experimental.pallas import tpu as pltpu

F32 = jnp.float32
BF16 = jnp.bfloat16

EPS = 1e-6
HEADS = 8
HEAD_W = 128
QK_DIM = 64
MIX_W = HEADS * HEAD_W
REL_BUCKETS = 32
REL_MAX_DIST = 128
LAMBDA_INIT = 0.8 - 0.6 * math.exp(-0.3 * 0)
LOG2E = math.log2(math.e)
MAX_STATIC_SOFTMAX_SPREAD = 60.0

V7X_MXU_WIDTH = 256
V7X_VMEM_BYTES = 64 * 1024 * 1024
VMEM_LIMIT = V7X_VMEM_BYTES * 7 // 8


def _cparams(sem):
    return pltpu.CompilerParams(dimension_semantics=sem, vmem_limit_bytes=VMEM_LIMIT)


def _sigmoid(x):
    return 1.0 / (1.0 + jnp.exp(-x))


def _rms(x, gain):
    ms = jnp.mean(x * x, axis=-1, keepdims=True)
    return x * lax.rsqrt(ms + EPS) * gain


def _dot(a, b):
    return jnp.dot(a, b, preferred_element_type=F32)


def _dot_nt(a, b):
    return lax.dot_general(a, b, (((1,), (1,)), ((), ())), preferred_element_type=F32)


def _dot_tn(a, b):
    return lax.dot_general(a, b, (((0,), (0,)), ((), ())), preferred_element_type=F32)


BF16_ROW_TILE = 16


class _SideCast:
    def __init__(self, arrays, steps, step_index):
        self.arrays = list(arrays)
        self.n = len(self.arrays)
        self.step_index = step_index
        self.chunks = []
        for w in self.arrays:
            rows = w.shape[0]
            self.chunks.append(max(c for c in range(1, steps + 1)
                                   if rows % c == 0 and (rows // c) % BF16_ROW_TILE == 0))

    def _specs(self):
        def spec(w, c):
            return pl.BlockSpec((w.shape[0] // c, w.shape[1]),
                                lambda *ids: (jnp.minimum(self.step_index(*ids), c - 1), 0))
        return [spec(w, c) for w, c in zip(self.arrays, self.chunks)]

    in_specs = property(_specs)
    out_specs = property(_specs)

    @property
    def out_shapes(self):
        return [jax.ShapeDtypeStruct(w.shape, BF16) for w in self.arrays]

    @staticmethod
    def run(in_refs, out_refs):
        for src, dst in zip(in_refs, out_refs):
            dst[...] = src[...].astype(BF16)


def _row_tile_fetch(x_hbm, xs_ref, sem, tm):
    i = pl.program_id(0)
    j = pl.program_id(1)

    def copy(row_tile):
        return pltpu.make_async_copy(x_hbm.at[pl.ds(row_tile * tm, tm), :], xs_ref, sem)

    @pl.when((i == 0) & (j == 0))
    def _():
        copy(0).start()

    @pl.when((j == 1) & (i + 1 < pl.num_programs(0)))
    def _():
        copy(i + 1).start()

    return copy(i)


def _ffn_kernel(x_hbm, g_ref, wg_ref, wu_ref, wd_ref, *rest, tm, n_side):
    side_in, (o_ref,), side_out = rest[:n_side], rest[n_side:n_side + 1], rest[n_side + 1:2 * n_side + 1]
    xs_ref, h_ref, sem = rest[2 * n_side + 1:]
    j = pl.program_id(1)
    x_copy = _row_tile_fetch(x_hbm, xs_ref, sem, tm)

    @pl.when(j == 0)
    def _():
        x_copy.wait()
        x = xs_ref[...]
        h_ref[...] = _rms(x, g_ref[...]).astype(BF16)
        o_ref[...] = x

    h = h_ref[...]
    gate = _dot(h, wg_ref[...])
    up = _dot(h, wu_ref[...])
    act = (gate * _sigmoid(gate) * (0.5 * up)).astype(BF16)
    o_ref[...] += _dot(act, wd_ref[...])
    _SideCast.run(side_in, side_out)


def _ffn(x, gain, w_gate, w_up, w_down, side_weights=(), *, tm, tf):
    m, d = x.shape
    f = w_gate.shape[1]
    nj = f // tf
    assert nj >= 2
    side = _SideCast(side_weights, (m // tm) * nj, lambda i, j: i * nj + j)
    out = pl.pallas_call(
        functools.partial(_ffn_kernel, tm=tm, n_side=side.n),
        out_shape=[jax.ShapeDtypeStruct((m, d), F32)] + side.out_shapes,
        grid=(m // tm, nj),
        in_specs=[
            pl.BlockSpec(memory_space=pl.ANY),
            pl.BlockSpec((1, d), lambda i, j: (0, 0)),
            pl.BlockSpec((d, tf), lambda i, j: (0, j)),
            pl.BlockSpec((d, tf), lambda i, j: (0, j)),
            pl.BlockSpec((tf, d), lambda i, j: (j, 0)),
        ] + side.in_specs,
        out_specs=[pl.BlockSpec((tm, d), lambda i, j: (i, 0))] + side.out_specs,
        scratch_shapes=[pltpu.VMEM((tm, d), F32), pltpu.VMEM((tm, d), BF16),
                        pltpu.SemaphoreType.DMA(())],
        compiler_params=_cparams(("arbitrary", "arbitrary")),
        name="ffn",
    )(x, gain.reshape(1, d), w_gate, w_up, w_down, *side.arrays)
    return out[0], out[1:]


def _proj_kernel(x_hbm, g_ref, w_ref, qg_ref, kg_ref, lb_ref, gm_ref, *rest, tm, tn, nj, n_side):
    side_in, rest = rest[:n_side], rest[n_side:]
    (oqkv_ref, oh_ref, olf_ref, og_ref), rest = rest[:4], rest[4:]
    side_out, (xs_ref, h_ref, acc_ref, sem) = rest[:n_side], rest[n_side:]
    j = pl.program_id(1)
    jp = j - 1
    tps = MIX_W // tn
    hpt = tn // HEAD_W

    def heads(vals):
        return [vals[:, s * HEAD_W:(s + 1) * HEAD_W] for s in range(hpt)]

    def epi_qk(acc):
        gain = jnp.where(jp < tps, qg_ref[...], kg_ref[...])
        pw = gm_ref.shape[0]
        for s in range(tn // pw):
            a = acc[:, s * pw:(s + 1) * pw]
            ms = _dot((a * a).astype(BF16), gm_ref[...])
            y = a * lax.rsqrt(ms + EPS)
            for u in range(pw // HEAD_W):
                oqkv_ref[0, s * (pw // HEAD_W) + u] = (
                    y[:, u * HEAD_W:(u + 1) * HEAD_W] * gain).astype(BF16)

    def epi_v(acc):
        for s, a in enumerate(heads(acc)):
            oqkv_ref[0, s] = a.astype(BF16)

    def epi_silu(acc):
        for s, a in enumerate(heads(acc * _sigmoid(acc))):
            oh_ref[0, s] = a.astype(BF16)

    def epi_forget(acc):
        jj = jp - 4 * tps
        for s, a in enumerate(heads(acc)):
            lb = lb_ref[pl.ds(jj * hpt + s, 1), :]
            forget = lb + (1.0 - lb) * _sigmoid(a)
            oh_ref[0, s] = (1.0 - forget).astype(BF16)
            olf_ref[0, s] = jnp.log2(forget)

    def epi_hv(acc):
        for s, a in enumerate(heads(acc)):
            oh_ref[0, s] = a.astype(BF16)

    def epi_gates(acc):
        og_ref[...] = _sigmoid(acc).astype(BF16)

    x_copy = _row_tile_fetch(x_hbm, xs_ref, sem, tm)

    @pl.when(j == 0)
    def _():
        x_copy.wait()
        h_ref[...] = _rms(xs_ref[...], g_ref[...]).astype(BF16)
        acc_ref[...] = _dot(h_ref[...], w_ref[...])
        _SideCast.run(side_in, side_out)

    def fused(cond, epi):
        @pl.when(cond & (j >= 1) & (j < nj))
        def _():
            acc = acc_ref[...]
            epi(acc)
            acc_ref[...] = _dot(h_ref[...], w_ref[...])
            _SideCast.run(side_in, side_out)

    fused(jp < 2 * tps, epi_qk)
    fused((jp >= 2 * tps) & (jp < 3 * tps), epi_v)
    fused(((jp >= 3 * tps) & (jp < 4 * tps)) | ((jp >= 6 * tps) & (jp < 7 * tps)), epi_silu)
    fused((jp >= 4 * tps) & (jp < 5 * tps), epi_forget)
    fused((jp >= 5 * tps) & (jp < 6 * tps), epi_hv)
    fused(jp >= 7 * tps, epi_gates)

    @pl.when(j == nj)
    def _():
        epi_gates(acc_ref[...])
        _SideCast.run(side_in, side_out)


def _proj(x, gain, w_in, qg, kg, lb, side_weights=(), *, batch, seq, tm, tn):
    m, d = x.shape
    n_in = w_in.shape[1]
    tps = MIX_W // tn
    hpt = tn // HEAD_W
    spb = seq // tm
    ng = 2 * d // tn
    nj = 7 * tps + ng
    assert n_in == 7 * MIX_W + 2 * d and MIX_W % tn == 0 and (2 * d) % tn == 0

    pw = math.gcd(tn, V7X_MXU_WIDTH)
    lane = np.arange(pw)
    gm = (lane[:, None] // QK_DIM == lane[None, :] // QK_DIM).astype(np.float32) / QK_DIM

    def hm_block(lo, n):
        return lambda i, j: (i // spb, jnp.clip(j - 1 - lo, 0, n - 1), i % spb, 0)

    side = _SideCast(side_weights, (m // tm) * (nj + 1), lambda i, j: i * (nj + 1) + j)
    out = pl.pallas_call(
        functools.partial(_proj_kernel, tm=tm, tn=tn, nj=nj, n_side=side.n),
        out_shape=[
            jax.ShapeDtypeStruct((batch, 3 * HEADS, seq, HEAD_W), BF16),
            jax.ShapeDtypeStruct((batch, 4 * HEADS, seq, HEAD_W), BF16),
            jax.ShapeDtypeStruct((batch, HEADS, seq, HEAD_W), F32),
            jax.ShapeDtypeStruct((m, 2 * d), BF16),
        ] + side.out_shapes,
        grid=(m // tm, nj + 1),
        in_specs=[
            pl.BlockSpec(memory_space=pl.ANY),
            pl.BlockSpec((1, d), lambda i, j: (0, 0)),
            pl.BlockSpec((d, tn), lambda i, j: (0, jnp.minimum(j, nj - 1))),
            pl.BlockSpec((1, HEAD_W), lambda i, j: (0, 0)),
            pl.BlockSpec((1, HEAD_W), lambda i, j: (0, 0)),
            pl.BlockSpec((HEADS, HEAD_W), lambda i, j: (0, 0)),
            pl.BlockSpec((pw, pw), lambda i, j: (0, 0)),
        ] + side.in_specs,
        out_specs=[
            pl.BlockSpec((1, hpt, tm, HEAD_W), hm_block(0, 3 * tps)),
            pl.BlockSpec((1, hpt, tm, HEAD_W), hm_block(3 * tps, 4 * tps)),
            pl.BlockSpec((1, hpt, tm, HEAD_W), hm_block(4 * tps, tps)),
            pl.BlockSpec((tm, tn), lambda i, j: (i, jnp.clip(j - 1 - 7 * tps, 0, ng - 1))),
        ] + side.out_specs,
        scratch_shapes=[pltpu.VMEM((tm, d), F32), pltpu.VMEM((tm, d), BF16),
                        pltpu.VMEM((tm, tn), F32), pltpu.SemaphoreType.DMA(())],
        compiler_params=_cparams(("arbitrary", "arbitrary")),
        name="proj",
    )(x, gain.reshape(1, d), w_in, qg, kg, lb, jnp.asarray(gm, BF16), *side.arrays)
    return out[0], out[1], out[2], out[3], out[4:]


def _attn_kernel(lam_ref, off_ref, q_ref, k_ref, v_ref, w_ref, sg_ref, o_ref, vt_ref, bias_ref, *,
                 t, nkb, qpg, hps, online):
    hg = pl.program_id(0)
    b = pl.program_id(1)
    i = pl.program_id(2)
    lam = lam_ref[0]

    @pl.when((b == 0) & (i == 0))
    def _():
        for hh in range(hps):
            for dd in range(2):
                rows = jnp.broadcast_to(w_ref[hh, dd:dd + 1, :], (t, 2 * t))
                bias_ref[hh, dd] = pltpu.roll(rows, 0, 1, stride=1, stride_axis=0)[:, 0:t]

    @pl.when(i == 0)
    def _():
        for hh in range(hps):
            for jb in range(nkb):
                vt_ref[hh, jb] = v_ref[0, hh, jb * t:(jb + 1) * t, :].astype(F32).T.astype(BF16)

    def colsum8(p):
        return jnp.sum(p.reshape(t // 8, 8, 2 * t), axis=0)

    def both(x):
        return jnp.concatenate([x, x], axis=1)

    def query_block(hh, iq, row0):
        far_off = off_ref[hg * hps + hh]
        q = q_ref[0, hh, row0:row0 + t, :]
        lane = lax.broadcasted_iota(jnp.int32, (t, HEAD_W), 1)
        zero = jnp.zeros_like(q)
        qs = jnp.concatenate([jnp.where(lane < QK_DIM, q, zero),
                              jnp.where(lane >= QK_DIM, q, zero)], axis=0)
        offsets = [far_off] * max(iq - 1, 0)
        if iq >= 1:
            offsets.append(both(bias_ref[hh, 1]))
        offsets.append(both(bias_ref[hh, 0]))
        m = l = acc = None
        for j, badd in enumerate(offsets):
            s = _dot_nt(k_ref[0, hh, j * t:(j + 1) * t, :], qs) + badd
            if online:
                bmax = jnp.max(s, axis=0, keepdims=True)
                m_new = bmax if m is None else jnp.maximum(m, bmax)
                p = jnp.exp2(s - m_new)
            else:
                p = jnp.exp2(s)
            psum = colsum8(p)
            pv = _dot(vt_ref[hh, j], p.astype(BF16))
            if l is None:
                l, acc = psum, pv
            elif online:
                alpha = jnp.exp2(m - m_new)
                l, acc = alpha * l + psum, alpha * acc + pv
            else:
                l, acc = l + psum, acc + pv
            if online:
                m = m_new
        l = jnp.sum(l, axis=0, keepdims=True)
        o_t = acc[:, 0:t] / l[:, 0:t] - lam * (acc[:, t:2 * t] / l[:, t:2 * t])
        o_ref[row0:row0 + t, hh * HEAD_W:(hh + 1) * HEAD_W] = (
            _rms(o_t.T, sg_ref[...]) * (1.0 - LAMBDA_INIT)).astype(BF16)

    def query_group(ig):
        for hh in range(hps):
            for u in range(qpg):
                query_block(hh, ig * qpg + u, u * t)

    for ig in range(nkb // qpg):
        pl.when(i == ig)(functools.partial(query_group, ig))


def _t5_bucket(dist):
    n = jnp.maximum(dist, 0)
    max_exact = REL_BUCKETS // 2
    nf = jnp.maximum(n, 1).astype(F32)
    large = max_exact + (jnp.log(nf / max_exact) / math.log(REL_MAX_DIST / max_exact)
                         * (REL_BUCKETS - max_exact)).astype(jnp.int32)
    large = jnp.minimum(large, REL_BUCKETS - 1)
    return jnp.where(n < max_exact, n, large)


def _bias_generators(f, t):
    hole = jnp.full((HEADS, 1), -jnp.inf, F32)
    w0 = jnp.concatenate([f[:, 0:t], jnp.full((HEADS, t), -jnp.inf, F32)], axis=1)
    w1 = jnp.concatenate([f[:, t:2 * t], hole, f[:, 1:t]], axis=1)
    return jnp.stack([w0, w1], axis=1)


def _attn(qkv, lam, rel_bias, qk_bound, subln, *, batch, seq, t, qpg, hps):
    assert t >= REL_MAX_DIST
    nq = seq // t
    rb = rel_bias.astype(F32)
    spread = 2.0 * qk_bound + jnp.max(jnp.max(rb, axis=0) - jnp.min(rb, axis=0))
    static_ok = spread <= MAX_STATIC_SOFTMAX_SPREAD
    upper = jnp.where(static_ok, qk_bound + jnp.max(rb, axis=0), 0.0)
    f = rb[_t5_bucket(jnp.arange(2 * t))].T - upper[:, None]
    bias = _bias_generators(f * LOG2E, t)
    far_off = (rb[REL_BUCKETS - 1] - upper) * LOG2E

    assert nq % qpg == 0 and HEADS % hps == 0
    ng = nq // qpg
    hgs = HEADS // hps

    def call(online):
        return pl.pallas_call(
            functools.partial(_attn_kernel, t=t, nkb=nq, qpg=qpg, hps=hps, online=online),
            out_shape=jax.ShapeDtypeStruct((batch * seq, MIX_W), BF16),
            grid=(hgs, batch, ng),
            in_specs=[
                pl.BlockSpec(memory_space=pltpu.SMEM),
                pl.BlockSpec(memory_space=pltpu.SMEM),
                pl.BlockSpec((1, hps, qpg * t, HEAD_W), lambda h, b, i: (b, h, i, 0)),
                pl.BlockSpec((1, hps, seq, HEAD_W), lambda h, b, i: (b, hgs + h, 0, 0)),
                pl.BlockSpec((1, hps, seq, HEAD_W), lambda h, b, i: (b, 2 * hgs + h, 0, 0)),
                pl.BlockSpec((hps, 2, 2 * t), lambda h, b, i: (h, 0, 0)),
                pl.BlockSpec((1, HEAD_W), lambda h, b, i: (0, 0)),
            ],
            out_specs=pl.BlockSpec((qpg * t, hps * HEAD_W), lambda h, b, i: (b * ng + i, h)),
            scratch_shapes=[pltpu.VMEM((hps, nq, HEAD_W, t), BF16),
                            pltpu.VMEM((hps, 2, t, t), F32)],
            compiler_params=_cparams(("arbitrary", "arbitrary", "arbitrary")),
            name="attn_online" if online else "attn",
        )(lam.reshape(1), far_off, qkv, qkv, qkv, bias, subln.reshape(1, HEAD_W))

    return lax.cond(static_ok, lambda: call(False), lambda: call(True))


SUBLANES = 8


def _hgrn_tables(c):
    levels = int(math.log2(c))
    assert 2 ** levels == c
    t = np.arange(c)[:, None]
    s = np.arange(c)[None, :]
    blocks = [(s <= t).astype(np.float32)]
    level_id = np.where(t == s, 0, -1)
    n_fine = 0
    for l in range(1, levels + 1):
        g, half = 2 ** l, 2 ** (l - 1)
        mid = (t // g) * g + half - 1
        upper = (t % g) >= half
        if half < SUBLANES:
            above = upper & (s > mid) & (s <= t)
            below = (~upper) & (s > t) & (s <= mid)
            blocks.append((above | below).astype(np.float32))
            n_fine += 1
        pair = (t // g == s // g) & upper & ((s % g) < half)
        level_id = np.where(pair, l, level_id)
    w = np.concatenate(blocks, axis=0)
    return jnp.asarray(w, BF16), jnp.asarray(level_id, jnp.int32), levels, n_fine


def _hgrn_kernel(w_ref, lvl_ref, q_ref, k_ref, v_ref, og_ref, lf_ref, gn_ref, o_ref, *,
                 c, levels, n_fine, seq, unroll, hpg):
    gain = gn_ref[...]

    def cumsums(hh, ci):
        rows = pl.ds(pl.multiple_of(ci * c, c), c)
        g = lf_ref[0, hh, rows, :]
        g_hi = g.astype(BF16)
        g_lo = (g - g_hi.astype(F32)).astype(BF16)
        e2 = _dot(w_ref[...], jnp.concatenate([g_hi, g_lo], axis=1))
        return hh, rows, e2[:, 0:HEAD_W] + e2[:, HEAD_W:2 * HEAD_W]

    def scores(hh, rows, e):
        qb = q_ref[0, hh, rows, :]
        kb = k_ref[0, hh, rows, :]
        q = qb.astype(F32)
        k = kb.astype(F32)
        b = e[0:c]
        lvl = lvl_ref[...]
        pairs = [(qb, kb)]
        for l in range(1, levels + 1):
            if l <= n_fine:
                d = e[l * c:(l + 1) * c]
            else:
                g, half = 2 ** l, 2 ** (l - 1)
                b3 = b.reshape(c // g, g, HEAD_W)
                mid = b3[:, half - 1:half, :]
                d = jnp.concatenate([mid - b3[:, 0:half, :], b3[:, half:g, :] - mid],
                                    axis=1).reshape(c, HEAD_W)
            x = jnp.exp2(d)
            pairs.append(((q * x).astype(BF16), (k * x).astype(BF16)))
        a = jnp.zeros((c, c), F32)
        zero = jnp.zeros((c, HEAD_W), BF16)
        for l in range(0, levels + 1, 2):
            (qa, ka), (qc, kc) = pairs[l], pairs[l + 1]
            keys = jnp.concatenate([jnp.concatenate([ka, zero], axis=1),
                                    jnp.concatenate([zero, kc], axis=1)], axis=0)
            r = _dot_nt(jnp.concatenate([qa, qc], axis=1), keys)
            a = jnp.where(lvl == l, r[:, 0:c], a)
            a = jnp.where(lvl == l + 1, r[:, c:2 * c], a)
        b_last = b[c - 1:c, :]
        q_in = (q * jnp.exp2(b)).astype(BF16)
        k_out = (k * jnp.exp2(b_last - b)).astype(BF16)
        return a.astype(BF16), q_in, k_out, jnp.exp2(b_last)

    def outputs(hh, rows, a, q_in, k_out, decay):
        v = v_ref[0, hh, rows, :]
        return hh, rows, _dot(a, v), q_in, _dot_tn(v, k_out), decay

    def body(it, states):
        units = [(hh, it * unroll + u) for u in range(unroll) for hh in range(hpg)]
        st1, st2, st3 = {}, {}, []
        for n in range(len(units) + 2):
            if n < len(units):
                st1[n] = cumsums(*units[n])
            if 0 <= n - 1 < len(units):
                st2[n - 1] = scores(*st1[n - 1])
            if 0 <= n - 2 < len(units):
                st3.append(outputs(*st1[n - 2][0:2], *st2[n - 2]))
        states = list(states)
        for hh, rows, o_intra, q_in, ds_t, decay in st3:
            o = o_intra + _dot_nt(q_in, states[hh].astype(BF16))
            states[hh] = states[hh] * decay + ds_t
            y = _rms(o, gain) * og_ref[0, hh, rows, :].astype(F32)
            o_ref[rows, hh * HEAD_W:(hh + 1) * HEAD_W] = y.astype(BF16)
        return tuple(states)

    lax.fori_loop(0, seq // (c * unroll), body,
                  tuple(jnp.zeros((HEAD_W, HEAD_W), F32) for _ in range(hpg)))


def _hgrn(hin, logf, gain, *, batch, seq, c, unroll, hpg):
    w, lvl, levels, n_fine = _hgrn_tables(c)
    assert seq % (c * unroll) == 0 and HEADS % hpg == 0
    groups = HEADS // hpg

    def head(slab):
        return pl.BlockSpec((1, hpg, seq, HEAD_W), lambda b, h: (b, slab * groups + h, 0, 0))

    return pl.pallas_call(
        functools.partial(_hgrn_kernel, c=c, levels=levels, n_fine=n_fine, seq=seq, unroll=unroll,
                          hpg=hpg),
        out_shape=jax.ShapeDtypeStruct((batch * seq, MIX_W), BF16),
        grid=(batch, groups),
        in_specs=[
            pl.BlockSpec(w.shape, lambda b, h: (0, 0)),
            pl.BlockSpec((c, c), lambda b, h: (0, 0)),
            head(0), head(1), head(2), head(3), head(0),
            pl.BlockSpec((1, HEAD_W), lambda b, h: (0, 0)),
        ],
        out_specs=pl.BlockSpec((seq, hpg * HEAD_W), lambda b, h: (b, h)),
        compiler_params=_cparams(("parallel", "parallel")),
        name="hgrn",
    )(w, lvl, hin, hin, hin, hin, logf, gain.reshape(1, HEAD_W))


def _merge_kernel(x_ref, ya_ref, yb_ref, gt_ref, wa_ref, wb_ref, wo_ref, *rest, n_side):
    side_in, (o_ref,), side_out = rest[:n_side], rest[n_side:n_side + 1], rest[n_side + 1:]
    tm, d = x_ref.shape
    halves = [slice(r, r + tm // 2) for r in (0, tm // 2)]
    branches = [(_dot(ya_ref[rows, :], wa_ref[...]), _dot(yb_ref[rows, :], wb_ref[...]))
                for rows in halves]
    for rows, (ta, tb) in zip(halves, branches):
        gt = gt_ref[rows, :].astype(F32)
        merged = gt[:, 0:d] * ta + gt[:, d:2 * d] * tb
        o_ref[rows, :] = x_ref[rows, :] + _dot(merged.astype(BF16), wo_ref[...])
    _SideCast.run(side_in, side_out)


def _resident(shape):
    return pl.BlockSpec(shape, lambda i: (0,) * len(shape), pipeline_mode=pl.Buffered(1))


def _merge(x, ya, yb, gates, wa, wb, wo, side_weights=(), *, tm):
    m, d = x.shape
    side = _SideCast(side_weights, m // tm, lambda i: i)
    out = pl.pallas_call(
        functools.partial(_merge_kernel, n_side=side.n),
        out_shape=[jax.ShapeDtypeStruct((m, d), F32)] + side.out_shapes,
        grid=(m // tm,),
        in_specs=[
            pl.BlockSpec((tm, d), lambda i: (i, 0)),
            pl.BlockSpec((tm, MIX_W), lambda i: (i, 0)),
            pl.BlockSpec((tm, MIX_W), lambda i: (i, 0)),
            pl.BlockSpec((tm, 2 * d), lambda i: (i, 0)),
            _resident(wa.shape), _resident(wb.shape), _resident(wo.shape),
        ] + side.in_specs,
        out_specs=[pl.BlockSpec((tm, d), lambda i: (i, 0))] + side.out_specs,
        compiler_params=_cparams(("arbitrary",)),
        name="merge",
    )(x, ya, yb, gates, wa, wb, wo, *side.arrays)
    return out[0], out[1:]


def _ple_kernel(x_ref, p_ref, gg_ref, pg_ref, wg_ref, wp_ref, o_ref):
    ple = _rms(_dot(p_ref[...].astype(BF16), wp_ref[...]), pg_ref[...])
    x = x_ref[...]
    gate = _sigmoid(_dot(_rms(x, gg_ref[...]).astype(BF16), wg_ref[...]))
    o_ref[...] = x + gate * ple


def _ple(x, p, gate_gain, post_gain, w_gate, w_proj, *, tm):
    m, d = x.shape
    pd = p.shape[1]
    return pl.pallas_call(
        _ple_kernel,
        out_shape=jax.ShapeDtypeStruct((m, d), F32),
        grid=(m // tm,),
        in_specs=[
            pl.BlockSpec((tm, d), lambda i: (i, 0)),
            pl.BlockSpec((tm, pd), lambda i: (i, 0)),
            pl.BlockSpec((1, d), lambda i: (0, 0)),
            pl.BlockSpec((1, d), lambda i: (0, 0)),
            _resident(w_gate.shape), _resident(w_proj.shape),
        ],
        out_specs=pl.BlockSpec((tm, d), lambda i: (i, 0)),
        compiler_params=_cparams(("parallel",)),
        name="ple",
    )(x, p, gate_gain.reshape(1, d), post_gain.reshape(1, d), w_gate, w_proj)


def _tiles(seq, d, f):
    def pick(n, choices):
        return next(c for c in choices if n % c == 0)
    return dict(
        tm=pick(seq, (512, 256, 128)),
        tfm=pick(seq, (1024, 512, 256, 128)),
        tf=pick(f, (512, 256, 128)),
        tp=pick(seq, (1024, 512, 256, 128)),
        tn=pick(math.gcd(MIX_W, 2 * d), (1024, 512, 256, 128)),
        ta=pick(seq, (512, 256, 128)),
        qpg=4,
        hps=4,
        c=128,
        cu=16,
        hpg=4,
    )


def kernel(x, p, ffn1_norm, ffn1_w_gate, ffn1_w_up, ffn1_w_down, mix_norm, w_in, q_norm, k_norm, lambda_q1, lambda_k1, lambda_q2, lambda_k2, diff_subln, rel_bias, hgrn_lb_logits, hgrn_norm, w_branch_a, w_branch_b, w_out, ffn2_norm, ffn2_w_gate, ffn2_w_up, ffn2_w_down, ple_gate_norm, w_ple_gate, w_ple_proj, ple_post_norm):
    batch, seq, d = x.shape
    depth = ffn1_norm.shape[0]
    assert depth == 1
    m = batch * seq
    tl = _tiles(seq, d, ffn1_w_gate.shape[2])
    bf = lambda w: w.astype(BF16)

    lower_bounds = jnp.cumsum(jax.nn.softmax(hgrn_lb_logits.astype(F32), axis=0), axis=0)
    lb = lower_bounds[0].reshape(HEADS, HEAD_W)
    lam = (jnp.exp(jnp.sum(lambda_q1[0].astype(F32) * lambda_k1[0].astype(F32)))
           - jnp.exp(jnp.sum(lambda_q2[0].astype(F32) * lambda_k2[0].astype(F32)))
           + LAMBDA_INIT)
    scale = QK_DIM ** -0.5
    qg = jnp.tile(q_norm[0].astype(F32), 2).reshape(1, HEAD_W) * (scale * LOG2E)
    kg = jnp.tile(k_norm[0].astype(F32), 2).reshape(1, HEAD_W)
    qk_bound = (QK_DIM * scale * jnp.max(jnp.abs(q_norm[0].astype(F32)))
                * jnp.max(jnp.abs(k_norm[0].astype(F32))))

    x2 = x.reshape(m, d)
    x2, (w_in_b, wa_b, wb_b, wo_b) = _ffn(
        x2, ffn1_norm[0], bf(ffn1_w_gate[0]), bf(ffn1_w_up[0]), bf(ffn1_w_down[0]),
        (w_in[0], w_branch_a[0], w_branch_b[0], w_out[0]), tm=tl["tfm"], tf=tl["tf"])

    qkv, hin, logf, gates, (wg2_b, wu2_b, wd2_b) = _proj(
        x2, mix_norm[0], w_in_b, qg, kg, lb, (ffn2_w_gate[0], ffn2_w_up[0], ffn2_w_down[0]),
        batch=batch, seq=seq, tm=tl["tp"], tn=tl["tn"])
    ya = _attn(qkv, lam, rel_bias, qk_bound, diff_subln[0], batch=batch, seq=seq,
               t=tl["ta"], qpg=tl["qpg"], hps=tl["hps"])
    yb = _hgrn(hin, logf, hgrn_norm[0], batch=batch, seq=seq, c=tl["c"], unroll=tl["cu"],
               hpg=tl["hpg"])
    x2, (wpg_b, wpp_b) = _merge(x2, ya, yb, gates, wa_b, wb_b, wo_b,
                                (w_ple_gate[0], w_ple_proj[0]), tm=tl["tm"])

    x2, _ = _ffn(x2, ffn2_norm[0], wg2_b, wu2_b, wd2_b, tm=tl["tfm"], tf=tl["tf"])
    x2 = _ple(x2, p[0].reshape(m, -1), ple_gate_norm[0], ple_post_norm[0], wpg_b, wpp_b, tm=tl["tm"])
    return x2.reshape(batch, seq, d)
```

```python
import functools
import math

import jax
import jax.numpy as jnp
import numpy as np
from jax import lax
from jax.experimental import pallas as pl
from jax.experimental.pallas import tpu as pltpu

F32 = jnp.float32
BF16 = jnp.bfloat16

EPS = 1e-6
HEADS = 8
HEAD_W = 128
QK_DIM = 64
MIX_W = HEADS * HEAD_W
REL_BUCKETS = 32
REL_MAX_DIST = 128
LAMBDA_INIT = 0.8 - 0.6 * math.exp(-0.3 * 0)
LOG2E = math.log2(math.e)
MAX_STATIC_SOFTMAX_SPREAD = 60.0

V7X_MXU_WIDTH = 256
V7X_VMEM_BYTES = 64 * 1024 * 1024
VMEM_LIMIT = V7X_VMEM_BYTES * 7 // 8


def _cparams(sem):
    return pltpu.CompilerParams(dimension_semantics=sem, vmem_limit_bytes=VMEM_LIMIT)


def _sigmoid(x):
    return 1.0 / (1.0 + jnp.exp(-x))


def _rms(x, gain):
    ms = jnp.mean(x * x, axis=-1, keepdims=True)
    return x * lax.rsqrt(ms + EPS) * gain


def _dot(a, b):
    return jnp.dot(a, b, preferred_element_type=F32)


def _dot_nt(a, b):
    return lax.dot_general(a, b, (((1,), (1,)), ((), ())), preferred_element_type=F32)


def _dot_tn(a, b):
    return lax.dot_general(a, b, (((0,), (0,)), ((), ())), preferred_element_type=F32)


BF16_ROW_TILE = 16


class _SideCast:
    def __init__(self, arrays, steps, step_index):
        self.arrays = list(arrays)
        self.n = len(self.arrays)
        self.step_index = step_index
        self.chunks = []
        for w in self.arrays:
            rows = w.shape[0]
            self.chunks.append(max(c for c in range(1, steps + 1)
                                   if rows % c == 0 and (rows // c) % BF16_ROW_TILE == 0))

    def _specs(self):
        def spec(w, c):
            return pl.BlockSpec((w.shape[0] // c, w.shape[1]),
                                lambda *ids: (jnp.minimum(self.step_index(*ids), c - 1), 0))
        return [spec(w, c) for w, c in zip(self.arrays, self.chunks)]

    in_specs = property(_specs)
    out_specs = property(_specs)

    @property
    def out_shapes(self):
        return [jax.ShapeDtypeStruct(w.shape, BF16) for w in self.arrays]

    @staticmethod
    def run(in_refs, out_refs):
        for src, dst in zip(in_refs, out_refs):
            dst[...] = src[...].astype(BF16)


def _row_tile_fetch(x_hbm, xs_ref, sem, tm):
    i = pl.program_id(0)
    j = pl.program_id(1)

    def copy(row_tile):
        return pltpu.make_async_copy(x_hbm.at[pl.ds(row_tile * tm, tm), :], xs_ref, sem)

    @pl.when((i == 0) & (j == 0))
    def _():
        copy(0).start()

    @pl.when((j == 1) & (i + 1 < pl.num_programs(0)))
    def _():
        copy(i + 1).start()

    return copy(i)


def _ffn_kernel(x_hbm, g_ref, wg_ref, wu_ref, wd_ref, *rest, tm, n_side):
    side_in, (o_ref,), side_out = rest[:n_side], rest[n_side:n_side + 1], rest[n_side + 1:2 * n_side + 1]
    xs_ref, h_ref, sem = rest[2 * n_side + 1:]
    j = pl.program_id(1)
    x_copy = _row_tile_fetch(x_hbm, xs_ref, sem, tm)

    @pl.when(j == 0)
    def _():
        x_copy.wait()
        x = xs_ref[...]
        h_ref[...] = _rms(x, g_ref[...]).astype(BF16)
        o_ref[...] = x

    h = h_ref[...]
    gate = _dot(h, wg_ref[...])
    up = _dot(h, wu_ref[...])
    act = (gate * _sigmoid(gate) * (0.5 * up)).astype(BF16)
    o_ref[...] += _dot(act, wd_ref[...])
    _SideCast.run(side_in, side_out)


def _ffn(x, gain, w_gate, w_up, w_down, side_weights=(), *, tm, tf):
    m, d = x.shape
    f = w_gate.shape[1]
    nj = f // tf
    assert nj >= 2
    side = _SideCast(side_weights, (m // tm) * nj, lambda i, j: i * nj + j)
    out = pl.pallas_call(
        functools.partial(_ffn_kernel, tm=tm, n_side=side.n),
        out_shape=[jax.ShapeDtypeStruct((m, d), F32)] + side.out_shapes,
        grid=(m // tm, nj),
        in_specs=[
            pl.BlockSpec(memory_space=pl.ANY),
            pl.BlockSpec((1, d), lambda i, j: (0, 0)),
            pl.BlockSpec((d, tf), lambda i, j: (0, j)),
            pl.BlockSpec((d, tf), lambda i, j: (0, j)),
            pl.BlockSpec((tf, d), lambda i, j: (j, 0)),
        ] + side.in_specs,
        out_specs=[pl.BlockSpec((tm, d), lambda i, j: (i, 0))] + side.out_specs,
        scratch_shapes=[pltpu.VMEM((tm, d), F32), pltpu.VMEM((tm, d), BF16),
                        pltpu.SemaphoreType.DMA(())],
        compiler_params=_cparams(("arbitrary", "arbitrary")),
        name="ffn",
    )(x, gain.reshape(1, d), w_gate, w_up, w_down, *side.arrays)
    return out[0], out[1:]


def _proj_kernel(x_hbm, g_ref, w_ref, qg_ref, kg_ref, lb_ref, gm_ref, *rest, tm, tn, nj, lag, n_side):
    side_in, rest = rest[:n_side], rest[n_side:]
    (oqkv_ref, oh_ref, olf_ref, og_ref), rest = rest[:4], rest[4:]
    side_out, (xs_ref, h_ref, acc_ref, sem) = rest[:n_side], rest[n_side:]
    j = pl.program_id(1)
    jp = j - lag
    tps = MIX_W // tn
    hpt = tn // HEAD_W

    def heads(vals):
        return [vals[:, s * HEAD_W:(s + 1) * HEAD_W] for s in range(hpt)]

    def epi_qk(acc):
        gain = jnp.where(jp < tps, qg_ref[...], kg_ref[...])
        pw = gm_ref.shape[0]
        for s in range(tn // pw):
            a = acc[:, s * pw:(s + 1) * pw]
            ms = _dot((a * a).astype(BF16), gm_ref[...])
            y = a * lax.rsqrt(ms + EPS)
            for u in range(pw // HEAD_W):
                oqkv_ref[0, s * (pw // HEAD_W) + u] = (
                    y[:, u * HEAD_W:(u + 1) * HEAD_W] * gain).astype(BF16)

    def epi_v(acc):
        for s, a in enumerate(heads(acc)):
            oqkv_ref[0, s] = a.astype(BF16)

    def epi_silu(acc):
        for s, a in enumerate(heads(acc * _sigmoid(acc))):
            oh_ref[0, s] = a.astype(BF16)

    def epi_forget(acc):
        jj = jp - 4 * tps
        for s, a in enumerate(heads(acc)):
            lb = lb_ref[pl.ds(jj * hpt + s, 1), :]
            forget = lb + (1.0 - lb) * _sigmoid(a)
            oh_ref[0, s] = (1.0 - forget).astype(BF16)
            olf_ref[0, s] = jnp.log2(forget)

    def epi_hv(acc):
        for s, a in enumerate(heads(acc)):
            oh_ref[0, s] = a.astype(BF16)

    def epi_gates(acc):
        og_ref[...] = _sigmoid(acc).astype(BF16)

    x_copy = _row_tile_fetch(x_hbm, xs_ref, sem, tm)

    @pl.when(j == 0)
    def _():
        x_copy.wait()
        h_ref[...] = _rms(xs_ref[...], g_ref[...]).astype(BF16)
        if lag:
            acc_ref[...] = _dot(h_ref[...], w_ref[...])
            _SideCast.run(side_in, side_out)

    if not lag:
        acc_ref[...] = _dot(h_ref[...], w_ref[...])
        _SideCast.run(side_in, side_out)

    def fused(cond, epi):
        if not lag:
            pl.when(cond)(lambda: epi(acc_ref[...]))
            return

        @pl.when(cond & (j >= 1) & (j < nj))
        def _():
            acc = acc_ref[...]
            epi(acc)
            acc_ref[...] = _dot(h_ref[...], w_ref[...])
            _SideCast.run(side_in, side_out)

    fused(jp < 2 * tps, epi_qk)
    fused((jp >= 2 * tps) & (jp < 3 * tps), epi_v)
    fused(((jp >= 3 * tps) & (jp < 4 * tps)) | ((jp >= 6 * tps) & (jp < 7 * tps)), epi_silu)
    fused((jp >= 4 * tps) & (jp < 5 * tps), epi_forget)
    fused((jp >= 5 * tps) & (jp < 6 * tps), epi_hv)
    fused(jp >= 7 * tps, epi_gates)

    if lag:
        @pl.when(j == nj)
        def _():
            epi_gates(acc_ref[...])
            _SideCast.run(side_in, side_out)


def _proj(x, gain, w_in, qg, kg, lb, side_weights=(), *, batch, seq, tm, tn, lag):
    m, d = x.shape
    n_in = w_in.shape[1]
    tps = MIX_W // tn
    hpt = tn // HEAD_W
    spb = seq // tm
    ng = 2 * d // tn
    nj = 7 * tps + ng
    assert n_in == 7 * MIX_W + 2 * d and MIX_W % tn == 0 and (2 * d) % tn == 0

    pw = math.gcd(tn, V7X_MXU_WIDTH)
    lane = np.arange(pw)
    gm = (lane[:, None] // QK_DIM == lane[None, :] // QK_DIM).astype(np.float32) / QK_DIM

    def hm_block(lo, n):
        return lambda i, j: (i // spb, jnp.clip(j - lag - lo, 0, n - 1), i % spb, 0)

    side = _SideCast(side_weights, (m // tm) * (nj + lag), lambda i, j: i * (nj + lag) + j)
    out = pl.pallas_call(
        functools.partial(_proj_kernel, tm=tm, tn=tn, nj=nj, lag=lag, n_side=side.n),
        out_shape=[
            jax.ShapeDtypeStruct((batch, 3 * HEADS, seq, HEAD_W), BF16),
            jax.ShapeDtypeStruct((batch, 4 * HEADS, seq, HEAD_W), BF16),
            jax.ShapeDtypeStruct((batch, HEADS, seq, HEAD_W), F32),
            jax.ShapeDtypeStruct((m, 2 * d), BF16),
        ] + side.out_shapes,
        grid=(m // tm, nj + lag),
        in_specs=[
            pl.BlockSpec(memory_space=pl.ANY),
            pl.BlockSpec((1, d), lambda i, j: (0, 0)),
            pl.BlockSpec((d, tn), lambda i, j: (0, jnp.minimum(j, nj - 1))),
            pl.BlockSpec((1, HEAD_W), lambda i, j: (0, 0)),
            pl.BlockSpec((1, HEAD_W), lambda i, j: (0, 0)),
            pl.BlockSpec((HEADS, HEAD_W), lambda i, j: (0, 0)),
            pl.BlockSpec((pw, pw), lambda i, j: (0, 0)),
        ] + side.in_specs,
        out_specs=[
            pl.BlockSpec((1, hpt, tm, HEAD_W), hm_block(0, 3 * tps)),
            pl.BlockSpec((1, hpt, tm, HEAD_W), hm_block(3 * tps, 4 * tps)),
            pl.BlockSpec((1, hpt, tm, HEAD_W), hm_block(4 * tps, tps)),
            pl.BlockSpec((tm, tn), lambda i, j: (i, jnp.clip(j - lag - 7 * tps, 0, ng - 1))),
        ] + side.out_specs,
        scratch_shapes=[pltpu.VMEM((tm, d), F32), pltpu.VMEM((tm, d), BF16),
                        pltpu.VMEM((tm, tn), F32), pltpu.SemaphoreType.DMA(())],
        compiler_params=_cparams(("arbitrary", "arbitrary")),
        name="proj",
    )(x, gain.reshape(1, d), w_in, qg, kg, lb, jnp.asarray(gm, BF16), *side.arrays)
    return out[0], out[1], out[2], out[3], out[4:]


def _attn_kernel(lam_ref, off_ref, q_ref, k_ref, v_ref, w_ref, sg_ref, o_ref, vt_ref, bias_ref, *,
                 t, nkb, qpg, hps, online):
    hg = pl.program_id(0)
    b = pl.program_id(1)
    i = pl.program_id(2)
    lam = lam_ref[0]

    @pl.when((b == 0) & (i == 0))
    def _():
        for hh in range(hps):
            for dd in range(2):
                rows = jnp.broadcast_to(w_ref[hh, dd:dd + 1, :], (t, 2 * t))
                bias_ref[hh, dd] = pltpu.roll(rows, 0, 1, stride=1, stride_axis=0)[:, 0:t]

    @pl.when(i == 0)
    def _():
        for hh in range(hps):
            for jb in range(nkb):
                vt_ref[hh, jb] = v_ref[0, hh, jb * t:(jb + 1) * t, :].astype(F32).T.astype(BF16)

    def colsum8(p):
        return jnp.sum(p.reshape(t // 8, 8, 2 * t), axis=0)

    def both(x):
        return jnp.concatenate([x, x], axis=1)

    def query_block(hh, iq, row0):
        far_off = off_ref[hg * hps + hh]
        q = q_ref[0, hh, row0:row0 + t, :]
        lane = lax.broadcasted_iota(jnp.int32, (t, HEAD_W), 1)
        zero = jnp.zeros_like(q)
        qs = jnp.concatenate([jnp.where(lane < QK_DIM, q, zero),
                              jnp.where(lane >= QK_DIM, q, zero)], axis=0)
        offsets = [far_off] * max(iq - 1, 0)
        if iq >= 1:
            offsets.append(both(bias_ref[hh, 1]))
        offsets.append(both(bias_ref[hh, 0]))
        m = l = acc = None
        for j, badd in enumerate(offsets):
            s = _dot_nt(k_ref[0, hh, j * t:(j + 1) * t, :], qs) + badd
            if online:
                bmax = jnp.max(s, axis=0, keepdims=True)
                m_new = bmax if m is None else jnp.maximum(m, bmax)
                p = jnp.exp2(s - m_new)
            else:
                p = jnp.exp2(s)
            psum = colsum8(p)
            pv = _dot(vt_ref[hh, j], p.astype(BF16))
            if l is None:
                l, acc = psum, pv
            elif online:
                alpha = jnp.exp2(m - m_new)
                l, acc = alpha * l + psum, alpha * acc + pv
            else:
                l, acc = l + psum, acc + pv
            if online:
                m = m_new
        l = jnp.sum(l, axis=0, keepdims=True)
        o_t = acc[:, 0:t] / l[:, 0:t] - lam * (acc[:, t:2 * t] / l[:, t:2 * t])
        o_ref[row0:row0 + t, hh * HEAD_W:(hh + 1) * HEAD_W] = (
            _rms(o_t.T, sg_ref[...]) * (1.0 - LAMBDA_INIT)).astype(BF16)

    def query_group(ig):
        for hh in range(hps):
            for u in range(qpg):
                query_block(hh, ig * qpg + u, u * t)

    for ig in range(nkb // qpg):
        pl.when(i == ig)(functools.partial(query_group, ig))


def _t5_bucket(dist):
    n = jnp.maximum(dist, 0)
    max_exact = REL_BUCKETS // 2
    nf = jnp.maximum(n, 1).astype(F32)
    large = max_exact + (jnp.log(nf / max_exact) / math.log(REL_MAX_DIST / max_exact)
                         * (REL_BUCKETS - max_exact)).astype(jnp.int32)
    large = jnp.minimum(large, REL_BUCKETS - 1)
    return jnp.where(n < max_exact, n, large)


def _bias_generators(f, t):
    hole = jnp.full((HEADS, 1), -jnp.inf, F32)
    w0 = jnp.concatenate([f[:, 0:t], jnp.full((HEADS, t), -jnp.inf, F32)], axis=1)
    w1 = jnp.concatenate([f[:, t:2 * t], hole, f[:, 1:t]], axis=1)
    return jnp.stack([w0, w1], axis=1)


def _attn(qkv, lam, rel_bias, qk_bound, subln, *, batch, seq, t, qpg, hps):
    assert t >= REL_MAX_DIST
    nq = seq // t
    rb = rel_bias.astype(F32)
    spread = 2.0 * qk_bound + jnp.max(jnp.max(rb, axis=0) - jnp.min(rb, axis=0))
    static_ok = spread <= MAX_STATIC_SOFTMAX_SPREAD
    upper = jnp.where(static_ok, qk_bound + jnp.max(rb, axis=0), 0.0)
    f = rb[_t5_bucket(jnp.arange(2 * t))].T - upper[:, None]
    bias = _bias_generators(f * LOG2E, t)
    far_off = (rb[REL_BUCKETS - 1] - upper) * LOG2E

    assert nq % qpg == 0 and HEADS % hps == 0
    ng = nq // qpg
    hgs = HEADS // hps

    def call(online):
        return pl.pallas_call(
            functools.partial(_attn_kernel, t=t, nkb=nq, qpg=qpg, hps=hps, online=online),
            out_shape=jax.ShapeDtypeStruct((batch * seq, MIX_W), BF16),
            grid=(hgs, batch, ng),
            in_specs=[
                pl.BlockSpec(memory_space=pltpu.SMEM),
                pl.BlockSpec(memory_space=pltpu.SMEM),
                pl.BlockSpec((1, hps, qpg * t, HEAD_W), lambda h, b, i: (b, h, i, 0)),
                pl.BlockSpec((1, hps, seq, HEAD_W), lambda h, b, i: (b, hgs + h, 0, 0)),
                pl.BlockSpec((1, hps, seq, HEAD_W), lambda h, b, i: (b, 2 * hgs + h, 0, 0)),
                pl.BlockSpec((hps, 2, 2 * t), lambda h, b, i: (h, 0, 0)),
                pl.BlockSpec((1, HEAD_W), lambda h, b, i: (0, 0)),
            ],
            out_specs=pl.BlockSpec((qpg * t, hps * HEAD_W), lambda h, b, i: (b * ng + i, h)),
            scratch_shapes=[pltpu.VMEM((hps, nq, HEAD_W, t), BF16),
                            pltpu.VMEM((hps, 2, t, t), F32)],
            compiler_params=_cparams(("arbitrary", "arbitrary", "arbitrary")),
            name="attn_online" if online else "attn",
        )(lam.reshape(1), far_off, qkv, qkv, qkv, bias, subln.reshape(1, HEAD_W))

    return lax.cond(static_ok, lambda: call(False), lambda: call(True))


SUBLANES = 8


def _hgrn_tables(c):
    levels = int(math.log2(c))
    assert 2 ** levels == c
    t = np.arange(c)[:, None]
    s = np.arange(c)[None, :]
    blocks = [(s <= t).astype(np.float32)]
    level_id = np.where(t == s, 0, -1)
    n_fine = 0
    for l in range(1, levels + 1):
        g, half = 2 ** l, 2 ** (l - 1)
        mid = (t // g) * g + half - 1
        upper = (t % g) >= half
        if half < SUBLANES:
            above = upper & (s > mid) & (s <= t)
            below = (~upper) & (s > t) & (s <= mid)
            blocks.append((above | below).astype(np.float32))
            n_fine += 1
        pair = (t // g == s // g) & upper & ((s % g) < half)
        level_id = np.where(pair, l, level_id)
    w = np.concatenate(blocks, axis=0)
    return jnp.asarray(w, BF16), jnp.asarray(level_id, jnp.int32), levels, n_fine


def _hgrn_kernel(w_ref, lvl_ref, q_ref, k_ref, v_ref, og_ref, lf_ref, gn_ref, o_ref, *,
                 c, levels, n_fine, seq, unroll, hpg):
    gain = gn_ref[...]

    def cumsums(hh, ci):
        rows = pl.ds(pl.multiple_of(ci * c, c), c)
        g = lf_ref[0, hh, rows, :]
        g_hi = g.astype(BF16)
        g_lo = (g - g_hi.astype(F32)).astype(BF16)
        e2 = _dot(w_ref[...], jnp.concatenate([g_hi, g_lo], axis=1))
        return hh, rows, e2[:, 0:HEAD_W] + e2[:, HEAD_W:2 * HEAD_W]

    def scores(hh, rows, e):
        qb = q_ref[0, hh, rows, :]
        kb = k_ref[0, hh, rows, :]
        q = qb.astype(F32)
        k = kb.astype(F32)
        b = e[0:c]
        lvl = lvl_ref[...]
        pairs = [(qb, kb)]
        for l in range(1, levels + 1):
            if l <= n_fine:
                d = e[l * c:(l + 1) * c]
            else:
                g, half = 2 ** l, 2 ** (l - 1)
                b3 = b.reshape(c // g, g, HEAD_W)
                mid = b3[:, half - 1:half, :]
                d = jnp.concatenate([mid - b3[:, 0:half, :], b3[:, half:g, :] - mid],
                                    axis=1).reshape(c, HEAD_W)
            x = jnp.exp2(d)
            pairs.append(((q * x).astype(BF16), (k * x).astype(BF16)))
        a = jnp.zeros((c, c), F32)
        zero = jnp.zeros((c, HEAD_W), BF16)
        for l in range(0, levels + 1, 2):
            (qa, ka), (qc, kc) = pairs[l], pairs[l + 1]
            keys = jnp.concatenate([jnp.concatenate([ka, zero], axis=1),
                                    jnp.concatenate([zero, kc], axis=1)], axis=0)
            r = _dot_nt(jnp.concatenate([qa, qc], axis=1), keys)
            a = jnp.where(lvl == l, r[:, 0:c], a)
            a = jnp.where(lvl == l + 1, r[:, c:2 * c], a)
        b_last = b[c - 1:c, :]
        q_in = (q * jnp.exp2(b)).astype(BF16)
        k_out = (k * jnp.exp2(b_last - b)).astype(BF16)
        return a.astype(BF16), q_in, k_out, jnp.exp2(b_last)

    def outputs(hh, rows, a, q_in, k_out, decay):
        v = v_ref[0, hh, rows, :]
        return hh, rows, _dot(a, v), q_in, _dot_tn(v, k_out), decay

    def body(it, states):
        units = [(hh, it * unroll + u) for u in range(unroll) for hh in range(hpg)]
        st1, st2, st3 = {}, {}, []
        for n in range(len(units) + 2):
            if n < len(units):
                st1[n] = cumsums(*units[n])
            if 0 <= n - 1 < len(units):
                st2[n - 1] = scores(*st1[n - 1])
            if 0 <= n - 2 < len(units):
                st3.append(outputs(*st1[n - 2][0:2], *st2[n - 2]))
        states = list(states)
        for hh, rows, o_intra, q_in, ds_t, decay in st3:
            o = o_intra + _dot_nt(q_in, states[hh].astype(BF16))
            states[hh] = states[hh] * decay + ds_t
            y = _rms(o, gain) * og_ref[0, hh, rows, :].astype(F32)
            o_ref[rows, hh * HEAD_W:(hh + 1) * HEAD_W] = y.astype(BF16)
        return tuple(states)

    lax.fori_loop(0, seq // (c * unroll), body,
                  tuple(jnp.zeros((HEAD_W, HEAD_W), F32) for _ in range(hpg)))


def _hgrn(hin, logf, gain, *, batch, seq, c, unroll, hpg):
    w, lvl, levels, n_fine = _hgrn_tables(c)
    assert seq % (c * unroll) == 0 and HEADS % hpg == 0
    groups = HEADS // hpg

    def head(slab):
        return pl.BlockSpec((1, hpg, seq, HEAD_W), lambda b, h: (b, slab * groups + h, 0, 0))

    return pl.pallas_call(
        functools.partial(_hgrn_kernel, c=c, levels=levels, n_fine=n_fine, seq=seq, unroll=unroll,
                          hpg=hpg),
        out_shape=jax.ShapeDtypeStruct((batch * seq, MIX_W), BF16),
        grid=(batch, groups),
        in_specs=[
            pl.BlockSpec(w.shape, lambda b, h: (0, 0)),
            pl.BlockSpec((c, c), lambda b, h: (0, 0)),
            head(0), head(1), head(2), head(3), head(0),
            pl.BlockSpec((1, HEAD_W), lambda b, h: (0, 0)),
        ],
        out_specs=pl.BlockSpec((seq, hpg * HEAD_W), lambda b, h: (b, h)),
        compiler_params=_cparams(("parallel", "parallel")),
        name="hgrn",
    )(w, lvl, hin, hin, hin, hin, logf, gain.reshape(1, HEAD_W))


def _merge_kernel(x_ref, ya_ref, yb_ref, gt_ref, wa_ref, wb_ref, wo_ref, *rest, n_side):
    side_in, (o_ref,), side_out = rest[:n_side], rest[n_side:n_side + 1], rest[n_side + 1:]
    d = x_ref.shape[1]
    gt = gt_ref[...].astype(F32)
    merged = gt[:, 0:d] * _dot(ya_ref[...], wa_ref[...]) + gt[:, d:2 * d] * _dot(yb_ref[...], wb_ref[...])
    o_ref[...] = x_ref[...] + _dot(merged.astype(BF16), wo_ref[...])
    _SideCast.run(side_in, side_out)


def _resident(shape):
    return pl.BlockSpec(shape, lambda i: (0,) * len(shape), pipeline_mode=pl.Buffered(1))


def _merge(x, ya, yb, gates, wa, wb, wo, side_weights=(), *, tm):
    m, d = x.shape
    side = _SideCast(side_weights, m // tm, lambda i: i)
    out = pl.pallas_call(
        functools.partial(_merge_kernel, n_side=side.n),
        out_shape=[jax.ShapeDtypeStruct((m, d), F32)] + side.out_shapes,
        grid=(m // tm,),
        in_specs=[
            pl.BlockSpec((tm, d), lambda i: (i, 0)),
            pl.BlockSpec((tm, MIX_W), lambda i: (i, 0)),
            pl.BlockSpec((tm, MIX_W), lambda i: (i, 0)),
            pl.BlockSpec((tm, 2 * d), lambda i: (i, 0)),
            _resident(wa.shape), _resident(wb.shape), _resident(wo.shape),
        ] + side.in_specs,
        out_specs=[pl.BlockSpec((tm, d), lambda i: (i, 0))] + side.out_specs,
        compiler_params=_cparams(("arbitrary",)),
        name="merge",
    )(x, ya, yb, gates, wa, wb, wo, *side.arrays)
    return out[0], out[1:]


def _ple_kernel(x_ref, p_ref, gg_ref, pg_ref, wg_ref, wp_ref, o_ref):
    ple = _rms(_dot(p_ref[...].astype(BF16), wp_ref[...]), pg_ref[...])
    x = x_ref[...]
    gate = _sigmoid(_dot(_rms(x, gg_ref[...]).astype(BF16), wg_ref[...]))
    o_ref[...] = x + gate * ple


def _ple(x, p, gate_gain, post_gain, w_gate, w_proj, *, tm):
    m, d = x.shape
    pd = p.shape[1]
    return pl.pallas_call(
        _ple_kernel,
        out_shape=jax.ShapeDtypeStruct((m, d), F32),
        grid=(m // tm,),
        in_specs=[
            pl.BlockSpec((tm, d), lambda i: (i, 0)),
            pl.BlockSpec((tm, pd), lambda i: (i, 0)),
            pl.BlockSpec((1, d), lambda i: (0, 0)),
            pl.BlockSpec((1, d), lambda i: (0, 0)),
            _resident(w_gate.shape), _resident(w_proj.shape),
        ],
        out_specs=pl.BlockSpec((tm, d), lambda i: (i, 0)),
        compiler_params=_cparams(("parallel",)),
        name="ple",
    )(x, p, gate_gain.reshape(1, d), post_gain.reshape(1, d), w_gate, w_proj)


def _tiles(seq, d, f):
    def pick(n, choices):
        return next(c for c in choices if n % c == 0)
    return dict(
        tm=pick(seq, (512, 256, 128)),
        tfm=pick(seq, (1024, 512, 256, 128)),
        tf=pick(f, (512, 256, 128)),
        tp=pick(seq, (1024, 512, 256, 128)),
        tn=pick(math.gcd(MIX_W, 2 * d), (1024, 512, 256, 128)),
        lag=0,
        ta=pick(seq, (512, 256, 128)),
        qpg=4,
        hps=4,
        c=128,
        cu=16,
        hpg=4,
    )


def kernel(x, p, ffn1_norm, ffn1_w_gate, ffn1_w_up, ffn1_w_down, mix_norm, w_in, q_norm, k_norm, lambda_q1, lambda_k1, lambda_q2, lambda_k2, diff_subln, rel_bias, hgrn_lb_logits, hgrn_norm, w_branch_a, w_branch_b, w_out, ffn2_norm, ffn2_w_gate, ffn2_w_up, ffn2_w_down, ple_gate_norm, w_ple_gate, w_ple_proj, ple_post_norm):
    batch, seq, d = x.shape
    depth = ffn1_norm.shape[0]
    assert depth == 1
    m = batch * seq
    tl = _tiles(seq, d, ffn1_w_gate.shape[2])
    bf = lambda w: w.astype(BF16)

    lower_bounds = jnp.cumsum(jax.nn.softmax(hgrn_lb_logits.astype(F32), axis=0), axis=0)
    lb = lower_bounds[0].reshape(HEADS, HEAD_W)
    lam = (jnp.exp(jnp.sum(lambda_q1[0].astype(F32) * lambda_k1[0].astype(F32)))
           - jnp.exp(jnp.sum(lambda_q2[0].astype(F32) * lambda_k2[0].astype(F32)))
           + LAMBDA_INIT)
    scale = QK_DIM ** -0.5
    qg = jnp.tile(q_norm[0].astype(F32), 2).reshape(1, HEAD_W) * (scale * LOG2E)
    kg = jnp.tile(k_norm[0].astype(F32), 2).reshape(1, HEAD_W)
    qk_bound = (QK_DIM * scale * jnp.max(jnp.abs(q_norm[0].astype(F32)))
                * jnp.max(jnp.abs(k_norm[0].astype(F32))))

    x2 = x.reshape(m, d)
    x2, (w_in_b, wa_b, wb_b, wo_b) = _ffn(
        x2, ffn1_norm[0], bf(ffn1_w_gate[0]), bf(ffn1_w_up[0]), bf(ffn1_w_down[0]),
        (w_in[0], w_branch_a[0], w_branch_b[0], w_out[0]), tm=tl["tfm"], tf=tl["tf"])

    qkv, hin, logf, gates, (wg2_b, wu2_b, wd2_b) = _proj(
        x2, mix_norm[0], w_in_b, qg, kg, lb, (ffn2_w_gate[0], ffn2_w_up[0], ffn2_w_down[0]),
        batch=batch, seq=seq, tm=tl["tp"], tn=tl["tn"], lag=tl["lag"])
    ya = _attn(qkv, lam, rel_bias, qk_bound, diff_subln[0], batch=batch, seq=seq,
               t=tl["ta"], qpg=tl["qpg"], hps=tl["hps"])
    yb = _hgrn(hin, logf, hgrn_norm[0], batch=batch, seq=seq, c=tl["c"], unroll=tl["cu"],
               hpg=tl["hpg"])
    x2, (wpg_b, wpp_b) = _merge(x2, ya, yb, gates, wa_b, wb_b, wo_b,
                                (w_ple_gate[0], w_ple_proj[0]), tm=tl["tm"])

    x2, _ = _ffn(x2, ffn2_norm[0], wg2_b, wu2_b, wd2_b, tm=tl["tfm"], tf=tl["tf"])
    x2 = _ple(x2, p[0].reshape(m, -1), ple_gate_norm[0], ple_post_norm[0], wpg_b, wpp_b, tm=tl["tm"])
    return x2.reshape(batch, seq, d)
```

```python
import functools
import math

import jax
import jax.numpy as jnp
import numpy as np
from jax import lax
from jax.experimental import pallas as pl
from jax.experimental.pallas import tpu as pltpu

F32 = jnp.float32
BF16 = jnp.bfloat16

EPS = 1e-6
HEADS = 8
HEAD_W = 128
QK_DIM = 64
MIX_W = HEADS * HEAD_W
REL_BUCKETS = 32
REL_MAX_DIST = 128
LAMBDA_INIT = 0.8 - 0.6 * math.exp(-0.3 * 0)
LOG2E = math.log2(math.e)
MAX_STATIC_SOFTMAX_SPREAD = 60.0

V7X_MXU_WIDTH = 256
V7X_VMEM_BYTES = 64 * 1024 * 1024
VMEM_LIMIT = V7X_VMEM_BYTES * 15 // 16


def _cparams(sem):
    return pltpu.CompilerParams(dimension_semantics=sem, vmem_limit_bytes=VMEM_LIMIT)


def _sigmoid(x):
    return 1.0 / (1.0 + jnp.exp(-x))


def _rms(x, gain):
    ms = jnp.mean(x * x, axis=-1, keepdims=True)
    return x * lax.rsqrt(ms + EPS) * gain


def _dot(a, b):
    return jnp.dot(a, b, preferred_element_type=F32)


def _dot_nt(a, b):
    return lax.dot_general(a, b, (((1,), (1,)), ((), ())), preferred_element_type=F32)


def _dot_tn(a, b):
    return lax.dot_general(a, b, (((0,), (0,)), ((), ())), preferred_element_type=F32)


BF16_ROW_TILE = 16


class _SideCast:
    def __init__(self, arrays, steps, step_index):
        self.arrays = list(arrays)
        self.n = len(self.arrays)
        self.step_index = step_index
        self.chunks = []
        for w in self.arrays:
            rows = w.shape[0]
            self.chunks.append(max(c for c in range(1, steps + 1)
                                   if rows % c == 0 and (rows // c) % BF16_ROW_TILE == 0))

    def _specs(self):
        def spec(w, c):
            return pl.BlockSpec((w.shape[0] // c, w.shape[1]),
                                lambda *ids: (jnp.minimum(self.step_index(*ids), c - 1), 0))
        return [spec(w, c) for w, c in zip(self.arrays, self.chunks)]

    in_specs = property(_specs)
    out_specs = property(_specs)

    @property
    def out_shapes(self):
        return [jax.ShapeDtypeStruct(w.shape, BF16) for w in self.arrays]

    @staticmethod
    def run(in_refs, out_refs):
        for src, dst in zip(in_refs, out_refs):
            dst[...] = src[...].astype(BF16)


def _row_tile_fetch(x_hbm, xs_ref, sem, tm):
    i = pl.program_id(0)
    j = pl.program_id(1)

    def copy(row_tile):
        return pltpu.make_async_copy(x_hbm.at[pl.ds(row_tile * tm, tm), :], xs_ref, sem)

    @pl.when((i == 0) & (j == 0))
    def _():
        copy(0).start()

    @pl.when((j == 1) & (i + 1 < pl.num_programs(0)))
    def _():
        copy(i + 1).start()

    return copy(i)


def _ffn_kernel(x_hbm, g_ref, wg_ref, wu_ref, wd_ref, *rest, tm, n_side):
    side_in, (o_ref,), side_out = rest[:n_side], rest[n_side:n_side + 1], rest[n_side + 1:2 * n_side + 1]
    xs_ref, h_ref, sem = rest[2 * n_side + 1:]
    j = pl.program_id(1)
    x_copy = _row_tile_fetch(x_hbm, xs_ref, sem, tm)

    @pl.when(j == 0)
    def _():
        x_copy.wait()
        x = xs_ref[...]
        h_ref[...] = _rms(x, g_ref[...]).astype(BF16)
        o_ref[...] = x

    h = h_ref[...]
    gate = _dot(h, wg_ref[...])
    up = _dot(h, wu_ref[...])
    act = (gate * _sigmoid(gate) * (0.5 * up)).astype(BF16)
    o_ref[...] += _dot(act, wd_ref[...])
    _SideCast.run(side_in, side_out)


def _ffn(x, gain, w_gate, w_up, w_down, side_weights=(), *, tm, tf):
    m, d = x.shape
    f = w_gate.shape[1]
    nj = f // tf
    assert nj >= 2
    side = _SideCast(side_weights, (m // tm) * nj, lambda i, j: i * nj + j)
    out = pl.pallas_call(
        functools.partial(_ffn_kernel, tm=tm, n_side=side.n),
        out_shape=[jax.ShapeDtypeStruct((m, d), F32)] + side.out_shapes,
        grid=(m // tm, nj),
        in_specs=[
            pl.BlockSpec(memory_space=pl.ANY),
            pl.BlockSpec((1, d), lambda i, j: (0, 0)),
            pl.BlockSpec((d, tf), lambda i, j: (0, j)),
            pl.BlockSpec((d, tf), lambda i, j: (0, j)),
            pl.BlockSpec((tf, d), lambda i, j: (j, 0)),
        ] + side.in_specs,
        out_specs=[pl.BlockSpec((tm, d), lambda i, j: (i, 0))] + side.out_specs,
        scratch_shapes=[pltpu.VMEM((tm, d), F32), pltpu.VMEM((tm, d), BF16),
                        pltpu.SemaphoreType.DMA(())],
        compiler_params=_cparams(("arbitrary", "arbitrary")),
        name="ffn",
    )(x, gain.reshape(1, d), w_gate, w_up, w_down, *side.arrays)
    return out[0], out[1:]


W_RING = 3


def _proj_kernel(x_hbm, g_ref, w_hbm, qg_ref, kg_ref, lb_ref, gm_ref, *rest, tm, tn, nj, n_side):
    side_in, rest = rest[:n_side], rest[n_side:]
    (oqkv_ref, oh_ref, olf_ref, og_ref), rest = rest[:4], rest[4:]
    side_out, (xs_ref, h_ref, acc_ref, w_ref, sem, w_sem) = rest[:n_side], rest[n_side:]
    j = pl.program_id(1)
    jp = j - 1
    tps = MIX_W // tn
    hpt = tn // HEAD_W

    def heads(vals):
        return [vals[:, s * HEAD_W:(s + 1) * HEAD_W] for s in range(hpt)]

    def epi_qk(acc):
        gain = jnp.where(jp < tps, qg_ref[...], kg_ref[...])
        pw = gm_ref.shape[0]
        for s in range(tn // pw):
            a = acc[:, s * pw:(s + 1) * pw]
            ms = _dot((a * a).astype(BF16), gm_ref[...])
            y = a * lax.rsqrt(ms + EPS)
            for u in range(pw // HEAD_W):
                oqkv_ref[0, s * (pw // HEAD_W) + u] = (
                    y[:, u * HEAD_W:(u + 1) * HEAD_W] * gain).astype(BF16)

    def epi_v(acc):
        for s, a in enumerate(heads(acc)):
            oqkv_ref[0, s] = a.astype(BF16)

    def epi_silu(acc):
        for s, a in enumerate(heads(acc * _sigmoid(acc))):
            oh_ref[0, s] = a.astype(BF16)

    def epi_forget(acc):
        jj = jp - 4 * tps
        for s, a in enumerate(heads(acc)):
            lb = lb_ref[pl.ds(jj * hpt + s, 1), :]
            forget = lb + (1.0 - lb) * _sigmoid(a)
            oh_ref[0, s] = (1.0 - forget).astype(BF16)
            olf_ref[0, s] = jnp.log2(forget)

    def epi_hv(acc):
        for s, a in enumerate(heads(acc)):
            oh_ref[0, s] = a.astype(BF16)

    def epi_gates(acc):
        og_ref[...] = _sigmoid(acc).astype(BF16)

    x_copy = _row_tile_fetch(x_hbm, xs_ref, sem, tm)

    i = pl.program_id(0)
    n_now = i * nj + j
    n_all = pl.num_programs(0) * nj

    def w_copy(n):
        col = pl.multiple_of(lax.rem(n, nj) * tn, tn)
        slot = lax.rem(n, W_RING)
        return pltpu.make_async_copy(w_hbm.at[:, pl.ds(col, tn)], w_ref.at[slot], w_sem.at[slot])

    @pl.when(n_now == 0)
    def _():
        for n in range(W_RING - 1):
            w_copy(n).start()

    @pl.when((j < nj) & (n_now + W_RING - 1 < n_all))
    def _():
        w_copy(n_now + W_RING - 1).start()

    def weights():
        w_copy(n_now).wait()
        return w_ref[lax.rem(n_now, W_RING)]

    @pl.when(j == 0)
    def _():
        x_copy.wait()
        h_ref[...] = _rms(xs_ref[...], g_ref[...]).astype(BF16)
        acc_ref[...] = _dot(h_ref[...], weights())
        _SideCast.run(side_in, side_out)

    def fused(cond, epi):
        @pl.when(cond & (j >= 1) & (j < nj))
        def _():
            acc = acc_ref[...]
            epi(acc)
            acc_ref[...] = _dot(h_ref[...], weights())
            _SideCast.run(side_in, side_out)

    fused(jp < 2 * tps, epi_qk)
    fused((jp >= 2 * tps) & (jp < 3 * tps), epi_v)
    fused(((jp >= 3 * tps) & (jp < 4 * tps)) | ((jp >= 6 * tps) & (jp < 7 * tps)), epi_silu)
    fused((jp >= 4 * tps) & (jp < 5 * tps), epi_forget)
    fused((jp >= 5 * tps) & (jp < 6 * tps), epi_hv)
    fused(jp >= 7 * tps, epi_gates)

    @pl.when(j == nj)
    def _():
        epi_gates(acc_ref[...])
        _SideCast.run(side_in, side_out)


def _proj(x, gain, w_in, qg, kg, lb, side_weights=(), *, batch, seq, tm, tn):
    m, d = x.shape
    n_in = w_in.shape[1]
    tps = MIX_W // tn
    hpt = tn // HEAD_W
    spb = seq // tm
    ng = 2 * d // tn
    nj = 7 * tps + ng
    assert n_in == 7 * MIX_W + 2 * d and MIX_W % tn == 0 and (2 * d) % tn == 0

    pw = math.gcd(tn, V7X_MXU_WIDTH)
    lane = np.arange(pw)
    gm = (lane[:, None] // QK_DIM == lane[None, :] // QK_DIM).astype(np.float32) / QK_DIM

    def hm_block(lo, n):
        return lambda i, j: (i // spb, jnp.clip(j - 1 - lo, 0, n - 1), i % spb, 0)

    side = _SideCast(side_weights, (m // tm) * (nj + 1), lambda i, j: i * (nj + 1) + j)
    out = pl.pallas_call(
        functools.partial(_proj_kernel, tm=tm, tn=tn, nj=nj, n_side=side.n),
        out_shape=[
            jax.ShapeDtypeStruct((batch, 3 * HEADS, seq, HEAD_W), BF16),
            jax.ShapeDtypeStruct((batch, 4 * HEADS, seq, HEAD_W), BF16),
            jax.ShapeDtypeStruct((batch, HEADS, seq, HEAD_W), F32),
            jax.ShapeDtypeStruct((m, 2 * d), BF16),
        ] + side.out_shapes,
        grid=(m // tm, nj + 1),
        in_specs=[
            pl.BlockSpec(memory_space=pl.ANY),
            pl.BlockSpec((1, d), lambda i, j: (0, 0)),
            pl.BlockSpec(memory_space=pl.ANY),
            pl.BlockSpec((1, HEAD_W), lambda i, j: (0, 0)),
            pl.BlockSpec((1, HEAD_W), lambda i, j: (0, 0)),
            pl.BlockSpec((HEADS, HEAD_W), lambda i, j: (0, 0)),
            pl.BlockSpec((pw, pw), lambda i, j: (0, 0)),
        ] + side.in_specs,
        out_specs=[
            pl.BlockSpec((1, hpt, tm, HEAD_W), hm_block(0, 3 * tps)),
            pl.BlockSpec((1, hpt, tm, HEAD_W), hm_block(3 * tps, 4 * tps)),
            pl.BlockSpec((1, hpt, tm, HEAD_W), hm_block(4 * tps, tps)),
            pl.BlockSpec((tm, tn), lambda i, j: (i, jnp.clip(j - 1 - 7 * tps, 0, ng - 1))),
        ] + side.out_specs,
        scratch_shapes=[pltpu.VMEM((tm, d), F32), pltpu.VMEM((tm, d), BF16),
                        pltpu.VMEM((tm, tn), F32), pltpu.VMEM((W_RING, d, tn), BF16),
                        pltpu.SemaphoreType.DMA(()), pltpu.SemaphoreType.DMA((W_RING,))],
        compiler_params=_cparams(("arbitrary", "arbitrary")),
        name="proj",
    )(x, gain.reshape(1, d), w_in, qg, kg, lb, jnp.asarray(gm, BF16), *side.arrays)
    return out[0], out[1], out[2], out[3], out[4:]


def _attn_kernel(lam_ref, off_ref, q_ref, k_ref, v_ref, w_ref, sg_ref, o_ref, vt_ref, bias_ref, *,
                 t, nkb, qpg, hps, online):
    hg = pl.program_id(0)
    b = pl.program_id(1)
    i = pl.program_id(2)
    lam = lam_ref[0]

    @pl.when((b == 0) & (i == 0))
    def _():
        for hh in range(hps):
            for dd in range(2):
                rows = jnp.broadcast_to(w_ref[hh, dd:dd + 1, :], (t, 2 * t))
                bias_ref[hh, dd] = pltpu.roll(rows, 0, 1, stride=1, stride_axis=0)[:, 0:t]

    @pl.when(i == 0)
    def _():
        for hh in range(hps):
            for jb in range(nkb):
                vt_ref[hh, jb] = v_ref[0, hh, jb * t:(jb + 1) * t, :].astype(F32).T.astype(BF16)

    def colsum8(p):
        return jnp.sum(p.reshape(t // 8, 8, 2 * t), axis=0)

    def both(x):
        return jnp.concatenate([x, x], axis=1)

    def query_block(hh, iq, row0):
        far_off = off_ref[hg * hps + hh]
        q = q_ref[0, hh, row0:row0 + t, :]
        lane = lax.broadcasted_iota(jnp.int32, (t, HEAD_W), 1)
        zero = jnp.zeros_like(q)
        qs = jnp.concatenate([jnp.where(lane < QK_DIM, q, zero),
                              jnp.where(lane >= QK_DIM, q, zero)], axis=0)
        offsets = [far_off] * max(iq - 1, 0)
        if iq >= 1:
            offsets.append(both(bias_ref[hh, 1]))
        offsets.append(both(bias_ref[hh, 0]))
        m = l = acc = None
        for j, badd in enumerate(offsets):
            s = _dot_nt(k_ref[0, hh, j * t:(j + 1) * t, :], qs) + badd
            if online:
                bmax = jnp.max(s, axis=0, keepdims=True)
                m_new = bmax if m is None else jnp.maximum(m, bmax)
                p = jnp.exp2(s - m_new)
            else:
                p = jnp.exp2(s)
            psum = colsum8(p)
            pv = _dot(vt_ref[hh, j], p.astype(BF16))
            if l is None:
                l, acc = psum, pv
            elif online:
                alpha = jnp.exp2(m - m_new)
                l, acc = alpha * l + psum, alpha * acc + pv
            else:
                l, acc = l + psum, acc + pv
            if online:
                m = m_new
        l = jnp.sum(l, axis=0, keepdims=True)
        o_t = acc[:, 0:t] / l[:, 0:t] - lam * (acc[:, t:2 * t] / l[:, t:2 * t])
        o_ref[row0:row0 + t, hh * HEAD_W:(hh + 1) * HEAD_W] = (
            _rms(o_t.T, sg_ref[...]) * (1.0 - LAMBDA_INIT)).astype(BF16)

    def query_group(ig):
        for hh in range(hps):
            for u in range(qpg):
                query_block(hh, ig * qpg + u, u * t)

    for ig in range(nkb // qpg):
        pl.when(i == ig)(functools.partial(query_group, ig))


def _t5_bucket(dist):
    n = jnp.maximum(dist, 0)
    max_exact = REL_BUCKETS // 2
    nf = jnp.maximum(n, 1).astype(F32)
    large = max_exact + (jnp.log(nf / max_exact) / math.log(REL_MAX_DIST / max_exact)
                         * (REL_BUCKETS - max_exact)).astype(jnp.int32)
    large = jnp.minimum(large, REL_BUCKETS - 1)
    return jnp.where(n < max_exact, n, large)


def _bias_generators(f, t):
    hole = jnp.full((HEADS, 1), -jnp.inf, F32)
    w0 = jnp.concatenate([f[:, 0:t], jnp.full((HEADS, t), -jnp.inf, F32)], axis=1)
    w1 = jnp.concatenate([f[:, t:2 * t], hole, f[:, 1:t]], axis=1)
    return jnp.stack([w0, w1], axis=1)


def _attn(qkv, lam, rel_bias, qk_bound, subln, *, batch, seq, t, qpg, hps):
    assert t >= REL_MAX_DIST
    nq = seq // t
    rb = rel_bias.astype(F32)
    spread = 2.0 * qk_bound + jnp.max(jnp.max(rb, axis=0) - jnp.min(rb, axis=0))
    static_ok = spread <= MAX_STATIC_SOFTMAX_SPREAD
    upper = jnp.where(static_ok, qk_bound + jnp.max(rb, axis=0), 0.0)
    f = rb[_t5_bucket(jnp.arange(2 * t))].T - upper[:, None]
    bias = _bias_generators(f * LOG2E, t)
    far_off = (rb[REL_BUCKETS - 1] - upper) * LOG2E

    assert nq % qpg == 0 and HEADS % hps == 0
    ng = nq // qpg
    hgs = HEADS // hps

    def call(online):
        return pl.pallas_call(
            functools.partial(_attn_kernel, t=t, nkb=nq, qpg=qpg, hps=hps, online=online),
            out_shape=jax.ShapeDtypeStruct((batch * seq, MIX_W), BF16),
            grid=(hgs, batch, ng),
            in_specs=[
                pl.BlockSpec(memory_space=pltpu.SMEM),
                pl.BlockSpec(memory_space=pltpu.SMEM),
                pl.BlockSpec((1, hps, qpg * t, HEAD_W), lambda h, b, i: (b, h, i, 0)),
                pl.BlockSpec((1, hps, seq, HEAD_W), lambda h, b, i: (b, hgs + h, 0, 0)),
                pl.BlockSpec((1, hps, seq, HEAD_W), lambda h, b, i: (b, 2 * hgs + h, 0, 0)),
                pl.BlockSpec((hps, 2, 2 * t), lambda h, b, i: (h, 0, 0)),
                pl.BlockSpec((1, HEAD_W), lambda h, b, i: (0, 0)),
            ],
            out_specs=pl.BlockSpec((qpg * t, hps * HEAD_W), lambda h, b, i: (b * ng + i, h)),
            scratch_shapes=[pltpu.VMEM((hps, nq, HEAD_W, t), BF16),
                            pltpu.VMEM((hps, 2, t, t), F32)],
            compiler_params=_cparams(("arbitrary", "arbitrary", "arbitrary")),
            name="attn_online" if online else "attn",
        )(lam.reshape(1), far_off, qkv, qkv, qkv, bias, subln.reshape(1, HEAD_W))

    return lax.cond(static_ok, lambda: call(False), lambda: call(True))


SUBLANES = 8


def _hgrn_tables(c):
    levels = int(math.log2(c))
    assert 2 ** levels == c
    t = np.arange(c)[:, None]
    s = np.arange(c)[None, :]
    blocks = [(s <= t).astype(np.float32)]
    level_id = np.where(t == s, 0, -1)
    n_fine = 0
    for l in range(1, levels + 1):
        g, half = 2 ** l, 2 ** (l - 1)
        mid = (t // g) * g + half - 1
        upper = (t % g) >= half
        if half < SUBLANES:
            above = upper & (s > mid) & (s <= t)
            below = (~upper) & (s > t) & (s <= mid)
            blocks.append((above | below).astype(np.float32))
            n_fine += 1
        pair = (t // g == s // g) & upper & ((s % g) < half)
        level_id = np.where(pair, l, level_id)
    w = np.concatenate(blocks, axis=0)
    return jnp.asarray(w, BF16), jnp.asarray(level_id, jnp.int32), levels, n_fine


def _hgrn_kernel(w_ref, lvl_ref, q_ref, k_ref, v_ref, og_ref, lf_ref, gn_ref, o_ref, *,
                 c, levels, n_fine, seq, unroll, hpg):
    gain = gn_ref[...]

    def cumsums(hh, ci):
        rows = pl.ds(pl.multiple_of(ci * c, c), c)
        g = lf_ref[0, hh, rows, :]
        g_hi = g.astype(BF16)
        g_lo = (g - g_hi.astype(F32)).astype(BF16)
        e2 = _dot(w_ref[...], jnp.concatenate([g_hi, g_lo], axis=1))
        return hh, rows, e2[:, 0:HEAD_W] + e2[:, HEAD_W:2 * HEAD_W]

    def scores(hh, rows, e):
        qb = q_ref[0, hh, rows, :]
        kb = k_ref[0, hh, rows, :]
        q = qb.astype(F32)
        k = kb.astype(F32)
        b = e[0:c]
        lvl = lvl_ref[...]
        pairs = [(qb, kb)]
        for l in range(1, levels + 1):
            if l <= n_fine:
                d = e[l * c:(l + 1) * c]
            else:
                g, half = 2 ** l, 2 ** (l - 1)
                b3 = b.reshape(c // g, g, HEAD_W)
                mid = b3[:, half - 1:half, :]
                d = jnp.concatenate([mid - b3[:, 0:half, :], b3[:, half:g, :] - mid],
                                    axis=1).reshape(c, HEAD_W)
            x = jnp.exp2(d)
            pairs.append(((q * x).astype(BF16), (k * x).astype(BF16)))
        a = jnp.zeros((c, c), F32)
        zero = jnp.zeros((c, HEAD_W), BF16)
        for l in range(0, levels + 1, 2):
            (qa, ka), (qc, kc) = pairs[l], pairs[l + 1]
            keys = jnp.concatenate([jnp.concatenate([ka, zero], axis=1),
                                    jnp.concatenate([zero, kc], axis=1)], axis=0)
            r = _dot_nt(jnp.concatenate([qa, qc], axis=1), keys)
            a = jnp.where(lvl == l, r[:, 0:c], a)
            a = jnp.where(lvl == l + 1, r[:, c:2 * c], a)
        b_last = b[c - 1:c, :]
        q_in = (q * jnp.exp2(b)).astype(BF16)
        k_out = (k * jnp.exp2(b_last - b)).astype(BF16)
        return a.astype(BF16), q_in, k_out, jnp.exp2(b_last)

    def outputs(hh, rows, a, q_in, k_out, decay):
        v = v_ref[0, hh, rows, :]
        return hh, rows, _dot(a, v), q_in, _dot_tn(v, k_out), decay

    def body(it, states):
        units = [(hh, it * unroll + u) for u in range(unroll) for hh in range(hpg)]
        st1, st2, st3 = {}, {}, []
        for n in range(len(units) + 2):
            if n < len(units):
                st1[n] = cumsums(*units[n])
            if 0 <= n - 1 < len(units):
                st2[n - 1] = scores(*st1[n - 1])
            if 0 <= n - 2 < len(units):
                st3.append(outputs(*st1[n - 2][0:2], *st2[n - 2]))
        states = list(states)
        for hh, rows, o_intra, q_in, ds_t, decay in st3:
            o = o_intra + _dot_nt(q_in, states[hh].astype(BF16))
            states[hh] = states[hh] * decay + ds_t
            y = _rms(o, gain) * og_ref[0, hh, rows, :].astype(F32)
            o_ref[rows, hh * HEAD_W:(hh + 1) * HEAD_W] = y.astype(BF16)
        return tuple(states)

    lax.fori_loop(0, seq // (c * unroll), body,
                  tuple(jnp.zeros((HEAD_W, HEAD_W), F32) for _ in range(hpg)))


def _hgrn(hin, logf, gain, *, batch, seq, c, unroll, hpg):
    w, lvl, levels, n_fine = _hgrn_tables(c)
    assert seq % (c * unroll) == 0 and HEADS % hpg == 0
    groups = HEADS // hpg

    def head(slab):
        return pl.BlockSpec((1, hpg, seq, HEAD_W), lambda b, h: (b, slab * groups + h, 0, 0))

    return pl.pallas_call(
        functools.partial(_hgrn_kernel, c=c, levels=levels, n_fine=n_fine, seq=seq, unroll=unroll,
                          hpg=hpg),
        out_shape=jax.ShapeDtypeStruct((batch * seq, MIX_W), BF16),
        grid=(batch, groups),
        in_specs=[
            pl.BlockSpec(w.shape, lambda b, h: (0, 0)),
            pl.BlockSpec((c, c), lambda b, h: (0, 0)),
            head(0), head(1), head(2), head(3), head(0),
            pl.BlockSpec((1, HEAD_W), lambda b, h: (0, 0)),
        ],
        out_specs=pl.BlockSpec((seq, hpg * HEAD_W), lambda b, h: (b, h)),
        compiler_params=_cparams(("parallel", "parallel")),
        name="hgrn",
    )(w, lvl, hin, hin, hin, hin, logf, gain.reshape(1, HEAD_W))


def _merge_kernel(x_ref, ya_ref, yb_ref, gt_ref, wa_ref, wb_ref, wo_ref, *rest, n_side):
    side_in, (o_ref,), side_out = rest[:n_side], rest[n_side:n_side + 1], rest[n_side + 1:]
    d = x_ref.shape[1]
    gt = gt_ref[...].astype(F32)
    merged = gt[:, 0:d] * _dot(ya_ref[...], wa_ref[...]) + gt[:, d:2 * d] * _dot(yb_ref[...], wb_ref[...])
    o_ref[...] = x_ref[...] + _dot(merged.astype(BF16), wo_ref[...])
    _SideCast.run(side_in, side_out)


def _resident(shape):
    return pl.BlockSpec(shape, lambda i: (0,) * len(shape), pipeline_mode=pl.Buffered(1))


def _merge(x, ya, yb, gates, wa, wb, wo, side_weights=(), *, tm):
    m, d = x.shape
    side = _SideCast(side_weights, m // tm, lambda i: i)
    out = pl.pallas_call(
        functools.partial(_merge_kernel, n_side=side.n),
        out_shape=[jax.ShapeDtypeStruct((m, d), F32)] + side.out_shapes,
        grid=(m // tm,),
        in_specs=[
            pl.BlockSpec((tm, d), lambda i: (i, 0)),
            pl.BlockSpec((tm, MIX_W), lambda i: (i, 0)),
            pl.BlockSpec((tm, MIX_W), lambda i: (i, 0)),
            pl.BlockSpec((tm, 2 * d), lambda i: (i, 0)),
            _resident(wa.shape), _resident(wb.shape), _resident(wo.shape),
        ] + side.in_specs,
        out_specs=[pl.BlockSpec((tm, d), lambda i: (i, 0))] + side.out_specs,
        compiler_params=_cparams(("arbitrary",)),
        name="merge",
    )(x, ya, yb, gates, wa, wb, wo, *side.arrays)
    return out[0], out[1:]


def _ple_kernel(x_ref, p_ref, gg_ref, pg_ref, wg_ref, wp_ref, o_ref):
    ple = _rms(_dot(p_ref[...].astype(BF16), wp_ref[...]), pg_ref[...])
    x = x_ref[...]
    gate = _sigmoid(_dot(_rms(x, gg_ref[...]).astype(BF16), wg_ref[...]))
    o_ref[...] = x + gate * ple


def _ple(x, p, gate_gain, post_gain, w_gate, w_proj, *, tm):
    m, d = x.shape
    pd = p.shape[1]
    return pl.pallas_call(
        _ple_kernel,
        out_shape=jax.ShapeDtypeStruct((m, d), F32),
        grid=(m // tm,),
        in_specs=[
            pl.BlockSpec((tm, d), lambda i: (i, 0)),
            pl.BlockSpec((tm, pd), lambda i: (i, 0)),
            pl.BlockSpec((1, d), lambda i: (0, 0)),
            pl.BlockSpec((1, d), lambda i: (0, 0)),
            _resident(w_gate.shape), _resident(w_proj.shape),
        ],
        out_specs=pl.BlockSpec((tm, d), lambda i: (i, 0)),
        compiler_params=_cparams(("parallel",)),
        name="ple",
    )(x, p, gate_gain.reshape(1, d), post_gain.reshape(1, d), w_gate, w_proj)


def _tiles(seq, d, f):
    def pick(n, choices):
        return next(c for c in choices if n % c == 0)
    return dict(
        tm=pick(seq, (512, 256, 128)),
        tfm=pick(seq, (1024, 512, 256, 128)),
        tf=pick(f, (512, 256, 128)),
        tp=pick(seq, (1024, 512, 256, 128)),
        tn=pick(math.gcd(MIX_W, 2 * d), (1024, 512, 256, 128)),
        ta=pick(seq, (512, 256, 128)),
        qpg=4,
        hps=4,
        c=128,
        cu=16,
        hpg=4,
    )


def kernel(x, p, ffn1_norm, ffn1_w_gate, ffn1_w_up, ffn1_w_down, mix_norm, w_in, q_norm, k_norm, lambda_q1, lambda_k1, lambda_q2, lambda_k2, diff_subln, rel_bias, hgrn_lb_logits, hgrn_norm, w_branch_a, w_branch_b, w_out, ffn2_norm, ffn2_w_gate, ffn2_w_up, ffn2_w_down, ple_gate_norm, w_ple_gate, w_ple_proj, ple_post_norm):
    batch, seq, d = x.shape
    depth = ffn1_norm.shape[0]
    assert depth == 1
    m = batch * seq
    tl = _tiles(seq, d, ffn1_w_gate.shape[2])
    bf = lambda w: w.astype(BF16)

    lower_bounds = jnp.cumsum(jax.nn.softmax(hgrn_lb_logits.astype(F32), axis=0), axis=0)
    lb = lower_bounds[0].reshape(HEADS, HEAD_W)
    lam = (jnp.exp(jnp.sum(lambda_q1[0].astype(F32) * lambda_k1[0].astype(F32)))
           - jnp.exp(jnp.sum(lambda_q2[0].astype(F32) * lambda_k2[0].astype(F32)))
           + LAMBDA_INIT)
    scale = QK_DIM ** -0.5
    qg = jnp.tile(q_norm[0].astype(F32), 2).reshape(1, HEAD_W) * (scale * LOG2E)
    kg = jnp.tile(k_norm[0].astype(F32), 2).reshape(1, HEAD_W)
    qk_bound = (QK_DIM * scale * jnp.max(jnp.abs(q_norm[0].astype(F32)))
                * jnp.max(jnp.abs(k_norm[0].astype(F32))))

    x2 = x.reshape(m, d)
    x2, (w_in_b, wa_b, wb_b, wo_b) = _ffn(
        x2, ffn1_norm[0], bf(ffn1_w_gate[0]), bf(ffn1_w_up[0]), bf(ffn1_w_down[0]),
        (w_in[0], w_branch_a[0], w_branch_b[0], w_out[0]), tm=tl["tfm"], tf=tl["tf"])

    qkv, hin, logf, gates, (wg2_b, wu2_b, wd2_b) = _proj(
        x2, mix_norm[0], w_in_b, qg, kg, lb, (ffn2_w_gate[0], ffn2_w_up[0], ffn2_w_down[0]),
        batch=batch, seq=seq, tm=tl["tp"], tn=tl["tn"])
    ya = _attn(qkv, lam, rel_bias, qk_bound, diff_subln[0], batch=batch, seq=seq,
               t=tl["ta"], qpg=tl["qpg"], hps=tl["hps"])
    yb = _hgrn(hin, logf, hgrn_norm[0], batch=batch, seq=seq, c=tl["c"], unroll=tl["cu"],
               hpg=tl["hpg"])
    x2, (wpg_b, wpp_b) = _merge(x2, ya, yb, gates, wa_b, wb_b, wo_b,
                                (w_ple_gate[0], w_ple_proj[0]), tm=tl["tm"])

    x2, _ = _ffn(x2, ffn2_norm[0], wg2_b, wu2_b, wd2_b, tm=tl["tfm"], tf=tl["tf"])
    x2 = _ple(x2, p[0].reshape(m, -1), ple_gate_norm[0], ple_post_norm[0], wpg_b, wpp_b, tm=tl["tm"])
    return x2.reshape(batch, seq, d)
```
